```python
import jax, jax.numpy as jnp
from jax import lax
import numpy as np

D_MODEL = 1024
BATCH = 8
SEQ = 2048
DEPTH = 4

GRID_W = 64
CTX_LEN = 256
N_MIXERS = 4
EPS = 1e-6
F32 = jnp.float32

ML_HEADS = 4
ML_DK = 256
ML_DV = 512
ML_QK = ML_HEADS * ML_DK
ML_INNER = ML_HEADS * ML_DV
ML_CHUNK = 64
ML_SPLITS = (ML_QK, 2 * ML_QK, 2 * ML_QK + ML_INNER, 2 * ML_QK + 2 * ML_INNER, 2 * ML_QK + 3 * ML_INNER)
ML_PROJ = 2 * ML_QK + 3 * ML_INNER + 4 * ML_HEADS

AT_HEADS = 16
AT_KV_HEADS = 4
AT_GROUP = AT_HEADS // AT_KV_HEADS
AT_HEAD_DIM = 64
AT_WINDOW = 128
AT_BLOCK = 128
AT_Q = AT_HEADS * AT_HEAD_DIM
AT_KV = AT_KV_HEADS * AT_HEAD_DIM
AT_SPLITS = (AT_Q, AT_Q + AT_KV, AT_Q + 2 * AT_KV)
AT_PROJ = 2 * AT_Q + 2 * AT_KV
ROPE_BASE = 10000.0

SC_WIDTH = D_MODEL
SC_KSIZE = 3
SC_PROJ = 4 * SC_WIDTH

HG_EXPAND = 128
HG_HEADS = D_MODEL // HG_EXPAND
HG_FDIM = HG_HEADS * HG_EXPAND
HG_IDIM = D_MODEL
HG_HEAD_I = HG_IDIM // HG_HEADS
HG_CHUNK = 64
HG_SPLITS = (HG_FDIM, 2 * HG_FDIM, 3 * HG_FDIM, 3 * HG_FDIM + HG_IDIM)
HG_PROJ = 3 * HG_FDIM + 2 * HG_IDIM

kernel_name = 'hybrid_interleaved_mlstm_swa_conv_hgrn2_dit'


def _layers_of(kind):
    return len(range(kind, DEPTH, N_MIXERS))


def rms_norm(a, g):
    af = a.astype(F32)
    af = af * lax.rsqrt(jnp.mean(af * af, axis=-1, keepdims=True) + EPS)
    return (af * g.astype(F32)).astype(a.dtype)


def modulate(a, g, shift, scale):
    return rms_norm(a, g) * (1 + scale) + shift


def _split_heads(a, n_heads):
    b, t, w = a.shape
    return a.reshape(b, t, n_heads, w // n_heads).transpose(0, 2, 1, 3)


def _merge_heads(a):
    b, h, t, d = a.shape
    return a.transpose(0, 2, 1, 3).reshape(b, t, h * d)


def _head_rms(a, g):
    h, d = a.shape[1], a.shape[3]
    a = a * lax.rsqrt(jnp.mean(a * a, axis=-1, keepdims=True) + EPS)
    return a * g.astype(F32).reshape(1, h, 1, d)


def _flip_time(a, direction):
    return a if direction == 0 else jnp.flip(a, axis=2)


def _to_chunks(a, chunk):
    b, h, t = a.shape[:3]
    a = a.reshape(b, h, t // chunk, chunk, *a.shape[3:])
    return jnp.moveaxis(a, 2, 0)


def _from_chunks(a):
    nc, b, h, l = a.shape[:4]
    return jnp.moveaxis(a, 0, 2).reshape(b, h, nc * l, *a.shape[4:])


def axial_rope(n_tokens, head_dim):
    rows = n_tokens // GRID_W
    row = jnp.repeat(jnp.arange(rows), GRID_W).astype(F32)
    col = jnp.tile(jnp.arange(GRID_W), rows).astype(F32)
    n_freq = head_dim // 4
    freqs = jnp.power(ROPE_BASE, -jnp.arange(n_freq, dtype=F32) / n_freq)
    ang = jnp.concatenate([row[:, None] * freqs, col[:, None] * freqs], axis=-1)
    return jnp.cos(ang), jnp.sin(ang)


def apply_rope(a, cos, sin):
    a1, a2 = jnp.split(a, 2, axis=-1)
    cos = cos[None, :, None, :].astype(a.dtype)
    sin = sin[None, :, None, :].astype(a.dtype)
    return jnp.concatenate([a1 * cos - a2 * sin, a1 * sin + a2 * cos], axis=-1)


def sink_softmax(scores, sink):
    m = sink
    for s in scores:
        m = jnp.maximum(m, jnp.max(s, axis=-1))
    exps = [jnp.exp(s - m[..., None]) for s in scores]
    den = jnp.exp(sink - m)
    for e in exps:
        den = den + jnp.sum(e, axis=-1)
    return tuple(e / den[..., None] for e in exps)


def mlstm_scan(q, k, v, log_i, log_f, state):
    causal = jnp.tril(jnp.ones((ML_CHUNK, ML_CHUNK), dtype=bool))

    def step(carry, inp):
        c_mat, n_vec, m_stab = carry
        qc, kc, vc, lic, lfc = inp
        b = jnp.cumsum(lfc, axis=-1)
        log_d = jnp.where(causal, b[..., :, None] - b[..., None, :] + lic[..., None, :], -jnp.inf)
        log_inter = b + m_stab[..., None]
        m_row = jnp.maximum(log_inter, jnp.max(log_d, axis=-1))
        inter = jnp.exp(log_inter - m_row)
        s = jnp.einsum('bhtd,bhsd->bhts', qc, kc) * jnp.exp(log_d - m_row[..., None])
        num = inter[..., None] * jnp.einsum('bhtd,bhde->bhte', qc, c_mat) + jnp.einsum('bhts,bhse->bhte', s, vc)
        den = inter * jnp.einsum('bhtd,bhd->bht', qc, n_vec) + jnp.sum(s, axis=-1)
        h = num / jnp.maximum(jnp.abs(den), jnp.exp(-m_row))[..., None]
        b_last = b[..., -1]
        log_w = b_last[..., None] - b + lic
        m_new = jnp.maximum(b_last + m_stab, jnp.max(log_w, axis=-1))
        decay = jnp.exp(b_last + m_stab - m_new)
        kw = kc * jnp.exp(log_w - m_new[..., None])[..., None]
        c_new = decay[..., None, None] * c_mat + jnp.einsum('bhsd,bhse->bhde', kw, vc)
        n_new = decay[..., None] * n_vec + jnp.sum(kw, axis=2)
        return (c_new, n_new, m_new), h

    xs = tuple(_to_chunks(a.astype(F32), ML_CHUNK) for a in (q, k, v, log_i, log_f))
    state, h = lax.scan(step, state, xs)
    return _from_chunks(h), state


def mlstm_mixer(h_lat, h_ctx, w_in, gate_b, head_g, w_out, need_ctx):
    def project(h):
        b, t, _ = h.shape
        q, k, v, o, z, g = jnp.split(h @ w_in, ML_SPLITS, axis=-1)
        q = _split_heads(q, ML_HEADS).astype(F32) * (ML_DK ** -0.5)
        k = _split_heads(k, ML_HEADS).astype(F32)
        v = _split_heads(v, ML_HEADS).astype(F32)
        g = (g.reshape(b, t, 4, ML_HEADS) + gate_b).astype(F32).transpose(2, 0, 3, 1)
        return q, k, v, g, o, z

    ql, kl, vl, gl, ol, zl = project(h_lat)
    qc, kc, vc, gc, oc, zc = project(h_ctx)
    bsz = h_lat.shape[0]
    zero = (jnp.zeros((bsz, ML_HEADS, ML_DK, ML_DV), F32),
            jnp.zeros((bsz, ML_HEADS, ML_DK), F32),
            jnp.zeros((bsz, ML_HEADS), F32))
    h_lat_dirs, h_ctx_dirs = [], []
    for d in range(2):
        li_c, lf_c = gc[2 * d], jax.nn.log_sigmoid(gc[2 * d + 1])
        li_l, lf_l = gl[2 * d], jax.nn.log_sigmoid(gl[2 * d + 1])
        hc, st = mlstm_scan(_flip_time(qc, d), _flip_time(kc, d), _flip_time(vc, d),
                            _flip_time(li_c, d), _flip_time(lf_c, d), zero)
        hl, _ = mlstm_scan(_flip_time(ql, d), _flip_time(kl, d), _flip_time(vl, d),
                           _flip_time(li_l, d), _flip_time(lf_l, d), st)
        h_lat_dirs.append(_flip_time(hl, d))
        h_ctx_dirs.append(_flip_time(hc, d))

    def finish(h_sum, o, z):
        hn = _merge_heads(_head_rms(h_sum, head_g)).astype(o.dtype)
        return (jax.nn.sigmoid(o) * hn * jax.nn.silu(z)) @ w_out

    y_lat = finish(h_lat_dirs[0] + h_lat_dirs[1], ol, zl)
    y_ctx = finish(h_ctx_dirs[0] + h_ctx_dirs[1], oc, zc) if need_ctx else None
    return y_lat, y_ctx


def attn_mixer(h_lat, h_ctx, w_in, q_g, k_g, sink, w_out, need_ctx):
    bsz, t_lat, _ = h_lat.shape
    scale = AT_HEAD_DIM ** -0.5
    sink_f = sink.astype(F32).reshape(AT_KV_HEADS, AT_GROUP)

    def project(h):
        b, t, _ = h.shape
        q, k, v, z = jnp.split(h @ w_in, AT_SPLITS, axis=-1)
        q = rms_norm(q.reshape(b, t, AT_HEADS, AT_HEAD_DIM), q_g)
        k = rms_norm(k.reshape(b, t, AT_KV_HEADS, AT_HEAD_DIM), k_g)
        return q, k, v.reshape(b, t, AT_KV_HEADS, AT_HEAD_DIM), z

    ql, kl, vl, zl = project(h_lat)
    qc, kc, vc, zc = project(h_ctx)
    cos, sin = axial_rope(t_lat, AT_HEAD_DIM)
    ql = apply_rope(ql, cos, sin) * scale
    kl = apply_rope(kl, cos, sin)

    nb = t_lat // AT_BLOCK
    qb = ql.reshape(bsz, nb, AT_BLOCK, AT_KV_HEADS, AT_GROUP, AT_HEAD_DIM)
    pad = ((0, 0), (AT_BLOCK, AT_BLOCK), (0, 0), (0, 0))
    kp = jnp.pad(kl, pad).reshape(bsz, nb + 2, AT_BLOCK, AT_KV_HEADS, AT_HEAD_DIM)
    vp = jnp.pad(vl, pad).reshape(bsz, nb + 2, AT_BLOCK, AT_KV_HEADS, AT_HEAD_DIM)
    kb = jnp.concatenate([kp[:, :-2], kp[:, 1:-1], kp[:, 2:]], axis=2)
    vb = jnp.concatenate([vp[:, :-2], vp[:, 1:-1], vp[:, 2:]], axis=2)
    q_pos = jnp.arange(t_lat).reshape(nb, AT_BLOCK)
    k_pos = (jnp.arange(nb)[:, None] - 1) * AT_BLOCK + jnp.arange(3 * AT_BLOCK)[None, :]
    band = ((jnp.abs(q_pos[:, :, None] - k_pos[:, None, :]) <= AT_WINDOW)
            & (k_pos[:, None, :] >= 0) & (k_pos[:, None, :] < t_lat))
    s_loc = jnp.einsum('bnqkgd,bnskd->bnkgqs', qb, kb).astype(F32)
    s_loc = jnp.where(band[None, :, None, None], s_loc, -jnp.inf)
    s_ctx = jnp.einsum('bnqkgd,bckd->bnkgqc', qb, kc).astype(F32)
    p_loc, p_ctx = sink_softmax((s_loc, s_ctx), sink_f[None, None, :, :, None])
    o = (jnp.einsum('bnkgqs,bnskd->bnqkgd', p_loc, vb.astype(F32))
         + jnp.einsum('bnkgqc,bckd->bnqkgd', p_ctx, vc.astype(F32)))
    o = o.reshape(bsz, t_lat, AT_Q).astype(h_lat.dtype)
    y_lat = (o * jax.nn.silu(zl)) @ w_out

    y_ctx = None
    if need_ctx:
        qcs = (qc * scale).reshape(bsz, -1, AT_KV_HEADS, AT_GROUP, AT_HEAD_DIM)
        s_cc = jnp.einsum('bqkgd,bskd->bkgqs', qcs, kc).astype(F32)
        (p_cc,) = sink_softmax((s_cc,), sink_f[None, :, :, None])
        oc = jnp.einsum('bkgqs,bskd->bqkgd', p_cc, vc.astype(F32)).reshape(bsz, -1, AT_Q).astype(h_ctx.dtype)
        y_ctx = (oc * jax.nn.silu(zc)) @ w_out
    return y_lat, y_ctx


def _dwconv(a, w, b):
    y = lax.conv_general_dilated(a, w[:, None, :].astype(a.dtype), window_strides=(1,),
                                 padding=((SC_KSIZE // 2, SC_KSIZE // 2),),
                                 dimension_numbers=('NWC', 'WIO', 'NWC'),
                                 feature_group_count=a.shape[-1])
    return y + b.astype(a.dtype)


def conv_mixer(h_lat, h_ctx, w_in, conv_w, conv_b, w_out, need_ctx):
    def run(h):
        xin, b_gate, c_gate, z = jnp.split(h @ w_in, 4, axis=-1)
        y = _dwconv(c_gate * xin, conv_w, conv_b)
        return (b_gate * y * jax.nn.silu(z)) @ w_out
    return run(h_lat), (run(h_ctx) if need_ctx else None)


def hgrn_scan(q, k, i, log_f, state):
    causal = jnp.tril(jnp.ones((HG_CHUNK, HG_CHUNK), dtype=bool))[:, :, None]

    def step(s_mat, inp):
        qc, kc, ic, lfc = inp
        a = jnp.cumsum(lfc, axis=2)
        rel = jnp.where(causal, a[:, :, :, None, :] - a[:, :, None, :, :], -jnp.inf)
        attn = jnp.einsum('bhtf,bhsf,bhtsf->bhts', qc, kc, jnp.exp(rel))
        o = jnp.einsum('bhts,bhsi->bhti', attn, ic) + jnp.einsum('bhtf,bhfi->bhti', qc * jnp.exp(a), s_mat)
        a_last = a[:, :, -1]
        kd = kc * jnp.exp(a_last[:, :, None, :] - a)
        s_new = jnp.exp(a_last)[..., None] * s_mat + jnp.einsum('bhsf,bhsi->bhfi', kd, ic)
        return s_new, o

    xs = tuple(_to_chunks(t.astype(F32), HG_CHUNK) for t in (q, k, i, log_f))
    state, o = lax.scan(step, state, xs)
    return _from_chunks(o), state


def hgrn_mixer(h_lat, h_ctx, w_in, f_b, lb_param, head_g, w_out, layer, need_ctx):
    p = jax.nn.softmax(lb_param.astype(F32), axis=1)
    lb = (jnp.cumsum(p, axis=1) - p[:, :1])[:, layer]

    def project(h):
        q, f_fw, f_bw, i, z = jnp.split(h @ w_in, HG_SPLITS, axis=-1)
        q = _split_heads(jax.nn.silu(q).astype(F32), HG_HEADS)
        i = _split_heads(i.astype(F32), HG_HEADS)
        dirs = []
        for d, fp in enumerate((f_fw, f_bw)):
            f = lb[d] + (1.0 - lb[d]) * jax.nn.sigmoid(fp.astype(F32) + f_b[d].astype(F32))
            dirs.append((_split_heads(1.0 - f, HG_HEADS), _split_heads(jnp.log(f), HG_HEADS)))
        return q, i, dirs, z

    ql, il, dl, zl = project(h_lat)
    qc, ic, dc, zc = project(h_ctx)
    s0 = jnp.zeros((h_lat.shape[0], HG_HEADS, HG_EXPAND, HG_HEAD_I), F32)
    o_lat, o_ctx = [], []
    for d in range(2):
        oc_d, s_ctx = hgrn_scan(_flip_time(qc, d), _flip_time(dc[d][0], d), _flip_time(ic, d),
                                _flip_time(dc[d][1], d), s0)
        ol_d, _ = hgrn_scan(_flip_time(ql, d), _flip_time(dl[d][0], d), _flip_time(il, d),
                            _flip_time(dl[d][1], d), s_ctx)
        o_lat.append(_flip_time(ol_d, d))
        o_ctx.append(_flip_time(oc_d, d))

    def finish(o, z):
        hn = _merge_heads(_head_rms(o, head_g)).astype(z.dtype)
        return (hn * jax.nn.silu(z)) @ w_out

    y_lat = finish(o_lat[0] + o_lat[1], zl)
    y_ctx = finish(o_ctx[0] + o_ctx[1], zc) if need_ctx else None
    return y_lat, y_ctx


def setup_inputs(seed: int = 0) -> dict:
    key = jax.random.key(seed)
    ks = iter(jax.random.split(key, 40))

    def nrm(shape, scale):
        return scale * jax.random.normal(next(ks), shape, F32)

    n_a, n_b, n_c, n_d = (_layers_of(m) for m in range(N_MIXERS))
    d = D_MODEL
    f_bias = jnp.linspace(3.0, 6.0, ML_HEADS, dtype=F32)
    ml_gate_b = jnp.stack([nrm((n_a, ML_HEADS), 0.1), f_bias + nrm((n_a, ML_HEADS), 0.1),
                           nrm((n_a, ML_HEADS), 0.1), f_bias + nrm((n_a, ML_HEADS), 0.1)], axis=1)
    return {
        'x': nrm((BATCH, SEQ, d), 1.0),
        'c': nrm((BATCH, d), 1.0),
        'ctx': nrm((BATCH, CTX_LEN, d), 1.0),
        'c_ctx': nrm((d,), 1.0),
        'ada_w': nrm((DEPTH, d, 3 * d), 0.5 * d ** -0.5),
        'ada_b': nrm((DEPTH, 3 * d), 0.02),
        'norm_g': 1.0 + nrm((DEPTH, d), 0.02),
        'ml_w_in': nrm((n_a, d, ML_PROJ), d ** -0.5),
        'ml_gate_b': ml_gate_b,
        'ml_head_g': 1.0 + nrm((n_a, ML_INNER), 0.02),
        'ml_w_out': nrm((n_a, ML_INNER, d), ML_INNER ** -0.5),
        'at_w_in': nrm((n_b, d, AT_PROJ), d ** -0.5),
        'at_q_g': 1.0 + nrm((n_b, AT_HEAD_DIM), 0.02),
        'at_k_g': 1.0 + nrm((n_b, AT_HEAD_DIM), 0.02),
        'at_sink': nrm((n_b, AT_HEADS), 1.0),
        'at_w_out': nrm((n_b, AT_Q, d), AT_Q ** -0.5),
        'sc_w_in': nrm((n_c, d, SC_PROJ), d ** -0.5),
        'sc_conv_w': nrm((n_c, SC_KSIZE, SC_WIDTH), SC_KSIZE ** -0.5),
        'sc_conv_b': nrm((n_c, SC_WIDTH), 0.02),
        'sc_w_out': nrm((n_c, SC_WIDTH, d), SC_WIDTH ** -0.5),
        'hg_w_in': nrm((n_d, d, HG_PROJ), d ** -0.5),
        'hg_f_b': nrm((n_d, 2, HG_FDIM), 0.1),
        'hg_lb': nrm((n_d, 2, DEPTH, HG_FDIM), 0.1),
        'hg_head_g': 1.0 + nrm((n_d, HG_IDIM), 0.02),
        'hg_w_out': nrm((n_d, HG_IDIM, d), HG_IDIM ** -0.5),
    }


def reference(x, c, ctx, c_ctx, ada_w, ada_b, norm_g,
              ml_w_in, ml_gate_b, ml_head_g, ml_w_out,
              at_w_in, at_q_g, at_k_g, at_sink, at_w_out,
              sc_w_in, sc_conv_w, sc_conv_b, sc_w_out,
              hg_w_in, hg_f_b, hg_lb, hg_head_g, hg_w_out):
    for layer in range(DEPTH):
        kind, j = layer % N_MIXERS, layer // N_MIXERS
        need_ctx = layer < DEPTH - 1
        mod_lat = jax.nn.silu(c) @ ada_w[layer] + ada_b[layer]
        mod_ctx = jax.nn.silu(c_ctx) @ ada_w[layer] + ada_b[layer]
        sh_l, sc_l, g_l = jnp.split(mod_lat[:, None, :], 3, axis=-1)
        sh_c, sc_c, g_c = jnp.split(mod_ctx[None, None, :], 3, axis=-1)
        h_lat = modulate(x, norm_g[layer], sh_l, sc_l)
        h_ctx = modulate(ctx, norm_g[layer], sh_c, sc_c)
        if kind == 0:
            y_lat, y_ctx = mlstm_mixer(h_lat, h_ctx, ml_w_in[j], ml_gate_b[j], ml_head_g[j], ml_w_out[j], need_ctx)
        elif kind == 1:
            y_lat, y_ctx = attn_mixer(h_lat, h_ctx, at_w_in[j], at_q_g[j], at_k_g[j], at_sink[j], at_w_out[j], need_ctx)
        elif kind == 2:
            y_lat, y_ctx = conv_mixer(h_lat, h_ctx, sc_w_in[j], sc_conv_w[j], sc_conv_b[j], sc_w_out[j], need_ctx)
        else:
            y_lat, y_ctx = hgrn_mixer(h_lat, h_ctx, hg_w_in[j], hg_f_b[j], hg_lb[j], hg_head_g[j], hg_w_out[j],
                                      layer, need_ctx)
        x = x + g_l * y_lat
        if need_ctx:
            ctx = ctx + g_c * y_ctx
    return x
```

```python
import functools

import numpy as np
import jax
import jax.numpy as jnp
from jax import lax
from jax.experimental import pallas as pl
from jax.experimental.pallas import tpu as pltpu

F32 = jnp.float32
BF16 = jnp.bfloat16
EPS = 1e-6
DEPTH = 4
GRID_W = 64
ROPE_BASE = 10000.0

ML_HEADS, ML_DK, ML_DV = 4, 256, 512
ML_QK = ML_HEADS * ML_DK
ML_INNER = ML_HEADS * ML_DV
ML_CHUNK = 256

AT_HEADS, AT_KV_HEADS, AT_HEAD_DIM = 16, 4, 64
AT_GROUP = AT_HEADS // AT_KV_HEADS
AT_BLOCK = 128
AT_Q = AT_HEADS * AT_HEAD_DIM
AT_KV = AT_KV_HEADS * AT_HEAD_DIM

SC_KSIZE = 3

HG_HEADS, HG_DIM = 8, 128
HG_CHUNK = 128

LANES = 128
SUBLANES = 8
VMEM_LIMIT_BYTES = 56 * 1024 * 1024

ROW_BLOCK = 256
OUT_ROWS = 256


def _cparams(*sem):
    return pltpu.CompilerParams(dimension_semantics=sem, vmem_limit_bytes=VMEM_LIMIT_BYTES)


def _silu(x):
    return x * jax.nn.sigmoid(x)


def _ada_kernel(c_ref, w_ref, b_ref, o_ref):
    s = _silu(c_ref[...])
    o_ref[0] = jnp.dot(s.astype(BF16), w_ref[0].astype(BF16), preferred_element_type=F32) + b_ref[0]


def _ada_mod(c, c_ctx, ada_w, ada_b):
    b, d = c.shape
    depth = ada_w.shape[0]
    rows = -(-(b + 1) // SUBLANES) * SUBLANES
    cc = jnp.zeros((rows, d), F32).at[:b].set(c).at[b].set(c_ctx)
    tn = 1024
    out = pl.pallas_call(
        _ada_kernel,
        out_shape=jax.ShapeDtypeStruct((depth, rows, 3 * d), F32),
        grid=(depth, 3 * d // tn),
        in_specs=[pl.BlockSpec((rows, d), lambda l, j: (0, 0)),
                  pl.BlockSpec((1, d, tn), lambda l, j: (l, 0, j)),
                  pl.BlockSpec((1, 1, tn), lambda l, j: (l, 0, j))],
        out_specs=pl.BlockSpec((1, rows, tn), lambda l, j: (l, 0, j)),
        compiler_params=_cparams("parallel", "parallel"),
        name="ada_mod",
    )(cc, ada_w, ada_b.reshape(depth, 1, 3 * d))
    return out.reshape(depth, rows, 3, d)


def _modulated(x_ref, ml_ref, mc_ref, g_ref, h_ref, tc):
    ta = x_ref.shape[1]
    g = g_ref[...]

    def rows(r0, m_ref):
        x = x_ref[0, pl.ds(r0, ROW_BLOCK), :]
        ms = jnp.mean(x * x, axis=-1, keepdims=True)
        xn = x * lax.rsqrt(ms + EPS) * g
        h = xn * (1.0 + m_ref[0, 1:2, :]) + m_ref[0, 0:1, :]
        h_ref[pl.ds(r0, ROW_BLOCK), :] = h.astype(BF16)

    for i in range(tc // ROW_BLOCK):
        rows(i * ROW_BLOCK, mc_ref)

    def body(i, carry):
        rows(pl.multiple_of(tc + i * ROW_BLOCK, ROW_BLOCK), ml_ref)
        return carry

    lax.fori_loop(0, (ta - tc) // ROW_BLOCK, body, 0)


def _inproj_kernel(x_ref, ml_ref, mc_ref, g_ref, w_ref, *rest, tc, has_small):
    if has_small:
        ws_ref, o_ref, os_ref, h_ref = rest
    else:
        o_ref, h_ref = rest

    @pl.when(pl.program_id(1) == 0)
    def _():
        _modulated(x_ref, ml_ref, mc_ref, g_ref, h_ref, tc)
        if has_small:
            os_ref[0] = jnp.dot(h_ref[...], ws_ref[...], preferred_element_type=F32)

    o_ref[0] = jnp.dot(h_ref[...], w_ref[...], preferred_element_type=F32).astype(o_ref.dtype)


def _inproj(xs, mod_l, norm_g, w, tc, tn=512, w_small=None, out_dtype=F32):
    b, ta, d = xs.shape
    n = w.shape[1]
    assert n % tn == 0 and tc % ROW_BLOCK == 0 and (ta - tc) % ROW_BLOCK == 0
    mod_lat = mod_l[:b]
    mod_ctx = mod_l[b:b + 1]
    has_small = w_small is not None
    in_specs = [pl.BlockSpec((1, ta, d), lambda i, j: (i, 0, 0)),
                pl.BlockSpec((1, 3, d), lambda i, j: (i, 0, 0)),
                pl.BlockSpec((1, 3, d), lambda i, j: (0, 0, 0)),
                pl.BlockSpec((1, d), lambda i, j: (0, 0)),
                pl.BlockSpec((d, tn), lambda i, j: (0, j))]
    args = [xs, mod_lat, mod_ctx, norm_g.reshape(1, d), w.astype(BF16)]
    out_shape = [jax.ShapeDtypeStruct((b, ta, n), out_dtype)]
    out_specs = [pl.BlockSpec((1, ta, tn), lambda i, j: (i, 0, j))]
    if has_small:
        ns = w_small.shape[1]
        in_specs.append(pl.BlockSpec((d, ns), lambda i, j: (0, 0)))
        args.append(w_small.astype(BF16))
        out_shape.append(jax.ShapeDtypeStruct((b, ta, ns), F32))
        out_specs.append(pl.BlockSpec((1, ta, ns), lambda i, j: (i, 0, 0)))
    res = pl.pallas_call(
        functools.partial(_inproj_kernel, tc=tc, has_small=has_small),
        out_shape=out_shape,
        grid=(b, n // tn),
        in_specs=in_specs,
        out_specs=out_specs,
        scratch_shapes=[pltpu.VMEM((ta, d), BF16)],
        compiler_params=_cparams("parallel", "arbitrary"),
        name="inproj",
    )(*args)
    return res if has_small else res[0]


def _head_rms(h, gain, n_heads):
    dh = h.shape[1] // n_heads
    parts = []
    for i in range(n_heads):
        hh = h[:, i * dh:(i + 1) * dh]
        ms = jnp.mean(hh * hh, axis=-1, keepdims=True)
        parts.append(hh * lax.rsqrt(ms + EPS))
    return jnp.concatenate(parts, axis=1) * gain


def _outproj_kernel(*refs, mode, tc, tm):
    if mode == "plain":
        u_ref, w_ref, x_ref, ml_ref, mc_ref, o_ref = refs
        u = u_ref[0]
    elif mode == "mlstm":
        h0_ref, h1_ref, og_ref, z_ref, hg_ref, w_ref, x_ref, ml_ref, mc_ref, o_ref = refs
        hn = _head_rms(h0_ref[0] + h1_ref[0], hg_ref[...], ML_HEADS)
        u = (jax.nn.sigmoid(og_ref[0]) * hn * _silu(z_ref[0])).astype(BF16)
    else:
        h0_ref, h1_ref, z_ref, hg_ref, w_ref, x_ref, ml_ref, mc_ref, o_ref = refs
        hn = _head_rms(h0_ref[0] + h1_ref[0], hg_ref[...], HG_HEADS)
        u = (hn * _silu(z_ref[0])).astype(BF16)
    y = jnp.dot(u, w_ref[...], preferred_element_type=F32)
    row = pl.program_id(1) * tm + lax.broadcasted_iota(jnp.int32, (tm, 1), 0)
    gate = jnp.where(row < tc, mc_ref[0, 2:3, :], ml_ref[0, 2:3, :])
    o_ref[0] = x_ref[0] + gate * y


def _outproj(mode, feats, w_out, xs, mod_l, tc, head_g=None):
    b, ta, d = xs.shape
    kdim = w_out.shape[0]
    tm = OUT_ROWS
    assert ta % tm == 0 and tc % tm == 0
    row_spec = lambda width, col: pl.BlockSpec((1, tm, width), lambda i, r, col=col: (i, r, col))
    in_specs, args = [], []
    for arr, width, col in feats:
        in_specs.append(row_spec(width, col))
        args.append(arr)
    if head_g is not None:
        in_specs.append(pl.BlockSpec((1, kdim), lambda i, r: (0, 0)))
        args.append(head_g.reshape(1, kdim))
    in_specs += [pl.BlockSpec((kdim, d), lambda i, r: (0, 0)),
                 row_spec(d, 0),
                 pl.BlockSpec((1, 3, d), lambda i, r: (i, 0, 0)),
                 pl.BlockSpec((1, 3, d), lambda i, r: (0, 0, 0))]
    args += [w_out.astype(BF16), xs, mod_l[:b], mod_l[b:b + 1]]
    return pl.pallas_call(
        functools.partial(_outproj_kernel, mode=mode, tc=tc, tm=tm),
        out_shape=jax.ShapeDtypeStruct((b, ta, d), F32),
        grid=(b, ta // tm),
        in_specs=in_specs,
        out_specs=row_spec(d, 0),
        compiler_params=_cparams("parallel", "parallel"),
        name="outproj_" + mode,
    )(*args)


def _conv_kernel(x_ref, ml_ref, mc_ref, g_ref, w_ref, cw_ref, cb_ref, o_ref, h_ref, *, tc, tw):
    @pl.when(pl.program_id(1) == 0)
    def _():
        _modulated(x_ref, ml_ref, mc_ref, g_ref, h_ref, tc)

    ta = h_ref.shape[0]
    p = jnp.dot(h_ref[...], w_ref[0], preferred_element_type=F32)
    xin, bg, cg, z = (p[:, i * tw:(i + 1) * tw] for i in range(4))
    u = cg * xin
    row = lax.broadcasted_iota(jnp.int32, (ta, 1), 0)
    first = (row == 0) | (row == tc)
    last = (row == tc - 1) | (row == ta - 1)
    u_prev = jnp.where(first, 0.0, pltpu.roll(u, 1, axis=0))
    u_next = jnp.where(last, 0.0, pltpu.roll(u, ta - 1, axis=0))
    cw = cw_ref[0]
    y = u_prev * cw[0:1, :] + u * cw[1:2, :] + u_next * cw[2:3, :] + cb_ref[0]
    o_ref[0] = (bg * y * _silu(z)).astype(o_ref.dtype)


def _conv_mixer(xs, mod_l, norm_g, w_in, conv_w, conv_b, tc, tw=128):
    b, ta, d = xs.shape
    e = conv_w.shape[1]
    nt = e // tw
    w = w_in.reshape(d, 4, nt, tw).transpose(2, 0, 1, 3).reshape(nt, d, 4 * tw).astype(BF16)
    cw = conv_w.reshape(SC_KSIZE, nt, tw).transpose(1, 0, 2)
    cb = conv_b.reshape(nt, 1, tw)
    return pl.pallas_call(
        functools.partial(_conv_kernel, tc=tc, tw=tw),
        out_shape=jax.ShapeDtypeStruct((b, ta, e), BF16),
        grid=(b, nt),
        in_specs=[pl.BlockSpec((1, ta, d), lambda i, j: (i, 0, 0)),
                  pl.BlockSpec((1, 3, d), lambda i, j: (i, 0, 0)),
                  pl.BlockSpec((1, 3, d), lambda i, j: (0, 0, 0)),
                  pl.BlockSpec((1, d), lambda i, j: (0, 0)),
                  pl.BlockSpec((1, d, 4 * tw), lambda i, j: (j, 0, 0)),
                  pl.BlockSpec((1, SC_KSIZE, tw), lambda i, j: (j, 0, 0)),
                  pl.BlockSpec((1, 1, tw), lambda i, j: (j, 0, 0))],
        out_specs=pl.BlockSpec((1, ta, tw), lambda i, j: (i, 0, j)),
        scratch_shapes=[pltpu.VMEM((ta, d), BF16)],
        compiler_params=_cparams("parallel", "arbitrary"),
        name="conv_mixer",
    )(xs, mod_l[:b], mod_l[b:b + 1], norm_g.reshape(1, d), w, cw, cb)


def _rope_tables(tc, t):
    rows = t // GRID_W
    row = np.repeat(np.arange(rows), GRID_W).astype(np.float32)
    col = np.tile(np.arange(GRID_W), rows).astype(np.float32)
    n_freq = AT_HEAD_DIM // 4
    freqs = jnp.power(ROPE_BASE, -jnp.arange(n_freq, dtype=F32) / n_freq)
    ang = jnp.concatenate([jnp.asarray(row)[:, None] * freqs, jnp.asarray(col)[:, None] * freqs], axis=-1)
    cos, sin = jnp.cos(ang), jnp.sin(ang)
    cos_h = jnp.concatenate([cos, cos], axis=-1)
    sin_h = jnp.concatenate([-sin, sin], axis=-1)
    cos_t = jnp.concatenate([jnp.ones((tc, LANES), F32), jnp.tile(cos_h, (1, 2))], axis=0)
    sin_t = jnp.concatenate([jnp.zeros((tc, LANES), F32), jnp.tile(sin_h, (1, 2))], axis=0)
    return cos_t, sin_t


def _norm_rope_pair(x, gain, cos, sin):
    lane = lax.broadcasted_iota(jnp.int32, x.shape, 1)
    lo = lane < AT_HEAD_DIM
    x2 = x * x
    s_lo = jnp.sum(jnp.where(lo, x2, 0.0), axis=-1, keepdims=True)
    s_hi = jnp.sum(jnp.where(lo, 0.0, x2), axis=-1, keepdims=True)
    ms = jnp.where(lo, s_lo, s_hi) * (1.0 / AT_HEAD_DIM)
    xn = x * lax.rsqrt(ms + EPS) * gain
    first = (lane % AT_HEAD_DIM) < (AT_HEAD_DIM // 2)
    swapped = jnp.where(first, pltpu.roll(xn, LANES - AT_HEAD_DIM // 2, axis=1),
                        pltpu.roll(xn, AT_HEAD_DIM // 2, axis=1))
    return xn * cos + swapped * sin


def _attn_kernel(sink_ref, q_ref, k_ref, v_ref, z_ref, qg_ref, kg_ref, cos_ref, sin_ref, o_ref,
                 qs_ref, ka_ref, kb_ref, va_ref, vb_ref, *, tc, need_ctx):
    ta = q_ref.shape[1]
    t = ta - tc
    nb = t // AT_BLOCK
    blk = AT_BLOCK
    g = pl.program_id(1)
    odd = (g % 2) == 1
    scale = AT_HEAD_DIM ** -0.5

    def prep(i, carry):
        r0 = pl.multiple_of(i * ROW_BLOCK, ROW_BLOCK)
        cos = cos_ref[pl.ds(r0, ROW_BLOCK), :]
        sin = sin_ref[pl.ds(r0, ROW_BLOCK), :]
        for c in range(2):
            qc = _norm_rope_pair(q_ref[0, pl.ds(r0, ROW_BLOCK), c * LANES:(c + 1) * LANES], qg_ref[...], cos, sin)
            qs_ref[pl.ds(r0, ROW_BLOCK), c * LANES:(c + 1) * LANES] = (qc * scale).astype(BF16)
        kn = _norm_rope_pair(k_ref[0, pl.ds(r0, ROW_BLOCK), :], kg_ref[...], cos, sin)
        vv = v_ref[0, pl.ds(r0, ROW_BLOCK), :]
        lane = lax.broadcasted_iota(jnp.int32, kn.shape, 1)
        own = (lane // AT_HEAD_DIM) == (g % 2)
        k_own = jnp.where(own, kn, 0.0)
        v_own = jnp.where(own, vv, 0.0)
        k_oth = pltpu.roll(k_own, AT_HEAD_DIM, axis=1)
        v_oth = pltpu.roll(v_own, AT_HEAD_DIM, axis=1)
        ka_ref[pl.ds(r0, ROW_BLOCK), :] = jnp.where(odd, k_oth, k_own).astype(BF16)
        kb_ref[pl.ds(r0, ROW_BLOCK), :] = jnp.where(odd, k_own, k_oth).astype(BF16)
        va_ref[pl.ds(r0, ROW_BLOCK), :] = jnp.where(odd, v_oth, v_own).astype(BF16)
        vb_ref[pl.ds(r0, ROW_BLOCK), :] = jnp.where(odd, v_own, v_oth).astype(BF16)
        return carry

    lax.fori_loop(0, ta // ROW_BLOCK, prep, 0)
    zeros = jnp.zeros((blk, LANES), BF16)
    for ref in (ka_ref, kb_ref, va_ref, vb_ref):
        ref[ta:ta + blk, :] = zeros

    half = lax.broadcasted_iota(jnp.int32, (2 * blk, 1), 0) < blk
    sink_a = jnp.where(half, sink_ref[g, 0], sink_ref[g, 2])
    sink_b = jnp.where(half, sink_ref[g, 1], sink_ref[g, 3])

    qi = lax.broadcasted_iota(jnp.int32, (2 * blk, 3 * blk), 0) % blk
    kj = lax.broadcasted_iota(jnp.int32, (2 * blk, 3 * blk), 1)
    band = (kj - qi >= 0) & (kj - qi <= 2 * blk)

    nt = (((1,), (1,)), ((), ()))

    def one_side(qt, k_ref_, v_ref_, sink, k0, mask):
        kc = k_ref_[0:tc, :]
        s_ctx = lax.dot_general(qt, kc, nt, preferred_element_type=F32)
        m = jnp.maximum(sink, jnp.max(s_ctx, axis=-1, keepdims=True))
        if mask is not None:
            kl = k_ref_[pl.ds(k0, 3 * blk), :]
            s_loc = lax.dot_general(qt, kl, nt, preferred_element_type=F32)
            s_loc = jnp.where(mask, s_loc, -jnp.inf)
            m = jnp.maximum(m, jnp.max(s_loc, axis=-1, keepdims=True))
        e_ctx = jnp.exp(s_ctx - m)
        den = jnp.exp(sink - m) + jnp.sum(e_ctx, axis=-1, keepdims=True)
        acc = jnp.dot(e_ctx.astype(BF16), v_ref_[0:tc, :], preferred_element_type=F32)
        if mask is not None:
            e_loc = jnp.exp(s_loc - m)
            den = den + jnp.sum(e_loc, axis=-1, keepdims=True)
            acc = acc + jnp.dot(e_loc.astype(BF16), v_ref_[pl.ds(k0, 3 * blk), :], preferred_element_type=F32)
        return acc / den

    def block(r0, k0, mask):
        qt = jnp.concatenate([qs_ref[pl.ds(r0, blk), 0:LANES], qs_ref[pl.ds(r0, blk), LANES:2 * LANES]], axis=0)
        o = one_side(qt, ka_ref, va_ref, sink_a, k0, mask) + one_side(qt, kb_ref, vb_ref, sink_b, k0, mask)
        for c in range(2):
            zc = z_ref[0, pl.ds(r0, blk), c * LANES:(c + 1) * LANES]
            o_ref[0, pl.ds(r0, blk), c * LANES:(c + 1) * LANES] = (o[c * blk:(c + 1) * blk] * _silu(zc)).astype(o_ref.dtype)

    def lat_block(n, carry):
        r0 = pl.multiple_of(tc + n * blk, blk)
        k0 = pl.multiple_of(tc + (n - 1) * blk, blk)
        kpos = (n - 1) * blk + kj
        block(r0, k0, band & (kpos >= 0) & (kpos < t))
        return carry

    lax.fori_loop(0, nb, lat_block, 0)
    if need_ctx:
        for n in range(tc // blk):
            block(n * blk, None, None)
    else:
        o_ref[0, 0:tc, :] = jnp.zeros((tc, o_ref.shape[2]), o_ref.dtype)


def _attn_mixer(p, at_q_g, at_k_g, at_sink, tc, need_ctx):
    b, ta, _ = p.shape
    t = ta - tc
    assert tc >= AT_BLOCK and tc % AT_BLOCK == 0 and t % AT_BLOCK == 0 and ta % ROW_BLOCK == 0
    cos_t, sin_t = _rope_tables(tc, t)
    gw = AT_GROUP * AT_HEAD_DIM
    kcol = AT_Q // LANES
    vcol = (AT_Q + AT_KV) // LANES
    zcol = (AT_Q + 2 * AT_KV) // gw
    return pl.pallas_call(
        functools.partial(_attn_kernel, tc=tc, need_ctx=need_ctx),
        out_shape=jax.ShapeDtypeStruct((b, ta, AT_Q), BF16),
        grid=(b, AT_KV_HEADS),
        in_specs=[pl.BlockSpec(memory_space=pltpu.SMEM),
                  pl.BlockSpec((1, ta, gw), lambda i, g: (i, 0, g)),
                  pl.BlockSpec((1, ta, LANES), lambda i, g: (i, 0, kcol + g // 2)),
                  pl.BlockSpec((1, ta, LANES), lambda i, g: (i, 0, vcol + g // 2)),
                  pl.BlockSpec((1, ta, gw), lambda i, g: (i, 0, zcol + g)),
                  pl.BlockSpec((1, LANES), lambda i, g: (0, 0)),
                  pl.BlockSpec((1, LANES), lambda i, g: (0, 0)),
                  pl.BlockSpec((ta, LANES), lambda i, g: (0, 0)),
                  pl.BlockSpec((ta, LANES), lambda i, g: (0, 0))],
        out_specs=pl.BlockSpec((1, ta, gw), lambda i, g: (i, 0, g)),
        scratch_shapes=[pltpu.VMEM((ta, gw), BF16)] + [pltpu.VMEM((ta + AT_BLOCK, LANES), BF16)] * 4,
        compiler_params=_cparams("parallel", "arbitrary"),
        name="attn_mixer",
    )(at_sink.reshape(AT_KV_HEADS, AT_GROUP), p, p, p, p,
      jnp.tile(at_q_g, 2).reshape(1, LANES), jnp.tile(at_k_g, 2).reshape(1, LANES), cos_t, sin_t)


def _chunk_order(nc, ncc, d, s):
    bwd = jnp.where(s < ncc, ncc - 1 - s, nc - 1 - (s - ncc))
    return jnp.where(d == 0, s, bwd)


ML_GATE_PERM = np.concatenate([np.arange(0, 4), np.arange(8, 12), np.arange(4, 8), np.arange(12, 16)])
ML_NQ = 6


def _mlstm_gate_kernel(g_ref, b_ref, o_ref, *, tc, lc):
    h = ML_HEADS
    ta = g_ref.shape[2]
    nc, ncc = ta // lc, tc // lc
    x = g_ref[0] + b_ref[...]
    li = x[0:2 * h]
    lfp = x[2 * h:4 * h]
    lf = jnp.minimum(lfp, 0.0) - jnp.log1p(jnp.exp(-jnp.abs(lfp)))
    fwd = lax.broadcasted_iota(jnp.int32, (2 * h, ta), 0) < h
    fwd1 = lax.broadcasted_iota(jnp.int32, (2 * h, 1), 0) < h
    pos = lax.broadcasted_iota(jnp.int32, (2 * h, ta), 1) % lc

    def seg_scan(y, op, fill):
        yf, yb = y, y
        s = 1
        while s < lc:
            yf = op(yf, jnp.where(pos >= s, pltpu.roll(yf, s, axis=1), fill))
            yb = op(yb, jnp.where(pos < lc - s, pltpu.roll(yb, ta - s, axis=1), fill))
            s *= 2
        return jnp.where(fwd, yf, yb)

    bsum = seg_scan(lf, jnp.add, 0.0)
    a = li - bsum
    cmax = seg_scan(a, jnp.maximum, -jnp.inf)

    def end_col(y, c):
        return jnp.where(fwd1, y[:, (c + 1) * lc - 1:(c + 1) * lc], y[:, c * lc:c * lc + 1])

    tot = [end_col(bsum, c) for c in range(nc)]
    amax = [end_col(cmax, c) for c in range(nc)]

    def chain(order):
        m = jnp.zeros((2 * h, 1), F32)
        m_in = [None] * nc
        for c in order:
            m_in[c] = m
            m = tot[c] + jnp.maximum(m, amax[c])
        return m_in

    m_f = chain(list(range(nc)))
    m_b = chain(list(range(ncc - 1, -1, -1)) + list(range(nc - 1, ncc - 1, -1)))
    for c in range(nc):
        m_in = jnp.where(fwd1, m_f[c], m_b[c])
        sl = slice(c * lc, (c + 1) * lc)
        g_run = jnp.maximum(m_in, cmax[:, sl])
        g_end = jnp.maximum(m_in, amax[c])
        o_ref[0, c, 0] = a[:, sl]
        o_ref[0, c, 1] = g_run
        o_ref[0, c, 2] = jnp.exp(m_in - g_run)
        o_ref[0, c, 3] = jnp.exp(-(bsum[:, sl] + g_run))
        o_ref[0, c, 4] = jnp.exp(a[:, sl] - g_end)
        o_ref[0, c, 5] = jnp.broadcast_to(jnp.exp(m_in - g_end), (2 * h, lc))


def _mlstm_scan_kernel(q_ref, k_ref, v_ref, a_ref, c_ref, o_ref, cs_ref, ns_ref):
    lc = q_ref.shape[1]
    d = pl.program_id(1)

    @pl.when(pl.program_id(2) == 0)
    def _():
        cs_ref[...] = jnp.zeros(cs_ref.shape, F32)
        ns_ref[...] = jnp.zeros(ns_ref.shape, F32)

    diff = lax.broadcasted_iota(jnp.int32, (lc, lc), 1) - lax.broadcasted_iota(jnp.int32, (lc, lc), 0)
    mask = diff * (1 - 2 * d) <= 0
    nt = (((1,), (1,)), ((), ()))
    tn = (((0,), (0,)), ((), ()))
    for h in range(ML_HEADS):
        q = q_ref[0, :, h * ML_DK:(h + 1) * ML_DK] * (ML_DK ** -0.5)
        k = k_ref[0, :, h * ML_DK:(h + 1) * ML_DK]
        vb = v_ref[0, :, h * ML_DV:(h + 1) * ML_DV].astype(BF16)
        a_row = a_ref[0, h, 0]
        cols = c_ref[0, h, 0]
        g_run, inter, eclamp, w, decay = (cols[:, i:i + 1] for i in range(5))
        qb = q.astype(BF16)
        qk = lax.dot_general(qb, k.astype(BF16), nt, preferred_element_type=F32)
        s = qk * jnp.where(mask, jnp.exp(a_row - g_run), 0.0)
        c_old = cs_ref[h]
        n_old = ns_ref[h]
        num = inter * jnp.dot(qb, c_old.astype(BF16), preferred_element_type=F32) \
            + jnp.dot(s.astype(BF16), vb, preferred_element_type=F32)
        den = inter * jnp.sum(q * n_old, axis=-1, keepdims=True) + jnp.sum(s, axis=-1, keepdims=True)
        o_ref[0, :, h * ML_DV:(h + 1) * ML_DV] = num / jnp.maximum(jnp.abs(den), eclamp)
        kw = k * w
        dec = decay[0:1, :]
        cs_ref[h] = dec * c_old + lax.dot_general(kw.astype(BF16), vb, tn, preferred_element_type=F32)
        ns_ref[h] = dec * n_old + jnp.sum(kw, axis=0, keepdims=True)


def _mlstm_scan(p, gates, gate_b, tc):
    b, ta, _ = p.shape
    lc = ML_CHUNK
    assert tc % lc == 0 and ta % lc == 0
    nc, ncc = ta // lc, tc // lc
    ng = 4 * ML_HEADS
    g_t = gates[:, :, :ng].transpose(0, 2, 1)
    bias = gate_b.reshape(ng)[ML_GATE_PERM].reshape(ng, 1)
    gp = pl.pallas_call(
        functools.partial(_mlstm_gate_kernel, tc=tc, lc=lc),
        out_shape=jax.ShapeDtypeStruct((b, nc, ML_NQ, 2 * ML_HEADS, lc), F32),
        grid=(b,),
        in_specs=[pl.BlockSpec((1, ng, ta), lambda i: (i, 0, 0)),
                  pl.BlockSpec((ng, 1), lambda i: (0, 0))],
        out_specs=pl.BlockSpec((1, nc, ML_NQ, 2 * ML_HEADS, lc), lambda i: (i, 0, 0, 0, 0)),
        compiler_params=_cparams("parallel"),
        name="mlstm_gates",
    )(g_t, bias)
    a_rows = gp[:, :, 0].transpose(0, 2, 1, 3).reshape(b, 2 * ML_HEADS, nc, 1, lc)
    cols = jnp.pad(gp[:, :, 1:].transpose(0, 3, 1, 4, 2), ((0, 0),) * 4 + ((0, SUBLANES - (ML_NQ - 1)),))
    chunk = functools.partial(_chunk_order, nc, ncc)
    return pl.pallas_call(
        _mlstm_scan_kernel,
        out_shape=jax.ShapeDtypeStruct((b, ta, 2 * ML_INNER), F32),
        grid=(b, 2, nc),
        in_specs=[pl.BlockSpec((1, lc, ML_QK), lambda i, d, s: (i, chunk(d, s), 0)),
                  pl.BlockSpec((1, lc, ML_QK), lambda i, d, s: (i, chunk(d, s), 1)),
                  pl.BlockSpec((1, lc, ML_INNER), lambda i, d, s: (i, chunk(d, s), 1)),
                  pl.BlockSpec((1, ML_HEADS, 1, 1, lc), lambda i, d, s: (i, d, chunk(d, s), 0, 0)),
                  pl.BlockSpec((1, ML_HEADS, 1, lc, SUBLANES), lambda i, d, s: (i, d, chunk(d, s), 0, 0))],
        out_specs=pl.BlockSpec((1, lc, ML_INNER), lambda i, d, s: (i, chunk(d, s), d)),
        scratch_shapes=[pltpu.VMEM((ML_HEADS, ML_DK, ML_DV), F32), pltpu.VMEM((ML_HEADS, 1, ML_DK), F32)],
        compiler_params=_cparams("parallel", "parallel", "arbitrary"),
        name="mlstm_scan",
    )(p, p, p, a_rows, cols)


def _hgrn_lb_kernel(p_ref, o_ref, *, layer):
    for d in range(p_ref.shape[0]):
        x = p_ref[d]
        e = jnp.exp(x - jnp.max(x, axis=0, keepdims=True))
        p = e / jnp.sum(e, axis=0, keepdims=True)
        acc = jnp.zeros((1, x.shape[1]), F32)
        for j in range(1, layer + 1):
            acc = acc + p[j:j + 1, :]
        o_ref[d:d + 1, :] = acc


def _cumsum_rows(x, rev):
    n = x.shape[0]
    row = lax.broadcasted_iota(jnp.int32, x.shape, 0)
    s = 1
    while s < n:
        if rev:
            x = x + jnp.where(row < n - s, pltpu.roll(x, n - s, axis=0), 0.0)
        else:
            x = x + jnp.where(row >= s, pltpu.roll(x, s, axis=0), 0.0)
        s *= 2
    return x


def _anchor_rows(a, m, rev):
    n, f = a.shape
    idx = m if rev else m - 1
    if 2 * m >= SUBLANES:
        a3 = a.reshape(n // (2 * m), 2 * m, f)
        return jnp.broadcast_to(a3[:, idx:idx + 1, :], a3.shape).reshape(n, f)
    a3 = a.reshape(n // SUBLANES, SUBLANES, f)
    sub = lax.broadcasted_iota(jnp.int32, a3.shape, 1)
    out = None
    for gi in range(SUBLANES // (2 * m) - 1, -1, -1):
        cand = jnp.broadcast_to(a3[:, gi * 2 * m + idx:gi * 2 * m + idx + 1, :], a3.shape)
        out = cand if out is None else jnp.where(sub < (gi + 1) * 2 * m, cand, out)
    return out.reshape(n, f)


def _hgrn_scan_kernel(q_ref, f_ref, i_ref, fb_ref, lb_ref, o_ref, s_ref, *, rev):
    lc = q_ref.shape[1]

    @pl.when(pl.program_id(1) == 0)
    def _():
        s_ref[...] = jnp.zeros(s_ref.shape, F32)

    row = lax.broadcasted_iota(jnp.int32, (lc, 1), 0)
    ti = lax.broadcasted_iota(jnp.int32, (lc, lc), 0)
    si = lax.broadcasted_iota(jnp.int32, (lc, lc), 1)
    nt = (((1,), (1,)), ((), ()))
    tn = (((0,), (0,)), ((), ()))
    end = 0 if rev else lc - 1
    dr = 1 if rev else 0

    def head(h, carry):
        c0 = pl.multiple_of(h * HG_DIM, HG_DIM)
        q = _silu(q_ref[0, :, pl.ds(c0, HG_DIM)])
        lb = lb_ref[dr:dr + 1, pl.ds(c0, HG_DIM)]
        f = lb + (1.0 - lb) * jax.nn.sigmoid(f_ref[0, :, pl.ds(c0, HG_DIM)] + fb_ref[dr:dr + 1, pl.ds(c0, HG_DIM)])
        k = 1.0 - f
        iv = i_ref[0, :, pl.ds(c0, HG_DIM)]
        ib = iv.astype(BF16)
        a = _cumsum_rows(jnp.log(f), rev)
        attn = jnp.zeros((lc, lc), F32)
        m = 1
        while m < lc:
            e = jnp.exp(-jnp.abs(a - _anchor_rows(a, m, rev)))
            upper = (row % (2 * m)) >= m
            is_q = jnp.logical_not(upper) if rev else upper
            qt = jnp.where(is_q, q * e, 0.0).astype(BF16)
            kt = jnp.where(is_q, 0.0, k * e).astype(BF16)
            pair = lax.dot_general(qt, kt, nt, preferred_element_type=F32)
            attn = attn + jnp.where((ti // (2 * m)) == (si // (2 * m)), pair, 0.0)
            m *= 2
        s_old = s_ref[h]
        o = jnp.dot(attn.astype(BF16), ib, preferred_element_type=F32)
        o = o + jnp.sum(q * k, axis=-1, keepdims=True) * iv
        o = o + lax.dot_general((q * jnp.exp(a)).astype(BF16), s_old.astype(BF16), nt, preferred_element_type=F32)
        o_ref[0, :, pl.ds(c0, HG_DIM)] = o
        a_end = a[end:end + 1, :]
        kd = (k * jnp.exp(a_end - a)).astype(BF16)
        s_ref[h] = s_old * jnp.exp(a_end) + lax.dot_general(ib, kd, tn, preferred_element_type=F32)
        return carry

    lax.fori_loop(0, HG_HEADS, head, 0)


def _hgrn_scan(p, f_b, lb, tc, rev):
    b, ta, _ = p.shape
    lc = HG_CHUNK
    hg = HG_HEADS * HG_DIM
    assert tc % lc == 0 and ta % lc == 0
    nc, ncc = ta // lc, tc // lc
    d = 1 if rev else 0
    chunk = lambda s: _chunk_order(nc, ncc, d, s)
    return pl.pallas_call(
        functools.partial(_hgrn_scan_kernel, rev=rev),
        out_shape=jax.ShapeDtypeStruct((b, ta, hg), F32),
        grid=(b, nc),
        in_specs=[pl.BlockSpec((1, lc, hg), lambda i, s: (i, chunk(s), 0)),
                  pl.BlockSpec((1, lc, hg), lambda i, s: (i, chunk(s), 1 + d)),
                  pl.BlockSpec((1, lc, hg), lambda i, s: (i, chunk(s), 3)),
                  pl.BlockSpec((2, hg), lambda i, s: (0, 0)),
                  pl.BlockSpec((2, hg), lambda i, s: (0, 0))],
        out_specs=pl.BlockSpec((1, lc, hg), lambda i, s: (i, chunk(s), 0)),
        scratch_shapes=[pltpu.VMEM((HG_HEADS, HG_DIM, HG_DIM), F32)],
        compiler_params=_cparams("parallel", "arbitrary"),
        name="hgrn_scan_bwd" if rev else "hgrn_scan_fwd",
    )(p, p, p, f_b, lb)


def kernel(x, c, ctx, c_ctx, ada_w, ada_b, norm_g, ml_w_in, ml_gate_b, ml_head_g, ml_w_out, at_w_in, at_q_g, at_k_g, at_sink, at_w_out, sc_w_in, sc_conv_w, sc_conv_b, sc_w_out, hg_w_in, hg_f_b, hg_lb, hg_head_g, hg_w_out):
    tc = ctx.shape[1]
    mod = _ada_mod(c, c_ctx, ada_w, ada_b)
    xs = jnp.concatenate([ctx, x], axis=1)
    for layer in range(DEPTH):
        kind, j = layer % 4, layer // 4
        need_ctx = layer < DEPTH - 1
        mod_l = mod[layer]
        if kind == 0:
            n_main = 2 * ML_QK + 3 * ML_INNER
            w_gate = jnp.zeros((ml_w_in.shape[1], LANES), F32).at[:, :4 * ML_HEADS].set(
                ml_w_in[j][:, n_main:][:, ML_GATE_PERM])
            p, gates = _inproj(xs, mod_l, norm_g[layer], ml_w_in[j][:, :n_main], tc, w_small=w_gate)
            hs = _mlstm_scan(p, gates, ml_gate_b[j], tc)
            feats = [(hs, ML_INNER, 0), (hs, ML_INNER, 1), (p, ML_INNER, 2), (p, ML_INNER, 3)]
            xs = _outproj("mlstm", feats, ml_w_out[j], xs, mod_l, tc, head_g=ml_head_g[j])
        elif kind == 1:
            p = _inproj(xs, mod_l, norm_g[layer], at_w_in[j], tc)
            u = _attn_mixer(p, at_q_g[j], at_k_g[j], at_sink[j], tc, need_ctx)
            xs = _outproj("plain", [(u, AT_Q, 0)], at_w_out[j], xs, mod_l, tc)
        elif kind == 2:
            u = _conv_mixer(xs, mod_l, norm_g[layer], sc_w_in[j], sc_conv_w[j], sc_conv_b[j], tc)
            xs = _outproj("plain", [(u, u.shape[2], 0)], sc_w_out[j], xs, mod_l, tc)
        else:
            hg = HG_HEADS * HG_DIM
            lb = pl.pallas_call(
                functools.partial(_hgrn_lb_kernel, layer=layer),
                out_shape=jax.ShapeDtypeStruct((2, hg), F32),
                name="hgrn_lb",
            )(hg_lb[j])
            p = _inproj(xs, mod_l, norm_g[layer], hg_w_in[j], tc)
            o_f = _hgrn_scan(p, hg_f_b[j], lb, tc, rev=False)
            o_b = _hgrn_scan(p, hg_f_b[j], lb, tc, rev=True)
            feats = [(o_f, hg, 0), (o_b, hg, 0), (p, hg, 4)]
            xs = _outproj("hgrn", feats, hg_w_out[j], xs, mod_l, tc, head_g=hg_head_g[j])
    return xs[:, tc:, :]
```

```python
import functools

import numpy as np
import jax
import jax.numpy as jnp
from jax import lax
from jax.experimental import pallas as pl
from jax.experimental.pallas import tpu as pltpu

F32 = jnp.float32
BF16 = jnp.bfloat16
EPS = 1e-6
DEPTH = 4
GRID_W = 64
ROPE_BASE = 10000.0

ML_HEADS, ML_DK, ML_DV = 4, 256, 512
ML_QK = ML_HEADS * ML_DK
ML_INNER = ML_HEADS * ML_DV
ML_CHUNK = 256

AT_HEADS, AT_KV_HEADS, AT_HEAD_DIM = 16, 4, 64
AT_GROUP = AT_HEADS // AT_KV_HEADS
AT_BLOCK = 128
AT_Q = AT_HEADS * AT_HEAD_DIM
AT_KV = AT_KV_HEADS * AT_HEAD_DIM

SC_KSIZE = 3

HG_HEADS, HG_DIM = 8, 128
HG_CHUNK = 128

LANES = 128
SUBLANES = 8
VMEM_LIMIT_BYTES = 56 * 1024 * 1024

ROW_BLOCK = 256
OUT_ROWS = 256


def _cparams(*sem):
    return pltpu.CompilerParams(dimension_semantics=sem, vmem_limit_bytes=VMEM_LIMIT_BYTES)


def _silu(x):
    return x * jax.nn.sigmoid(x)


def _ada_kernel(c_ref, w_ref, b_ref, o_ref):
    s = _silu(c_ref[...])
    o_ref[0] = jnp.dot(s.astype(BF16), w_ref[0].astype(BF16), preferred_element_type=F32) + b_ref[0]


def _ada_mod(c, c_ctx, ada_w, ada_b):
    b, d = c.shape
    depth = ada_w.shape[0]
    rows = -(-(b + 1) // SUBLANES) * SUBLANES
    cc = jnp.zeros((rows, d), F32).at[:b].set(c).at[b].set(c_ctx)
    tn = 1024
    out = pl.pallas_call(
        _ada_kernel,
        out_shape=jax.ShapeDtypeStruct((depth, rows, 3 * d), F32),
        grid=(depth, 3 * d // tn),
        in_specs=[pl.BlockSpec((rows, d), lambda l, j: (0, 0)),
                  pl.BlockSpec((1, d, tn), lambda l, j: (l, 0, j)),
                  pl.BlockSpec((1, 1, tn), lambda l, j: (l, 0, j))],
        out_specs=pl.BlockSpec((1, rows, tn), lambda l, j: (l, 0, j)),
        compiler_params=_cparams("parallel", "parallel"),
        name="ada_mod",
    )(cc, ada_w, ada_b.reshape(depth, 1, 3 * d))
    return out.reshape(depth, rows, 3, d)


def _modulated(x_ref, ml_ref, mc_ref, g_ref, h_ref, tc):
    ta = x_ref.shape[1]
    g = g_ref[...]

    def rows(r0, m_ref):
        x = x_ref[0, pl.ds(r0, ROW_BLOCK), :]
        ms = jnp.mean(x * x, axis=-1, keepdims=True)
        xn = x * lax.rsqrt(ms + EPS) * g
        h = xn * (1.0 + m_ref[0, 1:2, :]) + m_ref[0, 0:1, :]
        h_ref[pl.ds(r0, ROW_BLOCK), :] = h.astype(BF16)

    for i in range(tc // ROW_BLOCK):
        rows(i * ROW_BLOCK, mc_ref)

    def body(i, carry):
        rows(pl.multiple_of(tc + i * ROW_BLOCK, ROW_BLOCK), ml_ref)
        return carry

    lax.fori_loop(0, (ta - tc) // ROW_BLOCK, body, 0)


def _epilogue(kind, acc, pa_ref, pb_ref):
    if kind == "silu":
        return _silu(acc)
    if kind == "sigmoid":
        return jax.nn.sigmoid(acc)
    if kind == "fgate":
        lb = pa_ref[...]
        return lb + (1.0 - lb) * jax.nn.sigmoid(acc + pb_ref[...])
    return acc


def _inproj_kernel(x_ref, ml_ref, mc_ref, g_ref, w_ref, *rest, tc, has_small, epilogue):
    rest = list(rest)
    ws_ref = rest.pop(0) if has_small else None
    pa_ref, pb_ref = (rest.pop(0), rest.pop(0)) if epilogue else (None, None)
    o_ref = rest.pop(0)
    os_ref = rest.pop(0) if has_small else None
    h_ref = rest.pop(0)

    @pl.when(pl.program_id(1) == 0)
    def _():
        _modulated(x_ref, ml_ref, mc_ref, g_ref, h_ref, tc)
        if has_small:
            os_ref[0] = jnp.dot(h_ref[...], ws_ref[...], preferred_element_type=F32)

    acc = jnp.dot(h_ref[...], w_ref[...], preferred_element_type=F32)
    if not epilogue:
        o_ref[0] = acc.astype(o_ref.dtype)
        return
    j = pl.program_id(1)
    for lo, hi, kind in epilogue:

        @pl.when((j >= lo) & (j < hi))
        def _(kind=kind):
            o_ref[0] = _epilogue(kind, acc, pa_ref, pb_ref).astype(o_ref.dtype)


def _inproj(xs, mod_l, norm_g, w, tc, tn=512, w_small=None, out_dtype=F32, epilogue=None, pa=None, pb=None):
    b, ta, d = xs.shape
    n = w.shape[1]
    assert n % tn == 0 and tc % ROW_BLOCK == 0 and (ta - tc) % ROW_BLOCK == 0
    mod_lat = mod_l[:b]
    mod_ctx = mod_l[b:b + 1]
    has_small = w_small is not None
    in_specs = [pl.BlockSpec((1, ta, d), lambda i, j: (i, 0, 0)),
                pl.BlockSpec((1, 3, d), lambda i, j: (i, 0, 0)),
                pl.BlockSpec((1, 3, d), lambda i, j: (0, 0, 0)),
                pl.BlockSpec((1, d), lambda i, j: (0, 0)),
                pl.BlockSpec((d, tn), lambda i, j: (0, j))]
    args = [xs, mod_lat, mod_ctx, norm_g.reshape(1, d), w.astype(BF16)]
    out_shape = [jax.ShapeDtypeStruct((b, ta, n), out_dtype)]
    out_specs = [pl.BlockSpec((1, ta, tn), lambda i, j: (i, 0, j))]
    if has_small:
        ns = w_small.shape[1]
        in_specs.append(pl.BlockSpec((d, ns), lambda i, j: (0, 0)))
        args.append(w_small.astype(BF16))
    if epilogue:
        assert all(lo % tn == 0 and hi % tn == 0 for lo, hi, _ in epilogue)
        epilogue = tuple((lo // tn, hi // tn, kind) for lo, hi, kind in epilogue)
        zeros = jnp.zeros((n,), F32)
        for par in (pa, pb):
            in_specs.append(pl.BlockSpec((1, tn), lambda i, j: (0, j)))
            args.append((zeros if par is None else par).reshape(1, n))
    if has_small:
        out_shape.append(jax.ShapeDtypeStruct((b, ta, ns), F32))
        out_specs.append(pl.BlockSpec((1, ta, ns), lambda i, j: (i, 0, 0)))
    res = pl.pallas_call(
        functools.partial(_inproj_kernel, tc=tc, has_small=has_small, epilogue=epilogue),
        out_shape=out_shape,
        grid=(b, n // tn),
        in_specs=in_specs,
        out_specs=out_specs,
        scratch_shapes=[pltpu.VMEM((ta, d), BF16)],
        compiler_params=_cparams("parallel", "arbitrary"),
        name="inproj",
    )(*args)
    return res if has_small else res[0]


def _head_rms(h, gain, n_heads):
    dh = h.shape[1] // n_heads
    parts = []
    for i in range(n_heads):
        hh = h[:, i * dh:(i + 1) * dh]
        ms = jnp.mean(hh * hh, axis=-1, keepdims=True)
        parts.append(hh * lax.rsqrt(ms + EPS))
    return jnp.concatenate(parts, axis=1) * gain


def _outproj_kernel(*refs, mode, tc, tm):
    if mode == "plain":
        u_ref, w_ref, x_ref, ml_ref, mc_ref, o_ref = refs
        u = u_ref[0]
    elif mode == "mlstm":
        h0_ref, h1_ref, og_ref, z_ref, hg_ref, w_ref, x_ref, ml_ref, mc_ref, o_ref = refs
        hn = _head_rms(h0_ref[0] + h1_ref[0], hg_ref[...], ML_HEADS)
        u = (og_ref[0] * hn * z_ref[0]).astype(BF16)
    else:
        h0_ref, h1_ref, z_ref, hg_ref, w_ref, x_ref, ml_ref, mc_ref, o_ref = refs
        hn = _head_rms(h0_ref[0] + h1_ref[0], hg_ref[...], HG_HEADS)
        u = (hn * z_ref[0]).astype(BF16)
    y = jnp.dot(u, w_ref[...], preferred_element_type=F32)
    row = pl.program_id(1) * tm + lax.broadcasted_iota(jnp.int32, (tm, 1), 0)
    gate = jnp.where(row < tc, mc_ref[0, 2:3, :], ml_ref[0, 2:3, :])
    o_ref[0] = x_ref[0] + gate * y


def _outproj(mode, feats, w_out, xs, mod_l, tc, head_g=None):
    b, ta, d = xs.shape
    kdim = w_out.shape[0]
    tm = OUT_ROWS
    assert ta % tm == 0 and tc % tm == 0
    row_spec = lambda width, col: pl.BlockSpec((1, tm, width), lambda i, r, col=col: (i, r, col))
    in_specs, args = [], []
    for arr, width, col in feats:
        in_specs.append(row_spec(width, col))
        args.append(arr)
    if head_g is not None:
        in_specs.append(pl.BlockSpec((1, kdim), lambda i, r: (0, 0)))
        args.append(head_g.reshape(1, kdim))
    in_specs += [pl.BlockSpec((kdim, d), lambda i, r: (0, 0)),
                 row_spec(d, 0),
                 pl.BlockSpec((1, 3, d), lambda i, r: (i, 0, 0)),
                 pl.BlockSpec((1, 3, d), lambda i, r: (0, 0, 0))]
    args += [w_out.astype(BF16), xs, mod_l[:b], mod_l[b:b + 1]]
    return pl.pallas_call(
        functools.partial(_outproj_kernel, mode=mode, tc=tc, tm=tm),
        out_shape=jax.ShapeDtypeStruct((b, ta, d), F32),
        grid=(b, ta // tm),
        in_specs=in_specs,
        out_specs=row_spec(d, 0),
        compiler_params=_cparams("parallel", "parallel"),
        name="outproj_" + mode,
    )(*args)


def _conv_kernel(x_ref, ml_ref, mc_ref, g_ref, w_ref, cw_ref, cb_ref, o_ref, h_ref, *, tc, tw):
    @pl.when(pl.program_id(1) == 0)
    def _():
        _modulated(x_ref, ml_ref, mc_ref, g_ref, h_ref, tc)

    ta = h_ref.shape[0]
    p = jnp.dot(h_ref[...], w_ref[0], preferred_element_type=F32)
    xin, bg, cg, z = (p[:, i * tw:(i + 1) * tw] for i in range(4))
    u = cg * xin
    row = lax.broadcasted_iota(jnp.int32, (ta, 1), 0)
    first = (row == 0) | (row == tc)
    last = (row == tc - 1) | (row == ta - 1)
    u_prev = jnp.where(first, 0.0, pltpu.roll(u, 1, axis=0))
    u_next = jnp.where(last, 0.0, pltpu.roll(u, ta - 1, axis=0))
    cw = cw_ref[0]
    y = u_prev * cw[0:1, :] + u * cw[1:2, :] + u_next * cw[2:3, :] + cb_ref[0]
    o_ref[0] = (bg * y * _silu(z)).astype(o_ref.dtype)


def _conv_mixer(xs, mod_l, norm_g, w_in, conv_w, conv_b, tc, tw=128):
    b, ta, d = xs.shape
    e = conv_w.shape[1]
    nt = e // tw
    w = w_in.reshape(d, 4, nt, tw).transpose(2, 0, 1, 3).reshape(nt, d, 4 * tw).astype(BF16)
    cw = conv_w.reshape(SC_KSIZE, nt, tw).transpose(1, 0, 2)
    cb = conv_b.reshape(nt, 1, tw)
    return pl.pallas_call(
        functools.partial(_conv_kernel, tc=tc, tw=tw),
        out_shape=jax.ShapeDtypeStruct((b, ta, e), BF16),
        grid=(b, nt),
        in_specs=[pl.BlockSpec((1, ta, d), lambda i, j: (i, 0, 0)),
                  pl.BlockSpec((1, 3, d), lambda i, j: (i, 0, 0)),
                  pl.BlockSpec((1, 3, d), lambda i, j: (0, 0, 0)),
                  pl.BlockSpec((1, d), lambda i, j: (0, 0)),
                  pl.BlockSpec((1, d, 4 * tw), lambda i, j: (j, 0, 0)),
                  pl.BlockSpec((1, SC_KSIZE, tw), lambda i, j: (j, 0, 0)),
                  pl.BlockSpec((1, 1, tw), lambda i, j: (j, 0, 0))],
        out_specs=pl.BlockSpec((1, ta, tw), lambda i, j: (i, 0, j)),
        scratch_shapes=[pltpu.VMEM((ta, d), BF16)],
        compiler_params=_cparams("parallel", "arbitrary"),
        name="conv_mixer",
    )(xs, mod_l[:b], mod_l[b:b + 1], norm_g.reshape(1, d), w, cw, cb)


def _rope_tables(tc, t):
    rows = t // GRID_W
    row = np.repeat(np.arange(rows), GRID_W).astype(np.float32)
    col = np.tile(np.arange(GRID_W), rows).astype(np.float32)
    n_freq = AT_HEAD_DIM // 4
    freqs = jnp.power(ROPE_BASE, -jnp.arange(n_freq, dtype=F32) / n_freq)
    ang = jnp.concatenate([jnp.asarray(row)[:, None] * freqs, jnp.asarray(col)[:, None] * freqs], axis=-1)
    cos, sin = jnp.cos(ang), jnp.sin(ang)
    cos_h = jnp.concatenate([cos, cos], axis=-1)
    sin_h = jnp.concatenate([-sin, sin], axis=-1)
    cos_t = jnp.concatenate([jnp.ones((tc, LANES), F32), jnp.tile(cos_h, (1, 2))], axis=0)
    sin_t = jnp.concatenate([jnp.zeros((tc, LANES), F32), jnp.tile(sin_h, (1, 2))], axis=0)
    return cos_t, sin_t


def _norm_rope_pair(x, gain, cos, sin):
    lane = lax.broadcasted_iota(jnp.int32, x.shape, 1)
    lo = lane < AT_HEAD_DIM
    x2 = x * x
    s_lo = jnp.sum(jnp.where(lo, x2, 0.0), axis=-1, keepdims=True)
    s_hi = jnp.sum(jnp.where(lo, 0.0, x2), axis=-1, keepdims=True)
    ms = jnp.where(lo, s_lo, s_hi) * (1.0 / AT_HEAD_DIM)
    xn = x * lax.rsqrt(ms + EPS) * gain
    first = (lane % AT_HEAD_DIM) < (AT_HEAD_DIM // 2)
    swapped = jnp.where(first, pltpu.roll(xn, LANES - AT_HEAD_DIM // 2, axis=1),
                        pltpu.roll(xn, AT_HEAD_DIM // 2, axis=1))
    return xn * cos + swapped * sin


def _attn_kernel(sink_ref, q_ref, k_ref, v_ref, z_ref, qg_ref, kg_ref, cos_ref, sin_ref, o_ref,
                 qs_ref, ka_ref, kb_ref, va_ref, vb_ref, *, tc, need_ctx):
    ta = q_ref.shape[1]
    t = ta - tc
    nb = t // AT_BLOCK
    blk = AT_BLOCK
    g = pl.program_id(1)
    odd = (g % 2) == 1
    scale = AT_HEAD_DIM ** -0.5

    def prep(i, carry):
        r0 = pl.multiple_of(i * ROW_BLOCK, ROW_BLOCK)
        cos = cos_ref[pl.ds(r0, ROW_BLOCK), :]
        sin = sin_ref[pl.ds(r0, ROW_BLOCK), :]
        for c in range(2):
            qc = _norm_rope_pair(q_ref[0, pl.ds(r0, ROW_BLOCK), c * LANES:(c + 1) * LANES], qg_ref[...], cos, sin)
            qs_ref[pl.ds(r0, ROW_BLOCK), c * LANES:(c + 1) * LANES] = (qc * scale).astype(BF16)
        kn = _norm_rope_pair(k_ref[0, pl.ds(r0, ROW_BLOCK), :], kg_ref[...], cos, sin)
        vv = v_ref[0, pl.ds(r0, ROW_BLOCK), :]
        lane = lax.broadcasted_iota(jnp.int32, kn.shape, 1)
        own = (lane // AT_HEAD_DIM) == (g % 2)
        k_own = jnp.where(own, kn, 0.0)
        v_own = jnp.where(own, vv, 0.0)
        k_oth = pltpu.roll(k_own, AT_HEAD_DIM, axis=1)
        v_oth = pltpu.roll(v_own, AT_HEAD_DIM, axis=1)
        ka_ref[pl.ds(r0, ROW_BLOCK), :] = jnp.where(odd, k_oth, k_own).astype(BF16)
        kb_ref[pl.ds(r0, ROW_BLOCK), :] = jnp.where(odd, k_own, k_oth).astype(BF16)
        va_ref[pl.ds(r0, ROW_BLOCK), :] = jnp.where(odd, v_oth, v_own).astype(BF16)
        vb_ref[pl.ds(r0, ROW_BLOCK), :] = jnp.where(odd, v_own, v_oth).astype(BF16)
        return carry

    lax.fori_loop(0, ta // ROW_BLOCK, prep, 0)
    zeros = jnp.zeros((blk, LANES), BF16)
    for ref in (ka_ref, kb_ref, va_ref, vb_ref):
        ref[ta:ta + blk, :] = zeros

    half = lax.broadcasted_iota(jnp.int32, (2 * blk, 1), 0) < blk
    sink_a = jnp.where(half, sink_ref[g, 0], sink_ref[g, 2])
    sink_b = jnp.where(half, sink_ref[g, 1], sink_ref[g, 3])

    qi = lax.broadcasted_iota(jnp.int32, (2 * blk, 3 * blk), 0) % blk
    kj = lax.broadcasted_iota(jnp.int32, (2 * blk, 3 * blk), 1)
    band = (kj - qi >= 0) & (kj - qi <= 2 * blk)

    nt = (((1,), (1,)), ((), ()))

    def one_side(qt, k_ref_, v_ref_, sink, k0, mask):
        kc = k_ref_[0:tc, :]
        s_ctx = lax.dot_general(qt, kc, nt, preferred_element_type=F32)
        m = jnp.maximum(sink, jnp.max(s_ctx, axis=-1, keepdims=True))
        if mask is not None:
            kl = k_ref_[pl.ds(k0, 3 * blk), :]
            s_loc = lax.dot_general(qt, kl, nt, preferred_element_type=F32)
            s_loc = jnp.where(mask, s_loc, -jnp.inf)
            m = jnp.maximum(m, jnp.max(s_loc, axis=-1, keepdims=True))
        e_ctx = jnp.exp(s_ctx - m)
        den = jnp.exp(sink - m) + jnp.sum(e_ctx, axis=-1, keepdims=True)
        acc = jnp.dot(e_ctx.astype(BF16), v_ref_[0:tc, :], preferred_element_type=F32)
        if mask is not None:
            e_loc = jnp.exp(s_loc - m)
            den = den + jnp.sum(e_loc, axis=-1, keepdims=True)
            acc = acc + jnp.dot(e_loc.astype(BF16), v_ref_[pl.ds(k0, 3 * blk), :], preferred_element_type=F32)
        return acc / den

    def block(r0, k0, mask):
        qt = jnp.concatenate([qs_ref[pl.ds(r0, blk), 0:LANES], qs_ref[pl.ds(r0, blk), LANES:2 * LANES]], axis=0)
        o = one_side(qt, ka_ref, va_ref, sink_a, k0, mask) + one_side(qt, kb_ref, vb_ref, sink_b, k0, mask)
        for c in range(2):
            zc = z_ref[0, pl.ds(r0, blk), c * LANES:(c + 1) * LANES]
            o_ref[0, pl.ds(r0, blk), c * LANES:(c + 1) * LANES] = (o[c * blk:(c + 1) * blk] * zc).astype(o_ref.dtype)

    def lat_block(n, carry):
        r0 = pl.multiple_of(tc + n * blk, blk)
        k0 = pl.multiple_of(tc + (n - 1) * blk, blk)
        kpos = (n - 1) * blk + kj
        block(r0, k0, band & (kpos >= 0) & (kpos < t))
        return carry

    lax.fori_loop(0, nb, lat_block, 0)
    if need_ctx:
        for n in range(tc // blk):
            block(n * blk, None, None)
    else:
        o_ref[0, 0:tc, :] = jnp.zeros((tc, o_ref.shape[2]), o_ref.dtype)


def _attn_mixer(p, at_q_g, at_k_g, at_sink, tc, need_ctx):
    b, ta, _ = p.shape
    t = ta - tc
    assert tc >= AT_BLOCK and tc % AT_BLOCK == 0 and t % AT_BLOCK == 0 and ta % ROW_BLOCK == 0
    cos_t, sin_t = _rope_tables(tc, t)
    gw = AT_GROUP * AT_HEAD_DIM
    kcol = AT_Q // LANES
    vcol = (AT_Q + AT_KV) // LANES
    zcol = (AT_Q + 2 * AT_KV) // gw
    return pl.pallas_call(
        functools.partial(_attn_kernel, tc=tc, need_ctx=need_ctx),
        out_shape=jax.ShapeDtypeStruct((b, ta, AT_Q), BF16),
        grid=(b, AT_KV_HEADS),
        in_specs=[pl.BlockSpec(memory_space=pltpu.SMEM),
                  pl.BlockSpec((1, ta, gw), lambda i, g: (i, 0, g)),
                  pl.BlockSpec((1, ta, LANES), lambda i, g: (i, 0, kcol + g // 2)),
                  pl.BlockSpec((1, ta, LANES), lambda i, g: (i, 0, vcol + g // 2)),
                  pl.BlockSpec((1, ta, gw), lambda i, g: (i, 0, zcol + g)),
                  pl.BlockSpec((1, LANES), lambda i, g: (0, 0)),
                  pl.BlockSpec((1, LANES), lambda i, g: (0, 0)),
                  pl.BlockSpec((ta, LANES), lambda i, g: (0, 0)),
                  pl.BlockSpec((ta, LANES), lambda i, g: (0, 0))],
        out_specs=pl.BlockSpec((1, ta, gw), lambda i, g: (i, 0, g)),
        scratch_shapes=[pltpu.VMEM((ta, gw), BF16)] + [pltpu.VMEM((ta + AT_BLOCK, LANES), BF16)] * 4,
        compiler_params=_cparams("parallel", "arbitrary"),
        name="attn_mixer",
    )(at_sink.reshape(AT_KV_HEADS, AT_GROUP), p, p, p, p,
      jnp.tile(at_q_g, 2).reshape(1, LANES), jnp.tile(at_k_g, 2).reshape(1, LANES), cos_t, sin_t)


def _chunk_order(nc, ncc, d, s):
    bwd = jnp.where(s < ncc, ncc - 1 - s, nc - 1 - (s - ncc))
    return jnp.where(d == 0, s, bwd)


ML_GATE_PERM = np.concatenate([np.arange(0, 4), np.arange(8, 12), np.arange(4, 8), np.arange(12, 16)])
ML_NQ = 6


def _mlstm_gate_kernel(g_ref, b_ref, o_ref, *, tc, lc):
    h = ML_HEADS
    ta = g_ref.shape[2]
    nc, ncc = ta // lc, tc // lc
    x = g_ref[0] + b_ref[...]
    li = x[0:2 * h]
    lfp = x[2 * h:4 * h]
    lf = jnp.minimum(lfp, 0.0) - jnp.log1p(jnp.exp(-jnp.abs(lfp)))
    fwd = lax.broadcasted_iota(jnp.int32, (2 * h, ta), 0) < h
    fwd1 = lax.broadcasted_iota(jnp.int32, (2 * h, 1), 0) < h
    pos = lax.broadcasted_iota(jnp.int32, (2 * h, ta), 1) % lc

    def seg_scan(y, op, fill):
        yf, yb = y, y
        s = 1
        while s < lc:
            yf = op(yf, jnp.where(pos >= s, pltpu.roll(yf, s, axis=1), fill))
            yb = op(yb, jnp.where(pos < lc - s, pltpu.roll(yb, ta - s, axis=1), fill))
            s *= 2
        return jnp.where(fwd, yf, yb)

    bsum = seg_scan(lf, jnp.add, 0.0)
    a = li - bsum
    cmax = seg_scan(a, jnp.maximum, -jnp.inf)

    def end_col(y, c):
        return jnp.where(fwd1, y[:, (c + 1) * lc - 1:(c + 1) * lc], y[:, c * lc:c * lc + 1])

    tot = [end_col(bsum, c) for c in range(nc)]
    amax = [end_col(cmax, c) for c in range(nc)]

    def chain(order):
        m = jnp.zeros((2 * h, 1), F32)
        m_in = [None] * nc
        for c in order:
            m_in[c] = m
            m = tot[c] + jnp.maximum(m, amax[c])
        return m_in

    m_f = chain(list(range(nc)))
    m_b = chain(list(range(ncc - 1, -1, -1)) + list(range(nc - 1, ncc - 1, -1)))
    for c in range(nc):
        m_in = jnp.where(fwd1, m_f[c], m_b[c])
        sl = slice(c * lc, (c + 1) * lc)
        g_run = jnp.maximum(m_in, cmax[:, sl])
        g_end = jnp.maximum(m_in, amax[c])
        o_ref[0, c, 0] = a[:, sl]
        o_ref[0, c, 1] = g_run
        o_ref[0, c, 2] = jnp.exp(m_in - g_run)
        o_ref[0, c, 3] = jnp.exp(-(bsum[:, sl] + g_run))
        o_ref[0, c, 4] = jnp.exp(a[:, sl] - g_end)
        o_ref[0, c, 5] = jnp.broadcast_to(jnp.exp(m_in - g_end), (2 * h, lc))


def _mlstm_scan_kernel(q_ref, k_ref, v_ref, a_ref, c_ref, o_ref, cs_ref, ns_ref):
    lc = q_ref.shape[1]
    d = pl.program_id(1)

    @pl.when(pl.program_id(2) == 0)
    def _():
        cs_ref[...] = jnp.zeros(cs_ref.shape, F32)
        ns_ref[...] = jnp.zeros(ns_ref.shape, F32)

    diff = lax.broadcasted_iota(jnp.int32, (lc, lc), 1) - lax.broadcasted_iota(jnp.int32, (lc, lc), 0)
    mask = diff * (1 - 2 * d) <= 0
    nt = (((1,), (1,)), ((), ()))
    tn = (((0,), (0,)), ((), ()))
    for h in range(ML_HEADS):
        q = q_ref[0, :, h * ML_DK:(h + 1) * ML_DK] * (ML_DK ** -0.5)
        k = k_ref[0, :, h * ML_DK:(h + 1) * ML_DK]
        vb = v_ref[0, :, h * ML_DV:(h + 1) * ML_DV].astype(BF16)
        a_row = a_ref[0, h, 0]
        cols = c_ref[0, h, 0]
        g_run, inter, eclamp, w, decay = (cols[:, i:i + 1] for i in range(5))
        qb = q.astype(BF16)
        qk = lax.dot_general(qb, k.astype(BF16), nt, preferred_element_type=F32)
        s = qk * jnp.where(mask, jnp.exp(a_row - g_run), 0.0)
        c_old = cs_ref[h]
        n_old = ns_ref[h]
        num = inter * jnp.dot(qb, c_old.astype(BF16), preferred_element_type=F32) \
            + jnp.dot(s.astype(BF16), vb, preferred_element_type=F32)
        den = inter * jnp.sum(q * n_old, axis=-1, keepdims=True) + jnp.sum(s, axis=-1, keepdims=True)
        o_ref[0, :, h * ML_DV:(h + 1) * ML_DV] = num / jnp.maximum(jnp.abs(den), eclamp)
        kw = k * w
        dec = decay[0:1, :]
        cs_ref[h] = dec * c_old + lax.dot_general(kw.astype(BF16), vb, tn, preferred_element_type=F32)
        ns_ref[h] = dec * n_old + jnp.sum(kw, axis=0, keepdims=True)


def _mlstm_scan(p, gates, gate_b, tc):
    b, ta, _ = p.shape
    lc = ML_CHUNK
    assert tc % lc == 0 and ta % lc == 0
    nc, ncc = ta // lc, tc // lc
    ng = 4 * ML_HEADS
    g_t = gates[:, :, :ng].transpose(0, 2, 1)
    bias = gate_b.reshape(ng)[ML_GATE_PERM].reshape(ng, 1)
    gp = pl.pallas_call(
        functools.partial(_mlstm_gate_kernel, tc=tc, lc=lc),
        out_shape=jax.ShapeDtypeStruct((b, nc, ML_NQ, 2 * ML_HEADS, lc), F32),
        grid=(b,),
        in_specs=[pl.BlockSpec((1, ng, ta), lambda i: (i, 0, 0)),
                  pl.BlockSpec((ng, 1), lambda i: (0, 0))],
        out_specs=pl.BlockSpec((1, nc, ML_NQ, 2 * ML_HEADS, lc), lambda i: (i, 0, 0, 0, 0)),
        compiler_params=_cparams("parallel"),
        name="mlstm_gates",
    )(g_t, bias)
    a_rows = gp[:, :, 0].transpose(0, 2, 1, 3).reshape(b, 2 * ML_HEADS, nc, 1, lc)
    cols = jnp.pad(gp[:, :, 1:].transpose(0, 3, 1, 4, 2), ((0, 0),) * 4 + ((0, SUBLANES - (ML_NQ - 1)),))
    chunk = functools.partial(_chunk_order, nc, ncc)
    return pl.pallas_call(
        _mlstm_scan_kernel,
        out_shape=jax.ShapeDtypeStruct((b, ta, 2 * ML_INNER), F32),
        grid=(b, 2, nc),
        in_specs=[pl.BlockSpec((1, lc, ML_QK), lambda i, d, s: (i, chunk(d, s), 0)),
                  pl.BlockSpec((1, lc, ML_QK), lambda i, d, s: (i, chunk(d, s), 1)),
                  pl.BlockSpec((1, lc, ML_INNER), lambda i, d, s: (i, chunk(d, s), 1)),
                  pl.BlockSpec((1, ML_HEADS, 1, 1, lc), lambda i, d, s: (i, d, chunk(d, s), 0, 0)),
                  pl.BlockSpec((1, ML_HEADS, 1, lc, SUBLANES), lambda i, d, s: (i, d, chunk(d, s), 0, 0))],
        out_specs=pl.BlockSpec((1, lc, ML_INNER), lambda i, d, s: (i, chunk(d, s), d)),
        scratch_shapes=[pltpu.VMEM((ML_HEADS, ML_DK, ML_DV), F32), pltpu.VMEM((ML_HEADS, 1, ML_DK), F32)],
        compiler_params=_cparams("parallel", "parallel", "arbitrary"),
        name="mlstm_scan",
    )(p, p, p, a_rows, cols)


def _hgrn_lb_kernel(p_ref, o_ref, *, layer):
    for d in range(p_ref.shape[0]):
        x = p_ref[d]
        e = jnp.exp(x - jnp.max(x, axis=0, keepdims=True))
        p = e / jnp.sum(e, axis=0, keepdims=True)
        acc = jnp.zeros((1, x.shape[1]), F32)
        for j in range(1, layer + 1):
            acc = acc + p[j:j + 1, :]
        o_ref[d:d + 1, :] = acc


def _cumsum_rows(x, rev):
    n = x.shape[0]
    row = lax.broadcasted_iota(jnp.int32, x.shape, 0)
    s = 1
    while s < n:
        if rev:
            x = x + jnp.where(row < n - s, pltpu.roll(x, n - s, axis=0), 0.0)
        else:
            x = x + jnp.where(row >= s, pltpu.roll(x, s, axis=0), 0.0)
        s *= 2
    return x


def _anchor_rows(a, m, rev):
    n, f = a.shape
    idx = m if rev else m - 1
    if 2 * m >= SUBLANES:
        a3 = a.reshape(n // (2 * m), 2 * m, f)
        return jnp.broadcast_to(a3[:, idx:idx + 1, :], a3.shape).reshape(n, f)
    a3 = a.reshape(n // SUBLANES, SUBLANES, f)
    sub = lax.broadcasted_iota(jnp.int32, a3.shape, 1)
    out = None
    for gi in range(SUBLANES // (2 * m) - 1, -1, -1):
        cand = jnp.broadcast_to(a3[:, gi * 2 * m + idx:gi * 2 * m + idx + 1, :], a3.shape)
        out = cand if out is None else jnp.where(sub < (gi + 1) * 2 * m, cand, out)
    return out.reshape(n, f)


HG_MAX_LOG_SPAN = 64.0


def _hgrn_scan_kernel(q_ref, f_ref, i_ref, o_ref, s_ref, *, rev):
    lc = q_ref.shape[1]

    @pl.when(pl.program_id(1) == 0)
    def _():
        s_ref[...] = jnp.zeros(s_ref.shape, F32)

    row = lax.broadcasted_iota(jnp.int32, (lc, 1), 0)
    ti = lax.broadcasted_iota(jnp.int32, (lc, lc), 0)
    si = lax.broadcasted_iota(jnp.int32, (lc, lc), 1)
    nt = (((1,), (1,)), ((), ()))
    tn = (((0,), (0,)), ((), ()))
    end = 0 if rev else lc - 1

    def finish(h, c0, attn, q_dec, kd, a_end, iv, ib, extra):
        s_old = s_ref[h]
        o = jnp.dot(attn.astype(BF16), ib, preferred_element_type=F32)
        o = o + lax.dot_general(q_dec, s_old.astype(BF16), nt, preferred_element_type=F32)
        if extra is not None:
            o = o + extra
        o_ref[0, :, pl.ds(c0, HG_DIM)] = o
        s_ref[h] = s_old * jnp.exp(a_end) + lax.dot_general(ib, kd, tn, preferred_element_type=F32)

    def head_single_anchor(h, carry):
        c0 = h * HG_DIM
        q = q_ref[0, :, pl.ds(c0, HG_DIM)]
        f = f_ref[0, :, pl.ds(c0, HG_DIM)]
        iv = i_ref[0, :, pl.ds(c0, HG_DIM)]
        a = _cumsum_rows(jnp.log(f), rev)
        q_dec = (q * jnp.exp(a)).astype(BF16)
        k_inv = (1.0 - f) * jnp.exp(-a)
        pair = lax.dot_general(q_dec, k_inv.astype(BF16), nt, preferred_element_type=F32)
        attn = jnp.where((si >= ti) if rev else (si <= ti), pair, 0.0)
        a_end = a[end:end + 1, :]
        kd = (k_inv * jnp.exp(a_end)).astype(BF16)
        finish(h, c0, attn, q_dec, kd, a_end, iv, iv.astype(BF16), None)
        return carry

    def head_per_level(h, carry):
        c0 = pl.multiple_of(h * HG_DIM, HG_DIM)
        q = q_ref[0, :, pl.ds(c0, HG_DIM)]
        f = f_ref[0, :, pl.ds(c0, HG_DIM)]
        k = 1.0 - f
        iv = i_ref[0, :, pl.ds(c0, HG_DIM)]
        a = _cumsum_rows(jnp.log(f), rev)
        attn = jnp.zeros((lc, lc), F32)
        m = 1
        while m < lc:
            e = jnp.exp(-jnp.abs(a - _anchor_rows(a, m, rev)))
            upper = (row % (2 * m)) >= m
            is_q = jnp.logical_not(upper) if rev else upper
            qt = jnp.where(is_q, q * e, 0.0).astype(BF16)
            kt = jnp.where(is_q, 0.0, k * e).astype(BF16)
            pair = lax.dot_general(qt, kt, nt, preferred_element_type=F32)
            attn = attn + jnp.where((ti // (2 * m)) == (si // (2 * m)), pair, 0.0)
            m *= 2
        a_end = a[end:end + 1, :]
        kd = (k * jnp.exp(a_end - a)).astype(BF16)
        diag = jnp.sum(q * k, axis=-1, keepdims=True) * iv
        finish(h, c0, attn, (q * jnp.exp(a)).astype(BF16), kd, a_end, iv, iv.astype(BF16), diag)
        return carry

    f_min = f_ref[0, :, 0:HG_DIM]
    for h in range(1, HG_HEADS):
        f_min = jnp.minimum(f_min, f_ref[0, :, h * HG_DIM:(h + 1) * HG_DIM])
    f_min = jnp.min(f_min.reshape(lc // SUBLANES, SUBLANES, HG_DIM), axis=0)
    in_range = jnp.min(f_min) >= float(np.exp(-HG_MAX_LOG_SPAN / lc))

    @pl.when(in_range)
    def _():
        for h in range(HG_HEADS):
            head_single_anchor(h, 0)

    @pl.when(jnp.logical_not(in_range))
    def _():
        lax.fori_loop(0, HG_HEADS, head_per_level, 0)


def _hgrn_scan(p, tc, rev):
    b, ta, _ = p.shape
    lc = HG_CHUNK
    hg = HG_HEADS * HG_DIM
    assert tc % lc == 0 and ta % lc == 0
    nc, ncc = ta // lc, tc // lc
    d = 1 if rev else 0
    chunk = lambda s: _chunk_order(nc, ncc, d, s)
    return pl.pallas_call(
        functools.partial(_hgrn_scan_kernel, rev=rev),
        out_shape=jax.ShapeDtypeStruct((b, ta, hg), F32),
        grid=(b, nc),
        in_specs=[pl.BlockSpec((1, lc, hg), lambda i, s: (i, chunk(s), 0)),
                  pl.BlockSpec((1, lc, hg), lambda i, s: (i, chunk(s), 1 + d)),
                  pl.BlockSpec((1, lc, hg), lambda i, s: (i, chunk(s), 3))],
        out_specs=pl.BlockSpec((1, lc, hg), lambda i, s: (i, chunk(s), 0)),
        scratch_shapes=[pltpu.VMEM((HG_HEADS, HG_DIM, HG_DIM), F32)],
        compiler_params=_cparams("parallel", "arbitrary"),
        name="hgrn_scan_bwd" if rev else "hgrn_scan_fwd",
    )(p, p, p)


def kernel(x, c, ctx, c_ctx, ada_w, ada_b, norm_g, ml_w_in, ml_gate_b, ml_head_g, ml_w_out, at_w_in, at_q_g, at_k_g, at_sink, at_w_out, sc_w_in, sc_conv_w, sc_conv_b, sc_w_out, hg_w_in, hg_f_b, hg_lb, hg_head_g, hg_w_out):
    tc = ctx.shape[1]
    mod = _ada_mod(c, c_ctx, ada_w, ada_b)
    xs = jnp.concatenate([ctx, x], axis=1)
    for layer in range(DEPTH):
        kind, j = layer % 4, layer // 4
        need_ctx = layer < DEPTH - 1
        mod_l = mod[layer]
        if kind == 0:
            n_main = 2 * ML_QK + 3 * ML_INNER
            w_gate = jnp.zeros((ml_w_in.shape[1], LANES), F32).at[:, :4 * ML_HEADS].set(
                ml_w_in[j][:, n_main:][:, ML_GATE_PERM])
            o0 = 2 * ML_QK + ML_INNER
            ep = ((0, o0, "id"), (o0, o0 + ML_INNER, "sigmoid"), (o0 + ML_INNER, n_main, "silu"))
            p, gates = _inproj(xs, mod_l, norm_g[layer], ml_w_in[j][:, :n_main], tc, w_small=w_gate, epilogue=ep)
            hs = _mlstm_scan(p, gates, ml_gate_b[j], tc)
            feats = [(hs, ML_INNER, 0), (hs, ML_INNER, 1), (p, ML_INNER, 2), (p, ML_INNER, 3)]
            xs = _outproj("mlstm", feats, ml_w_out[j], xs, mod_l, tc, head_g=ml_head_g[j])
        elif kind == 1:
            z0 = AT_Q + 2 * AT_KV
            ep = ((0, z0, "id"), (z0, z0 + AT_Q, "silu"))
            p = _inproj(xs, mod_l, norm_g[layer], at_w_in[j], tc, epilogue=ep)
            u = _attn_mixer(p, at_q_g[j], at_k_g[j], at_sink[j], tc, need_ctx)
            xs = _outproj("plain", [(u, AT_Q, 0)], at_w_out[j], xs, mod_l, tc)
        elif kind == 2:
            u = _conv_mixer(xs, mod_l, norm_g[layer], sc_w_in[j], sc_conv_w[j], sc_conv_b[j], tc)
            xs = _outproj("plain", [(u, u.shape[2], 0)], sc_w_out[j], xs, mod_l, tc)
        else:
            hg = HG_HEADS * HG_DIM
            lb = pl.pallas_call(
                functools.partial(_hgrn_lb_kernel, layer=layer),
                out_shape=jax.ShapeDtypeStruct((2, hg), F32),
                name="hgrn_lb",
            )(hg_lb[j])
            ep = ((0, hg, "silu"), (hg, 3 * hg, "fgate"), (3 * hg, 4 * hg, "id"), (4 * hg, 5 * hg, "silu"))
            zeros = jnp.zeros((hg,), F32)
            pa = jnp.concatenate([zeros, lb.reshape(2 * hg), zeros, zeros])
            pb = jnp.concatenate([zeros, hg_f_b[j].reshape(2 * hg), zeros, zeros])
            p = _inproj(xs, mod_l, norm_g[layer], hg_w_in[j], tc, epilogue=ep, pa=pa, pb=pb)
            o_f = _hgrn_scan(p, tc, rev=False)
            o_b = _hgrn_scan(p, tc, rev=True)
            feats = [(o_f, hg, 0), (o_b, hg, 0), (p, hg, 4)]
            xs = _outproj("hgrn", feats, hg_w_out[j], xs, mod_l, tc, head_g=hg_head_g[j])
    return xs[:, tc:, :]
```

```python
import functools

import numpy as np
import jax
import jax.numpy as jnp
from jax import lax
from jax.experimental import pallas as pl
from jax.experimental.pallas import tpu as pltpu

F32 = jnp.float32
BF16 = jnp.bfloat16
EPS = 1e-6
DEPTH = 4
GRID_W = 64
ROPE_BASE = 10000.0

ML_HEADS, ML_DK, ML_DV = 4, 256, 512
ML_QK = ML_HEADS * ML_DK
ML_INNER = ML_HEADS * ML_DV
ML_CHUNK = 256

AT_HEADS, AT_KV_HEADS, AT_HEAD_DIM = 16, 4, 64
AT_GROUP = AT_HEADS // AT_KV_HEADS
AT_BLOCK = 128
AT_Q = AT_HEADS * AT_HEAD_DIM
AT_KV = AT_KV_HEADS * AT_HEAD_DIM

SC_KSIZE = 3

HG_HEADS, HG_DIM = 8, 128
HG_CHUNK = 128

LANES = 128
SUBLANES = 8
VMEM_LIMIT_BYTES = 56 * 1024 * 1024

ROW_BLOCK = 256
OUT_ROWS = 256


def _cparams(*sem):
    return pltpu.CompilerParams(dimension_semantics=sem, vmem_limit_bytes=VMEM_LIMIT_BYTES)


def _silu(x):
    return x * jax.nn.sigmoid(x)


def _ada_kernel(c_ref, w_ref, b_ref, o_ref):
    s = _silu(c_ref[...])
    o_ref[0] = jnp.dot(s.astype(BF16), w_ref[0].astype(BF16), preferred_element_type=F32) + b_ref[0]


def _ada_mod(c, c_ctx, ada_w, ada_b):
    b, d = c.shape
    depth = ada_w.shape[0]
    rows = -(-(b + 1) // SUBLANES) * SUBLANES
    cc = jnp.zeros((rows, d), F32).at[:b].set(c).at[b].set(c_ctx)
    tn = 1024
    out = pl.pallas_call(
        _ada_kernel,
        out_shape=jax.ShapeDtypeStruct((depth, rows, 3 * d), F32),
        grid=(depth, 3 * d // tn),
        in_specs=[pl.BlockSpec((rows, d), lambda l, j: (0, 0)),
                  pl.BlockSpec((1, d, tn), lambda l, j: (l, 0, j)),
                  pl.BlockSpec((1, 1, tn), lambda l, j: (l, 0, j))],
        out_specs=pl.BlockSpec((1, rows, tn), lambda l, j: (l, 0, j)),
        compiler_params=_cparams("parallel", "parallel"),
        name="ada_mod",
    )(cc, ada_w, ada_b.reshape(depth, 1, 3 * d))
    return out.reshape(depth, rows, 3, d)


def _modulated(x_refs, ml_ref, mc_ref, g_ref, h_ref, tc):
    ta = h_ref.shape[0]
    g = g_ref[...]
    xc_ref = x_refs[0]
    xl_ref, lat0 = (x_refs[1], 0) if len(x_refs) == 2 else (x_refs[0], tc)

    def rows(x_ref, src0, dst0, m_ref):
        x = x_ref[0, pl.ds(src0, ROW_BLOCK), :]
        ms = jnp.mean(x * x, axis=-1, keepdims=True)
        xn = x * lax.rsqrt(ms + EPS) * g
        h = xn * (1.0 + m_ref[0, 1:2, :]) + m_ref[0, 0:1, :]
        h_ref[pl.ds(dst0, ROW_BLOCK), :] = h.astype(BF16)

    for i in range(tc // ROW_BLOCK):
        rows(xc_ref, i * ROW_BLOCK, i * ROW_BLOCK, mc_ref)

    def body(i, carry):
        r0 = pl.multiple_of(i * ROW_BLOCK, ROW_BLOCK)
        rows(xl_ref, lat0 + r0, tc + r0, ml_ref)
        return carry

    lax.fori_loop(0, (ta - tc) // ROW_BLOCK, body, 0)


def _split_stream(xs):
    arrs = list(xs) if isinstance(xs, tuple) else [xs]
    specs = [pl.BlockSpec((1,) + a.shape[1:], lambda i, j: (i, 0, 0)) for a in arrs]
    ta = sum(a.shape[1] for a in arrs)
    return arrs, specs, ta


def _inproj_kernel(*refs, tc, n_x, has_small, epilogue):
    x_refs, (ml_ref, mc_ref, g_ref, w_ref), rest = refs[:n_x], refs[n_x:n_x + 4], list(refs[n_x + 4:])
    ws_ref = rest.pop(0) if has_small else None
    pa_ref, pb_ref = (rest.pop(0), rest.pop(0)) if epilogue else (None, None)
    o_ref = rest.pop(0)
    os_ref = rest.pop(0) if has_small else None
    h_ref = rest.pop(0)

    @pl.when(pl.program_id(1) == 0)
    def _():
        _modulated(x_refs, ml_ref, mc_ref, g_ref, h_ref, tc)
        if has_small:
            os_ref[0] = jnp.dot(h_ref[...], ws_ref[...], preferred_element_type=F32)

    acc = jnp.dot(h_ref[...], w_ref[...], preferred_element_type=F32)
    if not epilogue:
        o_ref[0] = acc.astype(o_ref.dtype)
        return
    j = pl.program_id(1)
    s = jax.nn.sigmoid(acc + pb_ref[...])
    out = acc
    for lo, hi, kind in epilogue:
        if kind == "id":
            continue
        val = acc * s if kind == "silu" else pa_ref[...] + (1.0 - pa_ref[...]) * s
        out = jnp.where((j >= lo) & (j < hi), val, out)
    o_ref[0] = out.astype(o_ref.dtype)


def _inproj(xs, mod_l, norm_g, w, tc, tn=512, w_small=None, out_dtype=F32, epilogue=None, pa=None, pb=None):
    x_arrs, x_specs, ta = _split_stream(xs)
    b, d = x_arrs[0].shape[0], x_arrs[0].shape[2]
    n = w.shape[1]
    assert n % tn == 0 and tc % ROW_BLOCK == 0 and (ta - tc) % ROW_BLOCK == 0
    mod_lat = mod_l[:b]
    mod_ctx = mod_l[b:b + 1]
    has_small = w_small is not None
    in_specs = x_specs + [pl.BlockSpec((1, 3, d), lambda i, j: (i, 0, 0)),
                          pl.BlockSpec((1, 3, d), lambda i, j: (0, 0, 0)),
                          pl.BlockSpec((1, d), lambda i, j: (0, 0)),
                          pl.BlockSpec((d, tn), lambda i, j: (0, j))]
    args = x_arrs + [mod_lat, mod_ctx, norm_g.reshape(1, d), w.astype(BF16)]
    out_shape = [jax.ShapeDtypeStruct((b, ta, n), out_dtype)]
    out_specs = [pl.BlockSpec((1, ta, tn), lambda i, j: (i, 0, j))]
    if has_small:
        ns = w_small.shape[1]
        in_specs.append(pl.BlockSpec((d, ns), lambda i, j: (0, 0)))
        args.append(w_small.astype(BF16))
    if epilogue:
        assert all(lo % tn == 0 and hi % tn == 0 for lo, hi, _ in epilogue)
        epilogue = tuple((lo // tn, hi // tn, kind) for lo, hi, kind in epilogue)
        zeros = jnp.zeros((n,), F32)
        for par in (pa, pb):
            in_specs.append(pl.BlockSpec((1, tn), lambda i, j: (0, j)))
            args.append((zeros if par is None else par).reshape(1, n))
    if has_small:
        out_shape.append(jax.ShapeDtypeStruct((b, ta, ns), F32))
        out_specs.append(pl.BlockSpec((1, ta, ns), lambda i, j: (i, 0, 0)))
    res = pl.pallas_call(
        functools.partial(_inproj_kernel, tc=tc, n_x=len(x_arrs), has_small=has_small, epilogue=epilogue),
        out_shape=out_shape,
        grid=(b, n // tn),
        in_specs=in_specs,
        out_specs=out_specs,
        scratch_shapes=[pltpu.VMEM((ta, d), BF16)],
        compiler_params=_cparams("parallel", "arbitrary"),
        name="inproj",
    )(*args)
    return res if has_small else res[0]


def _head_rms(h, gain, n_heads):
    dh = h.shape[1] // n_heads
    parts = []
    for i in range(n_heads):
        hh = h[:, i * dh:(i + 1) * dh]
        ms = jnp.mean(hh * hh, axis=-1, keepdims=True)
        parts.append(hh * lax.rsqrt(ms + EPS))
    return jnp.concatenate(parts, axis=1) * gain


def _outproj_kernel(*refs, mode, n_feat, n_x, tc, tm, row0):
    feats, rest = refs[:n_feat], list(refs[n_feat:])
    hg_ref = rest.pop(0) if mode != "plain" else None
    w_ref = rest.pop(0)
    x_refs = [rest.pop(0) for _ in range(n_x)]
    ml_ref, mc_ref, o_ref = rest
    if mode == "plain":
        u = feats[0][0]
    elif mode == "mlstm":
        h0_ref, h1_ref, og_ref, z_ref = feats
        hn = _head_rms(h0_ref[0].astype(F32) + h1_ref[0].astype(F32), hg_ref[...], ML_HEADS)
        u = (jax.nn.sigmoid(og_ref[0].astype(F32)) * hn * _silu(z_ref[0].astype(F32))).astype(BF16)
    else:
        h0_ref, h1_ref, z_ref = feats
        hn = _head_rms(h0_ref[0] + h1_ref[0], hg_ref[...], HG_HEADS)
        u = (hn * _silu(z_ref[0])).astype(BF16)
    y = jnp.dot(u, w_ref[...], preferred_element_type=F32)
    first = row0 + pl.program_id(1) * tm
    row = first + lax.broadcasted_iota(jnp.int32, (tm, 1), 0)
    gate = jnp.where(row < tc, mc_ref[0, 2:3, :], ml_ref[0, 2:3, :])
    x = x_refs[0][0] if n_x == 1 else jnp.where(first < tc, x_refs[0][0], x_refs[1][0])
    o_ref[0] = x + gate * y


def _outproj(mode, feats, w_out, xs, mod_l, tc, head_g=None, lat_only=False):
    x_arrs = list(xs) if isinstance(xs, tuple) else [xs]
    b, d = x_arrs[0].shape[0], x_arrs[0].shape[2]
    ta = sum(a.shape[1] for a in x_arrs)
    kdim = w_out.shape[0]
    tm = OUT_ROWS
    assert ta % tm == 0 and tc % tm == 0
    off = tc // tm if lat_only else 0
    nct = tc // tm
    row_spec = lambda width, col: pl.BlockSpec((1, tm, width), lambda i, r, col=col: (i, r + off, col))
    in_specs, args = [], []
    for arr, width, col in feats:
        in_specs.append(row_spec(width, col))
        args.append(arr)
    if head_g is not None:
        in_specs.append(pl.BlockSpec((1, kdim), lambda i, r: (0, 0)))
        args.append(head_g.reshape(1, kdim))
    in_specs.append(pl.BlockSpec((kdim, d), lambda i, r: (0, 0)))
    args.append(w_out.astype(BF16))
    if len(x_arrs) == 1:
        in_specs.append(row_spec(d, 0))
    elif lat_only:
        x_arrs = x_arrs[1:]
        in_specs.append(pl.BlockSpec((1, tm, d), lambda i, r: (i, r, 0)))
    else:
        in_specs += [pl.BlockSpec((1, tm, d), lambda i, r: (i, jnp.minimum(r, nct - 1), 0)),
                     pl.BlockSpec((1, tm, d), lambda i, r: (i, jnp.maximum(r - nct, 0), 0))]
    args += x_arrs
    in_specs += [pl.BlockSpec((1, 3, d), lambda i, r: (i, 0, 0)),
                 pl.BlockSpec((1, 3, d), lambda i, r: (0, 0, 0))]
    args += [mod_l[:b], mod_l[b:b + 1]]
    rows_out = ta - off * tm
    return pl.pallas_call(
        functools.partial(_outproj_kernel, mode=mode, n_feat=len(feats), n_x=len(x_arrs), tc=tc, tm=tm,
                          row0=off * tm),
        out_shape=jax.ShapeDtypeStruct((b, rows_out, d), F32),
        grid=(b, rows_out // tm),
        in_specs=in_specs,
        out_specs=pl.BlockSpec((1, tm, d), lambda i, r: (i, r, 0)),
        compiler_params=_cparams("parallel", "parallel"),
        name="outproj_" + mode,
    )(*args)


def _conv_kernel(x_ref, ml_ref, mc_ref, g_ref, w_ref, cw_ref, cb_ref, o_ref, h_ref, *, tc, tw):
    @pl.when(pl.program_id(1) == 0)
    def _():
        _modulated((x_ref,), ml_ref, mc_ref, g_ref, h_ref, tc)

    ta = h_ref.shape[0]
    p = jnp.dot(h_ref[...], w_ref[0], preferred_element_type=F32)
    xin, bg, cg, z = (p[:, i * tw:(i + 1) * tw] for i in range(4))
    u = cg * xin
    row = lax.broadcasted_iota(jnp.int32, (ta, 1), 0)
    first = (row == 0) | (row == tc)
    last = (row == tc - 1) | (row == ta - 1)
    u_prev = jnp.where(first, 0.0, pltpu.roll(u, 1, axis=0))
    u_next = jnp.where(last, 0.0, pltpu.roll(u, ta - 1, axis=0))
    cw = cw_ref[0]
    y = u_prev * cw[0:1, :] + u * cw[1:2, :] + u_next * cw[2:3, :] + cb_ref[0]
    o_ref[0] = (bg * y * _silu(z)).astype(o_ref.dtype)


def _conv_mixer(xs, mod_l, norm_g, w_in, conv_w, conv_b, tc, tw=128):
    b, ta, d = xs.shape
    e = conv_w.shape[1]
    nt = e // tw
    w = w_in.reshape(d, 4, nt, tw).transpose(2, 0, 1, 3).reshape(nt, d, 4 * tw).astype(BF16)
    cw = conv_w.reshape(SC_KSIZE, nt, tw).transpose(1, 0, 2)
    cb = conv_b.reshape(nt, 1, tw)
    return pl.pallas_call(
        functools.partial(_conv_kernel, tc=tc, tw=tw),
        out_shape=jax.ShapeDtypeStruct((b, ta, e), BF16),
        grid=(b, nt),
        in_specs=[pl.BlockSpec((1, ta, d), lambda i, j: (i, 0, 0)),
                  pl.BlockSpec((1, 3, d), lambda i, j: (i, 0, 0)),
                  pl.BlockSpec((1, 3, d), lambda i, j: (0, 0, 0)),
                  pl.BlockSpec((1, d), lambda i, j: (0, 0)),
                  pl.BlockSpec((1, d, 4 * tw), lambda i, j: (j, 0, 0)),
                  pl.BlockSpec((1, SC_KSIZE, tw), lambda i, j: (j, 0, 0)),
                  pl.BlockSpec((1, 1, tw), lambda i, j: (j, 0, 0))],
        out_specs=pl.BlockSpec((1, ta, tw), lambda i, j: (i, 0, j)),
        scratch_shapes=[pltpu.VMEM((ta, d), BF16)],
        compiler_params=_cparams("parallel", "arbitrary"),
        name="conv_mixer",
    )(xs, mod_l[:b], mod_l[b:b + 1], norm_g.reshape(1, d), w, cw, cb)


AT_HALF = AT_HEAD_DIM // 2
AT_TILE_PERM = np.concatenate([np.arange(0, AT_HALF), np.arange(2 * AT_HALF, 3 * AT_HALF),
                               np.arange(AT_HALF, 2 * AT_HALF), np.arange(3 * AT_HALF, 4 * AT_HALF)])
LOG2E = float(np.log2(np.e))


def _rope_tables(tc, t):
    rows = t // GRID_W
    row = np.repeat(np.arange(rows), GRID_W).astype(np.float32)
    col = np.tile(np.arange(GRID_W), rows).astype(np.float32)
    n_freq = AT_HEAD_DIM // 4
    freqs = jnp.power(ROPE_BASE, -jnp.arange(n_freq, dtype=F32) / n_freq)
    ang = jnp.concatenate([jnp.asarray(row)[:, None] * freqs, jnp.asarray(col)[:, None] * freqs], axis=-1)
    cos, sin = jnp.cos(ang), jnp.sin(ang)
    cos_t = jnp.concatenate([jnp.ones((tc, LANES), F32), jnp.tile(cos, (1, 4))], axis=0)
    sin_t = jnp.concatenate([jnp.zeros((tc, LANES), F32), jnp.concatenate([-sin, -sin, sin, sin], axis=-1)], axis=0)
    return cos_t, sin_t


def _norm_rope_pair(x, gain, cos, sin, same_head):
    x2 = x * x
    hi = x2.astype(BF16)
    lo = (x2 - hi.astype(F32)).astype(BF16)
    ss = jnp.dot(hi, same_head, preferred_element_type=F32) + jnp.dot(lo, same_head, preferred_element_type=F32)
    xn = x * lax.rsqrt(ss * (1.0 / AT_HEAD_DIM) + EPS) * gain
    return xn * cos + pltpu.roll(xn, 2 * AT_HALF, axis=1) * sin


def _attn_kernel(sink_ref, q_ref, k_ref, v_ref, z_ref, qg_ref, kg_ref, cos_ref, sin_ref, sh_ref, o_ref,
                 qs_ref, ka_ref, kb_ref, va_ref, vb_ref, *, tc, need_ctx):
    ta = q_ref.shape[1]
    t = ta - tc
    nb = t // AT_BLOCK
    blk = AT_BLOCK
    g = pl.program_id(1)
    odd = (g % 2) == 1
    scale = (AT_HEAD_DIM ** -0.5) * LOG2E

    def prep(i, carry):
        r0 = pl.multiple_of(i * ROW_BLOCK, ROW_BLOCK)
        cos = cos_ref[pl.ds(r0, ROW_BLOCK), :]
        sin = sin_ref[pl.ds(r0, ROW_BLOCK), :]
        same_head = sh_ref[...]
        for c in range(2):
            qc = _norm_rope_pair(q_ref[0, pl.ds(r0, ROW_BLOCK), c * LANES:(c + 1) * LANES], qg_ref[...], cos, sin,
                                 same_head)
            qs_ref[pl.ds(r0, ROW_BLOCK), c * LANES:(c + 1) * LANES] = (qc * scale).astype(BF16)
        kn = _norm_rope_pair(k_ref[0, pl.ds(r0, ROW_BLOCK), :], kg_ref[...], cos, sin, same_head)
        vv = v_ref[0, pl.ds(r0, ROW_BLOCK), :]
        lane = lax.broadcasted_iota(jnp.int32, kn.shape, 1)
        k_own = jnp.where(((lane // AT_HALF) % 2) == (g % 2), kn, 0.0)
        k_oth = pltpu.roll(k_own, jnp.where(odd, 3 * AT_HALF, AT_HALF), axis=1)
        ka_ref[pl.ds(r0, ROW_BLOCK), :] = jnp.where(odd, k_oth, k_own).astype(BF16)
        kb_ref[pl.ds(r0, ROW_BLOCK), :] = jnp.where(odd, k_own, k_oth).astype(BF16)
        v_own = jnp.where((lane // AT_HEAD_DIM) == (g % 2), vv, 0.0)
        v_oth = pltpu.roll(v_own, AT_HEAD_DIM, axis=1)
        va = jnp.where(odd, v_oth, v_own)
        vb = jnp.where(odd, v_own, v_oth)
        va_ref[pl.ds(r0, ROW_BLOCK), :] = jnp.where(lane == AT_HEAD_DIM, 1.0, va).astype(BF16)
        vb_ref[pl.ds(r0, ROW_BLOCK), :] = jnp.where(lane == 0, 1.0, vb).astype(BF16)
        return carry

    lax.fori_loop(0, ta // ROW_BLOCK, prep, 0)
    zeros = jnp.zeros((blk, LANES), BF16)
    for ref in (ka_ref, kb_ref, va_ref, vb_ref):
        ref[ta:ta + blk, :] = zeros

    half = lax.broadcasted_iota(jnp.int32, (2 * blk, 1), 0) < blk
    sink_a = jnp.where(half, sink_ref[g, 0], sink_ref[g, 2]) * LOG2E
    sink_b = jnp.where(half, sink_ref[g, 1], sink_ref[g, 3]) * LOG2E

    qi = lax.broadcasted_iota(jnp.int32, (2 * blk, 3 * blk), 0) % blk
    kj = lax.broadcasted_iota(jnp.int32, (2 * blk, 3 * blk), 1)
    band = (kj - qi >= 0) & (kj - qi <= 2 * blk)
    out_lo = lax.broadcasted_iota(jnp.int32, (2 * blk, LANES), 1) < AT_HEAD_DIM

    nt = (((1,), (1,)), ((), ()))

    def one_side(qt, k_ref_, v_ref_, sink, ones_lane, k0, mask):
        s_ctx = lax.dot_general(qt, k_ref_[0:tc, :], nt, preferred_element_type=F32)
        m = jnp.maximum(sink, jnp.max(s_ctx, axis=-1, keepdims=True))
        if mask is not None:
            s_loc = lax.dot_general(qt, k_ref_[pl.ds(k0, 3 * blk), :], nt, preferred_element_type=F32)
            s_loc = jnp.where(mask, s_loc, -jnp.inf)
            m = jnp.maximum(m, jnp.max(s_loc, axis=-1, keepdims=True))
        acc = jnp.dot(jnp.exp2(s_ctx - m).astype(BF16), v_ref_[0:tc, :], preferred_element_type=F32)
        if mask is not None:
            acc = acc + jnp.dot(jnp.exp2(s_loc - m).astype(BF16), v_ref_[pl.ds(k0, 3 * blk), :],
                                preferred_element_type=F32)
        den = jnp.exp2(sink - m) + acc[:, ones_lane:ones_lane + 1]
        return acc / den

    def block(r0, k0, mask):
        qt = jnp.concatenate([qs_ref[pl.ds(r0, blk), 0:LANES], qs_ref[pl.ds(r0, blk), LANES:2 * LANES]], axis=0)
        o = jnp.where(out_lo, one_side(qt, ka_ref, va_ref, sink_a, AT_HEAD_DIM, k0, mask),
                      one_side(qt, kb_ref, vb_ref, sink_b, 0, k0, mask))
        for c in range(2):
            zc = z_ref[0, pl.ds(r0, blk), c * LANES:(c + 1) * LANES]
            o_ref[0, pl.ds(r0, blk), c * LANES:(c + 1) * LANES] = (o[c * blk:(c + 1) * blk] * _silu(zc)).astype(o_ref.dtype)

    def lat_block(n, carry):
        r0 = pl.multiple_of(tc + n * blk, blk)
        k0 = pl.multiple_of(tc + (n - 1) * blk, blk)
        kpos = (n - 1) * blk + kj
        block(r0, k0, band & (kpos >= 0) & (kpos < t))
        return carry

    lax.fori_loop(0, nb, lat_block, 0, unroll=2)
    if need_ctx:
        for n in range(tc // blk):
            block(n * blk, None, None)
    else:
        o_ref[0, 0:tc, :] = jnp.zeros((tc, o_ref.shape[2]), o_ref.dtype)


def _attn_mixer(p, at_q_g, at_k_g, at_sink, tc, need_ctx):
    b, ta, _ = p.shape
    t = ta - tc
    assert tc >= AT_BLOCK and tc % AT_BLOCK == 0 and t % AT_BLOCK == 0 and ta % ROW_BLOCK == 0
    cos_t, sin_t = _rope_tables(tc, t)
    lane_head = (np.arange(LANES) // AT_HALF) % 2
    same_head = jnp.asarray(lane_head[:, None] == lane_head[None, :], BF16)
    tile_gain = lambda gain: jnp.tile(gain, 2)[AT_TILE_PERM].reshape(1, LANES)
    gw = AT_GROUP * AT_HEAD_DIM
    kcol = AT_Q // LANES
    vcol = (AT_Q + AT_KV) // LANES
    zcol = (AT_Q + 2 * AT_KV) // gw
    return pl.pallas_call(
        functools.partial(_attn_kernel, tc=tc, need_ctx=need_ctx),
        out_shape=jax.ShapeDtypeStruct((b, ta, AT_Q), BF16),
        grid=(b, AT_KV_HEADS),
        in_specs=[pl.BlockSpec(memory_space=pltpu.SMEM),
                  pl.BlockSpec((1, ta, gw), lambda i, g: (i, 0, g)),
                  pl.BlockSpec((1, ta, LANES), lambda i, g: (i, 0, kcol + g // 2)),
                  pl.BlockSpec((1, ta, LANES), lambda i, g: (i, 0, vcol + g // 2)),
                  pl.BlockSpec((1, ta, gw), lambda i, g: (i, 0, zcol + g)),
                  pl.BlockSpec((1, LANES), lambda i, g: (0, 0)),
                  pl.BlockSpec((1, LANES), lambda i, g: (0, 0)),
                  pl.BlockSpec((ta, LANES), lambda i, g: (0, 0)),
                  pl.BlockSpec((ta, LANES), lambda i, g: (0, 0)),
                  pl.BlockSpec((LANES, LANES), lambda i, g: (0, 0))],
        out_specs=pl.BlockSpec((1, ta, gw), lambda i, g: (i, 0, g)),
        scratch_shapes=[pltpu.VMEM((ta, gw), BF16)] + [pltpu.VMEM((ta + AT_BLOCK, LANES), BF16)] * 4,
        compiler_params=_cparams("parallel", "arbitrary"),
        name="attn_mixer",
    )(at_sink.reshape(AT_KV_HEADS, AT_GROUP), p, p, p, p, tile_gain(at_q_g), tile_gain(at_k_g), cos_t, sin_t,
      same_head)


def _attn_weight(w_in):
    d = w_in.shape[0]
    nqk = AT_Q + AT_KV
    qk = w_in[:, :nqk].reshape(d, nqk // LANES, LANES)[:, :, AT_TILE_PERM].reshape(d, nqk)
    return jnp.concatenate([qk, w_in[:, nqk:]], axis=1)


def _chunk_order(nc, ncc, d, s):
    bwd = jnp.where(s < ncc, ncc - 1 - s, nc - 1 - (s - ncc))
    return jnp.where(d == 0, s, bwd)


ML_GATE_PERM = np.concatenate([np.arange(0, 4), np.arange(8, 12), np.arange(4, 8), np.arange(12, 16)])
ML_NQ = 6


def _mlstm_gate_kernel(g_ref, b_ref, o_ref, *, tc, lc):
    h = ML_HEADS
    ta = g_ref.shape[2]
    nc, ncc = ta // lc, tc // lc
    x = g_ref[0] + b_ref[...]
    li = x[0:2 * h]
    lfp = x[2 * h:4 * h]
    lf = jnp.minimum(lfp, 0.0) - jnp.log1p(jnp.exp(-jnp.abs(lfp)))
    fwd = lax.broadcasted_iota(jnp.int32, (2 * h, ta), 0) < h
    fwd1 = lax.broadcasted_iota(jnp.int32, (2 * h, 1), 0) < h
    pos = lax.broadcasted_iota(jnp.int32, (2 * h, ta), 1) % lc

    def seg_scan(y, op, fill):
        yf, yb = y, y
        s = 1
        while s < lc:
            yf = op(yf, jnp.where(pos >= s, pltpu.roll(yf, s, axis=1), fill))
            yb = op(yb, jnp.where(pos < lc - s, pltpu.roll(yb, ta - s, axis=1), fill))
            s *= 2
        return jnp.where(fwd, yf, yb)

    bsum = seg_scan(lf, jnp.add, 0.0)
    a = li - bsum
    cmax = seg_scan(a, jnp.maximum, -jnp.inf)

    def end_col(y, c):
        return jnp.where(fwd1, y[:, (c + 1) * lc - 1:(c + 1) * lc], y[:, c * lc:c * lc + 1])

    tot = [end_col(bsum, c) for c in range(nc)]
    amax = [end_col(cmax, c) for c in range(nc)]

    def chain(order):
        m = jnp.zeros((2 * h, 1), F32)
        m_in = [None] * nc
        for c in order:
            m_in[c] = m
            m = tot[c] + jnp.maximum(m, amax[c])
        return m_in

    m_f = chain(list(range(nc)))
    m_b = chain(list(range(ncc - 1, -1, -1)) + list(range(nc - 1, ncc - 1, -1)))
    for c in range(nc):
        m_in = jnp.where(fwd1, m_f[c], m_b[c])
        sl = slice(c * lc, (c + 1) * lc)
        g_run = jnp.maximum(m_in, cmax[:, sl])
        g_end = jnp.maximum(m_in, amax[c])
        nr = 2 * h
        quantities = (a[:, sl], g_run, jnp.exp(m_in - g_run), jnp.exp(-(bsum[:, sl] + g_run)),
                      jnp.exp(a[:, sl] - g_end), jnp.broadcast_to(jnp.exp(m_in - g_end), (nr, lc)))
        for qi, val in enumerate(quantities):
            o_ref[0, c, qi * nr:(qi + 1) * nr, :] = val
        o_ref[0, c, ML_NQ * nr:, :] = jnp.zeros((o_ref.shape[2] - ML_NQ * nr, lc), F32)


def _mlstm_scan_kernel(q_ref, k_ref, v_ref, a_ref, c_ref, o_ref, cs_ref, ns_ref, *, rev):
    lc = q_ref.shape[1]
    nr = 2 * ML_HEADS

    @pl.when(pl.program_id(1) == 0)
    def _():
        cs_ref[...] = jnp.zeros(cs_ref.shape, F32)
        ns_ref[...] = jnp.zeros(ns_ref.shape, F32)

    ti = lax.broadcasted_iota(jnp.int32, (lc, lc), 0)
    si = lax.broadcasted_iota(jnp.int32, (lc, lc), 1)
    mask = (si >= ti) if rev else (si <= ti)
    nt = (((1,), (1,)), ((), ()))
    tn = (((0,), (0,)), ((), ()))
    for h in range(ML_HEADS):
        r = (ML_HEADS if rev else 0) + h
        qb = q_ref[0, :, h * ML_DK:(h + 1) * ML_DK]
        kb = k_ref[0, :, h * ML_DK:(h + 1) * ML_DK]
        vb = v_ref[0, :, h * ML_DV:(h + 1) * ML_DV]
        a_row = a_ref[0, 0, r:r + 1, :]
        g_run, inter, eclamp, w, decay = (c_ref[0, 0, :, qi * nr + r:qi * nr + r + 1] for qi in range(1, 6))
        qk = lax.dot_general(qb, kb, nt, preferred_element_type=F32)
        s = qk * jnp.where(mask, jnp.exp(a_row - g_run), 0.0)
        c_old = cs_ref[h]
        n_old = ns_ref[h]
        num = inter * jnp.dot(qb, c_old.astype(BF16), preferred_element_type=F32) \
            + jnp.dot(s.astype(BF16), vb, preferred_element_type=F32)
        den = inter * jnp.sum(qb.astype(F32) * n_old, axis=-1, keepdims=True) + jnp.sum(s, axis=-1, keepdims=True)
        o_ref[0, :, h * ML_DV:(h + 1) * ML_DV] = (num / jnp.maximum(jnp.abs(den), eclamp)).astype(o_ref.dtype)
        kw = kb.astype(F32) * w
        dec = decay[0:1, :]
        cs_ref[h] = dec * c_old + lax.dot_general(kw.astype(BF16), vb, tn, preferred_element_type=F32)
        ns_ref[h] = dec * n_old + jnp.sum(kw, axis=0, keepdims=True)


def _mlstm_scan(p, gates, gate_b, tc):
    b, ta, _ = p.shape
    lc = ML_CHUNK
    assert tc % lc == 0 and ta % lc == 0
    nc, ncc = ta // lc, tc // lc
    ng = 4 * ML_HEADS
    nlane = SUBLANES * 2 * ML_HEADS
    g_t = gates[:, :, :ng].transpose(0, 2, 1)
    bias = gate_b.reshape(ng)[ML_GATE_PERM].reshape(ng, 1)
    gp = pl.pallas_call(
        functools.partial(_mlstm_gate_kernel, tc=tc, lc=lc),
        out_shape=jax.ShapeDtypeStruct((b, nc, nlane, lc), F32),
        grid=(b,),
        in_specs=[pl.BlockSpec((1, ng, ta), lambda i: (i, 0, 0)),
                  pl.BlockSpec((ng, 1), lambda i: (0, 0))],
        out_specs=pl.BlockSpec((1, nc, nlane, lc), lambda i: (i, 0, 0, 0)),
        compiler_params=_cparams("parallel"),
        name="mlstm_gates",
    )(g_t, bias)
    cols = gp.transpose(0, 1, 3, 2)
    outs = []
    for d in range(2):
        chunk = functools.partial(_chunk_order, nc, ncc, d)
        outs.append(pl.pallas_call(
            functools.partial(_mlstm_scan_kernel, rev=bool(d)),
            out_shape=jax.ShapeDtypeStruct((b, ta, ML_INNER), BF16),
            grid=(b, nc),
            in_specs=[pl.BlockSpec((1, lc, ML_QK), lambda i, s, chunk=chunk: (i, chunk(s), 0)),
                      pl.BlockSpec((1, lc, ML_QK), lambda i, s, chunk=chunk: (i, chunk(s), 1)),
                      pl.BlockSpec((1, lc, ML_INNER), lambda i, s, chunk=chunk: (i, chunk(s), 1)),
                      pl.BlockSpec((1, 1, 2 * ML_HEADS, lc), lambda i, s, chunk=chunk: (i, chunk(s), 0, 0)),
                      pl.BlockSpec((1, 1, lc, nlane), lambda i, s, chunk=chunk: (i, chunk(s), 0, 0))],
            out_specs=pl.BlockSpec((1, lc, ML_INNER), lambda i, s, chunk=chunk: (i, chunk(s), 0)),
            scratch_shapes=[pltpu.VMEM((ML_HEADS, ML_DK, ML_DV), F32), pltpu.VMEM((ML_HEADS, 1, ML_DK), F32)],
            compiler_params=_cparams("parallel", "arbitrary"),
            name="mlstm_scan_bwd" if d else "mlstm_scan_fwd",
        )(p, p, p, gp, cols))
    return outs


def _hgrn_lb_kernel(p_ref, o_ref, *, layer):
    for d in range(p_ref.shape[0]):
        x = p_ref[d]
        e = jnp.exp(x - jnp.max(x, axis=0, keepdims=True))
        p = e / jnp.sum(e, axis=0, keepdims=True)
        acc = jnp.zeros((1, x.shape[1]), F32)
        for j in range(1, layer + 1):
            acc = acc + p[j:j + 1, :]
        o_ref[d:d + 1, :] = acc


def _cumsum_rows(x, rev):
    n = x.shape[0]
    row = lax.broadcasted_iota(jnp.int32, x.shape, 0)
    s = 1
    while s < n:
        if rev:
            x = x + jnp.where(row < n - s, pltpu.roll(x, n - s, axis=0), 0.0)
        else:
            x = x + jnp.where(row >= s, pltpu.roll(x, s, axis=0), 0.0)
        s *= 2
    return x


def _anchor_rows(a, m, rev):
    n, f = a.shape
    idx = m if rev else m - 1
    if 2 * m >= SUBLANES:
        a3 = a.reshape(n // (2 * m), 2 * m, f)
        return jnp.broadcast_to(a3[:, idx:idx + 1, :], a3.shape).reshape(n, f)
    a3 = a.reshape(n // SUBLANES, SUBLANES, f)
    sub = lax.broadcasted_iota(jnp.int32, a3.shape, 1)
    out = None
    for gi in range(SUBLANES // (2 * m) - 1, -1, -1):
        cand = jnp.broadcast_to(a3[:, gi * 2 * m + idx:gi * 2 * m + idx + 1, :], a3.shape)
        out = cand if out is None else jnp.where(sub < (gi + 1) * 2 * m, cand, out)
    return out.reshape(n, f)


HG_MAX_LOG_SPAN = 64.0


def _hgrn_scan_kernel(q_ref, f_ref, i_ref, o_ref, s_ref, *, rev):
    lc = q_ref.shape[1]

    @pl.when(pl.program_id(1) == 0)
    def _():
        s_ref[...] = jnp.zeros(s_ref.shape, F32)

    row = lax.broadcasted_iota(jnp.int32, (lc, 1), 0)
    ti = lax.broadcasted_iota(jnp.int32, (lc, lc), 0)
    si = lax.broadcasted_iota(jnp.int32, (lc, lc), 1)
    nt = (((1,), (1,)), ((), ()))
    tn = (((0,), (0,)), ((), ()))
    end = 0 if rev else lc - 1

    def finish(h, c0, attn, q_dec, kd, a_end, iv, ib, extra):
        s_old = s_ref[h]
        o = jnp.dot(attn.astype(BF16), ib, preferred_element_type=F32)
        o = o + lax.dot_general(q_dec, s_old.astype(BF16), nt, preferred_element_type=F32)
        if extra is not None:
            o = o + extra
        o_ref[0, :, pl.ds(c0, HG_DIM)] = o
        s_ref[h] = s_old * jnp.exp(a_end) + lax.dot_general(ib, kd, tn, preferred_element_type=F32)

    def head_single_anchor(h, carry):
        c0 = h * HG_DIM
        q = q_ref[0, :, pl.ds(c0, HG_DIM)]
        f = f_ref[0, :, pl.ds(c0, HG_DIM)]
        iv = i_ref[0, :, pl.ds(c0, HG_DIM)]
        a = _cumsum_rows(jnp.log(f), rev)
        q_dec = (q * jnp.exp(a)).astype(BF16)
        k_inv = (1.0 - f) * jnp.exp(-a)
        pair = lax.dot_general(q_dec, k_inv.astype(BF16), nt, preferred_element_type=F32)
        attn = jnp.where((si >= ti) if rev else (si <= ti), pair, 0.0)
        a_end = a[end:end + 1, :]
        kd = (k_inv * jnp.exp(a_end)).astype(BF16)
        finish(h, c0, attn, q_dec, kd, a_end, iv, iv.astype(BF16), None)
        return carry

    def head_per_level(h, carry):
        c0 = pl.multiple_of(h * HG_DIM, HG_DIM)
        q = q_ref[0, :, pl.ds(c0, HG_DIM)]
        f = f_ref[0, :, pl.ds(c0, HG_DIM)]
        k = 1.0 - f
        iv = i_ref[0, :, pl.ds(c0, HG_DIM)]
        a = _cumsum_rows(jnp.log(f), rev)
        attn = jnp.zeros((lc, lc), F32)
        m = 1
        while m < lc:
            e = jnp.exp(-jnp.abs(a - _anchor_rows(a, m, rev)))
            upper = (row % (2 * m)) >= m
            is_q = jnp.logical_not(upper) if rev else upper
            qt = jnp.where(is_q, q * e, 0.0).astype(BF16)
            kt = jnp.where(is_q, 0.0, k * e).astype(BF16)
            pair = lax.dot_general(qt, kt, nt, preferred_element_type=F32)
            attn = attn + jnp.where((ti // (2 * m)) == (si // (2 * m)), pair, 0.0)
            m *= 2
        a_end = a[end:end + 1, :]
        kd = (k * jnp.exp(a_end - a)).astype(BF16)
        diag = jnp.sum(q * k, axis=-1, keepdims=True) * iv
        finish(h, c0, attn, (q * jnp.exp(a)).astype(BF16), kd, a_end, iv, iv.astype(BF16), diag)
        return carry

    f_min = f_ref[0, :, 0:HG_DIM]
    for h in range(1, HG_HEADS):
        f_min = jnp.minimum(f_min, f_ref[0, :, h * HG_DIM:(h + 1) * HG_DIM])
    f_min = jnp.min(f_min.reshape(lc // SUBLANES, SUBLANES, HG_DIM), axis=0)
    in_range = jnp.min(f_min) >= float(np.exp(-HG_MAX_LOG_SPAN / lc))

    @pl.when(in_range)
    def _():
        for h in range(HG_HEADS):
            head_single_anchor(h, 0)

    @pl.when(jnp.logical_not(in_range))
    def _():
        lax.fori_loop(0, HG_HEADS, head_per_level, 0)


def _hgrn_scan(p, tc, rev):
    b, ta, _ = p.shape
    lc = HG_CHUNK
    hg = HG_HEADS * HG_DIM
    assert tc % lc == 0 and ta % lc == 0
    nc, ncc = ta // lc, tc // lc
    d = 1 if rev else 0
    chunk = lambda s: _chunk_order(nc, ncc, d, s)
    return pl.pallas_call(
        functools.partial(_hgrn_scan_kernel, rev=rev),
        out_shape=jax.ShapeDtypeStruct((b, ta, hg), F32),
        grid=(b, nc),
        in_specs=[pl.BlockSpec((1, lc, hg), lambda i, s: (i, chunk(s), 0)),
                  pl.BlockSpec((1, lc, hg), lambda i, s: (i, chunk(s), 1 + d)),
                  pl.BlockSpec((1, lc, hg), lambda i, s: (i, chunk(s), 3))],
        out_specs=pl.BlockSpec((1, lc, hg), lambda i, s: (i, chunk(s), 0)),
        scratch_shapes=[pltpu.VMEM((HG_HEADS, HG_DIM, HG_DIM), F32)],
        compiler_params=_cparams("parallel", "arbitrary"),
        name="hgrn_scan_bwd" if rev else "hgrn_scan_fwd",
    )(p, p, p)


def kernel(x, c, ctx, c_ctx, ada_w, ada_b, norm_g, ml_w_in, ml_gate_b, ml_head_g, ml_w_out, at_w_in, at_q_g, at_k_g, at_sink, at_w_out, sc_w_in, sc_conv_w, sc_conv_b, sc_w_out, hg_w_in, hg_f_b, hg_lb, hg_head_g, hg_w_out):
    tc = ctx.shape[1]
    mod = _ada_mod(c, c_ctx, ada_w, ada_b)
    xs = (ctx, x)
    for layer in range(DEPTH):
        kind, j = layer % 4, layer // 4
        need_ctx = layer < DEPTH - 1
        last = dict(lat_only=True) if layer == DEPTH - 1 else {}
        mod_l = mod[layer]
        if kind == 0:
            n_main = 2 * ML_QK + 3 * ML_INNER
            w_gate = jnp.zeros((ml_w_in.shape[1], LANES), F32).at[:, :4 * ML_HEADS].set(
                ml_w_in[j][:, n_main:][:, ML_GATE_PERM])
            w_main = ml_w_in[j][:, :n_main].at[:, :ML_QK].multiply(ML_DK ** -0.5)
            p, gates = _inproj(xs, mod_l, norm_g[layer], w_main, tc, w_small=w_gate, out_dtype=BF16)
            h_f, h_b = _mlstm_scan(p, gates, ml_gate_b[j], tc)
            feats = [(h_f, ML_INNER, 0), (h_b, ML_INNER, 0), (p, ML_INNER, 2), (p, ML_INNER, 3)]
            xs = _outproj("mlstm", feats, ml_w_out[j], xs, mod_l, tc, head_g=ml_head_g[j], **last)
        elif kind == 1:
            p = _inproj(xs, mod_l, norm_g[layer], _attn_weight(at_w_in[j]), tc)
            u = _attn_mixer(p, at_q_g[j], at_k_g[j], at_sink[j], tc, need_ctx)
            xs = _outproj("plain", [(u, AT_Q, 0)], at_w_out[j], xs, mod_l, tc, **last)
        elif kind == 2:
            if isinstance(xs, tuple):
                xs = jnp.concatenate(xs, axis=1)
            u = _conv_mixer(xs, mod_l, norm_g[layer], sc_w_in[j], sc_conv_w[j], sc_conv_b[j], tc)
            xs = _outproj("plain", [(u, u.shape[2], 0)], sc_w_out[j], xs, mod_l, tc, **last)
        else:
            hg = HG_HEADS * HG_DIM
            lb = pl.pallas_call(
                functools.partial(_hgrn_lb_kernel, layer=layer),
                out_shape=jax.ShapeDtypeStruct((2, hg), F32),
                name="hgrn_lb",
            )(hg_lb[j])
            ep = ((0, hg, "silu"), (hg, 3 * hg, "fgate"), (3 * hg, 5 * hg, "id"))
            zeros = jnp.zeros((hg,), F32)
            pa = jnp.concatenate([zeros, lb.reshape(2 * hg), zeros, zeros])
            pb = jnp.concatenate([zeros, hg_f_b[j].reshape(2 * hg), zeros, zeros])
            p = _inproj(xs, mod_l, norm_g[layer], hg_w_in[j], tc, epilogue=ep, pa=pa, pb=pb)
            o_f = _hgrn_scan(p, tc, rev=False)
            o_b = _hgrn_scan(p, tc, rev=True)
            feats = [(o_f, hg, 0), (o_b, hg, 0), (p, hg, 4)]
            xs = _outproj("hgrn", feats, hg_w_out[j], xs, mod_l, tc, head_g=hg_head_g[j], **last)
    return xs
```

```python
import functools

import numpy as np
import jax
import jax.numpy as jnp
from jax import lax
from jax.experimental import pallas as pl
from jax.experimental.pallas import tpu as pltpu

F32 = jnp.float32
BF16 = jnp.bfloat16
EPS = 1e-6
DEPTH = 4
GRID_W = 64
ROPE_BASE = 10000.0

ML_HEADS, ML_DK, ML_DV = 4, 256, 512
ML_QK = ML_HEADS * ML_DK
ML_INNER = ML_HEADS * ML_DV
ML_CHUNK = 256

AT_HEADS, AT_KV_HEADS, AT_HEAD_DIM = 16, 4, 64
AT_GROUP = AT_HEADS // AT_KV_HEADS
AT_BLOCK = 128
AT_Q = AT_HEADS * AT_HEAD_DIM
AT_KV = AT_KV_HEADS * AT_HEAD_DIM

SC_KSIZE = 3

HG_HEADS, HG_DIM = 8, 128
HG_CHUNK = 128
HG_STEP_CHUNKS = 2

LANES = 128
SUBLANES = 8
VMEM_LIMIT_BYTES = 56 * 1024 * 1024

ROW_BLOCK = 256
OUT_ROWS = 768
OUT_ROWS_SPLIT = 256


def _cparams(*sem):
    return pltpu.CompilerParams(dimension_semantics=sem, vmem_limit_bytes=VMEM_LIMIT_BYTES)


def _sigmoid(x):
    return 0.5 * jnp.tanh(0.5 * x) + 0.5


def _silu(x):
    return x * _sigmoid(x)


def _ada_kernel(c_ref, w_ref, b_ref, o_ref):
    s = _silu(c_ref[...])
    o_ref[0] = jnp.dot(s.astype(BF16), w_ref[0].astype(BF16), preferred_element_type=F32) + b_ref[0]


def _ada_mod(c, c_ctx, ada_w, ada_b):
    b, d = c.shape
    depth = ada_w.shape[0]
    rows = -(-(b + 1) // SUBLANES) * SUBLANES
    cc = jnp.zeros((rows, d), F32).at[:b].set(c).at[b].set(c_ctx)
    tn = 1024
    out = pl.pallas_call(
        _ada_kernel,
        out_shape=jax.ShapeDtypeStruct((depth, rows, 3 * d), F32),
        grid=(depth, 3 * d // tn),
        in_specs=[pl.BlockSpec((rows, d), lambda l, j: (0, 0)),
                  pl.BlockSpec((1, d, tn), lambda l, j: (l, 0, j)),
                  pl.BlockSpec((1, 1, tn), lambda l, j: (l, 0, j))],
        out_specs=pl.BlockSpec((1, rows, tn), lambda l, j: (l, 0, j)),
        compiler_params=_cparams("parallel", "parallel"),
        name="ada_mod",
    )(cc, ada_w, ada_b.reshape(depth, 1, 3 * d))
    return out.reshape(depth, rows, 3, d)


def _modulated(x_refs, ml_ref, mc_ref, g_ref, h_ref, tc):
    ta = h_ref.shape[0]
    g = g_ref[...]
    xc_ref = x_refs[0]
    xl_ref, lat0 = (x_refs[1], 0) if len(x_refs) == 2 else (x_refs[0], tc)

    def rows(x_ref, src0, dst0, m_ref):
        x = x_ref[0, pl.ds(src0, ROW_BLOCK), :]
        ms = jnp.mean(x * x, axis=-1, keepdims=True)
        xn = x * lax.rsqrt(ms + EPS) * g
        h = xn * (1.0 + m_ref[0, 1:2, :]) + m_ref[0, 0:1, :]
        h_ref[pl.ds(dst0, ROW_BLOCK), :] = h.astype(BF16)

    for i in range(tc // ROW_BLOCK):
        rows(xc_ref, i * ROW_BLOCK, i * ROW_BLOCK, mc_ref)

    def body(i, carry):
        r0 = pl.multiple_of(i * ROW_BLOCK, ROW_BLOCK)
        rows(xl_ref, lat0 + r0, tc + r0, ml_ref)
        return carry

    lax.fori_loop(0, (ta - tc) // ROW_BLOCK, body, 0)


def _split_stream(xs):
    arrs = list(xs) if isinstance(xs, tuple) else [xs]
    specs = [pl.BlockSpec((1,) + a.shape[1:], lambda i, j: (i, 0, 0)) for a in arrs]
    ta = sum(a.shape[1] for a in arrs)
    return arrs, specs, ta


def _inproj_kernel(*refs, tc, n_x, has_small, epilogue):
    x_refs, (ml_ref, mc_ref, g_ref, w_ref), rest = refs[:n_x], refs[n_x:n_x + 4], list(refs[n_x + 4:])
    ws_ref = rest.pop(0) if has_small else None
    pa_ref, pb_ref = (rest.pop(0), rest.pop(0)) if epilogue else (None, None)
    o_ref = rest.pop(0)
    os_ref = rest.pop(0) if has_small else None
    h_ref = rest.pop(0)

    @pl.when(pl.program_id(1) == 0)
    def _():
        _modulated(x_refs, ml_ref, mc_ref, g_ref, h_ref, tc)
        if has_small:
            os_ref[0] = jnp.dot(h_ref[...], ws_ref[...], preferred_element_type=F32)

    acc = jnp.dot(h_ref[...], w_ref[...], preferred_element_type=F32)
    if not epilogue:
        o_ref[0] = acc.astype(o_ref.dtype)
        return
    j = pl.program_id(1)
    s = _sigmoid(acc + pb_ref[...])
    out = acc
    for lo, hi, kind in epilogue:
        if kind == "id":
            continue
        val = acc * s if kind == "silu" else pa_ref[...] + (1.0 - pa_ref[...]) * s
        out = jnp.where((j >= lo) & (j < hi), val, out)
    o_ref[0] = out.astype(o_ref.dtype)


def _inproj(xs, mod_l, norm_g, w, tc, tn=512, w_small=None, out_dtype=F32, epilogue=None, pa=None, pb=None):
    x_arrs, x_specs, ta = _split_stream(xs)
    b, d = x_arrs[0].shape[0], x_arrs[0].shape[2]
    n = w.shape[1]
    assert n % tn == 0 and tc % ROW_BLOCK == 0 and (ta - tc) % ROW_BLOCK == 0
    mod_lat = mod_l[:b]
    mod_ctx = mod_l[b:b + 1]
    has_small = w_small is not None
    in_specs = x_specs + [pl.BlockSpec((1, 3, d), lambda i, j: (i, 0, 0)),
                          pl.BlockSpec((1, 3, d), lambda i, j: (0, 0, 0)),
                          pl.BlockSpec((1, d), lambda i, j: (0, 0)),
                          pl.BlockSpec((d, tn), lambda i, j: (0, j))]
    args = x_arrs + [mod_lat, mod_ctx, norm_g.reshape(1, d), w.astype(BF16)]
    out_shape = [jax.ShapeDtypeStruct((b, ta, n), out_dtype)]
    out_specs = [pl.BlockSpec((1, ta, tn), lambda i, j: (i, 0, j))]
    if has_small:
        ns = w_small.shape[1]
        in_specs.append(pl.BlockSpec((d, ns), lambda i, j: (0, 0)))
        args.append(w_small.astype(BF16))
    if epilogue:
        assert all(lo % tn == 0 and hi % tn == 0 for lo, hi, _ in epilogue)
        epilogue = tuple((lo // tn, hi // tn, kind) for lo, hi, kind in epilogue)
        zeros = jnp.zeros((n,), F32)
        for par in (pa, pb):
            in_specs.append(pl.BlockSpec((1, tn), lambda i, j: (0, j)))
            args.append((zeros if par is None else par).reshape(1, n))
    if has_small:
        out_shape.append(jax.ShapeDtypeStruct((b, ta, ns), F32))
        out_specs.append(pl.BlockSpec((1, ta, ns), lambda i, j: (i, 0, 0)))
    res = pl.pallas_call(
        functools.partial(_inproj_kernel, tc=tc, n_x=len(x_arrs), has_small=has_small, epilogue=epilogue),
        out_shape=out_shape,
        grid=(b, n // tn),
        in_specs=in_specs,
        out_specs=out_specs,
        scratch_shapes=[pltpu.VMEM((ta, d), BF16)],
        compiler_params=_cparams("parallel", "arbitrary"),
        name="inproj",
    )(*args)
    return res if has_small else res[0]


def _head_rms(h, gain, n_heads):
    dh = h.shape[1] // n_heads
    parts = []
    for i in range(n_heads):
        hh = h[:, i * dh:(i + 1) * dh]
        ms = jnp.mean(hh * hh, axis=-1, keepdims=True)
        parts.append(hh * lax.rsqrt(ms + EPS))
    return jnp.concatenate(parts, axis=1) * gain


def _outproj_kernel(*refs, mode, n_feat, n_x, tc, tm, row0):
    feats, rest = refs[:n_feat], list(refs[n_feat:])
    hg_ref = rest.pop(0) if mode != "plain" else None
    w_ref = rest.pop(0)
    x_refs = [rest.pop(0) for _ in range(n_x)]
    ml_ref, mc_ref, o_ref = rest
    if mode == "plain":
        u = feats[0][0]
    elif mode == "mlstm":
        h0_ref, h1_ref, og_ref, z_ref = feats
        hn = _head_rms(h0_ref[0].astype(F32) + h1_ref[0].astype(F32), hg_ref[...], ML_HEADS)
        u = (_sigmoid(og_ref[0].astype(F32)) * hn * _silu(z_ref[0].astype(F32))).astype(BF16)
    else:
        h0_ref, h1_ref, z_ref = feats
        hn = _head_rms(h0_ref[0] + h1_ref[0], hg_ref[...], HG_HEADS)
        u = (hn * _silu(z_ref[0].astype(F32))).astype(BF16)
    y = jnp.dot(u, w_ref[...], preferred_element_type=F32)
    first = row0 + pl.program_id(1) * tm
    row = first + lax.broadcasted_iota(jnp.int32, (tm, 1), 0)
    gate = jnp.where(row < tc, mc_ref[0, 2:3, :], ml_ref[0, 2:3, :])
    x = x_refs[0][0] if n_x == 1 else jnp.where(first < tc, x_refs[0][0], x_refs[1][0])
    o_ref[0] = x + gate * y


def _outproj(mode, feats, w_out, xs, mod_l, tc, head_g=None, lat_only=False):
    x_arrs = list(xs) if isinstance(xs, tuple) else [xs]
    b, d = x_arrs[0].shape[0], x_arrs[0].shape[2]
    ta = sum(a.shape[1] for a in x_arrs)
    kdim = w_out.shape[0]
    tm = OUT_ROWS if (len(x_arrs) == 1 and not lat_only and ta % OUT_ROWS == 0) else OUT_ROWS_SPLIT
    assert ta % tm == 0 and (tc % tm == 0 or (len(x_arrs) == 1 and not lat_only))
    off = tc // tm if lat_only else 0
    nct = tc // tm
    row_spec = lambda width, col: pl.BlockSpec((1, tm, width), lambda i, r, col=col: (i, r + off, col))
    in_specs, args = [], []
    for arr, width, col in feats:
        in_specs.append(row_spec(width, col))
        args.append(arr)
    if head_g is not None:
        in_specs.append(pl.BlockSpec((1, kdim), lambda i, r: (0, 0)))
        args.append(head_g.reshape(1, kdim))
    in_specs.append(pl.BlockSpec((kdim, d), lambda i, r: (0, 0)))
    args.append(w_out.astype(BF16))
    if len(x_arrs) == 1:
        in_specs.append(row_spec(d, 0))
    elif lat_only:
        x_arrs = x_arrs[1:]
        in_specs.append(pl.BlockSpec((1, tm, d), lambda i, r: (i, r, 0)))
    else:
        in_specs += [pl.BlockSpec((1, tm, d), lambda i, r: (i, jnp.minimum(r, nct - 1), 0)),
                     pl.BlockSpec((1, tm, d), lambda i, r: (i, jnp.maximum(r - nct, 0), 0))]
    args += x_arrs
    in_specs += [pl.BlockSpec((1, 3, d), lambda i, r: (i, 0, 0)),
                 pl.BlockSpec((1, 3, d), lambda i, r: (0, 0, 0))]
    args += [mod_l[:b], mod_l[b:b + 1]]
    rows_out = ta - off * tm
    return pl.pallas_call(
        functools.partial(_outproj_kernel, mode=mode, n_feat=len(feats), n_x=len(x_arrs), tc=tc, tm=tm,
                          row0=off * tm),
        out_shape=jax.ShapeDtypeStruct((b, rows_out, d), F32),
        grid=(b, rows_out // tm),
        in_specs=in_specs,
        out_specs=pl.BlockSpec((1, tm, d), lambda i, r: (i, r, 0)),
        compiler_params=_cparams("parallel", "parallel"),
        name="outproj_" + mode,
    )(*args)


def _conv_kernel(x_ref, ml_ref, mc_ref, g_ref, w_ref, cw_ref, cb_ref, o_ref, h_ref, *, tc, tw):
    @pl.when(pl.program_id(1) == 0)
    def _():
        _modulated((x_ref,), ml_ref, mc_ref, g_ref, h_ref, tc)

    ta = h_ref.shape[0]
    p = jnp.dot(h_ref[...], w_ref[0], preferred_element_type=F32)
    xin, bg, cg, z = (p[:, i * tw:(i + 1) * tw] for i in range(4))
    u = cg * xin
    row = lax.broadcasted_iota(jnp.int32, (ta, 1), 0)
    first = (row == 0) | (row == tc)
    last = (row == tc - 1) | (row == ta - 1)
    u_prev = jnp.where(first, 0.0, pltpu.roll(u, 1, axis=0))
    u_next = jnp.where(last, 0.0, pltpu.roll(u, ta - 1, axis=0))
    cw = cw_ref[0]
    y = u_prev * cw[0:1, :] + u * cw[1:2, :] + u_next * cw[2:3, :] + cb_ref[0]
    o_ref[0] = (bg * y * _silu(z)).astype(o_ref.dtype)


def _conv_mixer(xs, mod_l, norm_g, w_in, conv_w, conv_b, tc, tw=256):
    b, ta, d = xs.shape
    e = conv_w.shape[1]
    nt = e // tw
    w = w_in.reshape(d, 4, nt, tw).transpose(2, 0, 1, 3).reshape(nt, d, 4 * tw).astype(BF16)
    cw = conv_w.reshape(SC_KSIZE, nt, tw).transpose(1, 0, 2)
    cb = conv_b.reshape(nt, 1, tw)
    return pl.pallas_call(
        functools.partial(_conv_kernel, tc=tc, tw=tw),
        out_shape=jax.ShapeDtypeStruct((b, ta, e), BF16),
        grid=(b, nt),
        in_specs=[pl.BlockSpec((1, ta, d), lambda i, j: (i, 0, 0)),
                  pl.BlockSpec((1, 3, d), lambda i, j: (i, 0, 0)),
                  pl.BlockSpec((1, 3, d), lambda i, j: (0, 0, 0)),
                  pl.BlockSpec((1, d), lambda i, j: (0, 0)),
                  pl.BlockSpec((1, d, 4 * tw), lambda i, j: (j, 0, 0)),
                  pl.BlockSpec((1, SC_KSIZE, tw), lambda i, j: (j, 0, 0)),
                  pl.BlockSpec((1, 1, tw), lambda i, j: (j, 0, 0))],
        out_specs=pl.BlockSpec((1, ta, tw), lambda i, j: (i, 0, j)),
        scratch_shapes=[pltpu.VMEM((ta, d), BF16)],
        compiler_params=_cparams("parallel", "arbitrary"),
        name="conv_mixer",
    )(xs, mod_l[:b], mod_l[b:b + 1], norm_g.reshape(1, d), w, cw, cb)


AT_HALF = AT_HEAD_DIM // 2
AT_TILE_PERM = np.concatenate([np.arange(0, AT_HALF), np.arange(2 * AT_HALF, 3 * AT_HALF),
                               np.arange(AT_HALF, 2 * AT_HALF), np.arange(3 * AT_HALF, 4 * AT_HALF)])
LOG2E = float(np.log2(np.e))


def _rope_tables(tc, t):
    rows = t // GRID_W
    row = np.repeat(np.arange(rows), GRID_W).astype(np.float32)
    col = np.tile(np.arange(GRID_W), rows).astype(np.float32)
    n_freq = AT_HEAD_DIM // 4
    freqs = jnp.power(ROPE_BASE, -jnp.arange(n_freq, dtype=F32) / n_freq)
    ang = jnp.concatenate([jnp.asarray(row)[:, None] * freqs, jnp.asarray(col)[:, None] * freqs], axis=-1)
    cos, sin = jnp.cos(ang), jnp.sin(ang)
    cos_t = jnp.concatenate([jnp.ones((tc, LANES), F32), jnp.tile(cos, (1, 4))], axis=0)
    sin_t = jnp.concatenate([jnp.zeros((tc, LANES), F32), jnp.concatenate([-sin, -sin, sin, sin], axis=-1)], axis=0)
    return cos_t, sin_t


def _norm_rope_pair(x, gain, cos, sin, same_head):
    x2 = x * x
    hi = x2.astype(BF16)
    lo = (x2 - hi.astype(F32)).astype(BF16)
    ss = jnp.dot(hi, same_head, preferred_element_type=F32) + jnp.dot(lo, same_head, preferred_element_type=F32)
    xn = x * lax.rsqrt(ss * (1.0 / AT_HEAD_DIM) + EPS) * gain
    return xn * cos + pltpu.roll(xn, 2 * AT_HALF, axis=1) * sin


def _attn_kernel(sink_ref, q_ref, k_ref, v_ref, z_ref, qg_ref, kg_ref, cos_ref, sin_ref, sh_ref, o_ref,
                 qs_ref, ka_ref, kb_ref, va_ref, vb_ref, *, tc, need_ctx):
    ta = q_ref.shape[1]
    t = ta - tc
    nb = t // AT_BLOCK
    blk = AT_BLOCK
    g = pl.program_id(1)
    odd = (g % 2) == 1
    scale = (AT_HEAD_DIM ** -0.5) * LOG2E

    def prep(i, carry):
        r0 = pl.multiple_of(i * ROW_BLOCK, ROW_BLOCK)
        cos = cos_ref[pl.ds(r0, ROW_BLOCK), :]
        sin = sin_ref[pl.ds(r0, ROW_BLOCK), :]
        same_head = sh_ref[...]
        for c in range(2):
            qc = _norm_rope_pair(q_ref[0, pl.ds(r0, ROW_BLOCK), c * LANES:(c + 1) * LANES].astype(F32), qg_ref[...],
                                 cos, sin, same_head)
            qs_ref[pl.ds(r0, ROW_BLOCK), c * LANES:(c + 1) * LANES] = (qc * scale).astype(BF16)
        kn = _norm_rope_pair(k_ref[0, pl.ds(r0, ROW_BLOCK), :].astype(F32), kg_ref[...], cos, sin, same_head)
        vv = v_ref[0, pl.ds(r0, ROW_BLOCK), :].astype(F32)
        lane = lax.broadcasted_iota(jnp.int32, kn.shape, 1)
        k_own = jnp.where(((lane // AT_HALF) % 2) == (g % 2), kn, 0.0)
        k_oth = pltpu.roll(k_own, jnp.where(odd, 3 * AT_HALF, AT_HALF), axis=1)
        ka_ref[pl.ds(r0, ROW_BLOCK), :] = jnp.where(odd, k_oth, k_own).astype(BF16)
        kb_ref[pl.ds(r0, ROW_BLOCK), :] = jnp.where(odd, k_own, k_oth).astype(BF16)
        v_own = jnp.where((lane // AT_HEAD_DIM) == (g % 2), vv, 0.0)
        v_oth = pltpu.roll(v_own, AT_HEAD_DIM, axis=1)
        va = jnp.where(odd, v_oth, v_own)
        vb = jnp.where(odd, v_own, v_oth)
        va_ref[pl.ds(r0, ROW_BLOCK), :] = jnp.where(lane == AT_HEAD_DIM, 1.0, va).astype(BF16)
        vb_ref[pl.ds(r0, ROW_BLOCK), :] = jnp.where(lane == 0, 1.0, vb).astype(BF16)
        return carry

    lax.fori_loop(0, ta // ROW_BLOCK, prep, 0)
    zeros = jnp.zeros((blk, LANES), BF16)
    for ref in (ka_ref, kb_ref, va_ref, vb_ref):
        ref[ta:ta + blk, :] = zeros

    half = lax.broadcasted_iota(jnp.int32, (2 * blk, 1), 0) < blk
    sink_a = jnp.where(half, sink_ref[g, 0], sink_ref[g, 2]) * LOG2E
    sink_b = jnp.where(half, sink_ref[g, 1], sink_ref[g, 3]) * LOG2E

    qi = lax.broadcasted_iota(jnp.int32, (2 * blk, 3 * blk), 0) % blk
    kj = lax.broadcasted_iota(jnp.int32, (2 * blk, 3 * blk), 1)
    band = (kj - qi >= 0) & (kj - qi <= 2 * blk)
    out_lo = lax.broadcasted_iota(jnp.int32, (2 * blk, LANES), 1) < AT_HEAD_DIM

    nt = (((1,), (1,)), ((), ()))

    def one_side(qt, k_ref_, v_ref_, sink, ones_lane, k0, mask):
        s_ctx = lax.dot_general(qt, k_ref_[0:tc, :], nt, preferred_element_type=F32)
        m = jnp.maximum(sink, jnp.max(s_ctx, axis=-1, keepdims=True))
        if mask is not None:
            s_loc = lax.dot_general(qt, k_ref_[pl.ds(k0, 3 * blk), :], nt, preferred_element_type=F32)
            s_loc = jnp.where(mask, s_loc, -jnp.inf)
            m = jnp.maximum(m, jnp.max(s_loc, axis=-1, keepdims=True))
        acc = jnp.dot(jnp.exp2(s_ctx - m).astype(BF16), v_ref_[0:tc, :], preferred_element_type=F32)
        if mask is not None:
            acc = acc + jnp.dot(jnp.exp2(s_loc - m).astype(BF16), v_ref_[pl.ds(k0, 3 * blk), :],
                                preferred_element_type=F32)
        den = jnp.exp2(sink - m) + acc[:, ones_lane:ones_lane + 1]
        return acc / den

    def block(r0, k0, mask):
        qt = jnp.concatenate([qs_ref[pl.ds(r0, blk), 0:LANES], qs_ref[pl.ds(r0, blk), LANES:2 * LANES]], axis=0)
        o = jnp.where(out_lo, one_side(qt, ka_ref, va_ref, sink_a, AT_HEAD_DIM, k0, mask),
                      one_side(qt, kb_ref, vb_ref, sink_b, 0, k0, mask))
        for c in range(2):
            zc = z_ref[0, pl.ds(r0, blk), c * LANES:(c + 1) * LANES].astype(F32)
            o_ref[0, pl.ds(r0, blk), c * LANES:(c + 1) * LANES] = (o[c * blk:(c + 1) * blk] * _silu(zc)).astype(o_ref.dtype)

    def lat_block(n, carry):
        r0 = pl.multiple_of(tc + n * blk, blk)
        k0 = pl.multiple_of(tc + (n - 1) * blk, blk)
        kpos = (n - 1) * blk + kj
        block(r0, k0, band & (kpos >= 0) & (kpos < t))
        return carry

    lax.fori_loop(0, nb, lat_block, 0, unroll=2)
    if need_ctx:
        for n in range(tc // blk):
            block(n * blk, None, None)
    else:
        o_ref[0, 0:tc, :] = jnp.zeros((tc, o_ref.shape[2]), o_ref.dtype)


def _attn_mixer(p, at_q_g, at_k_g, at_sink, tc, need_ctx):
    b, ta, _ = p.shape
    t = ta - tc
    assert tc >= AT_BLOCK and tc % AT_BLOCK == 0 and t % AT_BLOCK == 0 and ta % ROW_BLOCK == 0
    cos_t, sin_t = _rope_tables(tc, t)
    lane_head = (np.arange(LANES) // AT_HALF) % 2
    same_head = jnp.asarray(lane_head[:, None] == lane_head[None, :], BF16)
    tile_gain = lambda gain: jnp.tile(gain, 2)[AT_TILE_PERM].reshape(1, LANES)
    gw = AT_GROUP * AT_HEAD_DIM
    kcol = AT_Q // LANES
    vcol = (AT_Q + AT_KV) // LANES
    zcol = (AT_Q + 2 * AT_KV) // gw
    return pl.pallas_call(
        functools.partial(_attn_kernel, tc=tc, need_ctx=need_ctx),
        out_shape=jax.ShapeDtypeStruct((b, ta, AT_Q), BF16),
        grid=(b, AT_KV_HEADS),
        in_specs=[pl.BlockSpec(memory_space=pltpu.SMEM),
                  pl.BlockSpec((1, ta, gw), lambda i, g: (i, 0, g)),
                  pl.BlockSpec((1, ta, LANES), lambda i, g: (i, 0, kcol + g // 2)),
                  pl.BlockSpec((1, ta, LANES), lambda i, g: (i, 0, vcol + g // 2)),
                  pl.BlockSpec((1, ta, gw), lambda i, g: (i, 0, zcol + g)),
                  pl.BlockSpec((1, LANES), lambda i, g: (0, 0)),
                  pl.BlockSpec((1, LANES), lambda i, g: (0, 0)),
                  pl.BlockSpec((ta, LANES), lambda i, g: (0, 0)),
                  pl.BlockSpec((ta, LANES), lambda i, g: (0, 0)),
                  pl.BlockSpec((LANES, LANES), lambda i, g: (0, 0))],
        out_specs=pl.BlockSpec((1, ta, gw), lambda i, g: (i, 0, g)),
        scratch_shapes=[pltpu.VMEM((ta, gw), BF16)] + [pltpu.VMEM((ta + AT_BLOCK, LANES), BF16)] * 4,
        compiler_params=_cparams("parallel", "arbitrary"),
        name="attn_mixer",
    )(at_sink.reshape(AT_KV_HEADS, AT_GROUP), p, p, p, p, tile_gain(at_q_g), tile_gain(at_k_g), cos_t, sin_t,
      same_head)


def _attn_weight(w_in):
    d = w_in.shape[0]
    nqk = AT_Q + AT_KV
    qk = w_in[:, :nqk].reshape(d, nqk // LANES, LANES)[:, :, AT_TILE_PERM].reshape(d, nqk)
    return jnp.concatenate([qk, w_in[:, nqk:]], axis=1)


def _chunk_order(nc, ncc, d, s):
    bwd = jnp.where(s < ncc, ncc - 1 - s, nc - 1 - (s - ncc))
    return jnp.where(d == 0, s, bwd)


ML_GATE_PERM = np.concatenate([np.arange(0, 4), np.arange(8, 12), np.arange(4, 8), np.arange(12, 16)])
ML_NQ = 6


def _mlstm_gate_kernel(g_ref, b_ref, o_ref, *, tc, lc):
    h = ML_HEADS
    ta = g_ref.shape[2]
    nc, ncc = ta // lc, tc // lc
    x = g_ref[0] + b_ref[...]
    li = x[0:2 * h]
    lfp = x[2 * h:4 * h]
    lf = jnp.minimum(lfp, 0.0) - jnp.log1p(jnp.exp(-jnp.abs(lfp)))
    fwd = lax.broadcasted_iota(jnp.int32, (2 * h, ta), 0) < h
    fwd1 = lax.broadcasted_iota(jnp.int32, (2 * h, 1), 0) < h
    pos = lax.broadcasted_iota(jnp.int32, (2 * h, ta), 1) % lc

    def seg_scan(y, op, fill):
        yf, yb = y, y
        s = 1
        while s < lc:
            yf = op(yf, jnp.where(pos >= s, pltpu.roll(yf, s, axis=1), fill))
            yb = op(yb, jnp.where(pos < lc - s, pltpu.roll(yb, ta - s, axis=1), fill))
            s *= 2
        return jnp.where(fwd, yf, yb)

    bsum = seg_scan(lf, jnp.add, 0.0)
    a = li - bsum
    cmax = seg_scan(a, jnp.maximum, -jnp.inf)

    def end_col(y, c):
        return jnp.where(fwd1, y[:, (c + 1) * lc - 1:(c + 1) * lc], y[:, c * lc:c * lc + 1])

    tot = [end_col(bsum, c) for c in range(nc)]
    amax = [end_col(cmax, c) for c in range(nc)]

    def chain(order):
        m = jnp.zeros((2 * h, 1), F32)
        m_in = [None] * nc
        for c in order:
            m_in[c] = m
            m = tot[c] + jnp.maximum(m, amax[c])
        return m_in

    m_f = chain(list(range(nc)))
    m_b = chain(list(range(ncc - 1, -1, -1)) + list(range(nc - 1, ncc - 1, -1)))
    for c in range(nc):
        m_in = jnp.where(fwd1, m_f[c], m_b[c])
        sl = slice(c * lc, (c + 1) * lc)
        g_run = jnp.maximum(m_in, cmax[:, sl])
        g_end = jnp.maximum(m_in, amax[c])
        nr = 2 * h
        quantities = (a[:, sl], g_run, jnp.exp(m_in - g_run), jnp.exp(-(bsum[:, sl] + g_run)),
                      jnp.exp(a[:, sl] - g_end), jnp.broadcast_to(jnp.exp(m_in - g_end), (nr, lc)))
        for qi, val in enumerate(quantities):
            o_ref[0, c, qi * nr:(qi + 1) * nr, :] = val
        o_ref[0, c, ML_NQ * nr:, :] = jnp.zeros((o_ref.shape[2] - ML_NQ * nr, lc), F32)


def _mlstm_scan_kernel(q_ref, k_ref, v_ref, a_ref, c_ref, o_ref, cs_ref, *, rev):
    lc = q_ref.shape[1]
    nr = 2 * ML_HEADS

    @pl.when(pl.program_id(1) == 0)
    def _():
        cs_ref[...] = jnp.zeros(cs_ref.shape, F32)

    ti = lax.broadcasted_iota(jnp.int32, (lc, lc), 0)
    si = lax.broadcasted_iota(jnp.int32, (lc, lc), 1)
    mask = (si >= ti) if rev else (si <= ti)
    ones = jnp.ones((lc, LANES), BF16)
    nt = (((1,), (1,)), ((), ()))
    tn = (((0,), (0,)), ((), ()))
    for h in range(ML_HEADS):
        r = (ML_HEADS if rev else 0) + h
        qb = q_ref[0, :, h * ML_DK:(h + 1) * ML_DK]
        kb = k_ref[0, :, h * ML_DK:(h + 1) * ML_DK]
        v_aug = jnp.concatenate([v_ref[0, :, h * ML_DV:(h + 1) * ML_DV], ones], axis=1)
        a_row = a_ref[0, 0, r:r + 1, :]
        g_run, inter, eclamp, w, decay = (c_ref[0, 0, :, qi * nr + r:qi * nr + r + 1] for qi in range(1, 6))
        qk = lax.dot_general(qb, kb, nt, preferred_element_type=F32)
        s = qk * jnp.where(mask, jnp.exp(a_row - g_run), 0.0)
        c_old = cs_ref[h]
        tot = inter * jnp.dot(qb, c_old.astype(BF16), preferred_element_type=F32) \
            + jnp.dot(s.astype(BF16), v_aug, preferred_element_type=F32)
        inv = 1.0 / jnp.maximum(jnp.abs(tot[:, ML_DV:]), eclamp)
        o_ref[0, :, h * ML_DV:(h + 1) * ML_DV] = (
            tot[:, :ML_DV] * jnp.concatenate([inv] * (ML_DV // LANES), axis=1)).astype(o_ref.dtype)
        kw = (kb.astype(F32) * w).astype(BF16)
        cs_ref[h] = decay[0:1, :] * c_old + lax.dot_general(kw, v_aug, tn, preferred_element_type=F32)


def _mlstm_scan(p, gates, gate_b, tc):
    b, ta, _ = p.shape
    lc = ML_CHUNK
    assert tc % lc == 0 and ta % lc == 0
    nc, ncc = ta // lc, tc // lc
    ng = 4 * ML_HEADS
    nlane = SUBLANES * 2 * ML_HEADS
    g_t = gates[:, :, :ng].transpose(0, 2, 1)
    bias = gate_b.reshape(ng)[ML_GATE_PERM].reshape(ng, 1)
    gp = pl.pallas_call(
        functools.partial(_mlstm_gate_kernel, tc=tc, lc=lc),
        out_shape=jax.ShapeDtypeStruct((b, nc, nlane, lc), F32),
        grid=(b,),
        in_specs=[pl.BlockSpec((1, ng, ta), lambda i: (i, 0, 0)),
                  pl.BlockSpec((ng, 1), lambda i: (0, 0))],
        out_specs=pl.BlockSpec((1, nc, nlane, lc), lambda i: (i, 0, 0, 0)),
        compiler_params=_cparams("parallel"),
        name="mlstm_gates",
    )(g_t, bias)
    cols = gp.transpose(0, 1, 3, 2)
    outs = []
    for d in range(2):
        chunk = functools.partial(_chunk_order, nc, ncc, d)
        outs.append(pl.pallas_call(
            functools.partial(_mlstm_scan_kernel, rev=bool(d)),
            out_shape=jax.ShapeDtypeStruct((b, ta, ML_INNER), BF16),
            grid=(b, nc),
            in_specs=[pl.BlockSpec((1, lc, ML_QK), lambda i, s, chunk=chunk: (i, chunk(s), 0)),
                      pl.BlockSpec((1, lc, ML_QK), lambda i, s, chunk=chunk: (i, chunk(s), 1)),
                      pl.BlockSpec((1, lc, ML_INNER), lambda i, s, chunk=chunk: (i, chunk(s), 1)),
                      pl.BlockSpec((1, 1, 2 * ML_HEADS, lc), lambda i, s, chunk=chunk: (i, chunk(s), 0, 0)),
                      pl.BlockSpec((1, 1, lc, nlane), lambda i, s, chunk=chunk: (i, chunk(s), 0, 0))],
            out_specs=pl.BlockSpec((1, lc, ML_INNER), lambda i, s, chunk=chunk: (i, chunk(s), 0)),
            scratch_shapes=[pltpu.VMEM((ML_HEADS, ML_DK, ML_DV + LANES), F32)],
            compiler_params=_cparams("parallel", "arbitrary"),
            name="mlstm_scan_bwd" if d else "mlstm_scan_fwd",
        )(p, p, p, gp, cols))
    return outs


def _hgrn_lb_kernel(p_ref, o_ref, *, layer):
    for d in range(p_ref.shape[0]):
        x = p_ref[d]
        e = jnp.exp(x - jnp.max(x, axis=0, keepdims=True))
        p = e / jnp.sum(e, axis=0, keepdims=True)
        acc = jnp.zeros((1, x.shape[1]), F32)
        for j in range(1, layer + 1):
            acc = acc + p[j:j + 1, :]
        o_ref[d:d + 1, :] = acc


def _cumsum_rows(x, tri2):
    hi = x.astype(BF16)
    lo = (x - hi.astype(F32)).astype(BF16)
    return jnp.dot(tri2, jnp.concatenate([hi, lo], axis=0), preferred_element_type=F32)


def _tri2(n, rev):
    t = np.triu(np.ones((n, n), np.float32)) if rev else np.tril(np.ones((n, n), np.float32))
    return jnp.asarray(np.concatenate([t, t], axis=1), BF16)


def _anchor_rows(a, m, rev):
    n, f = a.shape
    idx = m if rev else m - 1
    if 2 * m >= SUBLANES:
        a3 = a.reshape(n // (2 * m), 2 * m, f)
        return jnp.broadcast_to(a3[:, idx:idx + 1, :], a3.shape).reshape(n, f)
    a3 = a.reshape(n // SUBLANES, SUBLANES, f)
    sub = lax.broadcasted_iota(jnp.int32, a3.shape, 1)
    out = None
    for gi in range(SUBLANES // (2 * m) - 1, -1, -1):
        cand = jnp.broadcast_to(a3[:, gi * 2 * m + idx:gi * 2 * m + idx + 1, :], a3.shape)
        out = cand if out is None else jnp.where(sub < (gi + 1) * 2 * m, cand, out)
    return out.reshape(n, f)


HG_MAX_LOG_SPAN = 64.0


def _hgrn_scan_kernel(q_ref, f_ref, i_ref, tri_ref, o_ref, s_ref, *, rev):
    lc = HG_CHUNK
    subs = range(q_ref.shape[1] // lc)
    subs = tuple(reversed(subs)) if rev else tuple(subs)

    @pl.when(pl.program_id(1) == 0)
    def _():
        s_ref[...] = jnp.zeros(s_ref.shape, F32)

    row = lax.broadcasted_iota(jnp.int32, (lc, 1), 0)
    ti = lax.broadcasted_iota(jnp.int32, (lc, lc), 0)
    si = lax.broadcasted_iota(jnp.int32, (lc, lc), 1)
    nt = (((1,), (1,)), ((), ()))
    tn = (((0,), (0,)), ((), ()))
    end = 0 if rev else lc - 1

    def all_heads_single_anchor():
        causal = (si >= ti) if rev else (si <= ti)
        cols = [slice(h * HG_DIM, (h + 1) * HG_DIM) for h in range(HG_HEADS)]
        pre = {}
        for ci in subs:
            rows = slice(ci * lc, (ci + 1) * lc)
            f = f_ref[0, rows, :]
            a = _cumsum_rows(jnp.log(f), tri_ref[...])
            q_dec = (q_ref[0, rows, :].astype(F32) * jnp.exp(a)).astype(BF16)
            k_inv = (1.0 - f) * jnp.exp(-a)
            k_inv_b = k_inv.astype(BF16)
            e_end = jnp.exp(a[end:end + 1, :])
            kd = (k_inv * e_end).astype(BF16)
            attn = [jnp.where(causal, lax.dot_general(q_dec[:, c], k_inv_b[:, c], nt, preferred_element_type=F32),
                              0.0).astype(BF16) for c in cols]
            pre[ci] = (rows, q_dec, e_end, kd, attn)
        state = [s_ref[h] for h in range(HG_HEADS)]
        for ci in subs:
            rows, q_dec, e_end, kd, attn = pre[ci]
            for h, c in enumerate(cols):
                o = jnp.dot(attn[h], i_ref[0, rows, c], preferred_element_type=F32)
                o_ref[0, rows, c] = o + lax.dot_general(q_dec[:, c], state[h].astype(BF16), nt,
                                                        preferred_element_type=F32)
            state = [state[h] * e_end[:, c] + lax.dot_general(i_ref[0, rows, c], kd[:, c], tn,
                                                              preferred_element_type=F32)
                     for h, c in enumerate(cols)]
        for h in range(HG_HEADS):
            s_ref[h] = state[h]

    def head_per_level(r0, h, carry):
        c0 = pl.multiple_of(h * HG_DIM, HG_DIM)
        q = q_ref[0, r0:r0 + lc, pl.ds(c0, HG_DIM)].astype(F32)
        f = f_ref[0, r0:r0 + lc, pl.ds(c0, HG_DIM)]
        k = 1.0 - f
        iv = i_ref[0, r0:r0 + lc, pl.ds(c0, HG_DIM)].astype(F32)
        a = _cumsum_rows(jnp.log(f), tri_ref[...])
        attn = jnp.zeros((lc, lc), F32)
        m = 1
        while m < lc:
            e = jnp.exp(-jnp.abs(a - _anchor_rows(a, m, rev)))
            upper = (row % (2 * m)) >= m
            is_q = jnp.logical_not(upper) if rev else upper
            qt = jnp.where(is_q, q * e, 0.0).astype(BF16)
            kt = jnp.where(is_q, 0.0, k * e).astype(BF16)
            pair = lax.dot_general(qt, kt, nt, preferred_element_type=F32)
            attn = attn + jnp.where((ti // (2 * m)) == (si // (2 * m)), pair, 0.0)
            m *= 2
        a_end = a[end:end + 1, :]
        kd = (k * jnp.exp(a_end - a)).astype(BF16)
        ib = iv.astype(BF16)
        s_old = s_ref[h]
        o = jnp.dot(attn.astype(BF16), ib, preferred_element_type=F32)
        o = o + lax.dot_general((q * jnp.exp(a)).astype(BF16), s_old.astype(BF16), nt, preferred_element_type=F32)
        o_ref[0, r0:r0 + lc, pl.ds(c0, HG_DIM)] = o + jnp.sum(q * k, axis=-1, keepdims=True) * iv
        s_ref[h] = s_old * jnp.exp(a_end) + lax.dot_general(ib, kd, tn, preferred_element_type=F32)
        return carry

    f_min = f_ref[0, :, 0:HG_DIM]
    for h in range(1, HG_HEADS):
        f_min = jnp.minimum(f_min, f_ref[0, :, h * HG_DIM:(h + 1) * HG_DIM])
    f_min = jnp.min(f_min.reshape(f_min.shape[0] // SUBLANES, SUBLANES, HG_DIM), axis=0)
    in_range = jnp.min(f_min) >= float(np.exp(-HG_MAX_LOG_SPAN / lc))

    @pl.when(in_range)
    def _():
        all_heads_single_anchor()

    @pl.when(jnp.logical_not(in_range))
    def _():
        for ci in subs:
            lax.fori_loop(0, HG_HEADS, functools.partial(head_per_level, ci * lc), 0)


def _hgrn_scan(p, pf, tc, rev):
    b, ta, _ = p.shape
    lc = HG_CHUNK
    rows = HG_STEP_CHUNKS * lc
    hg = HG_HEADS * HG_DIM
    assert tc % rows == 0 and ta % rows == 0
    nc, ncc = ta // rows, tc // rows
    d = 1 if rev else 0
    chunk = lambda s: _chunk_order(nc, ncc, d, s)
    return pl.pallas_call(
        functools.partial(_hgrn_scan_kernel, rev=rev),
        out_shape=jax.ShapeDtypeStruct((b, ta, hg), F32),
        grid=(b, nc),
        in_specs=[pl.BlockSpec((1, rows, hg), lambda i, s: (i, chunk(s), 0)),
                  pl.BlockSpec((1, rows, hg), lambda i, s: (i, chunk(s), d)),
                  pl.BlockSpec((1, rows, hg), lambda i, s: (i, chunk(s), 1)),
                  pl.BlockSpec((lc, 2 * lc), lambda i, s: (0, 0))],
        out_specs=pl.BlockSpec((1, rows, hg), lambda i, s: (i, chunk(s), 0)),
        scratch_shapes=[pltpu.VMEM((HG_HEADS, HG_DIM, HG_DIM), F32)],
        compiler_params=_cparams("parallel", "arbitrary"),
        name="hgrn_scan_bwd" if rev else "hgrn_scan_fwd",
    )(p, pf, p, _tri2(lc, rev))


def kernel(x, c, ctx, c_ctx, ada_w, ada_b, norm_g, ml_w_in, ml_gate_b, ml_head_g, ml_w_out, at_w_in, at_q_g, at_k_g, at_sink, at_w_out, sc_w_in, sc_conv_w, sc_conv_b, sc_w_out, hg_w_in, hg_f_b, hg_lb, hg_head_g, hg_w_out):
    tc = ctx.shape[1]
    mod = _ada_mod(c, c_ctx, ada_w, ada_b)
    xs = (ctx, x)
    for layer in range(DEPTH):
        kind, j = layer % 4, layer // 4
        need_ctx = layer < DEPTH - 1
        last = dict(lat_only=True) if layer == DEPTH - 1 else {}
        mod_l = mod[layer]
        if kind == 0:
            n_main = 2 * ML_QK + 3 * ML_INNER
            w_gate = jnp.zeros((ml_w_in.shape[1], LANES), F32).at[:, :4 * ML_HEADS].set(
                ml_w_in[j][:, n_main:][:, ML_GATE_PERM])
            w_main = ml_w_in[j][:, :n_main].at[:, :ML_QK].multiply(ML_DK ** -0.5)
            p, gates = _inproj(xs, mod_l, norm_g[layer], w_main, tc, tn=1024, w_small=w_gate, out_dtype=BF16)
            h_f, h_b = _mlstm_scan(p, gates, ml_gate_b[j], tc)
            feats = [(h_f, ML_INNER, 0), (h_b, ML_INNER, 0), (p, ML_INNER, 2), (p, ML_INNER, 3)]
            xs = _outproj("mlstm", feats, ml_w_out[j], xs, mod_l, tc, head_g=ml_head_g[j], **last)
        elif kind == 1:
            p = _inproj(xs, mod_l, norm_g[layer], _attn_weight(at_w_in[j]), tc, out_dtype=BF16)
            u = _attn_mixer(p, at_q_g[j], at_k_g[j], at_sink[j], tc, need_ctx)
            xs = _outproj("plain", [(u, AT_Q, 0)], at_w_out[j], xs, mod_l, tc, **last)
        elif kind == 2:
            if isinstance(xs, tuple):
                xs = jnp.concatenate(xs, axis=1)
            u = _conv_mixer(xs, mod_l, norm_g[layer], sc_w_in[j], sc_conv_w[j], sc_conv_b[j], tc)
            xs = _outproj("plain", [(u, u.shape[2], 0)], sc_w_out[j], xs, mod_l, tc, **last)
        else:
            hg = HG_HEADS * HG_DIM
            lb = pl.pallas_call(
                functools.partial(_hgrn_lb_kernel, layer=layer),
                out_shape=jax.ShapeDtypeStruct((2, hg), F32),
                name="hgrn_lb",
            )(hg_lb[j])
            w = hg_w_in[j]
            w_qiz = jnp.concatenate([w[:, :hg], w[:, 3 * hg:]], axis=1)
            p = _inproj(xs, mod_l, norm_g[layer], w_qiz, tc, tn=1024, out_dtype=BF16,
                        epilogue=((0, hg, "silu"), (hg, 3 * hg, "id")))
            pf = _inproj(xs, mod_l, norm_g[layer], w[:, hg:3 * hg], tc, epilogue=((0, 2 * hg, "fgate"),),
                         pa=lb.reshape(2 * hg), pb=hg_f_b[j].reshape(2 * hg))
            o_f = _hgrn_scan(p, pf, tc, rev=False)
            o_b = _hgrn_scan(p, pf, tc, rev=True)
            feats = [(o_f, hg, 0), (o_b, hg, 0), (p, hg, 2)]
            xs = _outproj("hgrn", feats, hg_w_out[j], xs, mod_l, tc, head_g=hg_head_g[j], **last)
    return xs
```

```python
import functools

import numpy as np
import jax
import jax.numpy as jnp
from jax import lax
from jax.experimental import pallas as pl
from jax.experimental.pallas import tpu as pltpu

F32 = jnp.float32
BF16 = jnp.bfloat16
EPS = 1e-6
DEPTH = 4
GRID_W = 64
ROPE_BASE = 10000.0

ML_HEADS, ML_DK, ML_DV = 4, 256, 512
ML_QK = ML_HEADS * ML_DK
ML_INNER = ML_HEADS * ML_DV
ML_CHUNK = 256

AT_HEADS, AT_KV_HEADS, AT_HEAD_DIM = 16, 4, 64
AT_GROUP = AT_HEADS // AT_KV_HEADS
AT_BLOCK = 128
AT_Q = AT_HEADS * AT_HEAD_DIM
AT_KV = AT_KV_HEADS * AT_HEAD_DIM

SC_KSIZE = 3

HG_HEADS, HG_DIM = 8, 128
HG_CHUNK = 128
HG_STEP_CHUNKS = 2

LANES = 128
SUBLANES = 8
VMEM_LIMIT_BYTES = 56 * 1024 * 1024

ROW_BLOCK = 256
OUT_ROWS = 768
OUT_ROWS_SPLIT = 256


def _cparams(*sem):
    return pltpu.CompilerParams(dimension_semantics=sem, vmem_limit_bytes=VMEM_LIMIT_BYTES)


def _sigmoid(x):
    return 0.5 * jnp.tanh(0.5 * x) + 0.5


def _silu(x):
    return x * _sigmoid(x)


def _silu_of_half(xh):
    return xh * (1.0 + jnp.tanh(xh))


def _ada_kernel(c_ref, w_ref, b_ref, o_ref):
    s = _silu(c_ref[...])
    o_ref[0] = jnp.dot(s.astype(BF16), w_ref[0].astype(BF16), preferred_element_type=F32) + b_ref[0]


def _ada_mod(c, c_ctx, ada_w, ada_b):
    b, d = c.shape
    depth = ada_w.shape[0]
    rows = -(-(b + 1) // SUBLANES) * SUBLANES
    cc = jnp.zeros((rows, d), F32).at[:b].set(c).at[b].set(c_ctx)
    tn = 1024
    out = pl.pallas_call(
        _ada_kernel,
        out_shape=jax.ShapeDtypeStruct((depth, rows, 3 * d), F32),
        grid=(depth, 3 * d // tn),
        in_specs=[pl.BlockSpec((rows, d), lambda l, j: (0, 0)),
                  pl.BlockSpec((1, d, tn), lambda l, j: (l, 0, j)),
                  pl.BlockSpec((1, 1, tn), lambda l, j: (l, 0, j))],
        out_specs=pl.BlockSpec((1, rows, tn), lambda l, j: (l, 0, j)),
        compiler_params=_cparams("parallel", "parallel"),
        name="ada_mod",
    )(cc, ada_w, ada_b.reshape(depth, 1, 3 * d))
    return out.reshape(depth, rows, 3, d)


def _modulated(x_refs, ml_ref, mc_ref, g_ref, h_ref, tc):
    ta = h_ref.shape[0]
    g = g_ref[...]
    xc_ref = x_refs[0]
    xl_ref, lat0 = (x_refs[1], 0) if len(x_refs) == 2 else (x_refs[0], tc)

    def rows(x_ref, src0, dst0, m_ref):
        x = x_ref[0, pl.ds(src0, ROW_BLOCK), :]
        ms = jnp.mean(x * x, axis=-1, keepdims=True)
        xn = x * lax.rsqrt(ms + EPS) * g
        h = xn * (1.0 + m_ref[0, 1:2, :]) + m_ref[0, 0:1, :]
        h_ref[pl.ds(dst0, ROW_BLOCK), :] = h.astype(BF16)

    for i in range(tc // ROW_BLOCK):
        rows(xc_ref, i * ROW_BLOCK, i * ROW_BLOCK, mc_ref)

    def body(i, carry):
        r0 = pl.multiple_of(i * ROW_BLOCK, ROW_BLOCK)
        rows(xl_ref, lat0 + r0, tc + r0, ml_ref)
        return carry

    lax.fori_loop(0, (ta - tc) // ROW_BLOCK, body, 0)


def _split_stream(xs):
    arrs = list(xs) if isinstance(xs, tuple) else [xs]
    specs = [pl.BlockSpec((1,) + a.shape[1:], lambda i, j: (i, 0, 0)) for a in arrs]
    ta = sum(a.shape[1] for a in arrs)
    return arrs, specs, ta


def _inproj_kernel(*refs, tc, n_x, has_small, epilogue):
    x_refs, (ml_ref, mc_ref, g_ref, w_ref), rest = refs[:n_x], refs[n_x:n_x + 4], list(refs[n_x + 4:])
    ws_ref = rest.pop(0) if has_small else None
    par_refs = [rest.pop(0) for _ in range(3)] if epilogue else None
    o_ref = rest.pop(0)
    os_ref = rest.pop(0) if has_small else None
    h_ref = rest.pop(0)

    @pl.when(pl.program_id(1) == 0)
    def _():
        _modulated(x_refs, ml_ref, mc_ref, g_ref, h_ref, tc)
        if has_small:
            os_ref[0] = lax.dot_general(ws_ref[...], h_ref[...], (((1,), (1,)), ((), ())),
                                        preferred_element_type=F32)

    acc = jnp.dot(h_ref[...], w_ref[...], preferred_element_type=F32)
    if not epilogue:
        o_ref[0] = acc.astype(o_ref.dtype)
        return
    bias_ref, c0_ref, c1_ref = par_refs
    j = pl.program_id(1)
    t = jnp.tanh(acc + bias_ref[...])
    out = acc
    for lo, hi, kind in epilogue:
        if kind == "id":
            continue
        val = acc + acc * t if kind == "silu" else c0_ref[...] + c1_ref[...] * t
        out = jnp.where((j >= lo) & (j < hi), val, out)
    o_ref[0] = out.astype(o_ref.dtype)


def _inproj(xs, mod_l, norm_g, w, tc, tn=512, w_small=None, out_dtype=F32, epilogue=None, pars=None):
    x_arrs, x_specs, ta = _split_stream(xs)
    b, d = x_arrs[0].shape[0], x_arrs[0].shape[2]
    n = w.shape[1]
    assert n % tn == 0 and tc % ROW_BLOCK == 0 and (ta - tc) % ROW_BLOCK == 0
    mod_lat = mod_l[:b]
    mod_ctx = mod_l[b:b + 1]
    has_small = w_small is not None
    in_specs = x_specs + [pl.BlockSpec((1, 3, d), lambda i, j: (i, 0, 0)),
                          pl.BlockSpec((1, 3, d), lambda i, j: (0, 0, 0)),
                          pl.BlockSpec((1, d), lambda i, j: (0, 0)),
                          pl.BlockSpec((d, tn), lambda i, j: (0, j))]
    args = x_arrs + [mod_lat, mod_ctx, norm_g.reshape(1, d), w.astype(BF16)]
    out_shape = [jax.ShapeDtypeStruct((b, ta, n), out_dtype)]
    out_specs = [pl.BlockSpec((1, ta, tn), lambda i, j: (i, 0, j))]
    if has_small:
        ns = w_small.shape[0]
        in_specs.append(pl.BlockSpec((ns, d), lambda i, j: (0, 0)))
        args.append(w_small.astype(BF16))
    if epilogue:
        assert all(lo % tn == 0 and hi % tn == 0 for lo, hi, _ in epilogue)
        epilogue = tuple((lo // tn, hi // tn, kind) for lo, hi, kind in epilogue)
        zeros = jnp.zeros((n,), F32)
        for par in (pars if pars is not None else (zeros,) * 3):
            in_specs.append(pl.BlockSpec((1, tn), lambda i, j: (0, j)))
            args.append(par.reshape(1, n))
    if has_small:
        out_shape.append(jax.ShapeDtypeStruct((b, ns, ta), F32))
        out_specs.append(pl.BlockSpec((1, ns, ta), lambda i, j: (i, 0, 0)))
    res = pl.pallas_call(
        functools.partial(_inproj_kernel, tc=tc, n_x=len(x_arrs), has_small=has_small, epilogue=epilogue),
        out_shape=out_shape,
        grid=(b, n // tn),
        in_specs=in_specs,
        out_specs=out_specs,
        scratch_shapes=[pltpu.VMEM((ta, d), BF16)],
        compiler_params=_cparams("parallel", "arbitrary"),
        name="inproj",
    )(*args)
    return res if has_small else res[0]


def _head_rms(h, gain, n_heads):
    dh = h.shape[1] // n_heads
    parts = []
    for i in range(n_heads):
        hh = h[:, i * dh:(i + 1) * dh]
        ms = jnp.mean(hh * hh, axis=-1, keepdims=True)
        parts.append(hh * lax.rsqrt(ms + EPS))
    return jnp.concatenate(parts, axis=1) * gain


def _outproj_kernel(*refs, mode, n_feat, n_x, tc, tm, row0):
    feats, rest = refs[:n_feat], list(refs[n_feat:])
    hg_ref = rest.pop(0) if mode != "plain" else None
    w_ref = rest.pop(0)
    x_refs = [rest.pop(0) for _ in range(n_x)]
    ml_ref, mc_ref, o_ref = rest
    if mode == "plain":
        u = feats[0][0]
    elif mode == "mlstm":
        h0_ref, h1_ref, og_ref, z_ref = feats
        hn = _head_rms(h0_ref[0].astype(F32) + h1_ref[0].astype(F32), hg_ref[...], ML_HEADS)
        u = (hn * (1.0 + jnp.tanh(og_ref[0].astype(F32))) * _silu_of_half(z_ref[0].astype(F32))).astype(BF16)
    else:
        h0_ref, h1_ref, z_ref = feats
        hn = _head_rms(h0_ref[0] + h1_ref[0], hg_ref[...], HG_HEADS)
        u = (hn * _silu_of_half(z_ref[0].astype(F32))).astype(BF16)
    y = jnp.dot(u, w_ref[...], preferred_element_type=F32)
    first = row0 + pl.program_id(1) * tm
    row = first + lax.broadcasted_iota(jnp.int32, (tm, 1), 0)
    gate = jnp.where(row < tc, mc_ref[0, 2:3, :], ml_ref[0, 2:3, :])
    x = x_refs[0][0] if n_x == 1 else jnp.where(first < tc, x_refs[0][0], x_refs[1][0])
    o_ref[0] = x + gate * y


def _outproj(mode, feats, w_out, xs, mod_l, tc, head_g=None, lat_only=False):
    x_arrs = list(xs) if isinstance(xs, tuple) else [xs]
    b, d = x_arrs[0].shape[0], x_arrs[0].shape[2]
    ta = sum(a.shape[1] for a in x_arrs)
    kdim = w_out.shape[0]
    tm = OUT_ROWS if (len(x_arrs) == 1 and not lat_only and ta % OUT_ROWS == 0) else OUT_ROWS_SPLIT
    assert ta % tm == 0 and (tc % tm == 0 or (len(x_arrs) == 1 and not lat_only))
    off = tc // tm if lat_only else 0
    nct = tc // tm
    row_spec = lambda width, col: pl.BlockSpec((1, tm, width), lambda i, r, col=col: (i, r + off, col))
    in_specs, args = [], []
    for arr, width, col in feats:
        in_specs.append(row_spec(width, col))
        args.append(arr)
    if head_g is not None:
        in_specs.append(pl.BlockSpec((1, kdim), lambda i, r: (0, 0)))
        args.append(head_g.reshape(1, kdim))
    in_specs.append(pl.BlockSpec((kdim, d), lambda i, r: (0, 0)))
    args.append(w_out.astype(BF16))
    if len(x_arrs) == 1:
        in_specs.append(row_spec(d, 0))
    elif lat_only:
        x_arrs = x_arrs[1:]
        in_specs.append(pl.BlockSpec((1, tm, d), lambda i, r: (i, r, 0)))
    else:
        in_specs += [pl.BlockSpec((1, tm, d), lambda i, r: (i, jnp.minimum(r, nct - 1), 0)),
                     pl.BlockSpec((1, tm, d), lambda i, r: (i, jnp.maximum(r - nct, 0), 0))]
    args += x_arrs
    in_specs += [pl.BlockSpec((1, 3, d), lambda i, r: (i, 0, 0)),
                 pl.BlockSpec((1, 3, d), lambda i, r: (0, 0, 0))]
    args += [mod_l[:b], mod_l[b:b + 1]]
    rows_out = ta - off * tm
    return pl.pallas_call(
        functools.partial(_outproj_kernel, mode=mode, n_feat=len(feats), n_x=len(x_arrs), tc=tc, tm=tm,
                          row0=off * tm),
        out_shape=jax.ShapeDtypeStruct((b, rows_out, d), F32),
        grid=(b, rows_out // tm),
        in_specs=in_specs,
        out_specs=pl.BlockSpec((1, tm, d), lambda i, r: (i, r, 0)),
        compiler_params=_cparams("parallel", "parallel"),
        name="outproj_" + mode,
    )(*args)


def _conv_kernel(x_ref, ml_ref, mc_ref, g_ref, w_ref, cw_ref, cb_ref, o_ref, h_ref, *, tc, tw):
    @pl.when(pl.program_id(1) == 0)
    def _():
        _modulated((x_ref,), ml_ref, mc_ref, g_ref, h_ref, tc)

    ta = h_ref.shape[0]
    p = jnp.dot(h_ref[...], w_ref[0], preferred_element_type=F32)
    xin, bg, cg, z = (p[:, i * tw:(i + 1) * tw] for i in range(4))
    u = cg * xin
    row = lax.broadcasted_iota(jnp.int32, (ta, 1), 0)
    first = (row == 0) | (row == tc)
    last = (row == tc - 1) | (row == ta - 1)
    u_prev = jnp.where(first, 0.0, pltpu.roll(u, 1, axis=0))
    u_next = jnp.where(last, 0.0, pltpu.roll(u, ta - 1, axis=0))
    cw = cw_ref[0]
    y = u_prev * cw[0:1, :] + u * cw[1:2, :] + u_next * cw[2:3, :] + cb_ref[0]
    o_ref[0] = (bg * y * _silu_of_half(z)).astype(o_ref.dtype)


def _conv_mixer(xs, mod_l, norm_g, w_in, conv_w, conv_b, tc, tw=256):
    b, ta, d = xs.shape
    e = conv_w.shape[1]
    nt = e // tw
    half_z = jnp.asarray([1.0, 1.0, 1.0, 0.5], F32).reshape(1, 4, 1, 1)
    w = (w_in.reshape(d, 4, nt, tw) * half_z).transpose(2, 0, 1, 3).reshape(nt, d, 4 * tw).astype(BF16)
    cw = conv_w.reshape(SC_KSIZE, nt, tw).transpose(1, 0, 2)
    cb = conv_b.reshape(nt, 1, tw)
    return pl.pallas_call(
        functools.partial(_conv_kernel, tc=tc, tw=tw),
        out_shape=jax.ShapeDtypeStruct((b, ta, e), BF16),
        grid=(b, nt),
        in_specs=[pl.BlockSpec((1, ta, d), lambda i, j: (i, 0, 0)),
                  pl.BlockSpec((1, 3, d), lambda i, j: (i, 0, 0)),
                  pl.BlockSpec((1, 3, d), lambda i, j: (0, 0, 0)),
                  pl.BlockSpec((1, d), lambda i, j: (0, 0)),
                  pl.BlockSpec((1, d, 4 * tw), lambda i, j: (j, 0, 0)),
                  pl.BlockSpec((1, SC_KSIZE, tw), lambda i, j: (j, 0, 0)),
                  pl.BlockSpec((1, 1, tw), lambda i, j: (j, 0, 0))],
        out_specs=pl.BlockSpec((1, ta, tw), lambda i, j: (i, 0, j)),
        scratch_shapes=[pltpu.VMEM((ta, d), BF16)],
        compiler_params=_cparams("parallel", "arbitrary"),
        name="conv_mixer",
    )(xs, mod_l[:b], mod_l[b:b + 1], norm_g.reshape(1, d), w, cw, cb)


AT_HALF = AT_HEAD_DIM // 2
AT_TILE_PERM = np.concatenate([np.arange(0, AT_HALF), np.arange(2 * AT_HALF, 3 * AT_HALF),
                               np.arange(AT_HALF, 2 * AT_HALF), np.arange(3 * AT_HALF, 4 * AT_HALF)])
LOG2E = float(np.log2(np.e))


def _rope_tables(tc, t):
    rows = t // GRID_W
    row = np.repeat(np.arange(rows), GRID_W).astype(np.float32)
    col = np.tile(np.arange(GRID_W), rows).astype(np.float32)
    n_freq = AT_HEAD_DIM // 4
    freqs = jnp.power(ROPE_BASE, -jnp.arange(n_freq, dtype=F32) / n_freq)
    ang = jnp.concatenate([jnp.asarray(row)[:, None] * freqs, jnp.asarray(col)[:, None] * freqs], axis=-1)
    cos, sin = jnp.cos(ang), jnp.sin(ang)
    cos_t = jnp.concatenate([jnp.ones((tc, LANES), F32), jnp.tile(cos, (1, 4))], axis=0)
    sin_t = jnp.concatenate([jnp.zeros((tc, LANES), F32), jnp.concatenate([-sin, -sin, sin, sin], axis=-1)], axis=0)
    return cos_t, sin_t


def _norm_rope_pair(x, gain, cos, sin, same_head):
    x2 = x * x
    hi = x2.astype(BF16)
    lo = (x2 - hi.astype(F32)).astype(BF16)
    ss = jnp.dot(hi, same_head, preferred_element_type=F32) + jnp.dot(lo, same_head, preferred_element_type=F32)
    xn = x * lax.rsqrt(ss * (1.0 / AT_HEAD_DIM) + EPS) * gain
    return xn * cos + pltpu.roll(xn, 2 * AT_HALF, axis=1) * sin


def _attn_kernel(sink_ref, q_ref, k_ref, v_ref, z_ref, qg_ref, kg_ref, cos_ref, sin_ref, sh_ref, o_ref,
                 qs_ref, ka_ref, kb_ref, va_ref, vb_ref, *, tc, need_ctx):
    ta = q_ref.shape[1]
    t = ta - tc
    nb = t // AT_BLOCK
    blk = AT_BLOCK
    g = pl.program_id(1)
    odd = (g % 2) == 1
    scale = (AT_HEAD_DIM ** -0.5) * LOG2E

    def prep(i, carry):
        r0 = pl.multiple_of(i * ROW_BLOCK, ROW_BLOCK)
        cos = cos_ref[pl.ds(r0, ROW_BLOCK), :]
        sin = sin_ref[pl.ds(r0, ROW_BLOCK), :]
        same_head = sh_ref[...]
        for c in range(2):
            qc = _norm_rope_pair(q_ref[0, pl.ds(r0, ROW_BLOCK), c * LANES:(c + 1) * LANES].astype(F32), qg_ref[...],
                                 cos, sin, same_head)
            qs_ref[pl.ds(r0, ROW_BLOCK), c * LANES:(c + 1) * LANES] = (qc * scale).astype(BF16)
        kn = _norm_rope_pair(k_ref[0, pl.ds(r0, ROW_BLOCK), :].astype(F32), kg_ref[...], cos, sin, same_head)
        vv = v_ref[0, pl.ds(r0, ROW_BLOCK), :].astype(F32)
        lane = lax.broadcasted_iota(jnp.int32, kn.shape, 1)
        k_own = jnp.where(((lane // AT_HALF) % 2) == (g % 2), kn, 0.0)
        k_oth = pltpu.roll(k_own, jnp.where(odd, 3 * AT_HALF, AT_HALF), axis=1)
        ka_ref[pl.ds(r0, ROW_BLOCK), :] = jnp.where(odd, k_oth, k_own).astype(BF16)
        kb_ref[pl.ds(r0, ROW_BLOCK), :] = jnp.where(odd, k_own, k_oth).astype(BF16)
        v_own = jnp.where((lane // AT_HEAD_DIM) == (g % 2), vv, 0.0)
        v_oth = pltpu.roll(v_own, AT_HEAD_DIM, axis=1)
        va = jnp.where(odd, v_oth, v_own)
        vb = jnp.where(odd, v_own, v_oth)
        va_ref[pl.ds(r0, ROW_BLOCK), :] = jnp.where(lane == AT_HEAD_DIM, 1.0, va).astype(BF16)
        vb_ref[pl.ds(r0, ROW_BLOCK), :] = jnp.where(lane == 0, 1.0, vb).astype(BF16)
        return carry

    lax.fori_loop(0, ta // ROW_BLOCK, prep, 0)
    zeros = jnp.zeros((blk, LANES), BF16)
    for ref in (ka_ref, kb_ref, va_ref, vb_ref):
        ref[ta:ta + blk, :] = zeros

    half = lax.broadcasted_iota(jnp.int32, (2 * blk, 1), 0) < blk
    sink_a = jnp.where(half, sink_ref[g, 0], sink_ref[g, 2]) * LOG2E
    sink_b = jnp.where(half, sink_ref[g, 1], sink_ref[g, 3]) * LOG2E

    qi = lax.broadcasted_iota(jnp.int32, (2 * blk, 3 * blk), 0) % blk
    kj = lax.broadcasted_iota(jnp.int32, (2 * blk, 3 * blk), 1)
    band = (kj - qi >= 0) & (kj - qi <= 2 * blk)
    out_lo = lax.broadcasted_iota(jnp.int32, (2 * blk, LANES), 1) < AT_HEAD_DIM

    nt = (((1,), (1,)), ((), ()))

    def one_side(qt, k_ref_, v_ref_, sink, ones_lane, k0, mask):
        s_ctx = lax.dot_general(qt, k_ref_[0:tc, :], nt, preferred_element_type=F32)
        m = jnp.maximum(sink, jnp.max(s_ctx, axis=-1, keepdims=True))
        if mask is not None:
            s_loc = lax.dot_general(qt, k_ref_[pl.ds(k0, 3 * blk), :], nt, preferred_element_type=F32)
            s_loc = jnp.where(mask, s_loc, -jnp.inf)
            m = jnp.maximum(m, jnp.max(s_loc, axis=-1, keepdims=True))
        acc = jnp.dot(jnp.exp2(s_ctx - m).astype(BF16), v_ref_[0:tc, :], preferred_element_type=F32)
        if mask is not None:
            acc = acc + jnp.dot(jnp.exp2(s_loc - m).astype(BF16), v_ref_[pl.ds(k0, 3 * blk), :],
                                preferred_element_type=F32)
        den = jnp.exp2(sink - m) + acc[:, ones_lane:ones_lane + 1]
        return acc / den

    def block(r0, k0, mask):
        qt = jnp.concatenate([qs_ref[pl.ds(r0, blk), 0:LANES], qs_ref[pl.ds(r0, blk), LANES:2 * LANES]], axis=0)
        o = jnp.where(out_lo, one_side(qt, ka_ref, va_ref, sink_a, AT_HEAD_DIM, k0, mask),
                      one_side(qt, kb_ref, vb_ref, sink_b, 0, k0, mask))
        for c in range(2):
            zc = z_ref[0, pl.ds(r0, blk), c * LANES:(c + 1) * LANES].astype(F32)
            o_ref[0, pl.ds(r0, blk), c * LANES:(c + 1) * LANES] = (o[c * blk:(c + 1) * blk] * _silu_of_half(zc)).astype(o_ref.dtype)

    def lat_block(n, carry):
        r0 = pl.multiple_of(tc + n * blk, blk)
        k0 = pl.multiple_of(tc + (n - 1) * blk, blk)
        kpos = (n - 1) * blk + kj
        block(r0, k0, band & (kpos >= 0) & (kpos < t))
        return carry

    lax.fori_loop(0, nb, lat_block, 0, unroll=4)
    if need_ctx:
        for n in range(tc // blk):
            block(n * blk, None, None)
    else:
        o_ref[0, 0:tc, :] = jnp.zeros((tc, o_ref.shape[2]), o_ref.dtype)


def _attn_mixer(p, at_q_g, at_k_g, at_sink, tc, need_ctx):
    b, ta, _ = p.shape
    t = ta - tc
    assert tc >= AT_BLOCK and tc % AT_BLOCK == 0 and t % AT_BLOCK == 0 and ta % ROW_BLOCK == 0
    cos_t, sin_t = _rope_tables(tc, t)
    lane_head = (np.arange(LANES) // AT_HALF) % 2
    same_head = jnp.asarray(lane_head[:, None] == lane_head[None, :], BF16)
    tile_gain = lambda gain: jnp.tile(gain, 2)[AT_TILE_PERM].reshape(1, LANES)
    gw = AT_GROUP * AT_HEAD_DIM
    kcol = AT_Q // LANES
    vcol = (AT_Q + AT_KV) // LANES
    zcol = (AT_Q + 2 * AT_KV) // gw
    return pl.pallas_call(
        functools.partial(_attn_kernel, tc=tc, need_ctx=need_ctx),
        out_shape=jax.ShapeDtypeStruct((b, ta, AT_Q), BF16),
        grid=(b, AT_KV_HEADS),
        in_specs=[pl.BlockSpec(memory_space=pltpu.SMEM),
                  pl.BlockSpec((1, ta, gw), lambda i, g: (i, 0, g)),
                  pl.BlockSpec((1, ta, LANES), lambda i, g: (i, 0, kcol + g // 2)),
                  pl.BlockSpec((1, ta, LANES), lambda i, g: (i, 0, vcol + g // 2)),
                  pl.BlockSpec((1, ta, gw), lambda i, g: (i, 0, zcol + g)),
                  pl.BlockSpec((1, LANES), lambda i, g: (0, 0)),
                  pl.BlockSpec((1, LANES), lambda i, g: (0, 0)),
                  pl.BlockSpec((ta, LANES), lambda i, g: (0, 0)),
                  pl.BlockSpec((ta, LANES), lambda i, g: (0, 0)),
                  pl.BlockSpec((LANES, LANES), lambda i, g: (0, 0))],
        out_specs=pl.BlockSpec((1, ta, gw), lambda i, g: (i, 0, g)),
        scratch_shapes=[pltpu.VMEM((ta, gw), BF16)] + [pltpu.VMEM((ta + AT_BLOCK, LANES), BF16)] * 4,
        compiler_params=_cparams("parallel", "arbitrary"),
        name="attn_mixer",
    )(at_sink.reshape(AT_KV_HEADS, AT_GROUP), p, p, p, p, tile_gain(at_q_g), tile_gain(at_k_g), cos_t, sin_t,
      same_head)


def _attn_weight(w_in):
    d = w_in.shape[0]
    nqk = AT_Q + AT_KV
    qk = w_in[:, :nqk].reshape(d, nqk // LANES, LANES)[:, :, AT_TILE_PERM].reshape(d, nqk)
    z0 = nqk + AT_KV
    return jnp.concatenate([qk, w_in[:, nqk:z0], 0.5 * w_in[:, z0:]], axis=1)


def _chunk_order(nc, ncc, d, s):
    bwd = jnp.where(s < ncc, ncc - 1 - s, nc - 1 - (s - ncc))
    return jnp.where(d == 0, s, bwd)


ML_GATE_PERM = np.concatenate([np.arange(0, 4), np.arange(8, 12), np.arange(4, 8), np.arange(12, 16)])
ML_NQ = 6


def _mlstm_gate_kernel(g_ref, b_ref, a_ref, c_ref, *, tc, lc):
    h = ML_HEADS
    ta = g_ref.shape[2]
    nc, ncc = ta // lc, tc // lc
    x = g_ref[0] + b_ref[...]
    li = x[0:2 * h]
    lfp = x[2 * h:4 * h]
    lf = jnp.minimum(lfp, 0.0) - jnp.log1p(jnp.exp(-jnp.abs(lfp)))
    fwd = lax.broadcasted_iota(jnp.int32, (2 * h, ta), 0) < h
    fwd1 = lax.broadcasted_iota(jnp.int32, (2 * h, 1), 0) < h
    pos = lax.broadcasted_iota(jnp.int32, (2 * h, ta), 1) % lc

    def seg_scan(y, op, fill):
        yf, yb = y, y
        s = 1
        while s < lc:
            yf = op(yf, jnp.where(pos >= s, pltpu.roll(yf, s, axis=1), fill))
            yb = op(yb, jnp.where(pos < lc - s, pltpu.roll(yb, ta - s, axis=1), fill))
            s *= 2
        return jnp.where(fwd, yf, yb)

    bsum = seg_scan(lf, jnp.add, 0.0)
    a = li - bsum
    cmax = seg_scan(a, jnp.maximum, -jnp.inf)

    def end_col(y, c):
        return jnp.where(fwd1, y[:, (c + 1) * lc - 1:(c + 1) * lc], y[:, c * lc:c * lc + 1])

    tot = [end_col(bsum, c) for c in range(nc)]
    amax = [end_col(cmax, c) for c in range(nc)]

    def chain(order):
        m = jnp.zeros((2 * h, 1), F32)
        m_in = [None] * nc
        for c in order:
            m_in[c] = m
            m = tot[c] + jnp.maximum(m, amax[c])
        return m_in

    m_f = chain(list(range(nc)))
    m_b = chain(list(range(ncc - 1, -1, -1)) + list(range(nc - 1, ncc - 1, -1)))
    for c in range(nc):
        m_in = jnp.where(fwd1, m_f[c], m_b[c])
        sl = slice(c * lc, (c + 1) * lc)
        g_run = jnp.maximum(m_in, cmax[:, sl])
        g_end = jnp.maximum(m_in, amax[c])
        nr = 2 * h
        quantities = [a[:, sl], g_run, jnp.exp(m_in - g_run), jnp.exp(-(bsum[:, sl] + g_run)),
                      jnp.exp(a[:, sl] - g_end), jnp.broadcast_to(jnp.exp(m_in - g_end), (nr, lc))]
        a_ref[0, c] = quantities[0]
        pad = jnp.zeros((LANES - ML_NQ * nr, lc), F32)
        c_ref[0, c] = jnp.concatenate(quantities + [pad], axis=0).T


def _mlstm_scan_kernel(q_ref, k_ref, v_ref, a_ref, c_ref, o_ref, cs_ref, *, rev):
    lc = q_ref.shape[1]
    nr = 2 * ML_HEADS

    @pl.when(pl.program_id(1) == 0)
    def _():
        cs_ref[...] = jnp.zeros(cs_ref.shape, F32)

    ti = lax.broadcasted_iota(jnp.int32, (lc, lc), 0)
    si = lax.broadcasted_iota(jnp.int32, (lc, lc), 1)
    mask = (si >= ti) if rev else (si <= ti)
    ones = jnp.ones((lc, LANES), BF16)
    nt = (((1,), (1,)), ((), ()))
    tn = (((0,), (0,)), ((), ()))
    for h in range(ML_HEADS):
        r = (ML_HEADS if rev else 0) + h
        qb = q_ref[0, :, h * ML_DK:(h + 1) * ML_DK]
        kb = k_ref[0, :, h * ML_DK:(h + 1) * ML_DK]
        v_aug = jnp.concatenate([v_ref[0, :, h * ML_DV:(h + 1) * ML_DV], ones], axis=1)
        a_row = a_ref[0, 0, r:r + 1, :]
        g_run, inter, eclamp, w, decay = (c_ref[0, 0, :, qi * nr + r:qi * nr + r + 1] for qi in range(1, 6))
        qk = lax.dot_general(qb, kb, nt, preferred_element_type=F32)
        s = qk * jnp.where(mask, jnp.exp(a_row - g_run), 0.0)
        c_old = cs_ref[h]
        tot = inter * jnp.dot(qb, c_old.astype(BF16), preferred_element_type=F32) \
            + jnp.dot(s.astype(BF16), v_aug, preferred_element_type=F32)
        inv = 1.0 / jnp.maximum(jnp.abs(tot[:, ML_DV:]), eclamp)
        o_ref[0, :, h * ML_DV:(h + 1) * ML_DV] = (
            tot[:, :ML_DV] * jnp.concatenate([inv] * (ML_DV // LANES), axis=1)).astype(o_ref.dtype)
        kw = (kb.astype(F32) * w).astype(BF16)
        cs_ref[h] = decay[0:1, :] * c_old + lax.dot_general(kw, v_aug, tn, preferred_element_type=F32)


def _mlstm_scan(p, gates, gate_b, tc):
    b, ta, _ = p.shape
    lc = ML_CHUNK
    assert tc % lc == 0 and ta % lc == 0
    nc, ncc = ta // lc, tc // lc
    ng = 4 * ML_HEADS
    nr = 2 * ML_HEADS
    bias = gate_b.reshape(ng)[ML_GATE_PERM].reshape(ng, 1)
    a_rows, cols = pl.pallas_call(
        functools.partial(_mlstm_gate_kernel, tc=tc, lc=lc),
        out_shape=[jax.ShapeDtypeStruct((b, nc, nr, lc), F32), jax.ShapeDtypeStruct((b, nc, lc, LANES), F32)],
        grid=(b,),
        in_specs=[pl.BlockSpec((1, ng, ta), lambda i: (i, 0, 0)),
                  pl.BlockSpec((ng, 1), lambda i: (0, 0))],
        out_specs=[pl.BlockSpec((1, nc, nr, lc), lambda i: (i, 0, 0, 0)),
                   pl.BlockSpec((1, nc, lc, LANES), lambda i: (i, 0, 0, 0))],
        compiler_params=_cparams("parallel"),
        name="mlstm_gates",
    )(gates, bias)
    outs = []
    for d in range(2):
        chunk = functools.partial(_chunk_order, nc, ncc, d)
        outs.append(pl.pallas_call(
            functools.partial(_mlstm_scan_kernel, rev=bool(d)),
            out_shape=jax.ShapeDtypeStruct((b, ta, ML_INNER), BF16),
            grid=(b, nc),
            in_specs=[pl.BlockSpec((1, lc, ML_QK), lambda i, s, chunk=chunk: (i, chunk(s), 0)),
                      pl.BlockSpec((1, lc, ML_QK), lambda i, s, chunk=chunk: (i, chunk(s), 1)),
                      pl.BlockSpec((1, lc, ML_INNER), lambda i, s, chunk=chunk: (i, chunk(s), 1)),
                      pl.BlockSpec((1, 1, nr, lc), lambda i, s, chunk=chunk: (i, chunk(s), 0, 0)),
                      pl.BlockSpec((1, 1, lc, LANES), lambda i, s, chunk=chunk: (i, chunk(s), 0, 0))],
            out_specs=pl.BlockSpec((1, lc, ML_INNER), lambda i, s, chunk=chunk: (i, chunk(s), 0)),
            scratch_shapes=[pltpu.VMEM((ML_HEADS, ML_DK, ML_DV + LANES), F32)],
            compiler_params=_cparams("parallel", "arbitrary"),
            name="mlstm_scan_bwd" if d else "mlstm_scan_fwd",
        )(p, p, p, a_rows, cols))
    return outs


def _hgrn_lb_kernel(p_ref, o_ref, *, layer):
    for d in range(p_ref.shape[0]):
        x = p_ref[d]
        e = jnp.exp(x - jnp.max(x, axis=0, keepdims=True))
        p = e / jnp.sum(e, axis=0, keepdims=True)
        acc = jnp.zeros((1, x.shape[1]), F32)
        for j in range(1, layer + 1):
            acc = acc + p[j:j + 1, :]
        o_ref[d:d + 1, :] = acc


def _cumsum_rows(x, tri2):
    hi = x.astype(BF16)
    lo = (x - hi.astype(F32)).astype(BF16)
    return jnp.dot(tri2, jnp.concatenate([hi, lo], axis=0), preferred_element_type=F32)


def _tri2(n, rev):
    t = np.triu(np.ones((n, n), np.float32)) if rev else np.tril(np.ones((n, n), np.float32))
    return jnp.asarray(np.concatenate([t, t], axis=1), BF16)


def _anchor_rows(a, m, rev):
    n, f = a.shape
    idx = m if rev else m - 1
    if 2 * m >= SUBLANES:
        a3 = a.reshape(n // (2 * m), 2 * m, f)
        return jnp.broadcast_to(a3[:, idx:idx + 1, :], a3.shape).reshape(n, f)
    a3 = a.reshape(n // SUBLANES, SUBLANES, f)
    sub = lax.broadcasted_iota(jnp.int32, a3.shape, 1)
    out = None
    for gi in range(SUBLANES // (2 * m) - 1, -1, -1):
        cand = jnp.broadcast_to(a3[:, gi * 2 * m + idx:gi * 2 * m + idx + 1, :], a3.shape)
        out = cand if out is None else jnp.where(sub < (gi + 1) * 2 * m, cand, out)
    return out.reshape(n, f)


HG_MAX_LOG_SPAN = 64.0


def _hgrn_scan_kernel(q_ref, f_ref, i_ref, tri_ref, o_ref, s_ref, *, rev):
    lc = HG_CHUNK
    subs = range(q_ref.shape[1] // lc)
    subs = tuple(reversed(subs)) if rev else tuple(subs)

    @pl.when(pl.program_id(1) == 0)
    def _():
        s_ref[...] = jnp.zeros(s_ref.shape, F32)

    row = lax.broadcasted_iota(jnp.int32, (lc, 1), 0)
    ti = lax.broadcasted_iota(jnp.int32, (lc, lc), 0)
    si = lax.broadcasted_iota(jnp.int32, (lc, lc), 1)
    nt = (((1,), (1,)), ((), ()))
    tn = (((0,), (0,)), ((), ()))
    end = 0 if rev else lc - 1

    def all_heads_single_anchor():
        causal = (si >= ti) if rev else (si <= ti)
        cols = [slice(h * HG_DIM, (h + 1) * HG_DIM) for h in range(HG_HEADS)]
        pre = {}
        for ci in subs:
            rows = slice(ci * lc, (ci + 1) * lc)
            f = f_ref[0, rows, :]
            a = _cumsum_rows(jnp.log(f), tri_ref[...])
            q_dec = (q_ref[0, rows, :].astype(F32) * jnp.exp(a)).astype(BF16)
            k_inv = (1.0 - f) * jnp.exp(-a)
            k_inv_b = k_inv.astype(BF16)
            e_end = jnp.exp(a[end:end + 1, :])
            kd = (k_inv * e_end).astype(BF16)
            attn = [jnp.where(causal, lax.dot_general(q_dec[:, c], k_inv_b[:, c], nt, preferred_element_type=F32),
                              0.0).astype(BF16) for c in cols]
            pre[ci] = (rows, q_dec, e_end, kd, attn)
        state = [s_ref[h] for h in range(HG_HEADS)]
        for ci in subs:
            rows, q_dec, e_end, kd, attn = pre[ci]
            for h, c in enumerate(cols):
                o = jnp.dot(attn[h], i_ref[0, rows, c], preferred_element_type=F32)
                o_ref[0, rows, c] = o + lax.dot_general(q_dec[:, c], state[h].astype(BF16), nt,
                                                        preferred_element_type=F32)
            state = [state[h] * e_end[:, c] + lax.dot_general(i_ref[0, rows, c], kd[:, c], tn,
                                                              preferred_element_type=F32)
                     for h, c in enumerate(cols)]
        for h in range(HG_HEADS):
            s_ref[h] = state[h]

    def head_per_level(r0, h, carry):
        c0 = pl.multiple_of(h * HG_DIM, HG_DIM)
        q = q_ref[0, r0:r0 + lc, pl.ds(c0, HG_DIM)].astype(F32)
        f = f_ref[0, r0:r0 + lc, pl.ds(c0, HG_DIM)]
        k = 1.0 - f
        iv = i_ref[0, r0:r0 + lc, pl.ds(c0, HG_DIM)].astype(F32)
        a = _cumsum_rows(jnp.log(f), tri_ref[...])
        attn = jnp.zeros((lc, lc), F32)
        m = 1
        while m < lc:
            e = jnp.exp(-jnp.abs(a - _anchor_rows(a, m, rev)))
            upper = (row % (2 * m)) >= m
            is_q = jnp.logical_not(upper) if rev else upper
            qt = jnp.where(is_q, q * e, 0.0).astype(BF16)
            kt = jnp.where(is_q, 0.0, k * e).astype(BF16)
            pair = lax.dot_general(qt, kt, nt, preferred_element_type=F32)
            attn = attn + jnp.where((ti // (2 * m)) == (si // (2 * m)), pair, 0.0)
            m *= 2
        a_end = a[end:end + 1, :]
        kd = (k * jnp.exp(a_end - a)).astype(BF16)
        ib = iv.astype(BF16)
        s_old = s_ref[h]
        o = jnp.dot(attn.astype(BF16), ib, preferred_element_type=F32)
        o = o + lax.dot_general((q * jnp.exp(a)).astype(BF16), s_old.astype(BF16), nt, preferred_element_type=F32)
        o_ref[0, r0:r0 + lc, pl.ds(c0, HG_DIM)] = o + jnp.sum(q * k, axis=-1, keepdims=True) * iv
        s_ref[h] = s_old * jnp.exp(a_end) + lax.dot_general(ib, kd, tn, preferred_element_type=F32)
        return carry

    f_min = f_ref[0, :, 0:HG_DIM]
    for h in range(1, HG_HEADS):
        f_min = jnp.minimum(f_min, f_ref[0, :, h * HG_DIM:(h + 1) * HG_DIM])
    f_min = jnp.min(f_min.reshape(f_min.shape[0] // SUBLANES, SUBLANES, HG_DIM), axis=0)
    in_range = jnp.min(f_min) >= float(np.exp(-HG_MAX_LOG_SPAN / lc))

    @pl.when(in_range)
    def _():
        all_heads_single_anchor()

    @pl.when(jnp.logical_not(in_range))
    def _():
        for ci in subs:
            lax.fori_loop(0, HG_HEADS, functools.partial(head_per_level, ci * lc), 0)


def _hgrn_scan(p, pf, tc, rev):
    b, ta, _ = p.shape
    lc = HG_CHUNK
    rows = HG_STEP_CHUNKS * lc
    hg = HG_HEADS * HG_DIM
    assert tc % rows == 0 and ta % rows == 0
    nc, ncc = ta // rows, tc // rows
    d = 1 if rev else 0
    chunk = lambda s: _chunk_order(nc, ncc, d, s)
    return pl.pallas_call(
        functools.partial(_hgrn_scan_kernel, rev=rev),
        out_shape=jax.ShapeDtypeStruct((b, ta, hg), F32),
        grid=(b, nc),
        in_specs=[pl.BlockSpec((1, rows, hg), lambda i, s: (i, chunk(s), 0)),
                  pl.BlockSpec((1, rows, hg), lambda i, s: (i, chunk(s), d)),
                  pl.BlockSpec((1, rows, hg), lambda i, s: (i, chunk(s), 1)),
                  pl.BlockSpec((lc, 2 * lc), lambda i, s: (0, 0))],
        out_specs=pl.BlockSpec((1, rows, hg), lambda i, s: (i, chunk(s), 0)),
        scratch_shapes=[pltpu.VMEM((HG_HEADS, HG_DIM, HG_DIM), F32)],
        compiler_params=_cparams("parallel", "arbitrary"),
        name="hgrn_scan_bwd" if rev else "hgrn_scan_fwd",
    )(p, pf, p, _tri2(lc, rev))


def kernel(x, c, ctx, c_ctx, ada_w, ada_b, norm_g, ml_w_in, ml_gate_b, ml_head_g, ml_w_out, at_w_in, at_q_g, at_k_g, at_sink, at_w_out, sc_w_in, sc_conv_w, sc_conv_b, sc_w_out, hg_w_in, hg_f_b, hg_lb, hg_head_g, hg_w_out):
    tc = ctx.shape[1]
    mod = _ada_mod(c, c_ctx, ada_w, ada_b)
    xs = (ctx, x)
    for layer in range(DEPTH):
        kind, j = layer % 4, layer // 4
        need_ctx = layer < DEPTH - 1
        last = dict(lat_only=True) if layer == DEPTH - 1 else {}
        mod_l = mod[layer]
        if kind == 0:
            n_main = 2 * ML_QK + 3 * ML_INNER
            w_gate = jnp.zeros((LANES, ml_w_in.shape[1]), F32).at[:4 * ML_HEADS].set(
                ml_w_in[j][:, n_main:][:, ML_GATE_PERM].T)
            col_scale = jnp.concatenate([jnp.full((ML_QK,), ML_DK ** -0.5, F32), jnp.ones((ML_QK + ML_INNER,), F32),
                                         jnp.full((2 * ML_INNER,), 0.5, F32)])
            p, gates = _inproj(xs, mod_l, norm_g[layer], ml_w_in[j][:, :n_main] * col_scale, tc, tn=1024,
                               w_small=w_gate, out_dtype=BF16)
            h_f, h_b = _mlstm_scan(p, gates, ml_gate_b[j], tc)
            feats = [(h_f, ML_INNER, 0), (h_b, ML_INNER, 0), (p, ML_INNER, 2), (p, ML_INNER, 3)]
            xs = _outproj("mlstm", feats, ml_w_out[j], xs, mod_l, tc, head_g=0.5 * ml_head_g[j], **last)
        elif kind == 1:
            p = _inproj(xs, mod_l, norm_g[layer], _attn_weight(at_w_in[j]), tc, out_dtype=BF16)
            u = _attn_mixer(p, at_q_g[j], at_k_g[j], at_sink[j], tc, need_ctx)
            xs = _outproj("plain", [(u, AT_Q, 0)], at_w_out[j], xs, mod_l, tc, **last)
        elif kind == 2:
            if isinstance(xs, tuple):
                xs = jnp.concatenate(xs, axis=1)
            u = _conv_mixer(xs, mod_l, norm_g[layer], sc_w_in[j], sc_conv_w[j], sc_conv_b[j], tc)
            xs = _outproj("plain", [(u, u.shape[2], 0)], sc_w_out[j], xs, mod_l, tc, **last)
        else:
            hg = HG_HEADS * HG_DIM
            lb = pl.pallas_call(
                functools.partial(_hgrn_lb_kernel, layer=layer),
                out_shape=jax.ShapeDtypeStruct((2, hg), F32),
                name="hgrn_lb",
            )(hg_lb[j])
            w = hg_w_in[j]
            w_qiz = jnp.concatenate([0.5 * w[:, :hg], w[:, 3 * hg:4 * hg], 0.5 * w[:, 4 * hg:]], axis=1)
            p = _inproj(xs, mod_l, norm_g[layer], w_qiz, tc, tn=1024, out_dtype=BF16,
                        epilogue=((0, hg, "silu"), (hg, 3 * hg, "id")))
            lb2 = lb.reshape(2 * hg)
            pf = _inproj(xs, mod_l, norm_g[layer], 0.5 * w[:, hg:3 * hg], tc, epilogue=((0, 2 * hg, "fgate"),),
                         pars=(0.5 * hg_f_b[j].reshape(2 * hg), 0.5 * (1.0 + lb2), 0.5 * (1.0 - lb2)))
            o_f = _hgrn_scan(p, pf, tc, rev=False)
            o_b = _hgrn_scan(p, pf, tc, rev=True)
            feats = [(o_f, hg, 0), (o_b, hg, 0), (p, hg, 2)]
            xs = _outproj("hgrn", feats, hg_w_out[j], xs, mod_l, tc, head_g=hg_head_g[j], **last)
    return xs
```

```python
import functools

import numpy as np
import jax
import jax.numpy as jnp
from jax import lax
from jax.experimental import pallas as pl
from jax.experimental.pallas import tpu as pltpu

F32 = jnp.float32
BF16 = jnp.bfloat16
EPS = 1e-6
DEPTH = 4
GRID_W = 64
ROPE_BASE = 10000.0

ML_HEADS, ML_DK, ML_DV = 4, 256, 512
ML_QK = ML_HEADS * ML_DK
ML_INNER = ML_HEADS * ML_DV
ML_CHUNK = 256

AT_HEADS, AT_KV_HEADS, AT_HEAD_DIM = 16, 4, 64
AT_GROUP = AT_HEADS // AT_KV_HEADS
AT_BLOCK = 128
AT_PROJ_BLOCKS = 4
AT_Q = AT_HEADS * AT_HEAD_DIM
AT_KV = AT_KV_HEADS * AT_HEAD_DIM

SC_KSIZE = 3

HG_HEADS, HG_DIM = 8, 128
HG_CHUNK = 128
HG_STEP_CHUNKS = 2

LANES = 128
SUBLANES = 8
VMEM_LIMIT_BYTES = 56 * 1024 * 1024

ROW_BLOCK = 256
OUT_ROWS = 768
OUT_ROWS_SPLIT = 256


def _cparams(*sem):
    return pltpu.CompilerParams(dimension_semantics=sem, vmem_limit_bytes=VMEM_LIMIT_BYTES)


def _sigmoid(x):
    return 0.5 * jnp.tanh(0.5 * x) + 0.5


def _silu(x):
    return x * _sigmoid(x)


def _silu_of_half(xh):
    return xh * (1.0 + jnp.tanh(xh))


def _ada_kernel(c_ref, w_ref, b_ref, o_ref):
    s = _silu(c_ref[...])
    o_ref[0] = jnp.dot(s.astype(BF16), w_ref[0].astype(BF16), preferred_element_type=F32) + b_ref[0]


def _ada_mod(c, c_ctx, ada_w, ada_b):
    b, d = c.shape
    depth = ada_w.shape[0]
    rows = -(-(b + 1) // SUBLANES) * SUBLANES
    cc = jnp.zeros((rows, d), F32).at[:b].set(c).at[b].set(c_ctx)
    tn = 1024
    out = pl.pallas_call(
        _ada_kernel,
        out_shape=jax.ShapeDtypeStruct((depth, rows, 3 * d), F32),
        grid=(depth, 3 * d // tn),
        in_specs=[pl.BlockSpec((rows, d), lambda l, j: (0, 0)),
                  pl.BlockSpec((1, d, tn), lambda l, j: (l, 0, j)),
                  pl.BlockSpec((1, 1, tn), lambda l, j: (l, 0, j))],
        out_specs=pl.BlockSpec((1, rows, tn), lambda l, j: (l, 0, j)),
        compiler_params=_cparams("parallel", "parallel"),
        name="ada_mod",
    )(cc, ada_w, ada_b.reshape(depth, 1, 3 * d))
    return out.reshape(depth, rows, 3, d)


def _modulated(x_refs, ml_ref, mc_ref, g_ref, h_ref, tc):
    ta = h_ref.shape[0]
    g = g_ref[...]
    xc_ref = x_refs[0]
    xl_ref, lat0 = (x_refs[1], 0) if len(x_refs) == 2 else (x_refs[0], tc)

    def rows(x_ref, src0, dst0, m_ref):
        x = x_ref[0, pl.ds(src0, ROW_BLOCK), :]
        ms = jnp.mean(x * x, axis=-1, keepdims=True)
        xn = x * lax.rsqrt(ms + EPS) * g
        h = xn * (1.0 + m_ref[0, 1:2, :]) + m_ref[0, 0:1, :]
        h_ref[pl.ds(dst0, ROW_BLOCK), :] = h.astype(BF16)

    for i in range(tc // ROW_BLOCK):
        rows(xc_ref, i * ROW_BLOCK, i * ROW_BLOCK, mc_ref)

    def body(i, carry):
        r0 = pl.multiple_of(i * ROW_BLOCK, ROW_BLOCK)
        rows(xl_ref, lat0 + r0, tc + r0, ml_ref)
        return carry

    lax.fori_loop(0, (ta - tc) // ROW_BLOCK, body, 0)


def _split_stream(xs):
    arrs = list(xs) if isinstance(xs, tuple) else [xs]
    specs = [pl.BlockSpec((1,) + a.shape[1:], lambda i, j: (i, 0, 0)) for a in arrs]
    ta = sum(a.shape[1] for a in arrs)
    return arrs, specs, ta


def _inproj_kernel(*refs, tc, n_x, has_small, epilogue):
    x_refs, (ml_ref, mc_ref, g_ref, w_ref), rest = refs[:n_x], refs[n_x:n_x + 4], list(refs[n_x + 4:])
    ws_ref = rest.pop(0) if has_small else None
    par_refs = [rest.pop(0) for _ in range(3)] if epilogue else None
    o_ref = rest.pop(0)
    os_ref = rest.pop(0) if has_small else None
    h_ref = rest.pop(0)

    @pl.when(pl.program_id(1) == 0)
    def _():
        _modulated(x_refs, ml_ref, mc_ref, g_ref, h_ref, tc)
        if has_small:
            os_ref[0] = lax.dot_general(ws_ref[...], h_ref[...], (((1,), (1,)), ((), ())),
                                        preferred_element_type=F32)

    acc = jnp.dot(h_ref[...], w_ref[...], preferred_element_type=F32)
    if not epilogue:
        o_ref[0] = acc.astype(o_ref.dtype)
        return
    bias_ref, c0_ref, c1_ref = par_refs
    j = pl.program_id(1)
    t = jnp.tanh(acc + bias_ref[...])
    out = acc
    for lo, hi, kind in epilogue:
        if kind == "id":
            continue
        val = acc + acc * t if kind == "silu" else c0_ref[...] + c1_ref[...] * t
        out = jnp.where((j >= lo) & (j < hi), val, out)
    o_ref[0] = out.astype(o_ref.dtype)


def _inproj(xs, mod_l, norm_g, w, tc, tn=512, w_small=None, out_dtype=F32, epilogue=None, pars=None):
    x_arrs, x_specs, ta = _split_stream(xs)
    b, d = x_arrs[0].shape[0], x_arrs[0].shape[2]
    n = w.shape[1]
    assert n % tn == 0 and tc % ROW_BLOCK == 0 and (ta - tc) % ROW_BLOCK == 0
    mod_lat = mod_l[:b]
    mod_ctx = mod_l[b:b + 1]
    has_small = w_small is not None
    in_specs = x_specs + [pl.BlockSpec((1, 3, d), lambda i, j: (i, 0, 0)),
                          pl.BlockSpec((1, 3, d), lambda i, j: (0, 0, 0)),
                          pl.BlockSpec((1, d), lambda i, j: (0, 0)),
                          pl.BlockSpec((d, tn), lambda i, j: (0, j))]
    args = x_arrs + [mod_lat, mod_ctx, norm_g.reshape(1, d), w.astype(BF16)]
    out_shape = [jax.ShapeDtypeStruct((b, ta, n), out_dtype)]
    out_specs = [pl.BlockSpec((1, ta, tn), lambda i, j: (i, 0, j))]
    if has_small:
        ns = w_small.shape[0]
        in_specs.append(pl.BlockSpec((ns, d), lambda i, j: (0, 0)))
        args.append(w_small.astype(BF16))
    if epilogue:
        assert all(lo % tn == 0 and hi % tn == 0 for lo, hi, _ in epilogue)
        epilogue = tuple((lo // tn, hi // tn, kind) for lo, hi, kind in epilogue)
        zeros = jnp.zeros((n,), F32)
        for par in (pars if pars is not None else (zeros,) * 3):
            in_specs.append(pl.BlockSpec((1, tn), lambda i, j: (0, j)))
            args.append(par.reshape(1, n))
    if has_small:
        out_shape.append(jax.ShapeDtypeStruct((b, ns, ta), F32))
        out_specs.append(pl.BlockSpec((1, ns, ta), lambda i, j: (i, 0, 0)))
    res = pl.pallas_call(
        functools.partial(_inproj_kernel, tc=tc, n_x=len(x_arrs), has_small=has_small, epilogue=epilogue),
        out_shape=out_shape,
        grid=(b, n // tn),
        in_specs=in_specs,
        out_specs=out_specs,
        scratch_shapes=[pltpu.VMEM((ta, d), BF16)],
        compiler_params=_cparams("parallel", "arbitrary"),
        name="inproj",
    )(*args)
    return res if has_small else res[0]


def _head_rms(h, gain, n_heads):
    dh = h.shape[1] // n_heads
    parts = []
    for i in range(n_heads):
        hh = h[:, i * dh:(i + 1) * dh]
        ms = jnp.mean(hh * hh, axis=-1, keepdims=True)
        parts.append(hh * lax.rsqrt(ms + EPS))
    return jnp.concatenate(parts, axis=1) * gain


def _outproj_kernel(*refs, mode, n_feat, n_x, tc, tm, row0):
    feats, rest = refs[:n_feat], list(refs[n_feat:])
    hg_ref = rest.pop(0) if mode != "plain" else None
    w_ref = rest.pop(0)
    x_refs = [rest.pop(0) for _ in range(n_x)]
    ml_ref, mc_ref, o_ref = rest
    if mode == "plain":
        u = feats[0][0]
    elif mode == "mlstm":
        h0_ref, h1_ref, og_ref, z_ref = feats
        hn = _head_rms(h0_ref[0].astype(F32) + h1_ref[0].astype(F32), hg_ref[...], ML_HEADS)
        u = (hn * (1.0 + jnp.tanh(og_ref[0].astype(F32))) * _silu_of_half(z_ref[0].astype(F32))).astype(BF16)
    else:
        h0_ref, h1_ref, z_ref = feats
        hn = _head_rms(h0_ref[0].astype(F32) + h1_ref[0].astype(F32), hg_ref[...], HG_HEADS)
        u = (hn * _silu_of_half(z_ref[0].astype(F32))).astype(BF16)
    y = jnp.dot(u, w_ref[...], preferred_element_type=F32)
    first = row0 + pl.program_id(1) * tm
    row = first + lax.broadcasted_iota(jnp.int32, (tm, 1), 0)
    gate = jnp.where(row < tc, mc_ref[0, 2:3, :], ml_ref[0, 2:3, :])
    x = x_refs[0][0] if n_x == 1 else jnp.where(first < tc, x_refs[0][0], x_refs[1][0])
    o_ref[0] = x + gate * y


def _outproj(mode, feats, w_out, xs, mod_l, tc, head_g=None, lat_only=False):
    x_arrs = list(xs) if isinstance(xs, tuple) else [xs]
    b, d = x_arrs[0].shape[0], x_arrs[0].shape[2]
    ta = sum(a.shape[1] for a in x_arrs)
    kdim = w_out.shape[0]
    tm = OUT_ROWS if (len(x_arrs) == 1 and not lat_only and ta % OUT_ROWS == 0) else OUT_ROWS_SPLIT
    assert ta % tm == 0 and (tc % tm == 0 or (len(x_arrs) == 1 and not lat_only))
    off = tc // tm if lat_only else 0
    nct = tc // tm
    row_spec = lambda width, col: pl.BlockSpec((1, tm, width), lambda i, r, col=col: (i, r + off, col))
    in_specs, args = [], []
    for arr, width, col in feats:
        in_specs.append(row_spec(width, col))
        args.append(arr)
    if head_g is not None:
        in_specs.append(pl.BlockSpec((1, kdim), lambda i, r: (0, 0)))
        args.append(head_g.reshape(1, kdim))
    in_specs.append(pl.BlockSpec((kdim, d), lambda i, r: (0, 0)))
    args.append(w_out.astype(BF16))
    if len(x_arrs) == 1:
        in_specs.append(row_spec(d, 0))
    elif lat_only:
        x_arrs = x_arrs[1:]
        in_specs.append(pl.BlockSpec((1, tm, d), lambda i, r: (i, r, 0)))
    else:
        in_specs += [pl.BlockSpec((1, tm, d), lambda i, r: (i, jnp.minimum(r, nct - 1), 0)),
                     pl.BlockSpec((1, tm, d), lambda i, r: (i, jnp.maximum(r - nct, 0), 0))]
    args += x_arrs
    in_specs += [pl.BlockSpec((1, 3, d), lambda i, r: (i, 0, 0)),
                 pl.BlockSpec((1, 3, d), lambda i, r: (0, 0, 0))]
    args += [mod_l[:b], mod_l[b:b + 1]]
    rows_out = ta - off * tm
    return pl.pallas_call(
        functools.partial(_outproj_kernel, mode=mode, n_feat=len(feats), n_x=len(x_arrs), tc=tc, tm=tm,
                          row0=off * tm),
        out_shape=jax.ShapeDtypeStruct((b, rows_out, d), F32),
        grid=(b, rows_out // tm),
        in_specs=in_specs,
        out_specs=pl.BlockSpec((1, tm, d), lambda i, r: (i, r, 0)),
        compiler_params=_cparams("parallel", "parallel"),
        name="outproj_" + mode,
    )(*args)


def _conv_kernel(x_ref, ml_ref, mc_ref, g_ref, wx_ref, wb_ref, wc_ref, wz_ref, cw_ref, cb_ref, o_ref, h_ref, *, tc):
    @pl.when(pl.program_id(1) == 0)
    def _():
        _modulated((x_ref,), ml_ref, mc_ref, g_ref, h_ref, tc)

    ta = h_ref.shape[0]
    h = h_ref[...]
    xin, bg, cg, z = (jnp.dot(h, w_ref[...].astype(BF16), preferred_element_type=F32)
                      for w_ref in (wx_ref, wb_ref, wc_ref, wz_ref))
    u = cg * xin
    row = lax.broadcasted_iota(jnp.int32, (ta, 1), 0)
    first = (row == 0) | (row == tc)
    last = (row == tc - 1) | (row == ta - 1)
    u_prev = jnp.where(first, 0.0, pltpu.roll(u, 1, axis=0))
    u_next = jnp.where(last, 0.0, pltpu.roll(u, ta - 1, axis=0))
    cw = cw_ref[...]
    y = u_prev * cw[0:1, :] + u * cw[1:2, :] + u_next * cw[2:3, :] + cb_ref[...]
    o_ref[0] = (bg * y * _silu(z)).astype(o_ref.dtype)


def _conv_mixer(xs, mod_l, norm_g, w_in, conv_w, conv_b, tc, tw=256):
    b, ta, d = xs.shape
    e = conv_w.shape[1]
    nt = e // tw
    w_spec = lambda part: pl.BlockSpec((d, tw), lambda i, j, part=part: (0, part * nt + j))
    return pl.pallas_call(
        functools.partial(_conv_kernel, tc=tc),
        out_shape=jax.ShapeDtypeStruct((b, ta, e), BF16),
        grid=(b, nt),
        in_specs=[pl.BlockSpec((1, ta, d), lambda i, j: (i, 0, 0)),
                  pl.BlockSpec((1, 3, d), lambda i, j: (i, 0, 0)),
                  pl.BlockSpec((1, 3, d), lambda i, j: (0, 0, 0)),
                  pl.BlockSpec((1, d), lambda i, j: (0, 0)),
                  w_spec(0), w_spec(1), w_spec(2), w_spec(3),
                  pl.BlockSpec((SC_KSIZE, tw), lambda i, j: (0, j)),
                  pl.BlockSpec((1, tw), lambda i, j: (0, j))],
        out_specs=pl.BlockSpec((1, ta, tw), lambda i, j: (i, 0, j)),
        scratch_shapes=[pltpu.VMEM((ta, d), BF16)],
        compiler_params=_cparams("parallel", "arbitrary"),
        name="conv_mixer",
    )(xs, mod_l[:b], mod_l[b:b + 1], norm_g.reshape(1, d), w_in, w_in, w_in, w_in, conv_w, conv_b.reshape(1, e))


AT_HALF = AT_HEAD_DIM // 2
AT_TILE_PERM = np.concatenate([np.arange(0, AT_HALF), np.arange(2 * AT_HALF, 3 * AT_HALF),
                               np.arange(AT_HALF, 2 * AT_HALF), np.arange(3 * AT_HALF, 4 * AT_HALF)])
LOG2E = float(np.log2(np.e))


def _rope_tables(tc, t):
    rows = t // GRID_W
    row = np.repeat(np.arange(rows), GRID_W).astype(np.float32)
    col = np.tile(np.arange(GRID_W), rows).astype(np.float32)
    n_freq = AT_HEAD_DIM // 4
    freqs = jnp.power(ROPE_BASE, -jnp.arange(n_freq, dtype=F32) / n_freq)
    ang = jnp.concatenate([jnp.asarray(row)[:, None] * freqs, jnp.asarray(col)[:, None] * freqs], axis=-1)
    cos, sin = jnp.cos(ang), jnp.sin(ang)
    cos_t = jnp.concatenate([jnp.ones((tc, LANES), F32), jnp.tile(cos, (1, 4))], axis=0)
    sin_t = jnp.concatenate([jnp.zeros((tc, LANES), F32), jnp.concatenate([-sin, -sin, sin, sin], axis=-1)], axis=0)
    return cos_t, sin_t


def _norm_rope_pair(x, gain, cos, sin, same_head):
    x2 = x * x
    hi = x2.astype(BF16)
    lo = (x2 - hi.astype(F32)).astype(BF16)
    ss = jnp.dot(hi, same_head, preferred_element_type=F32) + jnp.dot(lo, same_head, preferred_element_type=F32)
    xn = x * lax.rsqrt(ss * (1.0 / AT_HEAD_DIM) + EPS) * gain
    return xn * cos + pltpu.roll(xn, 2 * AT_HALF, axis=1) * sin


def _attn_kernel(sink_ref, q_ref, k_ref, v_ref, z_ref, qg_ref, kg_ref, cos_ref, sin_ref, sh_ref, wo_ref, x_ref,
                 ml_ref, mc_ref, o_ref, qs_ref, ka_ref, kb_ref, va_ref, vb_ref, *, tc, need_ctx):
    ta = q_ref.shape[1]
    t = ta - tc
    nb = t // AT_BLOCK
    blk = AT_BLOCK
    gw = x_ref.shape[2]
    g = pl.program_id(1)
    odd = (g % 2) == 1
    scale = (AT_HEAD_DIM ** -0.5) * LOG2E

    @pl.when(g == 0)
    def _():
        o_ref[...] = jnp.zeros(o_ref.shape, F32)

    def prep(i, carry):
        r0 = pl.multiple_of(i * ROW_BLOCK, ROW_BLOCK)
        res_cols = pl.ds(pl.multiple_of(g * gw, gw), gw)
        o_ref[0, pl.ds(r0, ROW_BLOCK), res_cols] = (o_ref[0, pl.ds(r0, ROW_BLOCK), res_cols]
                                                    + x_ref[0, pl.ds(r0, ROW_BLOCK), :])
        cos = cos_ref[pl.ds(r0, ROW_BLOCK), :]
        sin = sin_ref[pl.ds(r0, ROW_BLOCK), :]
        same_head = sh_ref[...]
        for c in range(2):
            qc = _norm_rope_pair(q_ref[0, pl.ds(r0, ROW_BLOCK), c * LANES:(c + 1) * LANES].astype(F32), qg_ref[...],
                                 cos, sin, same_head)
            qs_ref[pl.ds(r0, ROW_BLOCK), c * LANES:(c + 1) * LANES] = (qc * scale).astype(BF16)
        kn = _norm_rope_pair(k_ref[0, pl.ds(r0, ROW_BLOCK), :].astype(F32), kg_ref[...], cos, sin, same_head)
        vv = v_ref[0, pl.ds(r0, ROW_BLOCK), :].astype(F32)
        lane = lax.broadcasted_iota(jnp.int32, kn.shape, 1)
        k_own = jnp.where(((lane // AT_HALF) % 2) == (g % 2), kn, 0.0)
        k_oth = pltpu.roll(k_own, jnp.where(odd, 3 * AT_HALF, AT_HALF), axis=1)
        ka_ref[pl.ds(r0, ROW_BLOCK), :] = jnp.where(odd, k_oth, k_own).astype(BF16)
        kb_ref[pl.ds(r0, ROW_BLOCK), :] = jnp.where(odd, k_own, k_oth).astype(BF16)
        v_own = jnp.where((lane // AT_HEAD_DIM) == (g % 2), vv, 0.0)
        v_oth = pltpu.roll(v_own, AT_HEAD_DIM, axis=1)
        va = jnp.where(odd, v_oth, v_own)
        vb = jnp.where(odd, v_own, v_oth)
        va_ref[pl.ds(r0, ROW_BLOCK), :] = jnp.where(lane == AT_HEAD_DIM, 1.0, va).astype(BF16)
        vb_ref[pl.ds(r0, ROW_BLOCK), :] = jnp.where(lane == 0, 1.0, vb).astype(BF16)
        return carry

    lax.fori_loop(0, ta // ROW_BLOCK, prep, 0)
    zeros = jnp.zeros((blk, LANES), BF16)
    for ref in (ka_ref, kb_ref, va_ref, vb_ref):
        ref[ta:ta + blk, :] = zeros

    half = lax.broadcasted_iota(jnp.int32, (2 * blk, 1), 0) < blk
    sink_a = jnp.where(half, sink_ref[g, 0], sink_ref[g, 2]) * LOG2E
    sink_b = jnp.where(half, sink_ref[g, 1], sink_ref[g, 3]) * LOG2E

    qi = lax.broadcasted_iota(jnp.int32, (2 * blk, 3 * blk), 0) % blk
    kj = lax.broadcasted_iota(jnp.int32, (2 * blk, 3 * blk), 1)
    band = (kj - qi >= 0) & (kj - qi <= 2 * blk)
    out_lo = lax.broadcasted_iota(jnp.int32, (2 * blk, LANES), 1) < AT_HEAD_DIM

    nt = (((1,), (1,)), ((), ()))

    def one_side(qt, k_ref_, v_ref_, sink, ones_lane, k0, mask):
        s_ctx = lax.dot_general(qt, k_ref_[0:tc, :], nt, preferred_element_type=F32)
        m = jnp.maximum(sink, jnp.max(s_ctx, axis=-1, keepdims=True))
        if mask is not None:
            s_loc = lax.dot_general(qt, k_ref_[pl.ds(k0, 3 * blk), :], nt, preferred_element_type=F32)
            s_loc = jnp.where(mask, s_loc, -jnp.inf)
            m = jnp.maximum(m, jnp.max(s_loc, axis=-1, keepdims=True))
        acc = jnp.dot(jnp.exp2(s_ctx - m).astype(BF16), v_ref_[0:tc, :], preferred_element_type=F32)
        if mask is not None:
            acc = acc + jnp.dot(jnp.exp2(s_loc - m).astype(BF16), v_ref_[pl.ds(k0, 3 * blk), :],
                                preferred_element_type=F32)
        den = jnp.exp2(sink - m) + acc[:, ones_lane:ones_lane + 1]
        return acc / den

    def block(r0, k0, mask):
        qt = jnp.concatenate([qs_ref[pl.ds(r0, blk), 0:LANES], qs_ref[pl.ds(r0, blk), LANES:2 * LANES]], axis=0)
        o = jnp.where(out_lo, one_side(qt, ka_ref, va_ref, sink_a, AT_HEAD_DIM, k0, mask),
                      one_side(qt, kb_ref, vb_ref, sink_b, 0, k0, mask))
        parts = []
        for c in range(2):
            zc = z_ref[0, pl.ds(r0, blk), c * LANES:(c + 1) * LANES].astype(F32)
            parts.append((o[c * blk:(c + 1) * blk] * _silu_of_half(zc)).astype(BF16))
        return jnp.concatenate(parts, axis=1)

    def project(r0, u, gate):
        y = jnp.dot(u, wo_ref[...], preferred_element_type=F32)
        rows = pl.ds(r0, u.shape[0])
        o_ref[0, rows, :] = o_ref[0, rows, :] + gate * y

    def lat_blocks(i, carry):
        us = []
        for j in range(AT_PROJ_BLOCKS):
            n = i * AT_PROJ_BLOCKS + j
            r0 = pl.multiple_of(tc + n * blk, blk)
            k0 = pl.multiple_of(tc + (n - 1) * blk, blk)
            kpos = (n - 1) * blk + kj
            us.append(block(r0, k0, band & (kpos >= 0) & (kpos < t)))
        r0 = pl.multiple_of(tc + i * AT_PROJ_BLOCKS * blk, AT_PROJ_BLOCKS * blk)
        project(r0, jnp.concatenate(us, axis=0), ml_ref[0, 2:3, :])
        return carry

    lax.fori_loop(0, nb // AT_PROJ_BLOCKS, lat_blocks, 0)
    if need_ctx:
        project(0, jnp.concatenate([block(n * blk, None, None) for n in range(tc // blk)], axis=0), mc_ref[0, 2:3, :])


def _attn_mixer(p, at_q_g, at_k_g, at_sink, w_out, xs, mod_l, tc, need_ctx):
    b, ta, _ = p.shape
    t = ta - tc
    d = xs.shape[2]
    assert tc >= AT_BLOCK and tc % AT_BLOCK == 0 and t % (AT_PROJ_BLOCKS * AT_BLOCK) == 0 and ta % ROW_BLOCK == 0
    assert d == AT_Q
    cos_t, sin_t = _rope_tables(tc, t)
    lane_head = (np.arange(LANES) // AT_HALF) % 2
    same_head = jnp.asarray(lane_head[:, None] == lane_head[None, :], BF16)
    tile_gain = lambda gain: jnp.tile(gain, 2)[AT_TILE_PERM].reshape(1, LANES)
    gw = AT_GROUP * AT_HEAD_DIM
    kcol = AT_Q // LANES
    vcol = (AT_Q + AT_KV) // LANES
    zcol = (AT_Q + 2 * AT_KV) // gw
    return pl.pallas_call(
        functools.partial(_attn_kernel, tc=tc, need_ctx=need_ctx),
        out_shape=jax.ShapeDtypeStruct((b, ta, d), F32),
        grid=(b, AT_KV_HEADS),
        in_specs=[pl.BlockSpec(memory_space=pltpu.SMEM),
                  pl.BlockSpec((1, ta, gw), lambda i, g: (i, 0, g)),
                  pl.BlockSpec((1, ta, LANES), lambda i, g: (i, 0, kcol + g // 2)),
                  pl.BlockSpec((1, ta, LANES), lambda i, g: (i, 0, vcol + g // 2)),
                  pl.BlockSpec((1, ta, gw), lambda i, g: (i, 0, zcol + g)),
                  pl.BlockSpec((1, LANES), lambda i, g: (0, 0)),
                  pl.BlockSpec((1, LANES), lambda i, g: (0, 0)),
                  pl.BlockSpec((ta, LANES), lambda i, g: (0, 0)),
                  pl.BlockSpec((ta, LANES), lambda i, g: (0, 0)),
                  pl.BlockSpec((LANES, LANES), lambda i, g: (0, 0)),
                  pl.BlockSpec((gw, d), lambda i, g: (g, 0)),
                  pl.BlockSpec((1, ta, gw), lambda i, g: (i, 0, g)),
                  pl.BlockSpec((1, 3, d), lambda i, g: (i, 0, 0)),
                  pl.BlockSpec((1, 3, d), lambda i, g: (0, 0, 0))],
        out_specs=pl.BlockSpec((1, ta, d), lambda i, g: (i, 0, 0)),
        scratch_shapes=[pltpu.VMEM((ta, gw), BF16)] + [pltpu.VMEM((ta + AT_BLOCK, LANES), BF16)] * 4,
        compiler_params=_cparams("parallel", "arbitrary"),
        name="attn_mixer",
    )(at_sink.reshape(AT_KV_HEADS, AT_GROUP), p, p, p, p, tile_gain(at_q_g), tile_gain(at_k_g), cos_t, sin_t,
      same_head, w_out.astype(BF16), xs, mod_l[:b], mod_l[b:b + 1])


def _attn_weight(w_in):
    d = w_in.shape[0]
    nqk = AT_Q + AT_KV
    qk = w_in[:, :nqk].reshape(d, nqk // LANES, LANES)[:, :, AT_TILE_PERM].reshape(d, nqk)
    z0 = nqk + AT_KV
    return jnp.concatenate([qk, w_in[:, nqk:z0], 0.5 * w_in[:, z0:]], axis=1)


def _chunk_order(nc, ncc, d, s):
    bwd = jnp.where(s < ncc, ncc - 1 - s, nc - 1 - (s - ncc))
    return jnp.where(d == 0, s, bwd)


ML_GATE_PERM = np.concatenate([np.arange(0, 4), np.arange(8, 12), np.arange(4, 8), np.arange(12, 16)])
ML_NQ = 6


def _mlstm_gate_kernel(g_ref, b_ref, a_ref, c_ref, *, tc, lc):
    h = ML_HEADS
    ta = g_ref.shape[2]
    nc, ncc = ta // lc, tc // lc
    x = g_ref[0] + b_ref[...]
    li = x[0:2 * h]
    lfp = x[2 * h:4 * h]
    lf = jnp.minimum(lfp, 0.0) - jnp.log1p(jnp.exp(-jnp.abs(lfp)))
    fwd = lax.broadcasted_iota(jnp.int32, (2 * h, ta), 0) < h
    fwd1 = lax.broadcasted_iota(jnp.int32, (2 * h, 1), 0) < h
    pos = lax.broadcasted_iota(jnp.int32, (2 * h, ta), 1) % lc

    def seg_scan(y, op, fill):
        yf, yb = y, y
        s = 1
        while s < lc:
            yf = op(yf, jnp.where(pos >= s, pltpu.roll(yf, s, axis=1), fill))
            yb = op(yb, jnp.where(pos < lc - s, pltpu.roll(yb, ta - s, axis=1), fill))
            s *= 2
        return jnp.where(fwd, yf, yb)

    bsum = seg_scan(lf, jnp.add, 0.0)
    a = li - bsum
    cmax = seg_scan(a, jnp.maximum, -jnp.inf)

    def end_col(y, c):
        return jnp.where(fwd1, y[:, (c + 1) * lc - 1:(c + 1) * lc], y[:, c * lc:c * lc + 1])

    tot = [end_col(bsum, c) for c in range(nc)]
    amax = [end_col(cmax, c) for c in range(nc)]

    def chain(order):
        m = jnp.zeros((2 * h, 1), F32)
        m_in = [None] * nc
        for c in order:
            m_in[c] = m
            m = tot[c] + jnp.maximum(m, amax[c])
        return m_in

    m_f = chain(list(range(nc)))
    m_b = chain(list(range(ncc - 1, -1, -1)) + list(range(nc - 1, ncc - 1, -1)))
    for c in range(nc):
        m_in = jnp.where(fwd1, m_f[c], m_b[c])
        sl = slice(c * lc, (c + 1) * lc)
        g_run = jnp.maximum(m_in, cmax[:, sl])
        g_end = jnp.maximum(m_in, amax[c])
        nr = 2 * h
        quantities = [a[:, sl], g_run, jnp.exp(m_in - g_run), jnp.exp(-(bsum[:, sl] + g_run)),
                      jnp.exp(a[:, sl] - g_end), jnp.broadcast_to(jnp.exp(m_in - g_end), (nr, lc))]
        a_ref[0, c] = quantities[0]
        pad = jnp.zeros((LANES - ML_NQ * nr, lc), F32)
        c_ref[0, c] = jnp.concatenate(quantities + [pad], axis=0).T


def _mlstm_scan_kernel(q_ref, k_ref, v_ref, a_ref, c_ref, o_ref, cs_ref, *, rev):
    lc = q_ref.shape[1]
    nr = 2 * ML_HEADS

    @pl.when(pl.program_id(1) == 0)
    def _():
        cs_ref[...] = jnp.zeros(cs_ref.shape, F32)

    ti = lax.broadcasted_iota(jnp.int32, (lc, lc), 0)
    si = lax.broadcasted_iota(jnp.int32, (lc, lc), 1)
    mask = (si >= ti) if rev else (si <= ti)
    ones = jnp.ones((lc, LANES), BF16)
    nt = (((1,), (1,)), ((), ()))
    tn = (((0,), (0,)), ((), ()))
    for h in range(ML_HEADS):
        r = (ML_HEADS if rev else 0) + h
        qb = q_ref[0, :, h * ML_DK:(h + 1) * ML_DK]
        kb = k_ref[0, :, h * ML_DK:(h + 1) * ML_DK]
        v_aug = jnp.concatenate([v_ref[0, :, h * ML_DV:(h + 1) * ML_DV], ones], axis=1)
        a_row = a_ref[0, 0, r:r + 1, :]
        g_run, inter, eclamp, w, decay = (c_ref[0, 0, :, qi * nr + r:qi * nr + r + 1] for qi in range(1, 6))
        qk = lax.dot_general(qb, kb, nt, preferred_element_type=F32)
        s = qk * jnp.where(mask, jnp.exp(a_row - g_run), 0.0)
        c_old = cs_ref[h]
        tot = inter * jnp.dot(qb, c_old.astype(BF16), preferred_element_type=F32) \
            + jnp.dot(s.astype(BF16), v_aug, preferred_element_type=F32)
        inv = 1.0 / jnp.maximum(jnp.abs(tot[:, ML_DV:]), eclamp)
        o_ref[0, :, h * ML_DV:(h + 1) * ML_DV] = (
            tot[:, :ML_DV] * jnp.concatenate([inv] * (ML_DV // LANES), axis=1)).astype(o_ref.dtype)
        kw = (kb.astype(F32) * w).astype(BF16)
        cs_ref[h] = decay[0:1, :] * c_old + lax.dot_general(kw, v_aug, tn, preferred_element_type=F32)


def _mlstm_scan(p, gates, gate_b, tc):
    b, ta, _ = p.shape
    lc = ML_CHUNK
    assert tc % lc == 0 and ta % lc == 0
    nc, ncc = ta // lc, tc // lc
    ng = 4 * ML_HEADS
    nr = 2 * ML_HEADS
    bias = gate_b.reshape(ng)[ML_GATE_PERM].reshape(ng, 1)
    a_rows, cols = pl.pallas_call(
        functools.partial(_mlstm_gate_kernel, tc=tc, lc=lc),
        out_shape=[jax.ShapeDtypeStruct((b, nc, nr, lc), F32), jax.ShapeDtypeStruct((b, nc, lc, LANES), F32)],
        grid=(b,),
        in_specs=[pl.BlockSpec((1, ng, ta), lambda i: (i, 0, 0)),
                  pl.BlockSpec((ng, 1), lambda i: (0, 0))],
        out_specs=[pl.BlockSpec((1, nc, nr, lc), lambda i: (i, 0, 0, 0)),
                   pl.BlockSpec((1, nc, lc, LANES), lambda i: (i, 0, 0, 0))],
        compiler_params=_cparams("parallel"),
        name="mlstm_gates",
    )(gates, bias)
    outs = []
    for d in range(2):
        chunk = functools.partial(_chunk_order, nc, ncc, d)
        outs.append(pl.pallas_call(
            functools.partial(_mlstm_scan_kernel, rev=bool(d)),
            out_shape=jax.ShapeDtypeStruct((b, ta, ML_INNER), BF16),
            grid=(b, nc),
            in_specs=[pl.BlockSpec((1, lc, ML_QK), lambda i, s, chunk=chunk: (i, chunk(s), 0)),
                      pl.BlockSpec((1, lc, ML_QK), lambda i, s, chunk=chunk: (i, chunk(s), 1)),
                      pl.BlockSpec((1, lc, ML_INNER), lambda i, s, chunk=chunk: (i, chunk(s), 1)),
                      pl.BlockSpec((1, 1, nr, lc), lambda i, s, chunk=chunk: (i, chunk(s), 0, 0)),
                      pl.BlockSpec((1, 1, lc, LANES), lambda i, s, chunk=chunk: (i, chunk(s), 0, 0))],
            out_specs=pl.BlockSpec((1, lc, ML_INNER), lambda i, s, chunk=chunk: (i, chunk(s), 0)),
            scratch_shapes=[pltpu.VMEM((ML_HEADS, ML_DK, ML_DV + LANES), F32)],
            compiler_params=_cparams("parallel", "arbitrary"),
            name="mlstm_scan_bwd" if d else "mlstm_scan_fwd",
        )(p, p, p, a_rows, cols))
    return outs


def _hgrn_lb_kernel(p_ref, o_ref, *, layer):
    for d in range(p_ref.shape[0]):
        x = p_ref[d]
        e = jnp.exp(x - jnp.max(x, axis=0, keepdims=True))
        p = e / jnp.sum(e, axis=0, keepdims=True)
        acc = jnp.zeros((1, x.shape[1]), F32)
        for j in range(1, layer + 1):
            acc = acc + p[j:j + 1, :]
        o_ref[d:d + 1, :] = acc


def _cumsum_rows(x, tri2):
    hi = x.astype(BF16)
    lo = (x - hi.astype(F32)).astype(BF16)
    return jnp.dot(tri2, jnp.concatenate([hi, lo], axis=0), preferred_element_type=F32)


def _tri2(n, rev):
    t = np.triu(np.ones((n, n), np.float32)) if rev else np.tril(np.ones((n, n), np.float32))
    return jnp.asarray(np.concatenate([t, t], axis=1), BF16)


def _anchor_rows(a, m, rev):
    n, f = a.shape
    idx = m if rev else m - 1
    if 2 * m >= SUBLANES:
        a3 = a.reshape(n // (2 * m), 2 * m, f)
        return jnp.broadcast_to(a3[:, idx:idx + 1, :], a3.shape).reshape(n, f)
    a3 = a.reshape(n // SUBLANES, SUBLANES, f)
    sub = lax.broadcasted_iota(jnp.int32, a3.shape, 1)
    out = None
    for gi in range(SUBLANES // (2 * m) - 1, -1, -1):
        cand = jnp.broadcast_to(a3[:, gi * 2 * m + idx:gi * 2 * m + idx + 1, :], a3.shape)
        out = cand if out is None else jnp.where(sub < (gi + 1) * 2 * m, cand, out)
    return out.reshape(n, f)


HG_MAX_LOG_SPAN = 64.0


def _hgrn_scan_kernel(q_ref, f_ref, i_ref, tri_ref, o_ref, s_ref, *, rev):
    lc = HG_CHUNK
    subs = range(q_ref.shape[1] // lc)
    subs = tuple(reversed(subs)) if rev else tuple(subs)

    @pl.when(pl.program_id(1) == 0)
    def _():
        s_ref[...] = jnp.zeros(s_ref.shape, F32)

    row = lax.broadcasted_iota(jnp.int32, (lc, 1), 0)
    ti = lax.broadcasted_iota(jnp.int32, (lc, lc), 0)
    si = lax.broadcasted_iota(jnp.int32, (lc, lc), 1)
    nt = (((1,), (1,)), ((), ()))
    tn = (((0,), (0,)), ((), ()))
    end = 0 if rev else lc - 1

    def all_heads_single_anchor():
        causal = (si >= ti) if rev else (si <= ti)
        cols = [slice(h * HG_DIM, (h + 1) * HG_DIM) for h in range(HG_HEADS)]
        pre = {}
        for ci in subs:
            rows = slice(ci * lc, (ci + 1) * lc)
            f = f_ref[0, rows, :]
            a = _cumsum_rows(jnp.log(f), tri_ref[...])
            q_dec = (q_ref[0, rows, :].astype(F32) * jnp.exp(a)).astype(BF16)
            k_inv = (1.0 - f) * jnp.exp(-a)
            k_inv_b = k_inv.astype(BF16)
            e_end = jnp.exp(a[end:end + 1, :])
            kd = (k_inv * e_end).astype(BF16)
            attn = [jnp.where(causal, lax.dot_general(q_dec[:, c], k_inv_b[:, c], nt, preferred_element_type=F32),
                              0.0).astype(BF16) for c in cols]
            pre[ci] = (rows, q_dec, e_end, kd, attn)
        state = [s_ref[h] for h in range(HG_HEADS)]
        for ci in subs:
            rows, q_dec, e_end, kd, attn = pre[ci]
            for h, c in enumerate(cols):
                o = jnp.dot(attn[h], i_ref[0, rows, c], preferred_element_type=F32)
                o = o + lax.dot_general(q_dec[:, c], state[h].astype(BF16), nt, preferred_element_type=F32)
                o_ref[0, rows, c] = o.astype(o_ref.dtype)
            state = [state[h] * e_end[:, c] + lax.dot_general(i_ref[0, rows, c], kd[:, c], tn,
                                                              preferred_element_type=F32)
                     for h, c in enumerate(cols)]
        for h in range(HG_HEADS):
            s_ref[h] = state[h]

    def head_per_level(r0, h, carry):
        c0 = pl.multiple_of(h * HG_DIM, HG_DIM)
        q = q_ref[0, r0:r0 + lc, pl.ds(c0, HG_DIM)].astype(F32)
        f = f_ref[0, r0:r0 + lc, pl.ds(c0, HG_DIM)]
        k = 1.0 - f
        iv = i_ref[0, r0:r0 + lc, pl.ds(c0, HG_DIM)].astype(F32)
        a = _cumsum_rows(jnp.log(f), tri_ref[...])
        attn = jnp.zeros((lc, lc), F32)
        m = 1
        while m < lc:
            e = jnp.exp(-jnp.abs(a - _anchor_rows(a, m, rev)))
            upper = (row % (2 * m)) >= m
            is_q = jnp.logical_not(upper) if rev else upper
            qt = jnp.where(is_q, q * e, 0.0).astype(BF16)
            kt = jnp.where(is_q, 0.0, k * e).astype(BF16)
            pair = lax.dot_general(qt, kt, nt, preferred_element_type=F32)
            attn = attn + jnp.where((ti // (2 * m)) == (si // (2 * m)), pair, 0.0)
            m *= 2
        a_end = a[end:end + 1, :]
        kd = (k * jnp.exp(a_end - a)).astype(BF16)
        ib = iv.astype(BF16)
        s_old = s_ref[h]
        o = jnp.dot(attn.astype(BF16), ib, preferred_element_type=F32)
        o = o + lax.dot_general((q * jnp.exp(a)).astype(BF16), s_old.astype(BF16), nt, preferred_element_type=F32)
        o_ref[0, r0:r0 + lc, pl.ds(c0, HG_DIM)] = (o + jnp.sum(q * k, axis=-1, keepdims=True) * iv).astype(o_ref.dtype)
        s_ref[h] = s_old * jnp.exp(a_end) + lax.dot_general(ib, kd, tn, preferred_element_type=F32)
        return carry

    f_min = f_ref[0, :, 0:HG_DIM]
    for h in range(1, HG_HEADS):
        f_min = jnp.minimum(f_min, f_ref[0, :, h * HG_DIM:(h + 1) * HG_DIM])
    f_min = jnp.min(f_min.reshape(f_min.shape[0] // SUBLANES, SUBLANES, HG_DIM), axis=0)
    in_range = jnp.min(f_min) >= float(np.exp(-HG_MAX_LOG_SPAN / lc))

    @pl.when(in_range)
    def _():
        all_heads_single_anchor()

    @pl.when(jnp.logical_not(in_range))
    def _():
        for ci in subs:
            lax.fori_loop(0, HG_HEADS, functools.partial(head_per_level, ci * lc), 0)


def _hgrn_scan(p, pf, tc, rev):
    b, ta, _ = p.shape
    lc = HG_CHUNK
    rows = HG_STEP_CHUNKS * lc
    hg = HG_HEADS * HG_DIM
    assert tc % rows == 0 and ta % rows == 0
    nc, ncc = ta // rows, tc // rows
    d = 1 if rev else 0
    chunk = lambda s: _chunk_order(nc, ncc, d, s)
    return pl.pallas_call(
        functools.partial(_hgrn_scan_kernel, rev=rev),
        out_shape=jax.ShapeDtypeStruct((b, ta, hg), BF16),
        grid=(b, nc),
        in_specs=[pl.BlockSpec((1, rows, hg), lambda i, s: (i, chunk(s), 0)),
                  pl.BlockSpec((1, rows, hg), lambda i, s: (i, chunk(s), d)),
                  pl.BlockSpec((1, rows, hg), lambda i, s: (i, chunk(s), 1)),
                  pl.BlockSpec((lc, 2 * lc), lambda i, s: (0, 0))],
        out_specs=pl.BlockSpec((1, rows, hg), lambda i, s: (i, chunk(s), 0)),
        scratch_shapes=[pltpu.VMEM((HG_HEADS, HG_DIM, HG_DIM), F32)],
        compiler_params=_cparams("parallel", "arbitrary"),
        name="hgrn_scan_bwd" if rev else "hgrn_scan_fwd",
    )(p, pf, p, _tri2(lc, rev))


def kernel(x, c, ctx, c_ctx, ada_w, ada_b, norm_g, ml_w_in, ml_gate_b, ml_head_g, ml_w_out, at_w_in, at_q_g, at_k_g, at_sink, at_w_out, sc_w_in, sc_conv_w, sc_conv_b, sc_w_out, hg_w_in, hg_f_b, hg_lb, hg_head_g, hg_w_out):
    tc = ctx.shape[1]
    mod = _ada_mod(c, c_ctx, ada_w, ada_b)
    xs = (ctx, x)
    for layer in range(DEPTH):
        kind, j = layer % 4, layer // 4
        need_ctx = layer < DEPTH - 1
        last = dict(lat_only=True) if layer == DEPTH - 1 else {}
        mod_l = mod[layer]
        if kind == 0:
            n_main = 2 * ML_QK + 3 * ML_INNER
            w_gate = jnp.zeros((LANES, ml_w_in.shape[1]), F32).at[:4 * ML_HEADS].set(
                ml_w_in[j][:, n_main:][:, ML_GATE_PERM].T)
            col_scale = jnp.concatenate([jnp.full((ML_QK,), ML_DK ** -0.5, F32), jnp.ones((ML_QK + ML_INNER,), F32),
                                         jnp.full((2 * ML_INNER,), 0.5, F32)])
            p, gates = _inproj(xs, mod_l, norm_g[layer], ml_w_in[j][:, :n_main] * col_scale, tc, tn=1024,
                               w_small=w_gate, out_dtype=BF16)
            h_f, h_b = _mlstm_scan(p, gates, ml_gate_b[j], tc)
            feats = [(h_f, ML_INNER, 0), (h_b, ML_INNER, 0), (p, ML_INNER, 2), (p, ML_INNER, 3)]
            xs = _outproj("mlstm", feats, ml_w_out[j], xs, mod_l, tc, head_g=0.5 * ml_head_g[j], **last)
        elif kind == 1:
            p = _inproj(xs, mod_l, norm_g[layer], _attn_weight(at_w_in[j]), tc, out_dtype=BF16)
            if isinstance(xs, tuple):
                xs = jnp.concatenate(xs, axis=1)
            xs = _attn_mixer(p, at_q_g[j], at_k_g[j], at_sink[j], at_w_out[j], xs, mod_l, tc, need_ctx)
            if last:
                xs = xs[:, tc:, :]
        elif kind == 2:
            if isinstance(xs, tuple):
                xs = jnp.concatenate(xs, axis=1)
            u = _conv_mixer(xs, mod_l, norm_g[layer], sc_w_in[j], sc_conv_w[j], sc_conv_b[j], tc)
            xs = _outproj("plain", [(u, u.shape[2], 0)], sc_w_out[j], xs, mod_l, tc, **last)
        else:
            hg = HG_HEADS * HG_DIM
            lb = pl.pallas_call(
                functools.partial(_hgrn_lb_kernel, layer=layer),
                out_shape=jax.ShapeDtypeStruct((2, hg), F32),
                name="hgrn_lb",
            )(hg_lb[j])
            w = hg_w_in[j]
            w_qiz = jnp.concatenate([0.5 * w[:, :hg], w[:, 3 * hg:4 * hg], 0.5 * w[:, 4 * hg:]], axis=1)
            p = _inproj(xs, mod_l, norm_g[layer], w_qiz, tc, tn=1024, out_dtype=BF16,
                        epilogue=((0, hg, "silu"), (hg, 3 * hg, "id")))
            lb2 = lb.reshape(2 * hg)
            pf = _inproj(xs, mod_l, norm_g[layer], 0.5 * w[:, hg:3 * hg], tc, tn=1024, epilogue=((0, 2 * hg, "fgate"),),
                         pars=(0.5 * hg_f_b[j].reshape(2 * hg), 0.5 * (1.0 + lb2), 0.5 * (1.0 - lb2)))
            o_f = _hgrn_scan(p, pf, tc, rev=False)
            o_b = _hgrn_scan(p, pf, tc, rev=True)
            feats = [(o_f, hg, 0), (o_b, hg, 0), (p, hg, 2)]
            xs = _outproj("hgrn", feats, hg_w_out[j], xs, mod_l, tc, head_g=hg_head_g[j], **last)
    return xs
```

```python
import functools

import numpy as np
import jax
import jax.numpy as jnp
from jax import lax
from jax.experimental import pallas as pl
from jax.experimental.pallas import tpu as pltpu

F32 = jnp.float32
BF16 = jnp.bfloat16
EPS = 1e-6
DEPTH = 4
GRID_W = 64
ROPE_BASE = 10000.0

ML_HEADS, ML_DK, ML_DV = 4, 256, 512
ML_QK = ML_HEADS * ML_DK
ML_INNER = ML_HEADS * ML_DV
ML_CHUNK = 256

AT_HEADS, AT_KV_HEADS, AT_HEAD_DIM = 16, 4, 64
AT_GROUP = AT_HEADS // AT_KV_HEADS
AT_BLOCK = 128
AT_PROJ_BLOCKS = 4
AT_Q = AT_HEADS * AT_HEAD_DIM
AT_KV = AT_KV_HEADS * AT_HEAD_DIM

SC_KSIZE = 3

HG_HEADS, HG_DIM = 8, 128
HG_CHUNK = 128
HG_STEP_CHUNKS = 2

LANES = 128
SUBLANES = 8
VMEM_LIMIT_BYTES = 56 * 1024 * 1024

ROW_BLOCK = 256
OUT_ROWS = 768
OUT_ROWS_SPLIT = 256


def _cparams(*sem):
    return pltpu.CompilerParams(dimension_semantics=sem, vmem_limit_bytes=VMEM_LIMIT_BYTES)


def _sigmoid(x):
    return 0.5 * jnp.tanh(0.5 * x) + 0.5


def _silu(x):
    return x * _sigmoid(x)


def _silu_of_half(xh):
    return xh * (1.0 + jnp.tanh(xh))


def _ada_kernel(c_ref, w_ref, b_ref, o_ref):
    s = _silu(c_ref[...])
    o_ref[0] = jnp.dot(s.astype(BF16), w_ref[0].astype(BF16), preferred_element_type=F32) + b_ref[0]


def _ada_mod(c, c_ctx, ada_w, ada_b):
    b, d = c.shape
    depth = ada_w.shape[0]
    rows = -(-(b + 1) // SUBLANES) * SUBLANES
    cc = jnp.zeros((rows, d), F32).at[:b].set(c).at[b].set(c_ctx)
    tn = 1024
    out = pl.pallas_call(
        _ada_kernel,
        out_shape=jax.ShapeDtypeStruct((depth, rows, 3 * d), F32),
        grid=(depth, 3 * d // tn),
        in_specs=[pl.BlockSpec((rows, d), lambda l, j: (0, 0)),
                  pl.BlockSpec((1, d, tn), lambda l, j: (l, 0, j)),
                  pl.BlockSpec((1, 1, tn), lambda l, j: (l, 0, j))],
        out_specs=pl.BlockSpec((1, rows, tn), lambda l, j: (l, 0, j)),
        compiler_params=_cparams("parallel", "parallel"),
        name="ada_mod",
    )(cc, ada_w, ada_b.reshape(depth, 1, 3 * d))
    return out.reshape(depth, rows, 3, d)


def _modulated(x_refs, ml_ref, mc_ref, g_ref, h_ref, tc):
    ta = h_ref.shape[0]
    g = g_ref[...]
    xc_ref = x_refs[0]
    xl_ref, lat0 = (x_refs[1], 0) if len(x_refs) == 2 else (x_refs[0], tc)

    def rows(x_ref, src0, dst0, m_ref):
        x = x_ref[0, pl.ds(src0, ROW_BLOCK), :]
        ms = jnp.mean(x * x, axis=-1, keepdims=True)
        xn = x * lax.rsqrt(ms + EPS) * g
        h = xn * (1.0 + m_ref[0, 1:2, :]) + m_ref[0, 0:1, :]
        h_ref[pl.ds(dst0, ROW_BLOCK), :] = h.astype(BF16)

    for i in range(tc // ROW_BLOCK):
        rows(xc_ref, i * ROW_BLOCK, i * ROW_BLOCK, mc_ref)

    def body(i, carry):
        r0 = pl.multiple_of(i * ROW_BLOCK, ROW_BLOCK)
        rows(xl_ref, lat0 + r0, tc + r0, ml_ref)
        return carry

    lax.fori_loop(0, (ta - tc) // ROW_BLOCK, body, 0)


def _split_stream(xs):
    arrs = list(xs) if isinstance(xs, tuple) else [xs]
    specs = [pl.BlockSpec((1,) + a.shape[1:], lambda i, j: (i, 0, 0)) for a in arrs]
    ta = sum(a.shape[1] for a in arrs)
    return arrs, specs, ta


def _inproj_kernel(*refs, tc, n_x, has_small, epilogue):
    x_refs, (ml_ref, mc_ref, g_ref, w_ref), rest = refs[:n_x], refs[n_x:n_x + 4], list(refs[n_x + 4:])
    ws_ref = rest.pop(0) if has_small else None
    par_refs = [rest.pop(0) for _ in range(3)] if epilogue else None
    o_ref = rest.pop(0)
    os_ref = rest.pop(0) if has_small else None
    h_ref = rest.pop(0)

    @pl.when(pl.program_id(1) == 0)
    def _():
        _modulated(x_refs, ml_ref, mc_ref, g_ref, h_ref, tc)
        if has_small:
            os_ref[0] = lax.dot_general(ws_ref[...], h_ref[...], (((1,), (1,)), ((), ())),
                                        preferred_element_type=F32)

    acc = jnp.dot(h_ref[...], w_ref[...], preferred_element_type=F32)
    if not epilogue:
        o_ref[0] = acc.astype(o_ref.dtype)
        return
    bias_ref, c0_ref, c1_ref = par_refs
    j = pl.program_id(1)
    t = jnp.tanh(acc + bias_ref[...])
    out = acc
    for lo, hi, kind in epilogue:
        if kind == "id":
            continue
        val = acc + acc * t if kind == "silu" else c0_ref[...] + c1_ref[...] * t
        out = jnp.where((j >= lo) & (j < hi), val, out)
    o_ref[0] = out.astype(o_ref.dtype)


def _inproj(xs, mod_l, norm_g, w, tc, tn=512, w_small=None, out_dtype=F32, epilogue=None, pars=None):
    x_arrs, x_specs, ta = _split_stream(xs)
    b, d = x_arrs[0].shape[0], x_arrs[0].shape[2]
    n = w.shape[1]
    assert n % tn == 0 and tc % ROW_BLOCK == 0 and (ta - tc) % ROW_BLOCK == 0
    mod_lat = mod_l[:b]
    mod_ctx = mod_l[b:b + 1]
    has_small = w_small is not None
    in_specs = x_specs + [pl.BlockSpec((1, 3, d), lambda i, j: (i, 0, 0)),
                          pl.BlockSpec((1, 3, d), lambda i, j: (0, 0, 0)),
                          pl.BlockSpec((1, d), lambda i, j: (0, 0)),
                          pl.BlockSpec((d, tn), lambda i, j: (0, j))]
    args = x_arrs + [mod_lat, mod_ctx, norm_g.reshape(1, d), w.astype(BF16)]
    out_shape = [jax.ShapeDtypeStruct((b, ta, n), out_dtype)]
    out_specs = [pl.BlockSpec((1, ta, tn), lambda i, j: (i, 0, j))]
    if has_small:
        ns = w_small.shape[0]
        in_specs.append(pl.BlockSpec((ns, d), lambda i, j: (0, 0)))
        args.append(w_small.astype(BF16))
    if epilogue:
        assert all(lo % tn == 0 and hi % tn == 0 for lo, hi, _ in epilogue)
        epilogue = tuple((lo // tn, hi // tn, kind) for lo, hi, kind in epilogue)
        zeros = jnp.zeros((n,), F32)
        for par in (pars if pars is not None else (zeros,) * 3):
            in_specs.append(pl.BlockSpec((1, tn), lambda i, j: (0, j)))
            args.append(par.reshape(1, n))
    if has_small:
        out_shape.append(jax.ShapeDtypeStruct((b, ns, ta), F32))
        out_specs.append(pl.BlockSpec((1, ns, ta), lambda i, j: (i, 0, 0)))
    res = pl.pallas_call(
        functools.partial(_inproj_kernel, tc=tc, n_x=len(x_arrs), has_small=has_small, epilogue=epilogue),
        out_shape=out_shape,
        grid=(b, n // tn),
        in_specs=in_specs,
        out_specs=out_specs,
        scratch_shapes=[pltpu.VMEM((ta, d), BF16)],
        compiler_params=_cparams("parallel", "arbitrary"),
        name="inproj",
    )(*args)
    return res if has_small else res[0]


def _head_rms(h, gain, n_heads):
    dh = h.shape[1] // n_heads
    parts = []
    for i in range(n_heads):
        hh = h[:, i * dh:(i + 1) * dh]
        ms = jnp.mean(hh * hh, axis=-1, keepdims=True)
        parts.append(hh * lax.rsqrt(ms + EPS))
    return jnp.concatenate(parts, axis=1) * gain


def _outproj_kernel(*refs, mode, n_feat, n_x, tc, tm, row0):
    feats, rest = refs[:n_feat], list(refs[n_feat:])
    hg_ref = rest.pop(0) if mode != "plain" else None
    w_ref = rest.pop(0)
    x_refs = [rest.pop(0) for _ in range(n_x)]
    ml_ref, mc_ref, o_ref = rest
    if mode == "plain":
        u = feats[0][0]
    elif mode == "mlstm":
        h0_ref, h1_ref, og_ref, z_ref = feats
        hn = _head_rms(h0_ref[0].astype(F32) + h1_ref[0].astype(F32), hg_ref[...], ML_HEADS)
        u = (hn * (1.0 + jnp.tanh(og_ref[0].astype(F32))) * _silu_of_half(z_ref[0].astype(F32))).astype(BF16)
    else:
        h0_ref, h1_ref, z_ref = feats
        hn = _head_rms(h0_ref[0].astype(F32) + h1_ref[0].astype(F32), hg_ref[...], HG_HEADS)
        u = (hn * _silu_of_half(z_ref[0].astype(F32))).astype(BF16)
    y = jnp.dot(u, w_ref[...], preferred_element_type=F32)
    first = row0 + pl.program_id(1) * tm
    row = first + lax.broadcasted_iota(jnp.int32, (tm, 1), 0)
    gate = jnp.where(row < tc, mc_ref[0, 2:3, :], ml_ref[0, 2:3, :])
    x = x_refs[0][0] if n_x == 1 else jnp.where(first < tc, x_refs[0][0], x_refs[1][0])
    o_ref[0] = x + gate * y


def _outproj(mode, feats, w_out, xs, mod_l, tc, head_g=None, lat_only=False):
    x_arrs = list(xs) if isinstance(xs, tuple) else [xs]
    b, d = x_arrs[0].shape[0], x_arrs[0].shape[2]
    ta = sum(a.shape[1] for a in x_arrs)
    kdim = w_out.shape[0]
    tm = OUT_ROWS if (len(x_arrs) == 1 and not lat_only and ta % OUT_ROWS == 0) else OUT_ROWS_SPLIT
    assert ta % tm == 0 and (tc % tm == 0 or (len(x_arrs) == 1 and not lat_only))
    off = tc // tm if lat_only else 0
    nct = tc // tm
    row_spec = lambda width, col: pl.BlockSpec((1, tm, width), lambda i, r, col=col: (i, r + off, col))
    in_specs, args = [], []
    for arr, width, col in feats:
        in_specs.append(row_spec(width, col))
        args.append(arr)
    if head_g is not None:
        in_specs.append(pl.BlockSpec((1, kdim), lambda i, r: (0, 0)))
        args.append(head_g.reshape(1, kdim))
    in_specs.append(pl.BlockSpec((kdim, d), lambda i, r: (0, 0)))
    args.append(w_out.astype(BF16))
    if len(x_arrs) == 1:
        in_specs.append(row_spec(d, 0))
    elif lat_only:
        x_arrs = x_arrs[1:]
        in_specs.append(pl.BlockSpec((1, tm, d), lambda i, r: (i, r, 0)))
    else:
        in_specs += [pl.BlockSpec((1, tm, d), lambda i, r: (i, jnp.minimum(r, nct - 1), 0)),
                     pl.BlockSpec((1, tm, d), lambda i, r: (i, jnp.maximum(r - nct, 0), 0))]
    args += x_arrs
    in_specs += [pl.BlockSpec((1, 3, d), lambda i, r: (i, 0, 0)),
                 pl.BlockSpec((1, 3, d), lambda i, r: (0, 0, 0))]
    args += [mod_l[:b], mod_l[b:b + 1]]
    rows_out = ta - off * tm
    return pl.pallas_call(
        functools.partial(_outproj_kernel, mode=mode, n_feat=len(feats), n_x=len(x_arrs), tc=tc, tm=tm,
                          row0=off * tm),
        out_shape=jax.ShapeDtypeStruct((b, rows_out, d), F32),
        grid=(b, rows_out // tm),
        in_specs=in_specs,
        out_specs=pl.BlockSpec((1, tm, d), lambda i, r: (i, r, 0)),
        compiler_params=_cparams("parallel", "parallel"),
        name="outproj_" + mode,
    )(*args)


def _conv_kernel(x_ref, ml_ref, mc_ref, g_ref, wx_ref, wb_ref, wc_ref, wz_ref, cw_ref, cb_ref, o_ref, h_ref, *, tc):
    @pl.when(pl.program_id(1) == 0)
    def _():
        _modulated((x_ref,), ml_ref, mc_ref, g_ref, h_ref, tc)

    ta = h_ref.shape[0]
    h = h_ref[...]
    xin, bg, cg, z = (jnp.dot(h, w_ref[...].astype(BF16), preferred_element_type=F32)
                      for w_ref in (wx_ref, wb_ref, wc_ref, wz_ref))
    u = cg * xin
    row = lax.broadcasted_iota(jnp.int32, (ta, 1), 0)
    first = (row == 0) | (row == tc)
    last = (row == tc - 1) | (row == ta - 1)
    u_prev = jnp.where(first, 0.0, pltpu.roll(u, 1, axis=0))
    u_next = jnp.where(last, 0.0, pltpu.roll(u, ta - 1, axis=0))
    cw = cw_ref[...]
    y = u_prev * cw[0:1, :] + u * cw[1:2, :] + u_next * cw[2:3, :] + cb_ref[...]
    o_ref[0] = (bg * y * _silu(z)).astype(o_ref.dtype)


def _conv_mixer(xs, mod_l, norm_g, w_in, conv_w, conv_b, tc, tw=256):
    b, ta, d = xs.shape
    e = conv_w.shape[1]
    nt = e // tw
    w_spec = lambda part: pl.BlockSpec((d, tw), lambda i, j, part=part: (0, part * nt + j))
    return pl.pallas_call(
        functools.partial(_conv_kernel, tc=tc),
        out_shape=jax.ShapeDtypeStruct((b, ta, e), BF16),
        grid=(b, nt),
        in_specs=[pl.BlockSpec((1, ta, d), lambda i, j: (i, 0, 0)),
                  pl.BlockSpec((1, 3, d), lambda i, j: (i, 0, 0)),
                  pl.BlockSpec((1, 3, d), lambda i, j: (0, 0, 0)),
                  pl.BlockSpec((1, d), lambda i, j: (0, 0)),
                  w_spec(0), w_spec(1), w_spec(2), w_spec(3),
                  pl.BlockSpec((SC_KSIZE, tw), lambda i, j: (0, j)),
                  pl.BlockSpec((1, tw), lambda i, j: (0, j))],
        out_specs=pl.BlockSpec((1, ta, tw), lambda i, j: (i, 0, j)),
        scratch_shapes=[pltpu.VMEM((ta, d), BF16)],
        compiler_params=_cparams("parallel", "arbitrary"),
        name="conv_mixer",
    )(xs, mod_l[:b], mod_l[b:b + 1], norm_g.reshape(1, d), w_in, w_in, w_in, w_in, conv_w, conv_b.reshape(1, e))


AT_HALF = AT_HEAD_DIM // 2
AT_TILE_PERM = np.concatenate([np.arange(0, AT_HALF), np.arange(2 * AT_HALF, 3 * AT_HALF),
                               np.arange(AT_HALF, 2 * AT_HALF), np.arange(3 * AT_HALF, 4 * AT_HALF)])
LOG2E = float(np.log2(np.e))


def _rope_tables(tc, t):
    rows = t // GRID_W
    row = np.repeat(np.arange(rows), GRID_W).astype(np.float32)
    col = np.tile(np.arange(GRID_W), rows).astype(np.float32)
    n_freq = AT_HEAD_DIM // 4
    freqs = jnp.power(ROPE_BASE, -jnp.arange(n_freq, dtype=F32) / n_freq)
    ang = jnp.concatenate([jnp.asarray(row)[:, None] * freqs, jnp.asarray(col)[:, None] * freqs], axis=-1)
    cos, sin = jnp.cos(ang), jnp.sin(ang)
    cos_t = jnp.concatenate([jnp.ones((tc, LANES), F32), jnp.tile(cos, (1, 4))], axis=0)
    sin_t = jnp.concatenate([jnp.zeros((tc, LANES), F32), jnp.concatenate([-sin, -sin, sin, sin], axis=-1)], axis=0)
    return cos_t, sin_t


def _norm_rope_tiles(xs, gains, cos, sin, same_head):
    sums = []
    for x in xs:
        x2 = x * x
        hi = x2.astype(BF16)
        lo = (x2 - hi.astype(F32)).astype(BF16)
        sums.append((jnp.dot(hi, same_head, preferred_element_type=F32),
                     jnp.dot(lo, same_head, preferred_element_type=F32)))
    out = []
    for x, gain, (s_hi, s_lo) in zip(xs, gains, sums):
        xn = x * lax.rsqrt((s_hi + s_lo) * (1.0 / AT_HEAD_DIM) + EPS) * gain
        out.append(xn * cos + pltpu.roll(xn, 2 * AT_HALF, axis=1) * sin)
    return out


def _attn_kernel(sink_ref, q_ref, k_ref, v_ref, z_ref, qg_ref, kg_ref, cos_ref, sin_ref, sh_ref, wo_ref, x_ref,
                 ml_ref, mc_ref, o_ref, qs_ref, ka_ref, kb_ref, va_ref, vb_ref, *, tc, need_ctx):
    ta = q_ref.shape[1]
    t = ta - tc
    nb = t // AT_BLOCK
    blk = AT_BLOCK
    gw = x_ref.shape[2]
    g = pl.program_id(1)
    odd = (g % 2) == 1
    scale = (AT_HEAD_DIM ** -0.5) * LOG2E

    @pl.when(g == 0)
    def _():
        o_ref[...] = jnp.zeros(o_ref.shape, F32)

    def prep(i, carry):
        r0 = pl.multiple_of(i * ROW_BLOCK, ROW_BLOCK)
        res_cols = pl.ds(pl.multiple_of(g * gw, gw), gw)
        o_ref[0, pl.ds(r0, ROW_BLOCK), res_cols] = (o_ref[0, pl.ds(r0, ROW_BLOCK), res_cols]
                                                    + x_ref[0, pl.ds(r0, ROW_BLOCK), :])
        cos = cos_ref[pl.ds(r0, ROW_BLOCK), :]
        sin = sin_ref[pl.ds(r0, ROW_BLOCK), :]
        same_head = sh_ref[...]
        tiles = [q_ref[0, pl.ds(r0, ROW_BLOCK), c * LANES:(c + 1) * LANES].astype(F32) for c in range(2)]
        tiles.append(k_ref[0, pl.ds(r0, ROW_BLOCK), :].astype(F32))
        q0, q1, kn = _norm_rope_tiles(tiles, (qg_ref[...], qg_ref[...], kg_ref[...]), cos, sin, same_head)
        for c, qc in enumerate((q0, q1)):
            qs_ref[pl.ds(r0, ROW_BLOCK), c * LANES:(c + 1) * LANES] = (qc * scale).astype(BF16)
        vv = v_ref[0, pl.ds(r0, ROW_BLOCK), :].astype(F32)
        lane = lax.broadcasted_iota(jnp.int32, kn.shape, 1)
        k_own = jnp.where(((lane // AT_HALF) % 2) == (g % 2), kn, 0.0)
        k_oth = pltpu.roll(k_own, jnp.where(odd, 3 * AT_HALF, AT_HALF), axis=1)
        ka_ref[pl.ds(r0, ROW_BLOCK), :] = jnp.where(odd, k_oth, k_own).astype(BF16)
        kb_ref[pl.ds(r0, ROW_BLOCK), :] = jnp.where(odd, k_own, k_oth).astype(BF16)
        v_own = jnp.where((lane // AT_HEAD_DIM) == (g % 2), vv, 0.0)
        v_oth = pltpu.roll(v_own, AT_HEAD_DIM, axis=1)
        va = jnp.where(odd, v_oth, v_own)
        vb = jnp.where(odd, v_own, v_oth)
        va_ref[pl.ds(r0, ROW_BLOCK), :] = jnp.where(lane == AT_HEAD_DIM, 1.0, va).astype(BF16)
        vb_ref[pl.ds(r0, ROW_BLOCK), :] = jnp.where(lane == 0, 1.0, vb).astype(BF16)
        return carry

    lax.fori_loop(0, ta // ROW_BLOCK, prep, 0)
    zeros = jnp.zeros((blk, LANES), BF16)
    for ref in (ka_ref, kb_ref, va_ref, vb_ref):
        ref[ta:ta + blk, :] = zeros

    half = lax.broadcasted_iota(jnp.int32, (2 * blk, 1), 0) < blk
    sink_a = jnp.where(half, sink_ref[g, 0], sink_ref[g, 2]) * LOG2E
    sink_b = jnp.where(half, sink_ref[g, 1], sink_ref[g, 3]) * LOG2E

    qi = lax.broadcasted_iota(jnp.int32, (2 * blk, 3 * blk), 0) % blk
    kj = lax.broadcasted_iota(jnp.int32, (2 * blk, 3 * blk), 1)
    band = (kj - qi >= 0) & (kj - qi <= 2 * blk)
    out_lo = lax.broadcasted_iota(jnp.int32, (2 * blk, LANES), 1) < AT_HEAD_DIM

    nt = (((1,), (1,)), ((), ()))

    sides = ((ka_ref, va_ref, sink_a, AT_HEAD_DIM), (kb_ref, vb_ref, sink_b, 0))

    def attend(blocks, r_first, gate):
        qts = [jnp.concatenate([qs_ref[pl.ds(r0, blk), 0:LANES], qs_ref[pl.ds(r0, blk), LANES:2 * LANES]], axis=0)
               for r0, _, _ in blocks]
        scores = []
        for qt, (_, k0, mask) in zip(qts, blocks):
            for k_ref_, _, _, _ in sides:
                s_ctx = lax.dot_general(qt, k_ref_[0:tc, :], nt, preferred_element_type=F32)
                s_loc = None if mask is None else lax.dot_general(qt, k_ref_[pl.ds(k0, 3 * blk), :], nt,
                                                                  preferred_element_type=F32)
                scores.append((s_ctx, s_loc))
        probs = []
        for idx, (s_ctx, s_loc) in enumerate(scores):
            mask = blocks[idx // 2][2]
            sink = sides[idx % 2][2]
            m = jnp.maximum(sink, jnp.max(s_ctx, axis=-1, keepdims=True))
            if s_loc is not None:
                s_loc = jnp.where(mask, s_loc, -jnp.inf)
                m = jnp.maximum(m, jnp.max(s_loc, axis=-1, keepdims=True))
            probs.append((jnp.exp2(s_ctx - m).astype(BF16),
                          None if s_loc is None else jnp.exp2(s_loc - m).astype(BF16), jnp.exp2(sink - m)))
        accs = []
        for idx, (p_ctx, p_loc, _) in enumerate(probs):
            k0 = blocks[idx // 2][1]
            v_ref_ = sides[idx % 2][1]
            acc = jnp.dot(p_ctx, v_ref_[0:tc, :], preferred_element_type=F32)
            if p_loc is not None:
                acc = acc + jnp.dot(p_loc, v_ref_[pl.ds(k0, 3 * blk), :], preferred_element_type=F32)
            accs.append(acc)
        us = []
        for bi, (r0, _, _) in enumerate(blocks):
            halves = []
            for si in range(2):
                acc, ones_lane = accs[2 * bi + si], sides[si][3]
                halves.append(acc / (probs[2 * bi + si][2] + acc[:, ones_lane:ones_lane + 1]))
            o = jnp.where(out_lo, halves[0], halves[1])
            parts = []
            for c in range(2):
                zc = z_ref[0, pl.ds(r0, blk), c * LANES:(c + 1) * LANES].astype(F32)
                parts.append((o[c * blk:(c + 1) * blk] * _silu_of_half(zc)).astype(BF16))
            us.append(jnp.concatenate(parts, axis=1))
        u = jnp.concatenate(us, axis=0)
        rows = pl.ds(r_first, u.shape[0])
        o_ref[0, rows, :] = o_ref[0, rows, :] + gate * jnp.dot(u, wo_ref[...], preferred_element_type=F32)

    def lat_blocks(i, carry):
        blocks = []
        for j in range(AT_PROJ_BLOCKS):
            n = i * AT_PROJ_BLOCKS + j
            kpos = (n - 1) * blk + kj
            blocks.append((pl.multiple_of(tc + n * blk, blk), pl.multiple_of(tc + (n - 1) * blk, blk),
                           band & (kpos >= 0) & (kpos < t)))
        attend(blocks, pl.multiple_of(tc + i * AT_PROJ_BLOCKS * blk, AT_PROJ_BLOCKS * blk), ml_ref[0, 2:3, :])
        return carry

    lax.fori_loop(0, nb // AT_PROJ_BLOCKS, lat_blocks, 0)
    if need_ctx:
        attend([(n * blk, None, None) for n in range(tc // blk)], 0, mc_ref[0, 2:3, :])


def _attn_mixer(p, at_q_g, at_k_g, at_sink, w_out, xs, mod_l, tc, need_ctx):
    b, ta, _ = p.shape
    t = ta - tc
    d = xs.shape[2]
    assert tc >= AT_BLOCK and tc % AT_BLOCK == 0 and t % (AT_PROJ_BLOCKS * AT_BLOCK) == 0 and ta % ROW_BLOCK == 0
    assert d == AT_Q
    cos_t, sin_t = _rope_tables(tc, t)
    lane_head = (np.arange(LANES) // AT_HALF) % 2
    same_head = jnp.asarray(lane_head[:, None] == lane_head[None, :], BF16)
    tile_gain = lambda gain: jnp.tile(gain, 2)[AT_TILE_PERM].reshape(1, LANES)
    gw = AT_GROUP * AT_HEAD_DIM
    kcol = AT_Q // LANES
    vcol = (AT_Q + AT_KV) // LANES
    zcol = (AT_Q + 2 * AT_KV) // gw
    return pl.pallas_call(
        functools.partial(_attn_kernel, tc=tc, need_ctx=need_ctx),
        out_shape=jax.ShapeDtypeStruct((b, ta, d), F32),
        grid=(b, AT_KV_HEADS),
        in_specs=[pl.BlockSpec(memory_space=pltpu.SMEM),
                  pl.BlockSpec((1, ta, gw), lambda i, g: (i, 0, g)),
                  pl.BlockSpec((1, ta, LANES), lambda i, g: (i, 0, kcol + g // 2)),
                  pl.BlockSpec((1, ta, LANES), lambda i, g: (i, 0, vcol + g // 2)),
                  pl.BlockSpec((1, ta, gw), lambda i, g: (i, 0, zcol + g)),
                  pl.BlockSpec((1, LANES), lambda i, g: (0, 0)),
                  pl.BlockSpec((1, LANES), lambda i, g: (0, 0)),
                  pl.BlockSpec((ta, LANES), lambda i, g: (0, 0)),
                  pl.BlockSpec((ta, LANES), lambda i, g: (0, 0)),
                  pl.BlockSpec((LANES, LANES), lambda i, g: (0, 0)),
                  pl.BlockSpec((gw, d), lambda i, g: (g, 0)),
                  pl.BlockSpec((1, ta, gw), lambda i, g: (i, 0, g)),
                  pl.BlockSpec((1, 3, d), lambda i, g: (i, 0, 0)),
                  pl.BlockSpec((1, 3, d), lambda i, g: (0, 0, 0))],
        out_specs=pl.BlockSpec((1, ta, d), lambda i, g: (i, 0, 0)),
        scratch_shapes=[pltpu.VMEM((ta, gw), BF16)] + [pltpu.VMEM((ta + AT_BLOCK, LANES), BF16)] * 4,
        compiler_params=_cparams("parallel", "arbitrary"),
        name="attn_mixer",
    )(at_sink.reshape(AT_KV_HEADS, AT_GROUP), p, p, p, p, tile_gain(at_q_g), tile_gain(at_k_g), cos_t, sin_t,
      same_head, w_out.astype(BF16), xs, mod_l[:b], mod_l[b:b + 1])


def _attn_weight(w_in):
    d = w_in.shape[0]
    nqk = AT_Q + AT_KV
    qk = w_in[:, :nqk].reshape(d, nqk // LANES, LANES)[:, :, AT_TILE_PERM].reshape(d, nqk)
    z0 = nqk + AT_KV
    return jnp.concatenate([qk, w_in[:, nqk:z0], 0.5 * w_in[:, z0:]], axis=1)


def _chunk_order(nc, ncc, d, s):
    bwd = jnp.where(s < ncc, ncc - 1 - s, nc - 1 - (s - ncc))
    return jnp.where(d == 0, s, bwd)


ML_GATE_PERM = np.concatenate([np.arange(0, 4), np.arange(8, 12), np.arange(4, 8), np.arange(12, 16)])
ML_NQ = 6


def _mlstm_gate_kernel(g_ref, b_ref, a_ref, c_ref, *, tc, lc):
    h = ML_HEADS
    ta = g_ref.shape[2]
    nc, ncc = ta // lc, tc // lc
    x = g_ref[0] + b_ref[...]
    li = x[0:2 * h]
    lfp = x[2 * h:4 * h]
    lf = jnp.minimum(lfp, 0.0) - jnp.log1p(jnp.exp(-jnp.abs(lfp)))
    fwd = lax.broadcasted_iota(jnp.int32, (2 * h, ta), 0) < h
    fwd1 = lax.broadcasted_iota(jnp.int32, (2 * h, 1), 0) < h
    pos = lax.broadcasted_iota(jnp.int32, (2 * h, ta), 1) % lc

    def seg_scan(y, op, fill):
        yf, yb = y, y
        s = 1
        while s < lc:
            yf = op(yf, jnp.where(pos >= s, pltpu.roll(yf, s, axis=1), fill))
            yb = op(yb, jnp.where(pos < lc - s, pltpu.roll(yb, ta - s, axis=1), fill))
            s *= 2
        return jnp.where(fwd, yf, yb)

    bsum = seg_scan(lf, jnp.add, 0.0)
    a = li - bsum
    cmax = seg_scan(a, jnp.maximum, -jnp.inf)

    def end_col(y, c):
        return jnp.where(fwd1, y[:, (c + 1) * lc - 1:(c + 1) * lc], y[:, c * lc:c * lc + 1])

    tot = [end_col(bsum, c) for c in range(nc)]
    amax = [end_col(cmax, c) for c in range(nc)]

    def chain(order):
        m = jnp.zeros((2 * h, 1), F32)
        m_in = [None] * nc
        for c in order:
            m_in[c] = m
            m = tot[c] + jnp.maximum(m, amax[c])
        return m_in

    m_f = chain(list(range(nc)))
    m_b = chain(list(range(ncc - 1, -1, -1)) + list(range(nc - 1, ncc - 1, -1)))
    for c in range(nc):
        m_in = jnp.where(fwd1, m_f[c], m_b[c])
        sl = slice(c * lc, (c + 1) * lc)
        g_run = jnp.maximum(m_in, cmax[:, sl])
        g_end = jnp.maximum(m_in, amax[c])
        nr = 2 * h
        quantities = [a[:, sl], g_run, jnp.exp(m_in - g_run), jnp.exp(-(bsum[:, sl] + g_run)),
                      jnp.exp(a[:, sl] - g_end), jnp.broadcast_to(jnp.exp(m_in - g_end), (nr, lc))]
        a_ref[0, c] = quantities[0]
        pad = jnp.zeros((LANES - ML_NQ * nr, lc), F32)
        c_ref[0, c] = jnp.concatenate(quantities + [pad], axis=0).T


def _mlstm_scan_kernel(q_ref, k_ref, v_ref, a_ref, c_ref, o_ref, cs_ref, *, rev):
    lc = q_ref.shape[1]
    nr = 2 * ML_HEADS

    @pl.when(pl.program_id(1) == 0)
    def _():
        cs_ref[...] = jnp.zeros(cs_ref.shape, F32)

    ti = lax.broadcasted_iota(jnp.int32, (lc, lc), 0)
    si = lax.broadcasted_iota(jnp.int32, (lc, lc), 1)
    mask = (si >= ti) if rev else (si <= ti)
    ones = jnp.ones((lc, LANES), BF16)
    nt = (((1,), (1,)), ((), ()))
    tn = (((0,), (0,)), ((), ()))
    heads = range(ML_HEADS)
    qb = [q_ref[0, :, h * ML_DK:(h + 1) * ML_DK] for h in heads]
    kb = [k_ref[0, :, h * ML_DK:(h + 1) * ML_DK] for h in heads]
    v_aug = [jnp.concatenate([v_ref[0, :, h * ML_DV:(h + 1) * ML_DV], ones], axis=1) for h in heads]
    cols = []
    for h in heads:
        r = (ML_HEADS if rev else 0) + h
        cols.append([a_ref[0, 0, r:r + 1, :]] + [c_ref[0, 0, :, qi * nr + r:qi * nr + r + 1] for qi in range(1, 6)])
    qk = [lax.dot_general(qb[h], kb[h], nt, preferred_element_type=F32) for h in heads]
    c_old = [cs_ref[h] for h in heads]
    q_c = [jnp.dot(qb[h], c_old[h].astype(BF16), preferred_element_type=F32) for h in heads]
    kw = [(kb[h].astype(F32) * cols[h][4]).astype(BF16) for h in heads]
    upd = [lax.dot_general(kw[h], v_aug[h], tn, preferred_element_type=F32) for h in heads]
    s = []
    for h in heads:
        a_row, g_run = cols[h][0], cols[h][1]
        s.append((qk[h] * jnp.where(mask, jnp.exp(a_row - g_run), 0.0)).astype(BF16))
    s_v = [jnp.dot(s[h], v_aug[h], preferred_element_type=F32) for h in heads]
    for h in heads:
        inter, eclamp, decay = cols[h][2], cols[h][3], cols[h][5]
        tot = inter * q_c[h] + s_v[h]
        inv = 1.0 / jnp.maximum(jnp.abs(tot[:, ML_DV:]), eclamp)
        o_ref[0, :, h * ML_DV:(h + 1) * ML_DV] = (
            tot[:, :ML_DV] * jnp.concatenate([inv] * (ML_DV // LANES), axis=1)).astype(o_ref.dtype)
        cs_ref[h] = decay[0:1, :] * c_old[h] + upd[h]


def _mlstm_scan(p, gates, gate_b, tc):
    b, ta, _ = p.shape
    lc = ML_CHUNK
    assert tc % lc == 0 and ta % lc == 0
    nc, ncc = ta // lc, tc // lc
    ng = 4 * ML_HEADS
    nr = 2 * ML_HEADS
    bias = gate_b.reshape(ng)[ML_GATE_PERM].reshape(ng, 1)
    a_rows, cols = pl.pallas_call(
        functools.partial(_mlstm_gate_kernel, tc=tc, lc=lc),
        out_shape=[jax.ShapeDtypeStruct((b, nc, nr, lc), F32), jax.ShapeDtypeStruct((b, nc, lc, LANES), F32)],
        grid=(b,),
        in_specs=[pl.BlockSpec((1, ng, ta), lambda i: (i, 0, 0)),
                  pl.BlockSpec((ng, 1), lambda i: (0, 0))],
        out_specs=[pl.BlockSpec((1, nc, nr, lc), lambda i: (i, 0, 0, 0)),
                   pl.BlockSpec((1, nc, lc, LANES), lambda i: (i, 0, 0, 0))],
        compiler_params=_cparams("parallel"),
        name="mlstm_gates",
    )(gates, bias)
    outs = []
    for d in range(2):
        chunk = functools.partial(_chunk_order, nc, ncc, d)
        outs.append(pl.pallas_call(
            functools.partial(_mlstm_scan_kernel, rev=bool(d)),
            out_shape=jax.ShapeDtypeStruct((b, ta, ML_INNER), BF16),
            grid=(b, nc),
            in_specs=[pl.BlockSpec((1, lc, ML_QK), lambda i, s, chunk=chunk: (i, chunk(s), 0)),
                      pl.BlockSpec((1, lc, ML_QK), lambda i, s, chunk=chunk: (i, chunk(s), 1)),
                      pl.BlockSpec((1, lc, ML_INNER), lambda i, s, chunk=chunk: (i, chunk(s), 1)),
                      pl.BlockSpec((1, 1, nr, lc), lambda i, s, chunk=chunk: (i, chunk(s), 0, 0)),
                      pl.BlockSpec((1, 1, lc, LANES), lambda i, s, chunk=chunk: (i, chunk(s), 0, 0))],
            out_specs=pl.BlockSpec((1, lc, ML_INNER), lambda i, s, chunk=chunk: (i, chunk(s), 0)),
            scratch_shapes=[pltpu.VMEM((ML_HEADS, ML_DK, ML_DV + LANES), F32)],
            compiler_params=_cparams("parallel", "arbitrary"),
            name="mlstm_scan_bwd" if d else "mlstm_scan_fwd",
        )(p, p, p, a_rows, cols))
    return outs


def _hgrn_lb_kernel(p_ref, o_ref, *, layer):
    for d in range(p_ref.shape[0]):
        x = p_ref[d]
        e = jnp.exp(x - jnp.max(x, axis=0, keepdims=True))
        p = e / jnp.sum(e, axis=0, keepdims=True)
        acc = jnp.zeros((1, x.shape[1]), F32)
        for j in range(1, layer + 1):
            acc = acc + p[j:j + 1, :]
        o_ref[d:d + 1, :] = acc


def _cumsum_rows(x, tri2):
    hi = x.astype(BF16)
    lo = (x - hi.astype(F32)).astype(BF16)
    return jnp.dot(tri2, jnp.concatenate([hi, lo], axis=0), preferred_element_type=F32)


def _tri2(n, rev):
    t = np.triu(np.ones((n, n), np.float32)) if rev else np.tril(np.ones((n, n), np.float32))
    return jnp.asarray(np.concatenate([t, t], axis=1), BF16)


def _anchor_rows(a, m, rev):
    n, f = a.shape
    idx = m if rev else m - 1
    if 2 * m >= SUBLANES:
        a3 = a.reshape(n // (2 * m), 2 * m, f)
        return jnp.broadcast_to(a3[:, idx:idx + 1, :], a3.shape).reshape(n, f)
    a3 = a.reshape(n // SUBLANES, SUBLANES, f)
    sub = lax.broadcasted_iota(jnp.int32, a3.shape, 1)
    out = None
    for gi in range(SUBLANES // (2 * m) - 1, -1, -1):
        cand = jnp.broadcast_to(a3[:, gi * 2 * m + idx:gi * 2 * m + idx + 1, :], a3.shape)
        out = cand if out is None else jnp.where(sub < (gi + 1) * 2 * m, cand, out)
    return out.reshape(n, f)


HG_MAX_LOG_SPAN = 64.0


def _hgrn_scan_kernel(q_ref, f_ref, i_ref, tri_ref, o_ref, s_ref, *, rev):
    lc = HG_CHUNK
    subs = range(q_ref.shape[1] // lc)
    subs = tuple(reversed(subs)) if rev else tuple(subs)

    @pl.when(pl.program_id(1) == 0)
    def _():
        s_ref[...] = jnp.zeros(s_ref.shape, F32)

    row = lax.broadcasted_iota(jnp.int32, (lc, 1), 0)
    ti = lax.broadcasted_iota(jnp.int32, (lc, lc), 0)
    si = lax.broadcasted_iota(jnp.int32, (lc, lc), 1)
    nt = (((1,), (1,)), ((), ()))
    tn = (((0,), (0,)), ((), ()))
    end = 0 if rev else lc - 1

    def all_heads_single_anchor():
        causal = (si >= ti) if rev else (si <= ti)
        cols = [slice(h * HG_DIM, (h + 1) * HG_DIM) for h in range(HG_HEADS)]
        pre = {}
        for ci in subs:
            rows = slice(ci * lc, (ci + 1) * lc)
            f = f_ref[0, rows, :]
            a = _cumsum_rows(jnp.log(f), tri_ref[...])
            q_dec = (q_ref[0, rows, :].astype(F32) * jnp.exp(a)).astype(BF16)
            k_inv = (1.0 - f) * jnp.exp(-a)
            k_inv_b = k_inv.astype(BF16)
            e_end = jnp.exp(a[end:end + 1, :])
            kd = (k_inv * e_end).astype(BF16)
            attn = [jnp.where(causal, lax.dot_general(q_dec[:, c], k_inv_b[:, c], nt, preferred_element_type=F32),
                              0.0).astype(BF16) for c in cols]
            pre[ci] = (rows, q_dec, e_end, kd, attn)
        state = [s_ref[h] for h in range(HG_HEADS)]
        for ci in subs:
            rows, q_dec, e_end, kd, attn = pre[ci]
            for h, c in enumerate(cols):
                o = jnp.dot(attn[h], i_ref[0, rows, c], preferred_element_type=F32)
                o = o + lax.dot_general(q_dec[:, c], state[h].astype(BF16), nt, preferred_element_type=F32)
                o_ref[0, rows, c] = o.astype(o_ref.dtype)
            state = [state[h] * e_end[:, c] + lax.dot_general(i_ref[0, rows, c], kd[:, c], tn,
                                                              preferred_element_type=F32)
                     for h, c in enumerate(cols)]
        for h in range(HG_HEADS):
            s_ref[h] = state[h]

    def head_per_level(r0, h, carry):
        c0 = pl.multiple_of(h * HG_DIM, HG_DIM)
        q = q_ref[0, r0:r0 + lc, pl.ds(c0, HG_DIM)].astype(F32)
        f = f_ref[0, r0:r0 + lc, pl.ds(c0, HG_DIM)]
        k = 1.0 - f
        iv = i_ref[0, r0:r0 + lc, pl.ds(c0, HG_DIM)].astype(F32)
        a = _cumsum_rows(jnp.log(f), tri_ref[...])
        attn = jnp.zeros((lc, lc), F32)
        m = 1
        while m < lc:
            e = jnp.exp(-jnp.abs(a - _anchor_rows(a, m, rev)))
            upper = (row % (2 * m)) >= m
            is_q = jnp.logical_not(upper) if rev else upper
            qt = jnp.where(is_q, q * e, 0.0).astype(BF16)
            kt = jnp.where(is_q, 0.0, k * e).astype(BF16)
            pair = lax.dot_general(qt, kt, nt, preferred_element_type=F32)
            attn = attn + jnp.where((ti // (2 * m)) == (si // (2 * m)), pair, 0.0)
            m *= 2
        a_end = a[end:end + 1, :]
        kd = (k * jnp.exp(a_end - a)).astype(BF16)
        ib = iv.astype(BF16)
        s_old = s_ref[h]
        o = jnp.dot(attn.astype(BF16), ib, preferred_element_type=F32)
        o = o + lax.dot_general((q * jnp.exp(a)).astype(BF16), s_old.astype(BF16), nt, preferred_element_type=F32)
        o_ref[0, r0:r0 + lc, pl.ds(c0, HG_DIM)] = (o + jnp.sum(q * k, axis=-1, keepdims=True) * iv).astype(o_ref.dtype)
        s_ref[h] = s_old * jnp.exp(a_end) + lax.dot_general(ib, kd, tn, preferred_element_type=F32)
        return carry

    f_min = f_ref[0, :, 0:HG_DIM]
    for h in range(1, HG_HEADS):
        f_min = jnp.minimum(f_min, f_ref[0, :, h * HG_DIM:(h + 1) * HG_DIM])
    f_min = jnp.min(f_min.reshape(f_min.shape[0] // SUBLANES, SUBLANES, HG_DIM), axis=0)
    in_range = jnp.min(f_min) >= float(np.exp(-HG_MAX_LOG_SPAN / lc))

    @pl.when(in_range)
    def _():
        all_heads_single_anchor()

    @pl.when(jnp.logical_not(in_range))
    def _():
        for ci in subs:
            lax.fori_loop(0, HG_HEADS, functools.partial(head_per_level, ci * lc), 0)


def _hgrn_scan(p, pf, tc, rev):
    b, ta, _ = p.shape
    lc = HG_CHUNK
    rows = HG_STEP_CHUNKS * lc
    hg = HG_HEADS * HG_DIM
    assert tc % rows == 0 and ta % rows == 0
    nc, ncc = ta // rows, tc // rows
    d = 1 if rev else 0
    chunk = lambda s: _chunk_order(nc, ncc, d, s)
    return pl.pallas_call(
        functools.partial(_hgrn_scan_kernel, rev=rev),
        out_shape=jax.ShapeDtypeStruct((b, ta, hg), BF16),
        grid=(b, nc),
        in_specs=[pl.BlockSpec((1, rows, hg), lambda i, s: (i, chunk(s), 0)),
                  pl.BlockSpec((1, rows, hg), lambda i, s: (i, chunk(s), d)),
                  pl.BlockSpec((1, rows, hg), lambda i, s: (i, chunk(s), 1)),
                  pl.BlockSpec((lc, 2 * lc), lambda i, s: (0, 0))],
        out_specs=pl.BlockSpec((1, rows, hg), lambda i, s: (i, chunk(s), 0)),
        scratch_shapes=[pltpu.VMEM((HG_HEADS, HG_DIM, HG_DIM), F32)],
        compiler_params=_cparams("parallel", "arbitrary"),
        name="hgrn_scan_bwd" if rev else "hgrn_scan_fwd",
    )(p, pf, p, _tri2(lc, rev))


def kernel(x, c, ctx, c_ctx, ada_w, ada_b, norm_g, ml_w_in, ml_gate_b, ml_head_g, ml_w_out, at_w_in, at_q_g, at_k_g, at_sink, at_w_out, sc_w_in, sc_conv_w, sc_conv_b, sc_w_out, hg_w_in, hg_f_b, hg_lb, hg_head_g, hg_w_out):
    tc = ctx.shape[1]
    mod = _ada_mod(c, c_ctx, ada_w, ada_b)
    xs = (ctx, x)
    for layer in range(DEPTH):
        kind, j = layer % 4, layer // 4
        need_ctx = layer < DEPTH - 1
        last = dict(lat_only=True) if layer == DEPTH - 1 else {}
        mod_l = mod[layer]
        if kind == 0:
            n_main = 2 * ML_QK + 3 * ML_INNER
            w_gate = jnp.zeros((LANES, ml_w_in.shape[1]), F32).at[:4 * ML_HEADS].set(
                ml_w_in[j][:, n_main:][:, ML_GATE_PERM].T)
            col_scale = jnp.concatenate([jnp.full((ML_QK,), ML_DK ** -0.5, F32), jnp.ones((ML_QK + ML_INNER,), F32),
                                         jnp.full((2 * ML_INNER,), 0.5, F32)])
            p, gates = _inproj(xs, mod_l, norm_g[layer], ml_w_in[j][:, :n_main] * col_scale, tc, tn=1024,
                               w_small=w_gate, out_dtype=BF16)
            h_f, h_b = _mlstm_scan(p, gates, ml_gate_b[j], tc)
            feats = [(h_f, ML_INNER, 0), (h_b, ML_INNER, 0), (p, ML_INNER, 2), (p, ML_INNER, 3)]
            xs = _outproj("mlstm", feats, ml_w_out[j], xs, mod_l, tc, head_g=0.5 * ml_head_g[j], **last)
        elif kind == 1:
            p = _inproj(xs, mod_l, norm_g[layer], _attn_weight(at_w_in[j]), tc, out_dtype=BF16)
            if isinstance(xs, tuple):
                xs = jnp.concatenate(xs, axis=1)
            xs = _attn_mixer(p, at_q_g[j], at_k_g[j], at_sink[j], at_w_out[j], xs, mod_l, tc, need_ctx)
            if last:
                xs = xs[:, tc:, :]
        elif kind == 2:
            if isinstance(xs, tuple):
                xs = jnp.concatenate(xs, axis=1)
            u = _conv_mixer(xs, mod_l, norm_g[layer], sc_w_in[j], sc_conv_w[j], sc_conv_b[j], tc)
            xs = _outproj("plain", [(u, u.shape[2], 0)], sc_w_out[j], xs, mod_l, tc, **last)
        else:
            hg = HG_HEADS * HG_DIM
            lb = pl.pallas_call(
                functools.partial(_hgrn_lb_kernel, layer=layer),
                out_shape=jax.ShapeDtypeStruct((2, hg), F32),
                name="hgrn_lb",
            )(hg_lb[j])
            w = hg_w_in[j]
            w_qiz = jnp.concatenate([0.5 * w[:, :hg], w[:, 3 * hg:4 * hg], 0.5 * w[:, 4 * hg:]], axis=1)
            p = _inproj(xs, mod_l, norm_g[layer], w_qiz, tc, tn=1024, out_dtype=BF16,
                        epilogue=((0, hg, "silu"), (hg, 3 * hg, "id")))
            lb2 = lb.reshape(2 * hg)
            pf = _inproj(xs, mod_l, norm_g[layer], 0.5 * w[:, hg:3 * hg], tc, tn=1024, epilogue=((0, 2 * hg, "fgate"),),
                         pars=(0.5 * hg_f_b[j].reshape(2 * hg), 0.5 * (1.0 + lb2), 0.5 * (1.0 - lb2)))
            o_f = _hgrn_scan(p, pf, tc, rev=False)
            o_b = _hgrn_scan(p, pf, tc, rev=True)
            feats = [(o_f, hg, 0), (o_b, hg, 0), (p, hg, 2)]
            xs = _outproj("hgrn", feats, hg_w_out[j], xs, mod_l, tc, head_g=hg_head_g[j], **last)
    return xs
```

```python
import functools

import numpy as np
import jax
import jax.numpy as jnp
from jax import lax
from jax.experimental import pallas as pl
from jax.experimental.pallas import tpu as pltpu

F32 = jnp.float32
BF16 = jnp.bfloat16
EPS = 1e-6
DEPTH = 4
GRID_W = 64
ROPE_BASE = 10000.0

ML_HEADS, ML_DK, ML_DV = 4, 256, 512
ML_QK = ML_HEADS * ML_DK
ML_INNER = ML_HEADS * ML_DV
ML_CHUNK = 256

AT_HEADS, AT_KV_HEADS, AT_HEAD_DIM = 16, 4, 64
AT_GROUP = AT_HEADS // AT_KV_HEADS
AT_BLOCK = 128
AT_PROJ_BLOCKS = 4
AT_Q = AT_HEADS * AT_HEAD_DIM
AT_KV = AT_KV_HEADS * AT_HEAD_DIM

SC_KSIZE = 3

HG_HEADS, HG_DIM = 8, 128
HG_CHUNK = 128
HG_STEP_CHUNKS = 2

LANES = 128
SUBLANES = 8
VMEM_LIMIT_BYTES = 56 * 1024 * 1024

ROW_BLOCK = 256
PROJ_ROWS = 768
OUT_ROWS = 768
OUT_ROWS_SPLIT = 256


def _cparams(*sem):
    return pltpu.CompilerParams(dimension_semantics=sem, vmem_limit_bytes=VMEM_LIMIT_BYTES)


def _sigmoid(x):
    return 0.5 * jnp.tanh(0.5 * x) + 0.5


def _silu(x):
    return x * _sigmoid(x)


def _silu_of_half(xh):
    return xh * (1.0 + jnp.tanh(xh))


def _ada_kernel(c_ref, w_ref, b_ref, o_ref):
    s = _silu(c_ref[...])
    o_ref[0] = jnp.dot(s.astype(BF16), w_ref[0].astype(BF16), preferred_element_type=F32) + b_ref[0]


def _ada_mod(c, c_ctx, ada_w, ada_b):
    b, d = c.shape
    depth = ada_w.shape[0]
    rows = -(-(b + 1) // SUBLANES) * SUBLANES
    cc = jnp.zeros((rows, d), F32).at[:b].set(c).at[b].set(c_ctx)
    tn = 1024
    out = pl.pallas_call(
        _ada_kernel,
        out_shape=jax.ShapeDtypeStruct((depth, rows, 3 * d), F32),
        grid=(depth, 3 * d // tn),
        in_specs=[pl.BlockSpec((rows, d), lambda l, j: (0, 0)),
                  pl.BlockSpec((1, d, tn), lambda l, j: (l, 0, j)),
                  pl.BlockSpec((1, 1, tn), lambda l, j: (l, 0, j))],
        out_specs=pl.BlockSpec((1, rows, tn), lambda l, j: (l, 0, j)),
        compiler_params=_cparams("parallel", "parallel"),
        name="ada_mod",
    )(cc, ada_w, ada_b.reshape(depth, 1, 3 * d))
    return out.reshape(depth, rows, 3, d)


def _modulated(x_refs, ml_ref, mc_ref, g_ref, h_ref, tc, row0=0, nrows=None):
    nrows = h_ref.shape[0] - row0 if nrows is None else nrows
    g = g_ref[...]
    xc_ref = x_refs[0]
    xl_ref, lat0 = (x_refs[1], 0) if len(x_refs) == 2 else (x_refs[0], tc)
    for r0 in range(row0, row0 + nrows, ROW_BLOCK):
        ctx_rows = r0 < tc
        x_ref, src0, m_ref = (xc_ref, r0, mc_ref) if ctx_rows else (xl_ref, lat0 + r0 - tc, ml_ref)
        x = x_ref[0, src0:src0 + ROW_BLOCK, :]
        ms = jnp.mean(x * x, axis=-1, keepdims=True)
        xn = x * lax.rsqrt(ms + EPS) * g
        h = xn * (1.0 + m_ref[0, 1:2, :]) + m_ref[0, 0:1, :]
        h_ref[r0:r0 + ROW_BLOCK, :] = h.astype(BF16)


def _split_stream(xs):
    arrs = list(xs) if isinstance(xs, tuple) else [xs]
    specs = [pl.BlockSpec((1,) + a.shape[1:], lambda i, j: (i, 0, 0)) for a in arrs]
    ta = sum(a.shape[1] for a in arrs)
    return arrs, specs, ta


def _inproj_kernel(*refs, tc, n_x, has_small, epilogue):
    x_refs, (ml_ref, mc_ref, g_ref, w_ref), rest = refs[:n_x], refs[n_x:n_x + 4], list(refs[n_x + 4:])
    ws_ref = rest.pop(0) if has_small else None
    par_refs = [rest.pop(0) for _ in range(3)] if epilogue else None
    o_ref = rest.pop(0)
    os_ref = rest.pop(0) if has_small else None
    h_ref = rest.pop(0)

    ta = h_ref.shape[0]
    chunk = PROJ_ROWS if ta % PROJ_ROWS == 0 else ta
    j = pl.program_id(1)

    def activation(acc):
        if not epilogue:
            return acc
        bias_ref, c0_ref, c1_ref = par_refs
        t = jnp.tanh(acc + bias_ref[...])
        out = acc
        for lo, hi, kind in epilogue:
            if kind == "id":
                continue
            val = acc + acc * t if kind == "silu" else c0_ref[...] + c1_ref[...] * t
            out = jnp.where((j >= lo) & (j < hi), val, out)
        return out

    def project(first):
        step = chunk if (first or epilogue) else ta
        for r0 in range(0, ta, step):
            if first:
                _modulated(x_refs, ml_ref, mc_ref, g_ref, h_ref, tc, r0, step)
            acc = jnp.dot(h_ref[r0:r0 + step, :], w_ref[...], preferred_element_type=F32)
            o_ref[0, r0:r0 + step, :] = activation(acc).astype(o_ref.dtype)

    @pl.when(j == 0)
    def _():
        project(True)
        if has_small:
            os_ref[0] = lax.dot_general(ws_ref[...], h_ref[...], (((1,), (1,)), ((), ())),
                                        preferred_element_type=F32)

    @pl.when(j != 0)
    def _():
        project(False)


def _inproj(xs, mod_l, norm_g, w, tc, tn=512, w_small=None, out_dtype=F32, epilogue=None, pars=None):
    x_arrs, x_specs, ta = _split_stream(xs)
    b, d = x_arrs[0].shape[0], x_arrs[0].shape[2]
    n = w.shape[1]
    assert n % tn == 0 and tc % ROW_BLOCK == 0 and (ta - tc) % ROW_BLOCK == 0
    mod_lat = mod_l[:b]
    mod_ctx = mod_l[b:b + 1]
    has_small = w_small is not None
    in_specs = x_specs + [pl.BlockSpec((1, 3, d), lambda i, j: (i, 0, 0)),
                          pl.BlockSpec((1, 3, d), lambda i, j: (0, 0, 0)),
                          pl.BlockSpec((1, d), lambda i, j: (0, 0)),
                          pl.BlockSpec((d, tn), lambda i, j: (0, j))]
    args = x_arrs + [mod_lat, mod_ctx, norm_g.reshape(1, d), w.astype(BF16)]
    out_shape = [jax.ShapeDtypeStruct((b, ta, n), out_dtype)]
    out_specs = [pl.BlockSpec((1, ta, tn), lambda i, j: (i, 0, j))]
    if has_small:
        ns = w_small.shape[0]
        in_specs.append(pl.BlockSpec((ns, d), lambda i, j: (0, 0)))
        args.append(w_small.astype(BF16))
    if epilogue:
        assert all(lo % tn == 0 and hi % tn == 0 for lo, hi, _ in epilogue)
        epilogue = tuple((lo // tn, hi // tn, kind) for lo, hi, kind in epilogue)
        zeros = jnp.zeros((n,), F32)
        for par in (pars if pars is not None else (zeros,) * 3):
            in_specs.append(pl.BlockSpec((1, tn), lambda i, j: (0, j)))
            args.append(par.reshape(1, n))
    if has_small:
        out_shape.append(jax.ShapeDtypeStruct((b, ns, ta), F32))
        out_specs.append(pl.BlockSpec((1, ns, ta), lambda i, j: (i, 0, 0)))
    res = pl.pallas_call(
        functools.partial(_inproj_kernel, tc=tc, n_x=len(x_arrs), has_small=has_small, epilogue=epilogue),
        out_shape=out_shape,
        grid=(b, n // tn),
        in_specs=in_specs,
        out_specs=out_specs,
        scratch_shapes=[pltpu.VMEM((ta, d), BF16)],
        compiler_params=_cparams("parallel", "arbitrary"),
        name="inproj",
    )(*args)
    return res if has_small else res[0]


def _head_rms(h, gain, n_heads):
    dh = h.shape[1] // n_heads
    parts = []
    for i in range(n_heads):
        hh = h[:, i * dh:(i + 1) * dh]
        ms = jnp.mean(hh * hh, axis=-1, keepdims=True)
        parts.append(hh * lax.rsqrt(ms + EPS))
    return jnp.concatenate(parts, axis=1) * gain


def _outproj_kernel(*refs, mode, n_feat, n_x, tc, tm, row0):
    feats, rest = refs[:n_feat], list(refs[n_feat:])
    hg_ref = rest.pop(0) if mode != "plain" else None
    w_ref = rest.pop(0)
    x_refs = [rest.pop(0) for _ in range(n_x)]
    ml_ref, mc_ref, o_ref = rest
    if mode == "plain":
        u = feats[0][0]
    elif mode == "mlstm":
        h0_ref, h1_ref, og_ref, z_ref = feats
        hn = _head_rms(h0_ref[0].astype(F32) + h1_ref[0].astype(F32), hg_ref[...], ML_HEADS)
        u = (hn * (1.0 + jnp.tanh(og_ref[0].astype(F32))) * _silu_of_half(z_ref[0].astype(F32))).astype(BF16)
    else:
        h0_ref, h1_ref, z_ref = feats
        hn = _head_rms(h0_ref[0].astype(F32) + h1_ref[0].astype(F32), hg_ref[...], HG_HEADS)
        u = (hn * _silu_of_half(z_ref[0].astype(F32))).astype(BF16)
    y = jnp.dot(u, w_ref[...], preferred_element_type=F32)
    first = row0 + pl.program_id(1) * tm
    row = first + lax.broadcasted_iota(jnp.int32, (tm, 1), 0)
    gate = jnp.where(row < tc, mc_ref[0, 2:3, :], ml_ref[0, 2:3, :])
    x = x_refs[0][0] if n_x == 1 else jnp.where(first < tc, x_refs[0][0], x_refs[1][0])
    o_ref[0] = x + gate * y


def _outproj(mode, feats, w_out, xs, mod_l, tc, head_g=None, lat_only=False):
    x_arrs = list(xs) if isinstance(xs, tuple) else [xs]
    b, d = x_arrs[0].shape[0], x_arrs[0].shape[2]
    ta = sum(a.shape[1] for a in x_arrs)
    kdim = w_out.shape[0]
    tm = OUT_ROWS if (len(x_arrs) == 1 and not lat_only and ta % OUT_ROWS == 0) else OUT_ROWS_SPLIT
    assert ta % tm == 0 and (tc % tm == 0 or (len(x_arrs) == 1 and not lat_only))
    off = tc // tm if lat_only else 0
    nct = tc // tm
    row_spec = lambda width, col: pl.BlockSpec((1, tm, width), lambda i, r, col=col: (i, r + off, col))
    in_specs, args = [], []
    for arr, width, col in feats:
        in_specs.append(row_spec(width, col))
        args.append(arr)
    if head_g is not None:
        in_specs.append(pl.BlockSpec((1, kdim), lambda i, r: (0, 0)))
        args.append(head_g.reshape(1, kdim))
    in_specs.append(pl.BlockSpec((kdim, d), lambda i, r: (0, 0)))
    args.append(w_out.astype(BF16))
    if len(x_arrs) == 1:
        in_specs.append(row_spec(d, 0))
    elif lat_only:
        x_arrs = x_arrs[1:]
        in_specs.append(pl.BlockSpec((1, tm, d), lambda i, r: (i, r, 0)))
    else:
        in_specs += [pl.BlockSpec((1, tm, d), lambda i, r: (i, jnp.minimum(r, nct - 1), 0)),
                     pl.BlockSpec((1, tm, d), lambda i, r: (i, jnp.maximum(r - nct, 0), 0))]
    args += x_arrs
    in_specs += [pl.BlockSpec((1, 3, d), lambda i, r: (i, 0, 0)),
                 pl.BlockSpec((1, 3, d), lambda i, r: (0, 0, 0))]
    args += [mod_l[:b], mod_l[b:b + 1]]
    rows_out = ta - off * tm
    return pl.pallas_call(
        functools.partial(_outproj_kernel, mode=mode, n_feat=len(feats), n_x=len(x_arrs), tc=tc, tm=tm,
                          row0=off * tm),
        out_shape=jax.ShapeDtypeStruct((b, rows_out, d), F32),
        grid=(b, rows_out // tm),
        in_specs=in_specs,
        out_specs=pl.BlockSpec((1, tm, d), lambda i, r: (i, r, 0)),
        compiler_params=_cparams("parallel", "parallel"),
        name="outproj_" + mode,
    )(*args)


def _conv_kernel(x_ref, ml_ref, mc_ref, g_ref, wx_ref, wb_ref, wc_ref, wz_ref, cw_ref, cb_ref, o_ref, h_ref, *, tc):
    ta = h_ref.shape[0]
    chunk = PROJ_ROWS if ta % PROJ_ROWS == 0 else ta

    def body(first_tile):
        wx, wc, wb, wz = (w_ref[...].astype(BF16) for w_ref in (wx_ref, wc_ref, wb_ref, wz_ref))
        step = chunk if first_tile else ta
        xin, cg = [], []
        for r0 in range(0, ta, step):
            if first_tile:
                _modulated((x_ref,), ml_ref, mc_ref, g_ref, h_ref, tc, r0, step)
            h = h_ref[r0:r0 + step, :]
            xin.append(jnp.dot(h, wx, preferred_element_type=F32))
            cg.append(jnp.dot(h, wc, preferred_element_type=F32))
        bg = jnp.dot(h_ref[...], wb, preferred_element_type=F32)
        z = jnp.dot(h_ref[...], wz, preferred_element_type=F32)
        u = jnp.concatenate(cg, axis=0) * jnp.concatenate(xin, axis=0)
        row = lax.broadcasted_iota(jnp.int32, (ta, 1), 0)
        first = (row == 0) | (row == tc)
        last = (row == tc - 1) | (row == ta - 1)
        u_prev = jnp.where(first, 0.0, pltpu.roll(u, 1, axis=0))
        u_next = jnp.where(last, 0.0, pltpu.roll(u, ta - 1, axis=0))
        cw = cw_ref[...]
        y = u_prev * cw[0:1, :] + u * cw[1:2, :] + u_next * cw[2:3, :] + cb_ref[...]
        o_ref[0] = (bg * y * _silu(z)).astype(o_ref.dtype)

    @pl.when(pl.program_id(1) == 0)
    def _():
        body(True)

    @pl.when(pl.program_id(1) != 0)
    def _():
        body(False)


def _conv_mixer(xs, mod_l, norm_g, w_in, conv_w, conv_b, tc, tw=256):
    b, ta, d = xs.shape
    e = conv_w.shape[1]
    nt = e // tw
    w_spec = lambda part: pl.BlockSpec((d, tw), lambda i, j, part=part: (0, part * nt + j))
    return pl.pallas_call(
        functools.partial(_conv_kernel, tc=tc),
        out_shape=jax.ShapeDtypeStruct((b, ta, e), BF16),
        grid=(b, nt),
        in_specs=[pl.BlockSpec((1, ta, d), lambda i, j: (i, 0, 0)),
                  pl.BlockSpec((1, 3, d), lambda i, j: (i, 0, 0)),
                  pl.BlockSpec((1, 3, d), lambda i, j: (0, 0, 0)),
                  pl.BlockSpec((1, d), lambda i, j: (0, 0)),
                  w_spec(0), w_spec(1), w_spec(2), w_spec(3),
                  pl.BlockSpec((SC_KSIZE, tw), lambda i, j: (0, j)),
                  pl.BlockSpec((1, tw), lambda i, j: (0, j))],
        out_specs=pl.BlockSpec((1, ta, tw), lambda i, j: (i, 0, j)),
        scratch_shapes=[pltpu.VMEM((ta, d), BF16)],
        compiler_params=_cparams("parallel", "arbitrary"),
        name="conv_mixer",
    )(xs, mod_l[:b], mod_l[b:b + 1], norm_g.reshape(1, d), w_in, w_in, w_in, w_in, conv_w, conv_b.reshape(1, e))


AT_HALF = AT_HEAD_DIM // 2
AT_TILE_PERM = np.concatenate([np.arange(0, AT_HALF), np.arange(2 * AT_HALF, 3 * AT_HALF),
                               np.arange(AT_HALF, 2 * AT_HALF), np.arange(3 * AT_HALF, 4 * AT_HALF)])
LOG2E = float(np.log2(np.e))


def _rope_tables(tc, t):
    rows = t // GRID_W
    row = np.repeat(np.arange(rows), GRID_W).astype(np.float32)
    col = np.tile(np.arange(GRID_W), rows).astype(np.float32)
    n_freq = AT_HEAD_DIM // 4
    freqs = jnp.power(ROPE_BASE, -jnp.arange(n_freq, dtype=F32) / n_freq)
    ang = jnp.concatenate([jnp.asarray(row)[:, None] * freqs, jnp.asarray(col)[:, None] * freqs], axis=-1)
    cos, sin = jnp.cos(ang), jnp.sin(ang)
    cos_t = jnp.concatenate([jnp.ones((tc, LANES), F32), jnp.tile(cos, (1, 4))], axis=0)
    sin_t = jnp.concatenate([jnp.zeros((tc, LANES), F32), jnp.concatenate([-sin, -sin, sin, sin], axis=-1)], axis=0)
    return cos_t, sin_t


def _norm_rope_tiles(xs, gains, cos, sin, same_head):
    sums = []
    for x in xs:
        x2 = x * x
        hi = x2.astype(BF16)
        lo = (x2 - hi.astype(F32)).astype(BF16)
        sums.append((jnp.dot(hi, same_head, preferred_element_type=F32),
                     jnp.dot(lo, same_head, preferred_element_type=F32)))
    out = []
    for x, gain, (s_hi, s_lo) in zip(xs, gains, sums):
        xn = x * lax.rsqrt((s_hi + s_lo) * (1.0 / AT_HEAD_DIM) + EPS) * gain
        out.append(xn * cos + pltpu.roll(xn, 2 * AT_HALF, axis=1) * sin)
    return out


def _attn_kernel(sink_ref, q_ref, k_ref, v_ref, z_ref, qg_ref, kg_ref, cos_ref, sin_ref, sh_ref, wo_ref, x_ref,
                 ml_ref, mc_ref, o_ref, qs_ref, ka_ref, kb_ref, va_ref, vb_ref, *, tc, need_ctx):
    ta = q_ref.shape[1]
    t = ta - tc
    nb = t // AT_BLOCK
    blk = AT_BLOCK
    gw = x_ref.shape[2]
    g = pl.program_id(1)
    odd = (g % 2) == 1
    scale = (AT_HEAD_DIM ** -0.5) * LOG2E

    @pl.when(g == 0)
    def _():
        o_ref[...] = jnp.zeros(o_ref.shape, F32)

    def prep(i, carry):
        r0 = pl.multiple_of(i * ROW_BLOCK, ROW_BLOCK)
        res_cols = pl.ds(pl.multiple_of(g * gw, gw), gw)
        o_ref[0, pl.ds(r0, ROW_BLOCK), res_cols] = (o_ref[0, pl.ds(r0, ROW_BLOCK), res_cols]
                                                    + x_ref[0, pl.ds(r0, ROW_BLOCK), :])
        cos = cos_ref[pl.ds(r0, ROW_BLOCK), :]
        sin = sin_ref[pl.ds(r0, ROW_BLOCK), :]
        same_head = sh_ref[...]
        tiles = [q_ref[0, pl.ds(r0, ROW_BLOCK), c * LANES:(c + 1) * LANES].astype(F32) for c in range(2)]
        tiles.append(k_ref[0, pl.ds(r0, ROW_BLOCK), :].astype(F32))
        q0, q1, kn = _norm_rope_tiles(tiles, (qg_ref[...], qg_ref[...], kg_ref[...]), cos, sin, same_head)
        for c, qc in enumerate((q0, q1)):
            qs_ref[pl.ds(r0, ROW_BLOCK), c * LANES:(c + 1) * LANES] = (qc * scale).astype(BF16)
        vv = v_ref[0, pl.ds(r0, ROW_BLOCK), :].astype(F32)
        lane = lax.broadcasted_iota(jnp.int32, kn.shape, 1)
        k_own = jnp.where(((lane // AT_HALF) % 2) == (g % 2), kn, 0.0)
        k_oth = pltpu.roll(k_own, jnp.where(odd, 3 * AT_HALF, AT_HALF), axis=1)
        ka_ref[pl.ds(r0, ROW_BLOCK), :] = jnp.where(odd, k_oth, k_own).astype(BF16)
        kb_ref[pl.ds(r0, ROW_BLOCK), :] = jnp.where(odd, k_own, k_oth).astype(BF16)
        v_own = jnp.where((lane // AT_HEAD_DIM) == (g % 2), vv, 0.0)
        v_oth = pltpu.roll(v_own, AT_HEAD_DIM, axis=1)
        va = jnp.where(odd, v_oth, v_own)
        vb = jnp.where(odd, v_own, v_oth)
        va_ref[pl.ds(r0, ROW_BLOCK), :] = jnp.where(lane == AT_HEAD_DIM, 1.0, va).astype(BF16)
        vb_ref[pl.ds(r0, ROW_BLOCK), :] = jnp.where(lane == 0, 1.0, vb).astype(BF16)
        return carry

    lax.fori_loop(0, ta // ROW_BLOCK, prep, 0)
    zeros = jnp.zeros((blk, LANES), BF16)
    for ref in (ka_ref, kb_ref, va_ref, vb_ref):
        ref[ta:ta + blk, :] = zeros

    half = lax.broadcasted_iota(jnp.int32, (2 * blk, 1), 0) < blk
    sink_a = jnp.where(half, sink_ref[g, 0], sink_ref[g, 2]) * LOG2E
    sink_b = jnp.where(half, sink_ref[g, 1], sink_ref[g, 3]) * LOG2E

    qi = lax.broadcasted_iota(jnp.int32, (2 * blk, 3 * blk), 0) % blk
    kj = lax.broadcasted_iota(jnp.int32, (2 * blk, 3 * blk), 1)
    band = (kj - qi >= 0) & (kj - qi <= 2 * blk)
    out_lo = lax.broadcasted_iota(jnp.int32, (2 * blk, LANES), 1) < AT_HEAD_DIM

    nt = (((1,), (1,)), ((), ()))

    sides = ((ka_ref, va_ref, sink_a, AT_HEAD_DIM), (kb_ref, vb_ref, sink_b, 0))

    def attend(blocks, r_first, gate):
        qts = [jnp.concatenate([qs_ref[pl.ds(r0, blk), 0:LANES], qs_ref[pl.ds(r0, blk), LANES:2 * LANES]], axis=0)
               for r0, _, _ in blocks]
        scores = []
        for qt, (_, k0, mask) in zip(qts, blocks):
            for k_ref_, _, _, _ in sides:
                s_ctx = lax.dot_general(qt, k_ref_[0:tc, :], nt, preferred_element_type=F32)
                s_loc = None if mask is None else lax.dot_general(qt, k_ref_[pl.ds(k0, 3 * blk), :], nt,
                                                                  preferred_element_type=F32)
                scores.append((s_ctx, s_loc))
        probs = []
        for idx, (s_ctx, s_loc) in enumerate(scores):
            mask = blocks[idx // 2][2]
            sink = sides[idx % 2][2]
            m = jnp.maximum(sink, jnp.max(s_ctx, axis=-1, keepdims=True))
            if s_loc is not None:
                s_loc = jnp.where(mask, s_loc, -jnp.inf)
                m = jnp.maximum(m, jnp.max(s_loc, axis=-1, keepdims=True))
            probs.append((jnp.exp2(s_ctx - m).astype(BF16),
                          None if s_loc is None else jnp.exp2(s_loc - m).astype(BF16), jnp.exp2(sink - m)))
        accs = []
        for idx, (p_ctx, p_loc, _) in enumerate(probs):
            k0 = blocks[idx // 2][1]
            v_ref_ = sides[idx % 2][1]
            acc = jnp.dot(p_ctx, v_ref_[0:tc, :], preferred_element_type=F32)
            if p_loc is not None:
                acc = acc + jnp.dot(p_loc, v_ref_[pl.ds(k0, 3 * blk), :], preferred_element_type=F32)
            accs.append(acc)
        us = []
        for bi, (r0, _, _) in enumerate(blocks):
            halves = []
            for si in range(2):
                acc, ones_lane = accs[2 * bi + si], sides[si][3]
                halves.append(acc / (probs[2 * bi + si][2] + acc[:, ones_lane:ones_lane + 1]))
            o = jnp.where(out_lo, halves[0], halves[1])
            parts = []
            for c in range(2):
                zc = z_ref[0, pl.ds(r0, blk), c * LANES:(c + 1) * LANES].astype(F32)
                parts.append((o[c * blk:(c + 1) * blk] * _silu_of_half(zc)).astype(BF16))
            us.append(jnp.concatenate(parts, axis=1))
        u = jnp.concatenate(us, axis=0)
        rows = pl.ds(r_first, u.shape[0])
        o_ref[0, rows, :] = o_ref[0, rows, :] + gate * jnp.dot(u, wo_ref[...], preferred_element_type=F32)

    def lat_blocks(i, carry):
        blocks = []
        for j in range(AT_PROJ_BLOCKS):
            n = i * AT_PROJ_BLOCKS + j
            kpos = (n - 1) * blk + kj
            blocks.append((pl.multiple_of(tc + n * blk, blk), pl.multiple_of(tc + (n - 1) * blk, blk),
                           band & (kpos >= 0) & (kpos < t)))
        attend(blocks, pl.multiple_of(tc + i * AT_PROJ_BLOCKS * blk, AT_PROJ_BLOCKS * blk), ml_ref[0, 2:3, :])
        return carry

    lax.fori_loop(0, nb // AT_PROJ_BLOCKS, lat_blocks, 0)
    if need_ctx:
        attend([(n * blk, None, None) for n in range(tc // blk)], 0, mc_ref[0, 2:3, :])


def _attn_mixer(p, at_q_g, at_k_g, at_sink, w_out, xs, mod_l, tc, need_ctx):
    b, ta, _ = p.shape
    t = ta - tc
    d = xs.shape[2]
    assert tc >= AT_BLOCK and tc % AT_BLOCK == 0 and t % (AT_PROJ_BLOCKS * AT_BLOCK) == 0 and ta % ROW_BLOCK == 0
    assert d == AT_Q
    cos_t, sin_t = _rope_tables(tc, t)
    lane_head = (np.arange(LANES) // AT_HALF) % 2
    same_head = jnp.asarray(lane_head[:, None] == lane_head[None, :], BF16)
    tile_gain = lambda gain: jnp.tile(gain, 2)[AT_TILE_PERM].reshape(1, LANES)
    gw = AT_GROUP * AT_HEAD_DIM
    kcol = AT_Q // LANES
    vcol = (AT_Q + AT_KV) // LANES
    zcol = (AT_Q + 2 * AT_KV) // gw
    return pl.pallas_call(
        functools.partial(_attn_kernel, tc=tc, need_ctx=need_ctx),
        out_shape=jax.ShapeDtypeStruct((b, ta, d), F32),
        grid=(b, AT_KV_HEADS),
        in_specs=[pl.BlockSpec(memory_space=pltpu.SMEM),
                  pl.BlockSpec((1, ta, gw), lambda i, g: (i, 0, g)),
                  pl.BlockSpec((1, ta, LANES), lambda i, g: (i, 0, kcol + g // 2)),
                  pl.BlockSpec((1, ta, LANES), lambda i, g: (i, 0, vcol + g // 2)),
                  pl.BlockSpec((1, ta, gw), lambda i, g: (i, 0, zcol + g)),
                  pl.BlockSpec((1, LANES), lambda i, g: (0, 0)),
                  pl.BlockSpec((1, LANES), lambda i, g: (0, 0)),
                  pl.BlockSpec((ta, LANES), lambda i, g: (0, 0)),
                  pl.BlockSpec((ta, LANES), lambda i, g: (0, 0)),
                  pl.BlockSpec((LANES, LANES), lambda i, g: (0, 0)),
                  pl.BlockSpec((gw, d), lambda i, g: (g, 0)),
                  pl.BlockSpec((1, ta, gw), lambda i, g: (i, 0, g)),
                  pl.BlockSpec((1, 3, d), lambda i, g: (i, 0, 0)),
                  pl.BlockSpec((1, 3, d), lambda i, g: (0, 0, 0))],
        out_specs=pl.BlockSpec((1, ta, d), lambda i, g: (i, 0, 0)),
        scratch_shapes=[pltpu.VMEM((ta, gw), BF16)] + [pltpu.VMEM((ta + AT_BLOCK, LANES), BF16)] * 4,
        compiler_params=_cparams("parallel", "arbitrary"),
        name="attn_mixer",
    )(at_sink.reshape(AT_KV_HEADS, AT_GROUP), p, p, p, p, tile_gain(at_q_g), tile_gain(at_k_g), cos_t, sin_t,
      same_head, w_out.astype(BF16), xs, mod_l[:b], mod_l[b:b + 1])


def _attn_weight_kernel(w_ref, p_ref, o_ref):
    o_ref[...] = jnp.dot(w_ref[...].astype(BF16), p_ref[0], preferred_element_type=F32).astype(BF16)


def _attn_weight(w_in):
    d, n = w_in.shape
    n_qk, n_v = (AT_Q + AT_KV) // LANES, AT_KV // LANES
    perm = np.zeros((LANES, LANES), np.float32)
    perm[AT_TILE_PERM, np.arange(LANES)] = 1.0
    eye = np.eye(LANES, dtype=np.float32)
    mats = jnp.asarray(np.stack([perm, eye, 0.5 * eye]), BF16)
    return pl.pallas_call(
        _attn_weight_kernel,
        out_shape=jax.ShapeDtypeStruct((d, n), BF16),
        grid=(n // LANES,),
        in_specs=[pl.BlockSpec((d, LANES), lambda j: (0, j)),
                  pl.BlockSpec((1, LANES, LANES), lambda j: ((j >= n_qk).astype(jnp.int32)
                                                             + (j >= n_qk + n_v).astype(jnp.int32), 0, 0))],
        out_specs=pl.BlockSpec((d, LANES), lambda j: (0, j)),
        compiler_params=_cparams("parallel"),
        name="attn_weight",
    )(w_in, mats)


def _chunk_order(nc, ncc, d, s):
    bwd = jnp.where(s < ncc, ncc - 1 - s, nc - 1 - (s - ncc))
    return jnp.where(d == 0, s, bwd)


ML_GATE_PERM = np.concatenate([np.arange(0, 4), np.arange(8, 12), np.arange(4, 8), np.arange(12, 16)])
ML_NQ = 6


def _mlstm_gate_kernel(g_ref, b_ref, a_ref, c_ref, *, tc, lc):
    h = ML_HEADS
    ta = g_ref.shape[2]
    nc, ncc = ta // lc, tc // lc
    x = g_ref[0] + b_ref[...]
    li = x[0:2 * h]
    lfp = x[2 * h:4 * h]
    lf = jnp.minimum(lfp, 0.0) - jnp.log1p(jnp.exp(-jnp.abs(lfp)))
    fwd = lax.broadcasted_iota(jnp.int32, (2 * h, ta), 0) < h
    fwd1 = lax.broadcasted_iota(jnp.int32, (2 * h, 1), 0) < h
    pos = lax.broadcasted_iota(jnp.int32, (2 * h, ta), 1) % lc

    def seg_scan(y, op, fill):
        yf, yb = y, y
        s = 1
        while s < lc:
            yf = op(yf, jnp.where(pos >= s, pltpu.roll(yf, s, axis=1), fill))
            yb = op(yb, jnp.where(pos < lc - s, pltpu.roll(yb, ta - s, axis=1), fill))
            s *= 2
        return jnp.where(fwd, yf, yb)

    bsum = seg_scan(lf, jnp.add, 0.0)
    a = li - bsum
    cmax = seg_scan(a, jnp.maximum, -jnp.inf)

    def end_col(y, c):
        return jnp.where(fwd1, y[:, (c + 1) * lc - 1:(c + 1) * lc], y[:, c * lc:c * lc + 1])

    tot = [end_col(bsum, c) for c in range(nc)]
    amax = [end_col(cmax, c) for c in range(nc)]

    def chain(order):
        m = jnp.zeros((2 * h, 1), F32)
        m_in = [None] * nc
        for c in order:
            m_in[c] = m
            m = tot[c] + jnp.maximum(m, amax[c])
        return m_in

    m_f = chain(list(range(nc)))
    m_b = chain(list(range(ncc - 1, -1, -1)) + list(range(nc - 1, ncc - 1, -1)))
    for c in range(nc):
        m_in = jnp.where(fwd1, m_f[c], m_b[c])
        sl = slice(c * lc, (c + 1) * lc)
        g_run = jnp.maximum(m_in, cmax[:, sl])
        g_end = jnp.maximum(m_in, amax[c])
        nr = 2 * h
        quantities = [a[:, sl], g_run, jnp.exp(m_in - g_run), jnp.exp(-(bsum[:, sl] + g_run)),
                      jnp.exp(a[:, sl] - g_end), jnp.broadcast_to(jnp.exp(m_in - g_end), (nr, lc))]
        a_ref[0, c] = quantities[0]
        pad = jnp.zeros((LANES - ML_NQ * nr, lc), F32)
        c_ref[0, c] = jnp.concatenate(quantities + [pad], axis=0).T


def _mlstm_scan_kernel(q_ref, k_ref, v_ref, a_ref, c_ref, o_ref, cs_ref, *, rev):
    lc = q_ref.shape[1]
    nr = 2 * ML_HEADS

    @pl.when(pl.program_id(1) == 0)
    def _():
        cs_ref[...] = jnp.zeros(cs_ref.shape, F32)

    ti = lax.broadcasted_iota(jnp.int32, (lc, lc), 0)
    si = lax.broadcasted_iota(jnp.int32, (lc, lc), 1)
    mask = (si >= ti) if rev else (si <= ti)
    ones = jnp.ones((lc, LANES), BF16)
    nt = (((1,), (1,)), ((), ()))
    tn = (((0,), (0,)), ((), ()))
    heads = range(ML_HEADS)
    qb = [q_ref[0, :, h * ML_DK:(h + 1) * ML_DK] for h in heads]
    kb = [k_ref[0, :, h * ML_DK:(h + 1) * ML_DK] for h in heads]
    v_aug = [jnp.concatenate([v_ref[0, :, h * ML_DV:(h + 1) * ML_DV], ones], axis=1) for h in heads]
    cols = []
    for h in heads:
        r = (ML_HEADS if rev else 0) + h
        cols.append([a_ref[0, 0, r:r + 1, :]] + [c_ref[0, 0, :, qi * nr + r:qi * nr + r + 1] for qi in range(1, 6)])
    qk = [lax.dot_general(qb[h], kb[h], nt, preferred_element_type=F32) for h in heads]
    c_old = [cs_ref[h] for h in heads]
    q_c = [jnp.dot(qb[h], c_old[h].astype(BF16), preferred_element_type=F32) for h in heads]
    kw = [(kb[h].astype(F32) * cols[h][4]).astype(BF16) for h in heads]
    upd = [lax.dot_general(kw[h], v_aug[h], tn, preferred_element_type=F32) for h in heads]
    s = []
    for h in heads:
        a_row, g_run = cols[h][0], cols[h][1]
        s.append((qk[h] * jnp.where(mask, jnp.exp(a_row - g_run), 0.0)).astype(BF16))
    s_v = [jnp.dot(s[h], v_aug[h], preferred_element_type=F32) for h in heads]
    for h in heads:
        inter, eclamp, decay = cols[h][2], cols[h][3], cols[h][5]
        tot = inter * q_c[h] + s_v[h]
        inv = 1.0 / jnp.maximum(jnp.abs(tot[:, ML_DV:]), eclamp)
        o_ref[0, :, h * ML_DV:(h + 1) * ML_DV] = (
            tot[:, :ML_DV] * jnp.concatenate([inv] * (ML_DV // LANES), axis=1)).astype(o_ref.dtype)
        cs_ref[h] = decay[0:1, :] * c_old[h] + upd[h]


def _mlstm_scan(p, gates, gate_b, tc):
    b, ta, _ = p.shape
    lc = ML_CHUNK
    assert tc % lc == 0 and ta % lc == 0
    nc, ncc = ta // lc, tc // lc
    ng = 4 * ML_HEADS
    nr = 2 * ML_HEADS
    bias = gate_b.reshape(ng)[ML_GATE_PERM].reshape(ng, 1)
    a_rows, cols = pl.pallas_call(
        functools.partial(_mlstm_gate_kernel, tc=tc, lc=lc),
        out_shape=[jax.ShapeDtypeStruct((b, nc, nr, lc), F32), jax.ShapeDtypeStruct((b, nc, lc, LANES), F32)],
        grid=(b,),
        in_specs=[pl.BlockSpec((1, ng, ta), lambda i: (i, 0, 0)),
                  pl.BlockSpec((ng, 1), lambda i: (0, 0))],
        out_specs=[pl.BlockSpec((1, nc, nr, lc), lambda i: (i, 0, 0, 0)),
                   pl.BlockSpec((1, nc, lc, LANES), lambda i: (i, 0, 0, 0))],
        compiler_params=_cparams("parallel"),
        name="mlstm_gates",
    )(gates, bias)
    outs = []
    for d in range(2):
        chunk = functools.partial(_chunk_order, nc, ncc, d)
        outs.append(pl.pallas_call(
            functools.partial(_mlstm_scan_kernel, rev=bool(d)),
            out_shape=jax.ShapeDtypeStruct((b, ta, ML_INNER), BF16),
            grid=(b, nc),
            in_specs=[pl.BlockSpec((1, lc, ML_QK), lambda i, s, chunk=chunk: (i, chunk(s), 0)),
                      pl.BlockSpec((1, lc, ML_QK), lambda i, s, chunk=chunk: (i, chunk(s), 1)),
                      pl.BlockSpec((1, lc, ML_INNER), lambda i, s, chunk=chunk: (i, chunk(s), 1)),
                      pl.BlockSpec((1, 1, nr, lc), lambda i, s, chunk=chunk: (i, chunk(s), 0, 0)),
                      pl.BlockSpec((1, 1, lc, LANES), lambda i, s, chunk=chunk: (i, chunk(s), 0, 0))],
            out_specs=pl.BlockSpec((1, lc, ML_INNER), lambda i, s, chunk=chunk: (i, chunk(s), 0)),
            scratch_shapes=[pltpu.VMEM((ML_HEADS, ML_DK, ML_DV + LANES), F32)],
            compiler_params=_cparams("parallel", "arbitrary"),
            name="mlstm_scan_bwd" if d else "mlstm_scan_fwd",
        )(p, p, p, a_rows, cols))
    return outs


def _hgrn_lb_kernel(p_ref, o_ref, *, layer):
    for d in range(p_ref.shape[0]):
        x = p_ref[d]
        e = jnp.exp(x - jnp.max(x, axis=0, keepdims=True))
        p = e / jnp.sum(e, axis=0, keepdims=True)
        acc = jnp.zeros((1, x.shape[1]), F32)
        for j in range(1, layer + 1):
            acc = acc + p[j:j + 1, :]
        o_ref[d:d + 1, :] = acc


def _cumsum_rows(x, tri2):
    hi = x.astype(BF16)
    lo = (x - hi.astype(F32)).astype(BF16)
    return jnp.dot(tri2, jnp.concatenate([hi, lo], axis=0), preferred_element_type=F32)


def _tri2(n, rev):
    t = np.triu(np.ones((n, n), np.float32)) if rev else np.tril(np.ones((n, n), np.float32))
    return jnp.asarray(np.concatenate([t, t], axis=1), BF16)


def _anchor_rows(a, m, rev):
    n, f = a.shape
    idx = m if rev else m - 1
    if 2 * m >= SUBLANES:
        a3 = a.reshape(n // (2 * m), 2 * m, f)
        return jnp.broadcast_to(a3[:, idx:idx + 1, :], a3.shape).reshape(n, f)
    a3 = a.reshape(n // SUBLANES, SUBLANES, f)
    sub = lax.broadcasted_iota(jnp.int32, a3.shape, 1)
    out = None
    for gi in range(SUBLANES // (2 * m) - 1, -1, -1):
        cand = jnp.broadcast_to(a3[:, gi * 2 * m + idx:gi * 2 * m + idx + 1, :], a3.shape)
        out = cand if out is None else jnp.where(sub < (gi + 1) * 2 * m, cand, out)
    return out.reshape(n, f)


HG_MAX_LOG_SPAN = 64.0


def _hgrn_scan_kernel(q_ref, f_ref, i_ref, tri_ref, o_ref, s_ref, *, rev):
    lc = HG_CHUNK
    subs = range(q_ref.shape[1] // lc)
    subs = tuple(reversed(subs)) if rev else tuple(subs)

    @pl.when(pl.program_id(1) == 0)
    def _():
        s_ref[...] = jnp.zeros(s_ref.shape, F32)

    row = lax.broadcasted_iota(jnp.int32, (lc, 1), 0)
    ti = lax.broadcasted_iota(jnp.int32, (lc, lc), 0)
    si = lax.broadcasted_iota(jnp.int32, (lc, lc), 1)
    nt = (((1,), (1,)), ((), ()))
    tn = (((0,), (0,)), ((), ()))
    end = 0 if rev else lc - 1

    def all_heads_single_anchor():
        causal = (si >= ti) if rev else (si <= ti)
        cols = [slice(h * HG_DIM, (h + 1) * HG_DIM) for h in range(HG_HEADS)]
        pre = {}
        for ci in subs:
            rows = slice(ci * lc, (ci + 1) * lc)
            f = f_ref[0, rows, :]
            a = _cumsum_rows(jnp.log(f), tri_ref[...])
            q_dec = (q_ref[0, rows, :].astype(F32) * jnp.exp(a)).astype(BF16)
            k_inv = (1.0 - f) * jnp.exp(-a)
            k_inv_b = k_inv.astype(BF16)
            e_end = jnp.exp(a[end:end + 1, :])
            kd = (k_inv * e_end).astype(BF16)
            attn = [jnp.where(causal, lax.dot_general(q_dec[:, c], k_inv_b[:, c], nt, preferred_element_type=F32),
                              0.0).astype(BF16) for c in cols]
            pre[ci] = (rows, q_dec, e_end, kd, attn)
        state = [s_ref[h] for h in range(HG_HEADS)]
        for ci in subs:
            rows, q_dec, e_end, kd, attn = pre[ci]
            for h, c in enumerate(cols):
                o = jnp.dot(attn[h], i_ref[0, rows, c], preferred_element_type=F32)
                o = o + lax.dot_general(q_dec[:, c], state[h].astype(BF16), nt, preferred_element_type=F32)
                o_ref[0, rows, c] = o.astype(o_ref.dtype)
            state = [state[h] * e_end[:, c] + lax.dot_general(i_ref[0, rows, c], kd[:, c], tn,
                                                              preferred_element_type=F32)
                     for h, c in enumerate(cols)]
        for h in range(HG_HEADS):
            s_ref[h] = state[h]

    def head_per_level(r0, h, carry):
        c0 = pl.multiple_of(h * HG_DIM, HG_DIM)
        q = q_ref[0, r0:r0 + lc, pl.ds(c0, HG_DIM)].astype(F32)
        f = f_ref[0, r0:r0 + lc, pl.ds(c0, HG_DIM)]
        k = 1.0 - f
        iv = i_ref[0, r0:r0 + lc, pl.ds(c0, HG_DIM)].astype(F32)
        a = _cumsum_rows(jnp.log(f), tri_ref[...])
        attn = jnp.zeros((lc, lc), F32)
        m = 1
        while m < lc:
            e = jnp.exp(-jnp.abs(a - _anchor_rows(a, m, rev)))
            upper = (row % (2 * m)) >= m
            is_q = jnp.logical_not(upper) if rev else upper
            qt = jnp.where(is_q, q * e, 0.0).astype(BF16)
            kt = jnp.where(is_q, 0.0, k * e).astype(BF16)
            pair = lax.dot_general(qt, kt, nt, preferred_element_type=F32)
            attn = attn + jnp.where((ti // (2 * m)) == (si // (2 * m)), pair, 0.0)
            m *= 2
        a_end = a[end:end + 1, :]
        kd = (k * jnp.exp(a_end - a)).astype(BF16)
        ib = iv.astype(BF16)
        s_old = s_ref[h]
        o = jnp.dot(attn.astype(BF16), ib, preferred_element_type=F32)
        o = o + lax.dot_general((q * jnp.exp(a)).astype(BF16), s_old.astype(BF16), nt, preferred_element_type=F32)
        o_ref[0, r0:r0 + lc, pl.ds(c0, HG_DIM)] = (o + jnp.sum(q * k, axis=-1, keepdims=True) * iv).astype(o_ref.dtype)
        s_ref[h] = s_old * jnp.exp(a_end) + lax.dot_general(ib, kd, tn, preferred_element_type=F32)
        return carry

    f_min = f_ref[0, :, 0:HG_DIM]
    for h in range(1, HG_HEADS):
        f_min = jnp.minimum(f_min, f_ref[0, :, h * HG_DIM:(h + 1) * HG_DIM])
    f_min = jnp.min(f_min.reshape(f_min.shape[0] // SUBLANES, SUBLANES, HG_DIM), axis=0)
    in_range = jnp.min(f_min) >= float(np.exp(-HG_MAX_LOG_SPAN / lc))

    @pl.when(in_range)
    def _():
        all_heads_single_anchor()

    @pl.when(jnp.logical_not(in_range))
    def _():
        for ci in subs:
            lax.fori_loop(0, HG_HEADS, functools.partial(head_per_level, ci * lc), 0)


def _hgrn_scan(p, pf, tc, rev):
    b, ta, _ = p.shape
    lc = HG_CHUNK
    rows = HG_STEP_CHUNKS * lc
    hg = HG_HEADS * HG_DIM
    assert tc % rows == 0 and ta % rows == 0
    nc, ncc = ta // rows, tc // rows
    d = 1 if rev else 0
    chunk = lambda s: _chunk_order(nc, ncc, d, s)
    return pl.pallas_call(
        functools.partial(_hgrn_scan_kernel, rev=rev),
        out_shape=jax.ShapeDtypeStruct((b, ta, hg), BF16),
        grid=(b, nc),
        in_specs=[pl.BlockSpec((1, rows, hg), lambda i, s: (i, chunk(s), 0)),
                  pl.BlockSpec((1, rows, hg), lambda i, s: (i, chunk(s), d)),
                  pl.BlockSpec((1, rows, hg), lambda i, s: (i, chunk(s), 1)),
                  pl.BlockSpec((lc, 2 * lc), lambda i, s: (0, 0))],
        out_specs=pl.BlockSpec((1, rows, hg), lambda i, s: (i, chunk(s), 0)),
        scratch_shapes=[pltpu.VMEM((HG_HEADS, HG_DIM, HG_DIM), F32)],
        compiler_params=_cparams("parallel", "arbitrary"),
        name="hgrn_scan_bwd" if rev else "hgrn_scan_fwd",
    )(p, pf, p, _tri2(lc, rev))


def kernel(x, c, ctx, c_ctx, ada_w, ada_b, norm_g, ml_w_in, ml_gate_b, ml_head_g, ml_w_out, at_w_in, at_q_g, at_k_g, at_sink, at_w_out, sc_w_in, sc_conv_w, sc_conv_b, sc_w_out, hg_w_in, hg_f_b, hg_lb, hg_head_g, hg_w_out):
    tc = ctx.shape[1]
    mod = _ada_mod(c, c_ctx, ada_w, ada_b)
    xs = (ctx, x)
    for layer in range(DEPTH):
        kind, j = layer % 4, layer // 4
        need_ctx = layer < DEPTH - 1
        last = dict(lat_only=True) if layer == DEPTH - 1 else {}
        mod_l = mod[layer]
        if kind == 0:
            n_main = 2 * ML_QK + 3 * ML_INNER
            w_gate = jnp.zeros((LANES, ml_w_in.shape[1]), F32).at[:4 * ML_HEADS].set(
                ml_w_in[j][:, n_main:][:, ML_GATE_PERM].T)
            col_scale = jnp.concatenate([jnp.full((ML_QK,), ML_DK ** -0.5, F32), jnp.ones((ML_QK + ML_INNER,), F32),
                                         jnp.full((2 * ML_INNER,), 0.5, F32)])
            p, gates = _inproj(xs, mod_l, norm_g[layer], ml_w_in[j][:, :n_main] * col_scale, tc, tn=1024,
                               w_small=w_gate, out_dtype=BF16)
            h_f, h_b = _mlstm_scan(p, gates, ml_gate_b[j], tc)
            feats = [(h_f, ML_INNER, 0), (h_b, ML_INNER, 0), (p, ML_INNER, 2), (p, ML_INNER, 3)]
            xs = _outproj("mlstm", feats, ml_w_out[j], xs, mod_l, tc, head_g=0.5 * ml_head_g[j], **last)
        elif kind == 1:
            p = _inproj(xs, mod_l, norm_g[layer], _attn_weight(at_w_in[j]), tc, out_dtype=BF16)
            if isinstance(xs, tuple):
                xs = jnp.concatenate(xs, axis=1)
            xs = _attn_mixer(p, at_q_g[j], at_k_g[j], at_sink[j], at_w_out[j], xs, mod_l, tc, need_ctx)
            if last:
                xs = xs[:, tc:, :]
        elif kind == 2:
            if isinstance(xs, tuple):
                xs = jnp.concatenate(xs, axis=1)
            u = _conv_mixer(xs, mod_l, norm_g[layer], sc_w_in[j], sc_conv_w[j], sc_conv_b[j], tc)
            xs = _outproj("plain", [(u, u.shape[2], 0)], sc_w_out[j], xs, mod_l, tc, **last)
        else:
            hg = HG_HEADS * HG_DIM
            lb = pl.pallas_call(
                functools.partial(_hgrn_lb_kernel, layer=layer),
                out_shape=jax.ShapeDtypeStruct((2, hg), F32),
                name="hgrn_lb",
            )(hg_lb[j])
            w = hg_w_in[j]
            w_qiz = jnp.concatenate([0.5 * w[:, :hg], w[:, 3 * hg:4 * hg], 0.5 * w[:, 4 * hg:]], axis=1)
            p = _inproj(xs, mod_l, norm_g[layer], w_qiz, tc, tn=1024, out_dtype=BF16,
                        epilogue=((0, hg, "silu"), (hg, 3 * hg, "id")))
            lb2 = lb.reshape(2 * hg)
            pf = _inproj(xs, mod_l, norm_g[layer], 0.5 * w[:, hg:3 * hg], tc, tn=1024, epilogue=((0, 2 * hg, "fgate"),),
                         pars=(0.5 * hg_f_b[j].reshape(2 * hg), 0.5 * (1.0 + lb2), 0.5 * (1.0 - lb2)))
            o_f = _hgrn_scan(p, pf, tc, rev=False)
            o_b = _hgrn_scan(p, pf, tc, rev=True)
            feats = [(o_f, hg, 0), (o_b, hg, 0), (p, hg, 2)]
            xs = _outproj("hgrn", feats, hg_w_out[j], xs, mod_l, tc, head_g=hg_head_g[j], **last)
    return xs
```

```python
import functools

import numpy as np
import jax
import jax.numpy as jnp
from jax import lax
from jax.experimental import pallas as pl
from jax.experimental.pallas import tpu as pltpu

F32 = jnp.float32
BF16 = jnp.bfloat16
EPS = 1e-6
DEPTH = 4
GRID_W = 64
ROPE_BASE = 10000.0

ML_HEADS, ML_DK, ML_DV = 4, 256, 512
ML_QK = ML_HEADS * ML_DK
ML_INNER = ML_HEADS * ML_DV
ML_CHUNK = 256

AT_HEADS, AT_KV_HEADS, AT_HEAD_DIM = 16, 4, 64
AT_GROUP = AT_HEADS // AT_KV_HEADS
AT_BLOCK = 128
AT_PROJ_BLOCKS = 4
AT_Q = AT_HEADS * AT_HEAD_DIM
AT_KV = AT_KV_HEADS * AT_HEAD_DIM

SC_KSIZE = 3

HG_HEADS, HG_DIM = 8, 128
HG_CHUNK = 128
HG_STEP_CHUNKS = 2

LANES = 128
SUBLANES = 8
VMEM_LIMIT_BYTES = 56 * 1024 * 1024

ROW_BLOCK = 256
PROJ_ROWS = 768
OUT_ROWS = 768
OUT_ROWS_SPLIT = 256
OUT_ROWS_LAT = 1024


def _cparams(*sem):
    return pltpu.CompilerParams(dimension_semantics=sem, vmem_limit_bytes=VMEM_LIMIT_BYTES)


def _sigmoid(x):
    return 0.5 * jnp.tanh(0.5 * x) + 0.5


def _silu(x):
    return x * _sigmoid(x)


def _silu_of_half(xh):
    return xh * (1.0 + jnp.tanh(xh))


def _ada_kernel(c_ref, w_ref, b_ref, o_ref):
    s = _silu(c_ref[...])
    o_ref[0] = jnp.dot(s.astype(BF16), w_ref[0].astype(BF16), preferred_element_type=F32) + b_ref[0]


def _ada_mod(c, c_ctx, ada_w, ada_b):
    b, d = c.shape
    depth = ada_w.shape[0]
    rows = -(-(b + 1) // SUBLANES) * SUBLANES
    cc = jnp.zeros((rows, d), F32).at[:b].set(c).at[b].set(c_ctx)
    tn = 1024
    out = pl.pallas_call(
        _ada_kernel,
        out_shape=jax.ShapeDtypeStruct((depth, rows, 3 * d), F32),
        grid=(depth, 3 * d // tn),
        in_specs=[pl.BlockSpec((rows, d), lambda l, j: (0, 0)),
                  pl.BlockSpec((1, d, tn), lambda l, j: (l, 0, j)),
                  pl.BlockSpec((1, 1, tn), lambda l, j: (l, 0, j))],
        out_specs=pl.BlockSpec((1, rows, tn), lambda l, j: (l, 0, j)),
        compiler_params=_cparams("parallel", "parallel"),
        name="ada_mod",
    )(cc, ada_w, ada_b.reshape(depth, 1, 3 * d))
    return out.reshape(depth, rows, 3, d)


def _modulated(x_refs, ml_ref, mc_ref, g_ref, h_ref, tc, row0=0, nrows=None):
    nrows = h_ref.shape[0] - row0 if nrows is None else nrows
    g = g_ref[...]
    xc_ref = x_refs[0]
    xl_ref, lat0 = (x_refs[1], 0) if len(x_refs) == 2 else (x_refs[0], tc)
    for r0 in range(row0, row0 + nrows, ROW_BLOCK):
        ctx_rows = r0 < tc
        x_ref, src0, m_ref = (xc_ref, r0, mc_ref) if ctx_rows else (xl_ref, lat0 + r0 - tc, ml_ref)
        x = x_ref[0, src0:src0 + ROW_BLOCK, :]
        ms = jnp.mean(x * x, axis=-1, keepdims=True)
        xn = x * lax.rsqrt(ms + EPS) * g
        h = xn * (1.0 + m_ref[0, 1:2, :]) + m_ref[0, 0:1, :]
        h_ref[r0:r0 + ROW_BLOCK, :] = h.astype(BF16)


def _split_stream(xs):
    arrs = list(xs) if isinstance(xs, tuple) else [xs]
    specs = [pl.BlockSpec((1,) + a.shape[1:], lambda i, j: (i, 0, 0)) for a in arrs]
    ta = sum(a.shape[1] for a in arrs)
    return arrs, specs, ta


def _inproj_kernel(*refs, tc, n_x, has_small, epilogue):
    x_refs, (ml_ref, mc_ref, g_ref, w_ref), rest = refs[:n_x], refs[n_x:n_x + 4], list(refs[n_x + 4:])
    ws_ref = rest.pop(0) if has_small else None
    par_refs = [rest.pop(0) for _ in range(3)] if epilogue else None
    o_ref = rest.pop(0)
    os_ref = rest.pop(0) if has_small else None
    h_ref = rest.pop(0)

    ta = h_ref.shape[0]
    chunk = PROJ_ROWS if ta % PROJ_ROWS == 0 else ta
    j = pl.program_id(1)

    def activation(acc):
        if not epilogue:
            return acc
        bias_ref, c0_ref, c1_ref = par_refs
        t = jnp.tanh(acc + bias_ref[...])
        out = acc
        for lo, hi, kind in epilogue:
            if kind == "id":
                continue
            val = acc + acc * t if kind == "silu" else c0_ref[...] + c1_ref[...] * t
            out = jnp.where((j >= lo) & (j < hi), val, out)
        return out

    def project(first):
        step = chunk if (first or epilogue) else ta
        for r0 in range(0, ta, step):
            if first:
                _modulated(x_refs, ml_ref, mc_ref, g_ref, h_ref, tc, r0, step)
            acc = jnp.dot(h_ref[r0:r0 + step, :], w_ref[...], preferred_element_type=F32)
            o_ref[0, r0:r0 + step, :] = activation(acc).astype(o_ref.dtype)

    @pl.when(j == 0)
    def _():
        project(True)
        if has_small:
            os_ref[0] = lax.dot_general(ws_ref[...], h_ref[...], (((1,), (1,)), ((), ())),
                                        preferred_element_type=F32)

    @pl.when(j != 0)
    def _():
        project(False)


def _inproj(xs, mod_l, norm_g, w, tc, tn=512, w_small=None, out_dtype=F32, epilogue=None, pars=None):
    x_arrs, x_specs, ta = _split_stream(xs)
    b, d = x_arrs[0].shape[0], x_arrs[0].shape[2]
    n = w.shape[1]
    assert n % tn == 0 and tc % ROW_BLOCK == 0 and (ta - tc) % ROW_BLOCK == 0
    mod_lat = mod_l[:b]
    mod_ctx = mod_l[b:b + 1]
    has_small = w_small is not None
    in_specs = x_specs + [pl.BlockSpec((1, 3, d), lambda i, j: (i, 0, 0)),
                          pl.BlockSpec((1, 3, d), lambda i, j: (0, 0, 0)),
                          pl.BlockSpec((1, d), lambda i, j: (0, 0)),
                          pl.BlockSpec((d, tn), lambda i, j: (0, j))]
    args = x_arrs + [mod_lat, mod_ctx, norm_g.reshape(1, d), w.astype(BF16)]
    out_shape = [jax.ShapeDtypeStruct((b, ta, n), out_dtype)]
    out_specs = [pl.BlockSpec((1, ta, tn), lambda i, j: (i, 0, j))]
    if has_small:
        ns = w_small.shape[0]
        in_specs.append(pl.BlockSpec((ns, d), lambda i, j: (0, 0)))
        args.append(w_small.astype(BF16))
    if epilogue:
        assert all(lo % tn == 0 and hi % tn == 0 for lo, hi, _ in epilogue)
        epilogue = tuple((lo // tn, hi // tn, kind) for lo, hi, kind in epilogue)
        zeros = jnp.zeros((n,), F32)
        for par in (pars if pars is not None else (zeros,) * 3):
            in_specs.append(pl.BlockSpec((1, tn), lambda i, j: (0, j)))
            args.append(par.reshape(1, n))
    if has_small:
        out_shape.append(jax.ShapeDtypeStruct((b, ns, ta), F32))
        out_specs.append(pl.BlockSpec((1, ns, ta), lambda i, j: (i, 0, 0)))
    res = pl.pallas_call(
        functools.partial(_inproj_kernel, tc=tc, n_x=len(x_arrs), has_small=has_small, epilogue=epilogue),
        out_shape=out_shape,
        grid=(b, n // tn),
        in_specs=in_specs,
        out_specs=out_specs,
        scratch_shapes=[pltpu.VMEM((ta, d), BF16)],
        compiler_params=_cparams("parallel", "arbitrary"),
        name="inproj",
    )(*args)
    return res if has_small else res[0]


def _head_rms(h, gain, n_heads):
    dh = h.shape[1] // n_heads
    parts = []
    for i in range(n_heads):
        hh = h[:, i * dh:(i + 1) * dh]
        ms = jnp.mean(hh * hh, axis=-1, keepdims=True)
        parts.append(hh * lax.rsqrt(ms + EPS))
    return jnp.concatenate(parts, axis=1) * gain


def _outproj_kernel(*refs, mode, n_feat, n_x, tc, tm, row0):
    feats, rest = refs[:n_feat], list(refs[n_feat:])
    hg_ref = rest.pop(0) if mode != "plain" else None
    w_ref = rest.pop(0)
    x_refs = [rest.pop(0) for _ in range(n_x)]
    ml_ref, mc_ref, o_ref = rest
    if mode == "plain":
        u = feats[0][0]
    elif mode == "mlstm":
        h0_ref, h1_ref, og_ref, z_ref = feats
        hn = _head_rms(h0_ref[0].astype(F32) + h1_ref[0].astype(F32), hg_ref[...], ML_HEADS)
        u = (hn * og_ref[0].astype(F32) * z_ref[0].astype(F32)).astype(BF16)
    else:
        h0_ref, h1_ref, z_ref = feats
        hn = _head_rms(h0_ref[0].astype(F32) + h1_ref[0].astype(F32), hg_ref[...], HG_HEADS)
        u = (hn * z_ref[0].astype(F32)).astype(BF16)
    y = jnp.dot(u, w_ref[...], preferred_element_type=F32)
    first = row0 + pl.program_id(1) * tm
    row = first + lax.broadcasted_iota(jnp.int32, (tm, 1), 0)
    gate = jnp.where(row < tc, mc_ref[0, 2:3, :], ml_ref[0, 2:3, :])
    x = x_refs[0][0] if n_x == 1 else jnp.where(first < tc, x_refs[0][0], x_refs[1][0])
    o_ref[0] = x + gate * y


def _outproj(mode, feats, w_out, xs, mod_l, tc, head_g=None, lat_only=False):
    x_arrs = list(xs) if isinstance(xs, tuple) else [xs]
    b, d = x_arrs[0].shape[0], x_arrs[0].shape[2]
    ta = sum(a.shape[1] for a in x_arrs)
    kdim = w_out.shape[0]
    if lat_only:
        tm = OUT_ROWS_LAT if (ta - tc) % OUT_ROWS_LAT == 0 else OUT_ROWS_SPLIT
    else:
        tm = OUT_ROWS if (len(x_arrs) == 1 and ta % OUT_ROWS == 0) else OUT_ROWS_SPLIT
        assert ta % tm == 0 and (tc % tm == 0 or len(x_arrs) == 1)
    row0 = tc if lat_only else 0
    nct = tc // tm
    row_spec = lambda width, col: pl.BlockSpec((pl.Element(1), pl.Element(tm), pl.Element(width)),
                                               lambda i, r, col=col: (i, pl.multiple_of(row0 + r * tm, ROW_BLOCK),
                                                                      col * width))
    in_specs, args = [], []
    for arr, width, col in feats:
        in_specs.append(row_spec(width, col))
        args.append(arr)
    if head_g is not None:
        in_specs.append(pl.BlockSpec((1, kdim), lambda i, r: (0, 0)))
        args.append(head_g.reshape(1, kdim))
    in_specs.append(pl.BlockSpec((kdim, d), lambda i, r: (0, 0)))
    args.append(w_out.astype(BF16))
    if len(x_arrs) == 1:
        in_specs.append(row_spec(d, 0))
    elif lat_only:
        x_arrs = x_arrs[1:]
        in_specs.append(pl.BlockSpec((1, tm, d), lambda i, r: (i, r, 0)))
    else:
        in_specs += [pl.BlockSpec((1, tm, d), lambda i, r: (i, jnp.minimum(r, nct - 1), 0)),
                     pl.BlockSpec((1, tm, d), lambda i, r: (i, jnp.maximum(r - nct, 0), 0))]
    args += x_arrs
    in_specs += [pl.BlockSpec((1, 3, d), lambda i, r: (i, 0, 0)),
                 pl.BlockSpec((1, 3, d), lambda i, r: (0, 0, 0))]
    args += [mod_l[:b], mod_l[b:b + 1]]
    rows_out = ta - row0
    return pl.pallas_call(
        functools.partial(_outproj_kernel, mode=mode, n_feat=len(feats), n_x=len(x_arrs), tc=tc, tm=tm,
                          row0=row0),
        out_shape=jax.ShapeDtypeStruct((b, rows_out, d), F32),
        grid=(b, rows_out // tm),
        in_specs=in_specs,
        out_specs=pl.BlockSpec((1, tm, d), lambda i, r: (i, r, 0)),
        compiler_params=_cparams("parallel", "parallel"),
        name="outproj_" + mode,
    )(*args)


def _conv_kernel(x_ref, ml_ref, mc_ref, g_ref, wx_ref, wb_ref, wc_ref, wz_ref, cw_ref, cb_ref, o_ref, h_ref, *, tc):
    ta = h_ref.shape[0]
    chunk = PROJ_ROWS if ta % PROJ_ROWS == 0 else ta

    def body(first_tile):
        wx, wc, wb, wz = (w_ref[...].astype(BF16) for w_ref in (wx_ref, wc_ref, wb_ref, wz_ref))
        step = chunk if first_tile else ta
        xin, cg = [], []
        for r0 in range(0, ta, step):
            if first_tile:
                _modulated((x_ref,), ml_ref, mc_ref, g_ref, h_ref, tc, r0, step)
            h = h_ref[r0:r0 + step, :]
            xin.append(jnp.dot(h, wx, preferred_element_type=F32))
            cg.append(jnp.dot(h, wc, preferred_element_type=F32))
        bg = jnp.dot(h_ref[...], wb, preferred_element_type=F32)
        z = jnp.dot(h_ref[...], wz, preferred_element_type=F32)
        u = jnp.concatenate(cg, axis=0) * jnp.concatenate(xin, axis=0)
        row = lax.broadcasted_iota(jnp.int32, (ta, 1), 0)
        first = (row == 0) | (row == tc)
        last = (row == tc - 1) | (row == ta - 1)
        u_prev = jnp.where(first, 0.0, pltpu.roll(u, 1, axis=0))
        u_next = jnp.where(last, 0.0, pltpu.roll(u, ta - 1, axis=0))
        cw = cw_ref[...]
        y = u_prev * cw[0:1, :] + u * cw[1:2, :] + u_next * cw[2:3, :] + cb_ref[...]
        o_ref[0] = (bg * y * _silu(z)).astype(o_ref.dtype)

    @pl.when(pl.program_id(1) == 0)
    def _():
        body(True)

    @pl.when(pl.program_id(1) != 0)
    def _():
        body(False)


def _conv_mixer(xs, mod_l, norm_g, w_in, conv_w, conv_b, tc, tw=256):
    b, ta, d = xs.shape
    e = conv_w.shape[1]
    nt = e // tw
    w_spec = lambda part: pl.BlockSpec((d, tw), lambda i, j, part=part: (0, part * nt + j))
    return pl.pallas_call(
        functools.partial(_conv_kernel, tc=tc),
        out_shape=jax.ShapeDtypeStruct((b, ta, e), BF16),
        grid=(b, nt),
        in_specs=[pl.BlockSpec((1, ta, d), lambda i, j: (i, 0, 0)),
                  pl.BlockSpec((1, 3, d), lambda i, j: (i, 0, 0)),
                  pl.BlockSpec((1, 3, d), lambda i, j: (0, 0, 0)),
                  pl.BlockSpec((1, d), lambda i, j: (0, 0)),
                  w_spec(0), w_spec(1), w_spec(2), w_spec(3),
                  pl.BlockSpec((SC_KSIZE, tw), lambda i, j: (0, j)),
                  pl.BlockSpec((1, tw), lambda i, j: (0, j))],
        out_specs=pl.BlockSpec((1, ta, tw), lambda i, j: (i, 0, j)),
        scratch_shapes=[pltpu.VMEM((ta, d), BF16)],
        compiler_params=_cparams("parallel", "arbitrary"),
        name="conv_mixer",
    )(xs, mod_l[:b], mod_l[b:b + 1], norm_g.reshape(1, d), w_in, w_in, w_in, w_in, conv_w, conv_b.reshape(1, e))


AT_HALF = AT_HEAD_DIM // 2
AT_TILE_PERM = np.concatenate([np.arange(0, AT_HALF), np.arange(2 * AT_HALF, 3 * AT_HALF),
                               np.arange(AT_HALF, 2 * AT_HALF), np.arange(3 * AT_HALF, 4 * AT_HALF)])
LOG2E = float(np.log2(np.e))


def _rope_tables(tc, t):
    rows = t // GRID_W
    row = np.repeat(np.arange(rows), GRID_W).astype(np.float64)
    col = np.tile(np.arange(GRID_W), rows).astype(np.float64)
    n_freq = AT_HEAD_DIM // 4
    freqs = np.power(ROPE_BASE, -np.arange(n_freq, dtype=np.float64) / n_freq)
    ang = np.concatenate([row[:, None] * freqs, col[:, None] * freqs], axis=-1)
    cos, sin = np.cos(ang), np.sin(ang)
    cos_t = np.concatenate([np.ones((tc, LANES)), np.tile(cos, (1, 4))], axis=0)
    sin_t = np.concatenate([np.zeros((tc, LANES)), np.concatenate([-sin, -sin, sin, sin], axis=-1)], axis=0)
    return jnp.asarray(cos_t, F32), jnp.asarray(sin_t, F32)


def _norm_rope_tiles(xs, gains, cos, sin, same_head):
    sums = []
    for x in xs:
        x2 = x * x
        hi = x2.astype(BF16)
        lo = (x2 - hi.astype(F32)).astype(BF16)
        sums.append((jnp.dot(hi, same_head, preferred_element_type=F32),
                     jnp.dot(lo, same_head, preferred_element_type=F32)))
    out = []
    for x, gain, (s_hi, s_lo) in zip(xs, gains, sums):
        xn = x * lax.rsqrt((s_hi + s_lo) * (1.0 / AT_HEAD_DIM) + EPS) * gain
        out.append(xn * cos + pltpu.roll(xn, 2 * AT_HALF, axis=1) * sin)
    return out


def _attn_kernel(sink_ref, q_ref, k_ref, v_ref, z_ref, qg_ref, kg_ref, cos_ref, sin_ref, sh_ref, wo_ref, x_ref,
                 ml_ref, mc_ref, o_ref, qs_ref, ka_ref, kb_ref, va_ref, vb_ref, *, tc, need_ctx):
    ta = q_ref.shape[1]
    t = ta - tc
    nb = t // AT_BLOCK
    blk = AT_BLOCK
    gw = x_ref.shape[2]
    g = pl.program_id(1)
    odd = (g % 2) == 1
    scale = (AT_HEAD_DIM ** -0.5) * LOG2E

    @pl.when(g == 0)
    def _():
        o_ref[...] = jnp.zeros(o_ref.shape, F32)

    def prep(i, carry):
        r0 = pl.multiple_of(i * ROW_BLOCK, ROW_BLOCK)
        res_cols = pl.ds(pl.multiple_of(g * gw, gw), gw)
        o_ref[0, pl.ds(r0, ROW_BLOCK), res_cols] = (o_ref[0, pl.ds(r0, ROW_BLOCK), res_cols]
                                                    + x_ref[0, pl.ds(r0, ROW_BLOCK), :])
        cos = cos_ref[pl.ds(r0, ROW_BLOCK), :]
        sin = sin_ref[pl.ds(r0, ROW_BLOCK), :]
        same_head = sh_ref[...]
        tiles = [q_ref[0, pl.ds(r0, ROW_BLOCK), c * LANES:(c + 1) * LANES].astype(F32) for c in range(2)]
        tiles.append(k_ref[0, pl.ds(r0, ROW_BLOCK), :].astype(F32))
        q0, q1, kn = _norm_rope_tiles(tiles, (qg_ref[...], qg_ref[...], kg_ref[...]), cos, sin, same_head)
        for c, qc in enumerate((q0, q1)):
            qs_ref[pl.ds(r0, ROW_BLOCK), c * LANES:(c + 1) * LANES] = (qc * scale).astype(BF16)
        vv = v_ref[0, pl.ds(r0, ROW_BLOCK), :].astype(F32)
        lane = lax.broadcasted_iota(jnp.int32, kn.shape, 1)
        k_own = jnp.where(((lane // AT_HALF) % 2) == (g % 2), kn, 0.0)
        k_oth = pltpu.roll(k_own, jnp.where(odd, 3 * AT_HALF, AT_HALF), axis=1)
        ka_ref[pl.ds(r0, ROW_BLOCK), :] = jnp.where(odd, k_oth, k_own).astype(BF16)
        kb_ref[pl.ds(r0, ROW_BLOCK), :] = jnp.where(odd, k_own, k_oth).astype(BF16)
        v_own = jnp.where((lane // AT_HEAD_DIM) == (g % 2), vv, 0.0)
        v_oth = pltpu.roll(v_own, AT_HEAD_DIM, axis=1)
        va = jnp.where(odd, v_oth, v_own)
        vb = jnp.where(odd, v_own, v_oth)
        va_ref[pl.ds(r0, ROW_BLOCK), :] = jnp.where(lane == AT_HEAD_DIM, 1.0, va).astype(BF16)
        vb_ref[pl.ds(r0, ROW_BLOCK), :] = jnp.where(lane == 0, 1.0, vb).astype(BF16)
        return carry

    lax.fori_loop(0, ta // ROW_BLOCK, prep, 0)
    zeros = jnp.zeros((blk, LANES), BF16)
    for ref in (ka_ref, kb_ref, va_ref, vb_ref):
        ref[ta:ta + blk, :] = zeros

    half = lax.broadcasted_iota(jnp.int32, (2 * blk, 1), 0) < blk
    sink_a = jnp.where(half, sink_ref[g, 0], sink_ref[g, 2]) * LOG2E
    sink_b = jnp.where(half, sink_ref[g, 1], sink_ref[g, 3]) * LOG2E

    qi = lax.broadcasted_iota(jnp.int32, (2 * blk, 3 * blk), 0) % blk
    kj = lax.broadcasted_iota(jnp.int32, (2 * blk, 3 * blk), 1)
    band = (kj - qi >= 0) & (kj - qi <= 2 * blk)
    out_lo = lax.broadcasted_iota(jnp.int32, (2 * blk, LANES), 1) < AT_HEAD_DIM

    nt = (((1,), (1,)), ((), ()))

    sides = ((ka_ref, va_ref, sink_a, AT_HEAD_DIM), (kb_ref, vb_ref, sink_b, 0))

    def attend(blocks, r_first, gate):
        qts = [jnp.concatenate([qs_ref[pl.ds(r0, blk), 0:LANES], qs_ref[pl.ds(r0, blk), LANES:2 * LANES]], axis=0)
               for r0, _, _ in blocks]
        scores = []
        for qt, (_, k0, mask) in zip(qts, blocks):
            for k_ref_, _, _, _ in sides:
                s_ctx = lax.dot_general(qt, k_ref_[0:tc, :], nt, preferred_element_type=F32)
                s_loc = None if mask is None else lax.dot_general(qt, k_ref_[pl.ds(k0, 3 * blk), :], nt,
                                                                  preferred_element_type=F32)
                scores.append((s_ctx, s_loc))
        probs = []
        for idx, (s_ctx, s_loc) in enumerate(scores):
            mask = blocks[idx // 2][2]
            sink = sides[idx % 2][2]
            m = jnp.maximum(sink, jnp.max(s_ctx, axis=-1, keepdims=True))
            if s_loc is not None:
                s_loc = jnp.where(mask, s_loc, -jnp.inf)
                m = jnp.maximum(m, jnp.max(s_loc, axis=-1, keepdims=True))
            probs.append((jnp.exp2(s_ctx - m).astype(BF16),
                          None if s_loc is None else jnp.exp2(s_loc - m).astype(BF16), jnp.exp2(sink - m)))
        accs = []
        for idx, (p_ctx, p_loc, _) in enumerate(probs):
            k0 = blocks[idx // 2][1]
            v_ref_ = sides[idx % 2][1]
            acc = jnp.dot(p_ctx, v_ref_[0:tc, :], preferred_element_type=F32)
            if p_loc is not None:
                acc = acc + jnp.dot(p_loc, v_ref_[pl.ds(k0, 3 * blk), :], preferred_element_type=F32)
            accs.append(acc)
        us = []
        for bi, (r0, _, _) in enumerate(blocks):
            halves = []
            for si in range(2):
                acc, ones_lane = accs[2 * bi + si], sides[si][3]
                halves.append(acc / (probs[2 * bi + si][2] + acc[:, ones_lane:ones_lane + 1]))
            o = jnp.where(out_lo, halves[0], halves[1])
            parts = []
            for c in range(2):
                zc = z_ref[0, pl.ds(r0, blk), c * LANES:(c + 1) * LANES].astype(F32)
                parts.append((o[c * blk:(c + 1) * blk] * zc).astype(BF16))
            us.append(jnp.concatenate(parts, axis=1))
        u = jnp.concatenate(us, axis=0)
        rows = pl.ds(r_first, u.shape[0])
        o_ref[0, rows, :] = o_ref[0, rows, :] + gate * jnp.dot(u, wo_ref[...], preferred_element_type=F32)

    def lat_blocks(i, carry):
        blocks = []
        for j in range(AT_PROJ_BLOCKS):
            n = i * AT_PROJ_BLOCKS + j
            kpos = (n - 1) * blk + kj
            blocks.append((pl.multiple_of(tc + n * blk, blk), pl.multiple_of(tc + (n - 1) * blk, blk),
                           band & (kpos >= 0) & (kpos < t)))
        attend(blocks, pl.multiple_of(tc + i * AT_PROJ_BLOCKS * blk, AT_PROJ_BLOCKS * blk), ml_ref[0, 2:3, :])
        return carry

    lax.fori_loop(0, nb // AT_PROJ_BLOCKS, lat_blocks, 0)
    if need_ctx:
        attend([(n * blk, None, None) for n in range(tc // blk)], 0, mc_ref[0, 2:3, :])


def _attn_mixer(p, at_q_g, at_k_g, at_sink, w_out, xs, mod_l, tc, need_ctx):
    b, ta, _ = p.shape
    t = ta - tc
    d = xs.shape[2]
    assert tc >= AT_BLOCK and tc % AT_BLOCK == 0 and t % (AT_PROJ_BLOCKS * AT_BLOCK) == 0 and ta % ROW_BLOCK == 0
    assert d == AT_Q
    cos_t, sin_t = _rope_tables(tc, t)
    lane_head = (np.arange(LANES) // AT_HALF) % 2
    same_head = jnp.asarray(lane_head[:, None] == lane_head[None, :], BF16)
    tile_gain = lambda gain: jnp.tile(gain, 2)[AT_TILE_PERM].reshape(1, LANES)
    gw = AT_GROUP * AT_HEAD_DIM
    kcol = AT_Q // LANES
    vcol = (AT_Q + AT_KV) // LANES
    zcol = (AT_Q + 2 * AT_KV) // gw
    return pl.pallas_call(
        functools.partial(_attn_kernel, tc=tc, need_ctx=need_ctx),
        out_shape=jax.ShapeDtypeStruct((b, ta, d), F32),
        grid=(b, AT_KV_HEADS),
        in_specs=[pl.BlockSpec(memory_space=pltpu.SMEM),
                  pl.BlockSpec((1, ta, gw), lambda i, g: (i, 0, g)),
                  pl.BlockSpec((1, ta, LANES), lambda i, g: (i, 0, kcol + g // 2)),
                  pl.BlockSpec((1, ta, LANES), lambda i, g: (i, 0, vcol + g // 2)),
                  pl.BlockSpec((1, ta, gw), lambda i, g: (i, 0, zcol + g)),
                  pl.BlockSpec((1, LANES), lambda i, g: (0, 0)),
                  pl.BlockSpec((1, LANES), lambda i, g: (0, 0)),
                  pl.BlockSpec((ta, LANES), lambda i, g: (0, 0)),
                  pl.BlockSpec((ta, LANES), lambda i, g: (0, 0)),
                  pl.BlockSpec((LANES, LANES), lambda i, g: (0, 0)),
                  pl.BlockSpec((gw, d), lambda i, g: (g, 0)),
                  pl.BlockSpec((1, ta, gw), lambda i, g: (i, 0, g)),
                  pl.BlockSpec((1, 3, d), lambda i, g: (i, 0, 0)),
                  pl.BlockSpec((1, 3, d), lambda i, g: (0, 0, 0))],
        out_specs=pl.BlockSpec((1, ta, d), lambda i, g: (i, 0, 0)),
        scratch_shapes=[pltpu.VMEM((ta, gw), BF16)] + [pltpu.VMEM((ta + AT_BLOCK, LANES), BF16)] * 4,
        compiler_params=_cparams("parallel", "arbitrary"),
        name="attn_mixer",
    )(at_sink.reshape(AT_KV_HEADS, AT_GROUP), p, p, p, p, tile_gain(at_q_g), tile_gain(at_k_g), cos_t, sin_t,
      same_head, w_out.astype(BF16), xs, mod_l[:b], mod_l[b:b + 1])


def _attn_weight_kernel(w_ref, p_ref, o_ref):
    o_ref[...] = jnp.dot(w_ref[...].astype(BF16), p_ref[0], preferred_element_type=F32).astype(BF16)


def _attn_weight(w_in):
    d, n = w_in.shape
    n_qk, n_v = (AT_Q + AT_KV) // LANES, AT_KV // LANES
    perm = np.zeros((LANES, LANES), np.float32)
    perm[AT_TILE_PERM, np.arange(LANES)] = 1.0
    eye = np.eye(LANES, dtype=np.float32)
    mats = jnp.asarray(np.stack([perm, eye, 0.5 * eye]), BF16)
    return pl.pallas_call(
        _attn_weight_kernel,
        out_shape=jax.ShapeDtypeStruct((d, n), BF16),
        grid=(n // LANES,),
        in_specs=[pl.BlockSpec((d, LANES), lambda j: (0, j)),
                  pl.BlockSpec((1, LANES, LANES), lambda j: ((j >= n_qk).astype(jnp.int32)
                                                             + (j >= n_qk + n_v).astype(jnp.int32), 0, 0))],
        out_specs=pl.BlockSpec((d, LANES), lambda j: (0, j)),
        compiler_params=_cparams("parallel"),
        name="attn_weight",
    )(w_in, mats)


def _chunk_order(nc, ncc, d, s):
    bwd = jnp.where(s < ncc, ncc - 1 - s, nc - 1 - (s - ncc))
    return jnp.where(d == 0, s, bwd)


ML_GATE_PERM = np.concatenate([np.arange(0, 4), np.arange(8, 12), np.arange(4, 8), np.arange(12, 16)])
ML_NQ = 6


def _mlstm_gate_kernel(g_ref, b_ref, a_ref, c_ref, *, tc, lc):
    h = ML_HEADS
    ta = g_ref.shape[2]
    nc, ncc = ta // lc, tc // lc
    x = g_ref[0] + b_ref[...]
    li = x[0:2 * h]
    lfp = x[2 * h:4 * h]
    lf = jnp.minimum(lfp, 0.0) - jnp.log1p(jnp.exp(-jnp.abs(lfp)))
    fwd = lax.broadcasted_iota(jnp.int32, (2 * h, ta), 0) < h
    fwd1 = lax.broadcasted_iota(jnp.int32, (2 * h, 1), 0) < h
    pos = lax.broadcasted_iota(jnp.int32, (2 * h, ta), 1) % lc

    def seg_scan(y, op, fill):
        yf, yb = y, y
        s = 1
        while s < lc:
            yf = op(yf, jnp.where(pos >= s, pltpu.roll(yf, s, axis=1), fill))
            yb = op(yb, jnp.where(pos < lc - s, pltpu.roll(yb, ta - s, axis=1), fill))
            s *= 2
        return jnp.where(fwd, yf, yb)

    bsum = seg_scan(lf, jnp.add, 0.0)
    a = li - bsum
    cmax = seg_scan(a, jnp.maximum, -jnp.inf)

    def end_col(y, c):
        return jnp.where(fwd1, y[:, (c + 1) * lc - 1:(c + 1) * lc], y[:, c * lc:c * lc + 1])

    tot = [end_col(bsum, c) for c in range(nc)]
    amax = [end_col(cmax, c) for c in range(nc)]

    def chain(order):
        m = jnp.zeros((2 * h, 1), F32)
        m_in = [None] * nc
        for c in order:
            m_in[c] = m
            m = tot[c] + jnp.maximum(m, amax[c])
        return m_in

    m_f = chain(list(range(nc)))
    m_b = chain(list(range(ncc - 1, -1, -1)) + list(range(nc - 1, ncc - 1, -1)))
    for c in range(nc):
        m_in = jnp.where(fwd1, m_f[c], m_b[c])
        sl = slice(c * lc, (c + 1) * lc)
        g_run = jnp.maximum(m_in, cmax[:, sl])
        g_end = jnp.maximum(m_in, amax[c])
        nr = 2 * h
        quantities = [a[:, sl], g_run, jnp.exp(m_in - g_run), jnp.exp(-(bsum[:, sl] + g_run)),
                      jnp.exp(a[:, sl] - g_end), jnp.broadcast_to(jnp.exp(m_in - g_end), (nr, lc))]
        a_ref[0, c] = quantities[0]
        pad = jnp.zeros((LANES - ML_NQ * nr, lc), F32)
        c_ref[0, c] = jnp.concatenate(quantities + [pad], axis=0).T


def _mlstm_scan_kernel(q_ref, k_ref, v_ref, a_ref, c_ref, o_ref, cs_ref, *, rev):
    lc = q_ref.shape[1]
    nr = 2 * ML_HEADS

    @pl.when(pl.program_id(1) == 0)
    def _():
        cs_ref[...] = jnp.zeros(cs_ref.shape, F32)

    ti = lax.broadcasted_iota(jnp.int32, (lc, lc), 0)
    si = lax.broadcasted_iota(jnp.int32, (lc, lc), 1)
    mask = (si >= ti) if rev else (si <= ti)
    ones = jnp.ones((lc, LANES), BF16)
    nt = (((1,), (1,)), ((), ()))
    tn = (((0,), (0,)), ((), ()))
    heads = range(ML_HEADS)
    qb = [q_ref[0, :, h * ML_DK:(h + 1) * ML_DK] for h in heads]
    kb = [k_ref[0, :, h * ML_DK:(h + 1) * ML_DK] for h in heads]
    v_aug = [jnp.concatenate([v_ref[0, :, h * ML_DV:(h + 1) * ML_DV], ones], axis=1) for h in heads]
    cols = []
    for h in heads:
        r = (ML_HEADS if rev else 0) + h
        cols.append([a_ref[0, 0, r:r + 1, :]] + [c_ref[0, 0, :, qi * nr + r:qi * nr + r + 1] for qi in range(1, 6)])
    qk = [lax.dot_general(qb[h], kb[h], nt, preferred_element_type=F32) for h in heads]
    c_old = [cs_ref[h] for h in heads]
    q_c = [jnp.dot(qb[h], c_old[h].astype(BF16), preferred_element_type=F32) for h in heads]
    kw = [(kb[h].astype(F32) * cols[h][4]).astype(BF16) for h in heads]
    upd = [lax.dot_general(kw[h], v_aug[h], tn, preferred_element_type=F32) for h in heads]
    s = []
    for h in heads:
        a_row, g_run = cols[h][0], cols[h][1]
        s.append((qk[h] * jnp.where(mask, jnp.exp(a_row - g_run), 0.0)).astype(BF16))
    s_v = [jnp.dot(s[h], v_aug[h], preferred_element_type=F32) for h in heads]
    for h in heads:
        inter, eclamp, decay = cols[h][2], cols[h][3], cols[h][5]
        tot = inter * q_c[h] + s_v[h]
        inv = 1.0 / jnp.maximum(jnp.abs(tot[:, ML_DV:]), eclamp)
        o_ref[0, :, h * ML_DV:(h + 1) * ML_DV] = (
            tot[:, :ML_DV] * jnp.concatenate([inv] * (ML_DV // LANES), axis=1)).astype(o_ref.dtype)
        cs_ref[h] = decay[0:1, :] * c_old[h] + upd[h]


def _mlstm_scan(p, gates, gate_b, tc):
    b, ta, _ = p.shape
    lc = ML_CHUNK
    assert tc % lc == 0 and ta % lc == 0
    nc, ncc = ta // lc, tc // lc
    ng = 4 * ML_HEADS
    nr = 2 * ML_HEADS
    bias = gate_b.reshape(ng)[ML_GATE_PERM].reshape(ng, 1)
    a_rows, cols = pl.pallas_call(
        functools.partial(_mlstm_gate_kernel, tc=tc, lc=lc),
        out_shape=[jax.ShapeDtypeStruct((b, nc, nr, lc), F32), jax.ShapeDtypeStruct((b, nc, lc, LANES), F32)],
        grid=(b,),
        in_specs=[pl.BlockSpec((1, ng, ta), lambda i: (i, 0, 0)),
                  pl.BlockSpec((ng, 1), lambda i: (0, 0))],
        out_specs=[pl.BlockSpec((1, nc, nr, lc), lambda i: (i, 0, 0, 0)),
                   pl.BlockSpec((1, nc, lc, LANES), lambda i: (i, 0, 0, 0))],
        compiler_params=_cparams("parallel"),
        name="mlstm_gates",
    )(gates, bias)
    outs = []
    for d in range(2):
        chunk = functools.partial(_chunk_order, nc, ncc, d)
        outs.append(pl.pallas_call(
            functools.partial(_mlstm_scan_kernel, rev=bool(d)),
            out_shape=jax.ShapeDtypeStruct((b, ta, ML_INNER), BF16),
            grid=(b, nc),
            in_specs=[pl.BlockSpec((1, lc, ML_QK), lambda i, s, chunk=chunk: (i, chunk(s), 0)),
                      pl.BlockSpec((1, lc, ML_QK), lambda i, s, chunk=chunk: (i, chunk(s), 1)),
                      pl.BlockSpec((1, lc, ML_INNER), lambda i, s, chunk=chunk: (i, chunk(s), 1)),
                      pl.BlockSpec((1, 1, nr, lc), lambda i, s, chunk=chunk: (i, chunk(s), 0, 0)),
                      pl.BlockSpec((1, 1, lc, LANES), lambda i, s, chunk=chunk: (i, chunk(s), 0, 0))],
            out_specs=pl.BlockSpec((1, lc, ML_INNER), lambda i, s, chunk=chunk: (i, chunk(s), 0)),
            scratch_shapes=[pltpu.VMEM((ML_HEADS, ML_DK, ML_DV + LANES), F32)],
            compiler_params=_cparams("parallel", "arbitrary"),
            name="mlstm_scan_bwd" if d else "mlstm_scan_fwd",
        )(p, p, p, a_rows, cols))
    return outs


def _hgrn_lb_kernel(p_ref, o_ref, *, layer):
    for d in range(p_ref.shape[0]):
        x = p_ref[d]
        e = jnp.exp(x - jnp.max(x, axis=0, keepdims=True))
        p = e / jnp.sum(e, axis=0, keepdims=True)
        acc = jnp.zeros((1, x.shape[1]), F32)
        for j in range(1, layer + 1):
            acc = acc + p[j:j + 1, :]
        o_ref[d:d + 1, :] = acc


def _cumsum_rows(x, tri2):
    hi = x.astype(BF16)
    lo = (x - hi.astype(F32)).astype(BF16)
    return jnp.dot(tri2, jnp.concatenate([hi, lo], axis=0), preferred_element_type=F32)


def _tri2(n, rev):
    t = np.triu(np.ones((n, n), np.float32)) if rev else np.tril(np.ones((n, n), np.float32))
    return jnp.asarray(np.concatenate([t, t], axis=1), BF16)


def _anchor_rows(a, m, rev):
    n, f = a.shape
    idx = m if rev else m - 1
    if 2 * m >= SUBLANES:
        a3 = a.reshape(n // (2 * m), 2 * m, f)
        return jnp.broadcast_to(a3[:, idx:idx + 1, :], a3.shape).reshape(n, f)
    a3 = a.reshape(n // SUBLANES, SUBLANES, f)
    sub = lax.broadcasted_iota(jnp.int32, a3.shape, 1)
    out = None
    for gi in range(SUBLANES // (2 * m) - 1, -1, -1):
        cand = jnp.broadcast_to(a3[:, gi * 2 * m + idx:gi * 2 * m + idx + 1, :], a3.shape)
        out = cand if out is None else jnp.where(sub < (gi + 1) * 2 * m, cand, out)
    return out.reshape(n, f)


HG_MAX_LOG_SPAN = 64.0


def _hgrn_scan_kernel(q_ref, f_ref, i_ref, tri_ref, o_ref, s_ref, *, rev):
    lc = HG_CHUNK
    subs = range(q_ref.shape[1] // lc)
    subs = tuple(reversed(subs)) if rev else tuple(subs)

    @pl.when(pl.program_id(1) == 0)
    def _():
        s_ref[...] = jnp.zeros(s_ref.shape, F32)

    row = lax.broadcasted_iota(jnp.int32, (lc, 1), 0)
    ti = lax.broadcasted_iota(jnp.int32, (lc, lc), 0)
    si = lax.broadcasted_iota(jnp.int32, (lc, lc), 1)
    nt = (((1,), (1,)), ((), ()))
    tn = (((0,), (0,)), ((), ()))
    end = 0 if rev else lc - 1

    def all_heads_single_anchor():
        causal = (si >= ti) if rev else (si <= ti)
        cols = [slice(h * HG_DIM, (h + 1) * HG_DIM) for h in range(HG_HEADS)]
        pre = {}
        for ci in subs:
            rows = slice(ci * lc, (ci + 1) * lc)
            f = f_ref[0, rows, :]
            a = _cumsum_rows(jnp.log(f), tri_ref[...])
            q_dec = (q_ref[0, rows, :].astype(F32) * jnp.exp(a)).astype(BF16)
            k_inv = (1.0 - f) * jnp.exp(-a)
            k_inv_b = k_inv.astype(BF16)
            e_end = jnp.exp(a[end:end + 1, :])
            kd = (k_inv * e_end).astype(BF16)
            attn = [jnp.where(causal, lax.dot_general(q_dec[:, c], k_inv_b[:, c], nt, preferred_element_type=F32),
                              0.0).astype(BF16) for c in cols]
            pre[ci] = (rows, q_dec, e_end, kd, attn)
        state = [s_ref[h] for h in range(HG_HEADS)]
        for ci in subs:
            rows, q_dec, e_end, kd, attn = pre[ci]
            for h, c in enumerate(cols):
                o = jnp.dot(attn[h], i_ref[0, rows, c], preferred_element_type=F32)
                o = o + lax.dot_general(q_dec[:, c], state[h].astype(BF16), nt, preferred_element_type=F32)
                o_ref[0, rows, c] = o.astype(o_ref.dtype)
            state = [state[h] * e_end[:, c] + lax.dot_general(i_ref[0, rows, c], kd[:, c], tn,
                                                              preferred_element_type=F32)
                     for h, c in enumerate(cols)]
        for h in range(HG_HEADS):
            s_ref[h] = state[h]

    def head_per_level(r0, h, carry):
        c0 = pl.multiple_of(h * HG_DIM, HG_DIM)
        q = q_ref[0, r0:r0 + lc, pl.ds(c0, HG_DIM)].astype(F32)
        f = f_ref[0, r0:r0 + lc, pl.ds(c0, HG_DIM)]
        k = 1.0 - f
        iv = i_ref[0, r0:r0 + lc, pl.ds(c0, HG_DIM)].astype(F32)
        a = _cumsum_rows(jnp.log(f), tri_ref[...])
        attn = jnp.zeros((lc, lc), F32)
        m = 1
        while m < lc:
            e = jnp.exp(-jnp.abs(a - _anchor_rows(a, m, rev)))
            upper = (row % (2 * m)) >= m
            is_q = jnp.logical_not(upper) if rev else upper
            qt = jnp.where(is_q, q * e, 0.0).astype(BF16)
            kt = jnp.where(is_q, 0.0, k * e).astype(BF16)
            pair = lax.dot_general(qt, kt, nt, preferred_element_type=F32)
            attn = attn + jnp.where((ti // (2 * m)) == (si // (2 * m)), pair, 0.0)
            m *= 2
        a_end = a[end:end + 1, :]
        kd = (k * jnp.exp(a_end - a)).astype(BF16)
        ib = iv.astype(BF16)
        s_old = s_ref[h]
        o = jnp.dot(attn.astype(BF16), ib, preferred_element_type=F32)
        o = o + lax.dot_general((q * jnp.exp(a)).astype(BF16), s_old.astype(BF16), nt, preferred_element_type=F32)
        o_ref[0, r0:r0 + lc, pl.ds(c0, HG_DIM)] = (o + jnp.sum(q * k, axis=-1, keepdims=True) * iv).astype(o_ref.dtype)
        s_ref[h] = s_old * jnp.exp(a_end) + lax.dot_general(ib, kd, tn, preferred_element_type=F32)
        return carry

    f_min = f_ref[0, :, 0:HG_DIM]
    for h in range(1, HG_HEADS):
        f_min = jnp.minimum(f_min, f_ref[0, :, h * HG_DIM:(h + 1) * HG_DIM])
    f_min = jnp.min(f_min.reshape(f_min.shape[0] // SUBLANES, SUBLANES, HG_DIM), axis=0)
    in_range = jnp.min(f_min) >= float(np.exp(-HG_MAX_LOG_SPAN / lc))

    @pl.when(in_range)
    def _():
        all_heads_single_anchor()

    @pl.when(jnp.logical_not(in_range))
    def _():
        for ci in subs:
            lax.fori_loop(0, HG_HEADS, functools.partial(head_per_level, ci * lc), 0)


def _hgrn_scan(p, pf, tc, rev):
    b, ta, _ = p.shape
    lc = HG_CHUNK
    rows = HG_STEP_CHUNKS * lc
    hg = HG_HEADS * HG_DIM
    assert tc % rows == 0 and ta % rows == 0
    nc, ncc = ta // rows, tc // rows
    d = 1 if rev else 0
    chunk = lambda s: _chunk_order(nc, ncc, d, s)
    return pl.pallas_call(
        functools.partial(_hgrn_scan_kernel, rev=rev),
        out_shape=jax.ShapeDtypeStruct((b, ta, hg), BF16),
        grid=(b, nc),
        in_specs=[pl.BlockSpec((1, rows, hg), lambda i, s: (i, chunk(s), 0)),
                  pl.BlockSpec((1, rows, hg), lambda i, s: (i, chunk(s), d)),
                  pl.BlockSpec((1, rows, hg), lambda i, s: (i, chunk(s), 1)),
                  pl.BlockSpec((lc, 2 * lc), lambda i, s: (0, 0))],
        out_specs=pl.BlockSpec((1, rows, hg), lambda i, s: (i, chunk(s), 0)),
        scratch_shapes=[pltpu.VMEM((HG_HEADS, HG_DIM, HG_DIM), F32)],
        compiler_params=_cparams("parallel", "arbitrary"),
        name="hgrn_scan_bwd" if rev else "hgrn_scan_fwd",
    )(p, pf, p, _tri2(lc, rev))


def kernel(x, c, ctx, c_ctx, ada_w, ada_b, norm_g, ml_w_in, ml_gate_b, ml_head_g, ml_w_out, at_w_in, at_q_g, at_k_g, at_sink, at_w_out, sc_w_in, sc_conv_w, sc_conv_b, sc_w_out, hg_w_in, hg_f_b, hg_lb, hg_head_g, hg_w_out):
    tc = ctx.shape[1]
    mod = _ada_mod(c, c_ctx, ada_w, ada_b)
    xs = (ctx, x)
    for layer in range(DEPTH):
        kind, j = layer % 4, layer // 4
        need_ctx = layer < DEPTH - 1
        last = dict(lat_only=True) if layer == DEPTH - 1 else {}
        mod_l = mod[layer]
        if kind == 0:
            n_main = 2 * ML_QK + 3 * ML_INNER
            w_gate = jnp.zeros((LANES, ml_w_in.shape[1]), F32).at[:4 * ML_HEADS].set(
                ml_w_in[j][:, n_main:][:, ML_GATE_PERM].T)
            col_scale = jnp.concatenate([jnp.full((ML_QK,), ML_DK ** -0.5, F32), jnp.ones((ML_QK + ML_INNER,), F32),
                                         jnp.full((2 * ML_INNER,), 0.5, F32)])
            o0 = 2 * ML_QK + ML_INNER
            half_o = jnp.zeros((n_main,), F32).at[o0:o0 + ML_INNER].set(0.5)
            p, gates = _inproj(xs, mod_l, norm_g[layer], ml_w_in[j][:, :n_main] * col_scale, tc, tn=1024,
                               w_small=w_gate, out_dtype=BF16,
                               epilogue=((0, o0, "id"), (o0, o0 + ML_INNER, "fgate"), (o0 + ML_INNER, n_main, "silu")),
                               pars=(jnp.zeros((n_main,), F32), half_o, half_o))
            h_f, h_b = _mlstm_scan(p, gates, ml_gate_b[j], tc)
            feats = [(h_f, ML_INNER, 0), (h_b, ML_INNER, 0), (p, ML_INNER, 2), (p, ML_INNER, 3)]
            xs = _outproj("mlstm", feats, ml_w_out[j], xs, mod_l, tc, head_g=ml_head_g[j], **last)
        elif kind == 1:
            z0 = AT_Q + 2 * AT_KV
            p = _inproj(xs, mod_l, norm_g[layer], _attn_weight(at_w_in[j]), tc, out_dtype=BF16,
                        epilogue=((0, z0, "id"), (z0, z0 + AT_Q, "silu")))
            if isinstance(xs, tuple):
                xs = jnp.concatenate(xs, axis=1)
            xs = _attn_mixer(p, at_q_g[j], at_k_g[j], at_sink[j], at_w_out[j], xs, mod_l, tc, need_ctx)
            if last:
                xs = xs[:, tc:, :]
        elif kind == 2:
            if isinstance(xs, tuple):
                xs = jnp.concatenate(xs, axis=1)
            u = _conv_mixer(xs, mod_l, norm_g[layer], sc_w_in[j], sc_conv_w[j], sc_conv_b[j], tc)
            xs = _outproj("plain", [(u, u.shape[2], 0)], sc_w_out[j], xs, mod_l, tc, **last)
        else:
            hg = HG_HEADS * HG_DIM
            lb = pl.pallas_call(
                functools.partial(_hgrn_lb_kernel, layer=layer),
                out_shape=jax.ShapeDtypeStruct((2, hg), F32),
                name="hgrn_lb",
            )(hg_lb[j])
            w = hg_w_in[j]
            w_qiz = jnp.concatenate([0.5 * w[:, :hg], w[:, 3 * hg:4 * hg], 0.5 * w[:, 4 * hg:]], axis=1)
            p = _inproj(xs, mod_l, norm_g[layer], w_qiz, tc, tn=1024, out_dtype=BF16,
                        epilogue=((0, hg, "silu"), (hg, 2 * hg, "id"), (2 * hg, 3 * hg, "silu")))
            lb2 = lb.reshape(2 * hg)
            pf = _inproj(xs, mod_l, norm_g[layer], 0.5 * w[:, hg:3 * hg], tc, tn=1024, epilogue=((0, 2 * hg, "fgate"),),
                         pars=(0.5 * hg_f_b[j].reshape(2 * hg), 0.5 * (1.0 + lb2), 0.5 * (1.0 - lb2)))
            o_f = _hgrn_scan(p, pf, tc, rev=False)
            o_b = _hgrn_scan(p, pf, tc, rev=True)
            feats = [(o_f, hg, 0), (o_b, hg, 0), (p, hg, 2)]
            xs = _outproj("hgrn", feats, hg_w_out[j], xs, mod_l, tc, head_g=hg_head_g[j], **last)
    return xs
```

```python
import functools

import numpy as np
import jax
import jax.numpy as jnp
from jax import lax
from jax.experimental import pallas as pl
from jax.experimental.pallas import tpu as pltpu

F32 = jnp.float32
BF16 = jnp.bfloat16
EPS = 1e-6
DEPTH = 4
GRID_W = 64
ROPE_BASE = 10000.0

ML_HEADS, ML_DK, ML_DV = 4, 256, 512
ML_QK = ML_HEADS * ML_DK
ML_INNER = ML_HEADS * ML_DV
ML_CHUNK = 256

AT_HEADS, AT_KV_HEADS, AT_HEAD_DIM = 16, 4, 64
AT_GROUP = AT_HEADS // AT_KV_HEADS
AT_BLOCK = 128
AT_PROJ_BLOCKS = 4
AT_Q = AT_HEADS * AT_HEAD_DIM
AT_KV = AT_KV_HEADS * AT_HEAD_DIM

SC_KSIZE = 3

HG_HEADS, HG_DIM = 8, 128
HG_CHUNK = 128
HG_STEP_CHUNKS = 2

LANES = 128
SUBLANES = 8
VMEM_LIMIT_BYTES = 56 * 1024 * 1024

ROW_BLOCK = 256
PROJ_ROWS = 768
OUT_ROWS = 768
OUT_ROWS_SPLIT = 256
OUT_ROWS_LAT = 1024


def _cparams(*sem):
    return pltpu.CompilerParams(dimension_semantics=sem, vmem_limit_bytes=VMEM_LIMIT_BYTES)


def _sigmoid(x):
    return 0.5 * jnp.tanh(0.5 * x) + 0.5


def _silu(x):
    return x * _sigmoid(x)


def _silu_of_half(xh):
    return xh * (1.0 + jnp.tanh(xh))


def _ada_kernel(c_ref, w_ref, b_ref, o_ref):
    s = _silu(c_ref[...])
    o_ref[0] = jnp.dot(s.astype(BF16), w_ref[0].astype(BF16), preferred_element_type=F32) + b_ref[0]


def _ada_mod(c, c_ctx, ada_w, ada_b):
    b, d = c.shape
    depth = ada_w.shape[0]
    rows = -(-(b + 1) // SUBLANES) * SUBLANES
    cc = jnp.zeros((rows, d), F32).at[:b].set(c).at[b].set(c_ctx)
    tn = 1024
    out = pl.pallas_call(
        _ada_kernel,
        out_shape=jax.ShapeDtypeStruct((depth, rows, 3 * d), F32),
        grid=(depth, 3 * d // tn),
        in_specs=[pl.BlockSpec((rows, d), lambda l, j: (0, 0)),
                  pl.BlockSpec((1, d, tn), lambda l, j: (l, 0, j)),
                  pl.BlockSpec((1, 1, tn), lambda l, j: (l, 0, j))],
        out_specs=pl.BlockSpec((1, rows, tn), lambda l, j: (l, 0, j)),
        compiler_params=_cparams("parallel", "parallel"),
        name="ada_mod",
    )(cc, ada_w, ada_b.reshape(depth, 1, 3 * d))
    return out.reshape(depth, rows, 3, d)


def _modulated(x_refs, ml_ref, mc_ref, g_ref, h_ref, tc, row0=0, nrows=None):
    nrows = h_ref.shape[0] - row0 if nrows is None else nrows
    g = g_ref[...]
    xc_ref = x_refs[0]
    xl_ref, lat0 = (x_refs[1], 0) if len(x_refs) == 2 else (x_refs[0], tc)
    for r0 in range(row0, row0 + nrows, ROW_BLOCK):
        ctx_rows = r0 < tc
        x_ref, src0, m_ref = (xc_ref, r0, mc_ref) if ctx_rows else (xl_ref, lat0 + r0 - tc, ml_ref)
        x = x_ref[0, src0:src0 + ROW_BLOCK, :]
        ms = jnp.mean(x * x, axis=-1, keepdims=True)
        xn = x * lax.rsqrt(ms + EPS) * g
        h = xn * (1.0 + m_ref[0, 1:2, :]) + m_ref[0, 0:1, :]
        h_ref[r0:r0 + ROW_BLOCK, :] = h.astype(BF16)


def _split_stream(xs):
    arrs = list(xs) if isinstance(xs, tuple) else [xs]
    specs = [pl.BlockSpec((1,) + a.shape[1:], lambda i, j: (i, 0, 0)) for a in arrs]
    ta = sum(a.shape[1] for a in arrs)
    return arrs, specs, ta


def _inproj_kernel(*refs, tc, n_x, has_small, epilogue, w_rows_out):
    x_refs, (ml_ref, mc_ref, g_ref, w_ref), rest = refs[:n_x], refs[n_x:n_x + 4], list(refs[n_x + 4:])
    ws_ref = rest.pop(0) if has_small else None
    par_refs = [rest.pop(0) for _ in range(3)] if epilogue else None
    o_ref = rest.pop(0)
    os_ref = rest.pop(0) if has_small else None
    h_ref = rest.pop(0)

    ta = h_ref.shape[0]
    chunk = PROJ_ROWS if ta % PROJ_ROWS == 0 else ta
    j = pl.program_id(1)
    w_dims = (((1,), (1 if w_rows_out else 0,)), ((), ()))

    def activation(acc):
        if not epilogue:
            return acc
        bias_ref, c0_ref, c1_ref = par_refs
        t = jnp.tanh(acc + bias_ref[...])
        out = acc
        for lo, hi, kind in epilogue:
            if kind == "id":
                continue
            val = acc + acc * t if kind == "silu" else c0_ref[...] + c1_ref[...] * t
            out = jnp.where((j >= lo) & (j < hi), val, out)
        return out

    def project(first):
        step = chunk if (first or epilogue) else ta
        for r0 in range(0, ta, step):
            if first:
                _modulated(x_refs, ml_ref, mc_ref, g_ref, h_ref, tc, r0, step)
            acc = lax.dot_general(h_ref[r0:r0 + step, :], w_ref[...], w_dims, preferred_element_type=F32)
            o_ref[0, r0:r0 + step, :] = activation(acc).astype(o_ref.dtype)

    @pl.when(j == 0)
    def _():
        project(True)
        if has_small:
            os_ref[0] = lax.dot_general(ws_ref[...], h_ref[...], (((1,), (1,)), ((), ())),
                                        preferred_element_type=F32)

    @pl.when(j != 0)
    def _():
        project(False)


def _inproj(xs, mod_l, norm_g, w, tc, tn=512, w_small=None, out_dtype=F32, epilogue=None, pars=None,
            w_rows_out=False):
    x_arrs, x_specs, ta = _split_stream(xs)
    b, d = x_arrs[0].shape[0], x_arrs[0].shape[2]
    n = w.shape[0] if w_rows_out else w.shape[1]
    assert n % tn == 0 and tc % ROW_BLOCK == 0 and (ta - tc) % ROW_BLOCK == 0
    mod_lat = mod_l[:b]
    mod_ctx = mod_l[b:b + 1]
    has_small = w_small is not None
    in_specs = x_specs + [pl.BlockSpec((1, 3, d), lambda i, j: (i, 0, 0)),
                          pl.BlockSpec((1, 3, d), lambda i, j: (0, 0, 0)),
                          pl.BlockSpec((1, d), lambda i, j: (0, 0)),
                          pl.BlockSpec((tn, d), lambda i, j: (j, 0)) if w_rows_out
                          else pl.BlockSpec((d, tn), lambda i, j: (0, j))]
    args = x_arrs + [mod_lat, mod_ctx, norm_g.reshape(1, d), w.astype(BF16)]
    out_shape = [jax.ShapeDtypeStruct((b, ta, n), out_dtype)]
    out_specs = [pl.BlockSpec((1, ta, tn), lambda i, j: (i, 0, j))]
    if has_small:
        ns = w_small.shape[0]
        in_specs.append(pl.BlockSpec((ns, d), lambda i, j: (0, 0)))
        args.append(w_small.astype(BF16))
    if epilogue:
        assert all(lo % tn == 0 and hi % tn == 0 for lo, hi, _ in epilogue)
        epilogue = tuple((lo // tn, hi // tn, kind) for lo, hi, kind in epilogue)
        zeros = jnp.zeros((n,), F32)
        for par in (pars if pars is not None else (zeros,) * 3):
            in_specs.append(pl.BlockSpec((1, tn), lambda i, j: (0, j)))
            args.append(par.reshape(1, n))
    if has_small:
        out_shape.append(jax.ShapeDtypeStruct((b, ns, ta), F32))
        out_specs.append(pl.BlockSpec((1, ns, ta), lambda i, j: (i, 0, 0)))
    res = pl.pallas_call(
        functools.partial(_inproj_kernel, tc=tc, n_x=len(x_arrs), has_small=has_small, epilogue=epilogue,
                          w_rows_out=w_rows_out),
        out_shape=out_shape,
        grid=(b, n // tn),
        in_specs=in_specs,
        out_specs=out_specs,
        scratch_shapes=[pltpu.VMEM((ta, d), BF16)],
        compiler_params=_cparams("parallel", "arbitrary"),
        name="inproj",
    )(*args)
    return res if has_small else res[0]


def _head_rms(h, gain, n_heads):
    dh = h.shape[1] // n_heads
    parts = []
    for i in range(n_heads):
        hh = h[:, i * dh:(i + 1) * dh]
        ms = jnp.mean(hh * hh, axis=-1, keepdims=True)
        parts.append(hh * lax.rsqrt(ms + EPS))
    return jnp.concatenate(parts, axis=1) * gain


def _outproj_kernel(*refs, mode, n_feat, n_x, tc, tm, row0):
    feats, rest = refs[:n_feat], list(refs[n_feat:])
    hg_ref = rest.pop(0) if mode != "plain" else None
    w_ref = rest.pop(0)
    x_refs = [rest.pop(0) for _ in range(n_x)]
    ml_ref, mc_ref, o_ref = rest
    if mode == "plain":
        u = feats[0][0]
    elif mode == "mlstm":
        h0_ref, h1_ref, og_ref, z_ref = feats
        hn = _head_rms(h0_ref[0].astype(F32) + h1_ref[0].astype(F32), hg_ref[...], ML_HEADS)
        u = (hn * (1.0 + jnp.tanh(og_ref[0].astype(F32))) * _silu_of_half(z_ref[0].astype(F32))).astype(BF16)
    else:
        h0_ref, h1_ref, z_ref = feats
        hn = _head_rms(h0_ref[0].astype(F32) + h1_ref[0].astype(F32), hg_ref[...], HG_HEADS)
        u = (hn * z_ref[0].astype(F32)).astype(BF16)
    y = jnp.dot(u, w_ref[...], preferred_element_type=F32)
    first = row0 + pl.program_id(1) * tm
    row = first + lax.broadcasted_iota(jnp.int32, (tm, 1), 0)
    gate = jnp.where(row < tc, mc_ref[0, 2:3, :], ml_ref[0, 2:3, :])
    x = x_refs[0][0] if n_x == 1 else jnp.where(first < tc, x_refs[0][0], x_refs[1][0])
    o_ref[0] = x + gate * y


def _outproj(mode, feats, w_out, xs, mod_l, tc, head_g=None, lat_only=False):
    x_arrs = list(xs) if isinstance(xs, tuple) else [xs]
    b, d = x_arrs[0].shape[0], x_arrs[0].shape[2]
    ta = sum(a.shape[1] for a in x_arrs)
    kdim = w_out.shape[0]
    if lat_only:
        tm = OUT_ROWS_LAT if (ta - tc) % OUT_ROWS_LAT == 0 else OUT_ROWS_SPLIT
    else:
        tm = OUT_ROWS if (len(x_arrs) == 1 and ta % OUT_ROWS == 0) else OUT_ROWS_SPLIT
        assert ta % tm == 0 and (tc % tm == 0 or len(x_arrs) == 1)
    row0 = tc if lat_only else 0
    nct = tc // tm
    row_spec = lambda width, col: pl.BlockSpec((pl.Element(1), pl.Element(tm), pl.Element(width)),
                                               lambda i, r, col=col: (i, pl.multiple_of(row0 + r * tm, ROW_BLOCK),
                                                                      col * width))
    in_specs, args = [], []
    for arr, width, col in feats:
        in_specs.append(row_spec(width, col))
        args.append(arr)
    if head_g is not None:
        in_specs.append(pl.BlockSpec((1, kdim), lambda i, r: (0, 0)))
        args.append(head_g.reshape(1, kdim))
    in_specs.append(pl.BlockSpec((kdim, d), lambda i, r: (0, 0)))
    args.append(w_out.astype(BF16))
    if len(x_arrs) == 1:
        in_specs.append(row_spec(d, 0))
    elif lat_only:
        x_arrs = x_arrs[1:]
        in_specs.append(pl.BlockSpec((1, tm, d), lambda i, r: (i, r, 0)))
    else:
        in_specs += [pl.BlockSpec((1, tm, d), lambda i, r: (i, jnp.minimum(r, nct - 1), 0)),
                     pl.BlockSpec((1, tm, d), lambda i, r: (i, jnp.maximum(r - nct, 0), 0))]
    args += x_arrs
    in_specs += [pl.BlockSpec((1, 3, d), lambda i, r: (i, 0, 0)),
                 pl.BlockSpec((1, 3, d), lambda i, r: (0, 0, 0))]
    args += [mod_l[:b], mod_l[b:b + 1]]
    rows_out = ta - row0
    return pl.pallas_call(
        functools.partial(_outproj_kernel, mode=mode, n_feat=len(feats), n_x=len(x_arrs), tc=tc, tm=tm,
                          row0=row0),
        out_shape=jax.ShapeDtypeStruct((b, rows_out, d), F32),
        grid=(b, rows_out // tm),
        in_specs=in_specs,
        out_specs=pl.BlockSpec((1, tm, d), lambda i, r: (i, r, 0)),
        compiler_params=_cparams("parallel", "parallel"),
        name="outproj_" + mode,
    )(*args)


def _conv_kernel(x_ref, ml_ref, mc_ref, g_ref, wx_ref, wb_ref, wc_ref, wz_ref, cw_ref, cb_ref, o_ref, h_ref, *, tc):
    ta = h_ref.shape[0]
    chunk = PROJ_ROWS if ta % PROJ_ROWS == 0 else ta

    def body(first_tile):
        wx, wc, wb, wz = (w_ref[...].astype(BF16) for w_ref in (wx_ref, wc_ref, wb_ref, wz_ref))
        step = chunk if first_tile else ta
        xin, cg = [], []
        for r0 in range(0, ta, step):
            if first_tile:
                _modulated((x_ref,), ml_ref, mc_ref, g_ref, h_ref, tc, r0, step)
            h = h_ref[r0:r0 + step, :]
            xin.append(jnp.dot(h, wx, preferred_element_type=F32))
            cg.append(jnp.dot(h, wc, preferred_element_type=F32))
        bg = jnp.dot(h_ref[...], wb, preferred_element_type=F32)
        z = jnp.dot(h_ref[...], wz, preferred_element_type=F32)
        u = jnp.concatenate(cg, axis=0) * jnp.concatenate(xin, axis=0)
        row = lax.broadcasted_iota(jnp.int32, (ta, 1), 0)
        first = (row == 0) | (row == tc)
        last = (row == tc - 1) | (row == ta - 1)
        u_prev = jnp.where(first, 0.0, pltpu.roll(u, 1, axis=0))
        u_next = jnp.where(last, 0.0, pltpu.roll(u, ta - 1, axis=0))
        cw = cw_ref[...]
        y = u_prev * cw[0:1, :] + u * cw[1:2, :] + u_next * cw[2:3, :] + cb_ref[...]
        o_ref[0] = (bg * y * _silu(z)).astype(o_ref.dtype)

    @pl.when(pl.program_id(1) == 0)
    def _():
        body(True)

    @pl.when(pl.program_id(1) != 0)
    def _():
        body(False)


def _conv_mixer(xs, mod_l, norm_g, w_in, conv_w, conv_b, tc, tw=256):
    b, ta, d = xs.shape
    e = conv_w.shape[1]
    nt = e // tw
    w_spec = lambda part: pl.BlockSpec((d, tw), lambda i, j, part=part: (0, part * nt + j))
    return pl.pallas_call(
        functools.partial(_conv_kernel, tc=tc),
        out_shape=jax.ShapeDtypeStruct((b, ta, e), BF16),
        grid=(b, nt),
        in_specs=[pl.BlockSpec((1, ta, d), lambda i, j: (i, 0, 0)),
                  pl.BlockSpec((1, 3, d), lambda i, j: (i, 0, 0)),
                  pl.BlockSpec((1, 3, d), lambda i, j: (0, 0, 0)),
                  pl.BlockSpec((1, d), lambda i, j: (0, 0)),
                  w_spec(0), w_spec(1), w_spec(2), w_spec(3),
                  pl.BlockSpec((SC_KSIZE, tw), lambda i, j: (0, j)),
                  pl.BlockSpec((1, tw), lambda i, j: (0, j))],
        out_specs=pl.BlockSpec((1, ta, tw), lambda i, j: (i, 0, j)),
        scratch_shapes=[pltpu.VMEM((ta, d), BF16)],
        compiler_params=_cparams("parallel", "arbitrary"),
        name="conv_mixer",
    )(xs, mod_l[:b], mod_l[b:b + 1], norm_g.reshape(1, d), w_in, w_in, w_in, w_in, conv_w, conv_b.reshape(1, e))


AT_HALF = AT_HEAD_DIM // 2
AT_TILE_PERM = np.concatenate([np.arange(0, AT_HALF), np.arange(2 * AT_HALF, 3 * AT_HALF),
                               np.arange(AT_HALF, 2 * AT_HALF), np.arange(3 * AT_HALF, 4 * AT_HALF)])
LOG2E = float(np.log2(np.e))


def _rope_tables(tc, t):
    rows = t // GRID_W
    row = np.repeat(np.arange(rows), GRID_W).astype(np.float64)
    col = np.tile(np.arange(GRID_W), rows).astype(np.float64)
    n_freq = AT_HEAD_DIM // 4
    freqs = np.power(ROPE_BASE, -np.arange(n_freq, dtype=np.float64) / n_freq)
    ang = np.concatenate([row[:, None] * freqs, col[:, None] * freqs], axis=-1)
    cos, sin = np.cos(ang), np.sin(ang)
    cos_t = np.concatenate([np.ones((tc, LANES)), np.tile(cos, (1, 4))], axis=0)
    sin_t = np.concatenate([np.zeros((tc, LANES)), np.concatenate([-sin, -sin, sin, sin], axis=-1)], axis=0)
    return jnp.asarray(cos_t, F32), jnp.asarray(sin_t, F32)


def _norm_rope_tiles(xs, gains, cos, sin, same_head):
    sums = []
    for x in xs:
        x2 = x * x
        hi = x2.astype(BF16)
        lo = (x2 - hi.astype(F32)).astype(BF16)
        sums.append((jnp.dot(hi, same_head, preferred_element_type=F32),
                     jnp.dot(lo, same_head, preferred_element_type=F32)))
    out = []
    for x, gain, (s_hi, s_lo) in zip(xs, gains, sums):
        xn = x * lax.rsqrt((s_hi + s_lo) * (1.0 / AT_HEAD_DIM) + EPS) * gain
        out.append(xn * cos + pltpu.roll(xn, 2 * AT_HALF, axis=1) * sin)
    return out


def _attn_kernel(sink_ref, q_ref, k_ref, v_ref, z_ref, qg_ref, kg_ref, cos_ref, sin_ref, sh_ref, wo_ref, x_ref,
                 ml_ref, mc_ref, o_ref, qs_ref, ka_ref, kb_ref, va_ref, vb_ref, *, tc, need_ctx):
    ta = q_ref.shape[1]
    t = ta - tc
    nb = t // AT_BLOCK
    blk = AT_BLOCK
    gw = x_ref.shape[2]
    g = pl.program_id(1)
    odd = (g % 2) == 1
    scale = (AT_HEAD_DIM ** -0.5) * LOG2E

    @pl.when(g == 0)
    def _():
        o_ref[...] = jnp.zeros(o_ref.shape, F32)

    def prep(i, carry):
        r0 = pl.multiple_of(i * ROW_BLOCK, ROW_BLOCK)
        res_cols = pl.ds(pl.multiple_of(g * gw, gw), gw)
        o_ref[0, pl.ds(r0, ROW_BLOCK), res_cols] = (o_ref[0, pl.ds(r0, ROW_BLOCK), res_cols]
                                                    + x_ref[0, pl.ds(r0, ROW_BLOCK), :])
        cos = cos_ref[pl.ds(r0, ROW_BLOCK), :]
        sin = sin_ref[pl.ds(r0, ROW_BLOCK), :]
        same_head = sh_ref[...]
        tiles = [q_ref[0, pl.ds(r0, ROW_BLOCK), c * LANES:(c + 1) * LANES].astype(F32) for c in range(2)]
        tiles.append(k_ref[0, pl.ds(r0, ROW_BLOCK), :].astype(F32))
        q0, q1, kn = _norm_rope_tiles(tiles, (qg_ref[...], qg_ref[...], kg_ref[...]), cos, sin, same_head)
        for c, qc in enumerate((q0, q1)):
            qs_ref[pl.ds(r0, ROW_BLOCK), c * LANES:(c + 1) * LANES] = (qc * scale).astype(BF16)
        vv = v_ref[0, pl.ds(r0, ROW_BLOCK), :].astype(F32)
        lane = lax.broadcasted_iota(jnp.int32, kn.shape, 1)
        k_own = jnp.where(((lane // AT_HALF) % 2) == (g % 2), kn, 0.0)
        k_oth = pltpu.roll(k_own, jnp.where(odd, 3 * AT_HALF, AT_HALF), axis=1)
        ka_ref[pl.ds(r0, ROW_BLOCK), :] = jnp.where(odd, k_oth, k_own).astype(BF16)
        kb_ref[pl.ds(r0, ROW_BLOCK), :] = jnp.where(odd, k_own, k_oth).astype(BF16)
        v_own = jnp.where((lane // AT_HEAD_DIM) == (g % 2), vv, 0.0)
        v_oth = pltpu.roll(v_own, AT_HEAD_DIM, axis=1)
        va = jnp.where(odd, v_oth, v_own)
        vb = jnp.where(odd, v_own, v_oth)
        va_ref[pl.ds(r0, ROW_BLOCK), :] = jnp.where(lane == AT_HEAD_DIM, 1.0, va).astype(BF16)
        vb_ref[pl.ds(r0, ROW_BLOCK), :] = jnp.where(lane == 0, 1.0, vb).astype(BF16)
        return carry

    lax.fori_loop(0, ta // ROW_BLOCK, prep, 0)
    zeros = jnp.zeros((blk, LANES), BF16)
    for ref in (ka_ref, kb_ref, va_ref, vb_ref):
        ref[ta:ta + blk, :] = zeros

    half = lax.broadcasted_iota(jnp.int32, (2 * blk, 1), 0) < blk
    sink_a = jnp.where(half, sink_ref[g, 0], sink_ref[g, 2]) * LOG2E
    sink_b = jnp.where(half, sink_ref[g, 1], sink_ref[g, 3]) * LOG2E

    qi = lax.broadcasted_iota(jnp.int32, (2 * blk, 3 * blk), 0) % blk
    kj = lax.broadcasted_iota(jnp.int32, (2 * blk, 3 * blk), 1)
    band = (kj - qi >= 0) & (kj - qi <= 2 * blk)
    out_lo = lax.broadcasted_iota(jnp.int32, (2 * blk, LANES), 1) < AT_HEAD_DIM

    nt = (((1,), (1,)), ((), ()))

    sides = ((ka_ref, va_ref, sink_a, AT_HEAD_DIM), (kb_ref, vb_ref, sink_b, 0))

    def attend(blocks, r_first, gate):
        qts = [jnp.concatenate([qs_ref[pl.ds(r0, blk), 0:LANES], qs_ref[pl.ds(r0, blk), LANES:2 * LANES]], axis=0)
               for r0, _, _ in blocks]
        scores = []
        for qt, (_, k0, mask) in zip(qts, blocks):
            for k_ref_, _, _, _ in sides:
                s_ctx = lax.dot_general(qt, k_ref_[0:tc, :], nt, preferred_element_type=F32)
                s_loc = None if mask is None else lax.dot_general(qt, k_ref_[pl.ds(k0, 3 * blk), :], nt,
                                                                  preferred_element_type=F32)
                scores.append((s_ctx, s_loc))
        probs = []
        for idx, (s_ctx, s_loc) in enumerate(scores):
            mask = blocks[idx // 2][2]
            sink = sides[idx % 2][2]
            m = jnp.maximum(sink, jnp.max(s_ctx, axis=-1, keepdims=True))
            if s_loc is not None:
                s_loc = jnp.where(mask, s_loc, -jnp.inf)
                m = jnp.maximum(m, jnp.max(s_loc, axis=-1, keepdims=True))
            probs.append((jnp.exp2(s_ctx - m).astype(BF16),
                          None if s_loc is None else jnp.exp2(s_loc - m).astype(BF16), jnp.exp2(sink - m)))
        accs = []
        for idx, (p_ctx, p_loc, _) in enumerate(probs):
            k0 = blocks[idx // 2][1]
            v_ref_ = sides[idx % 2][1]
            acc = jnp.dot(p_ctx, v_ref_[0:tc, :], preferred_element_type=F32)
            if p_loc is not None:
                acc = acc + jnp.dot(p_loc, v_ref_[pl.ds(k0, 3 * blk), :], preferred_element_type=F32)
            accs.append(acc)
        us = []
        for bi, (r0, _, _) in enumerate(blocks):
            halves = []
            for si in range(2):
                acc, ones_lane = accs[2 * bi + si], sides[si][3]
                halves.append(acc / (probs[2 * bi + si][2] + acc[:, ones_lane:ones_lane + 1]))
            o = jnp.where(out_lo, halves[0], halves[1])
            parts = []
            for c in range(2):
                zc = z_ref[0, pl.ds(r0, blk), c * LANES:(c + 1) * LANES].astype(F32)
                parts.append((o[c * blk:(c + 1) * blk] * _silu_of_half(zc)).astype(BF16))
            us.append(jnp.concatenate(parts, axis=1))
        u = jnp.concatenate(us, axis=0)
        rows = pl.ds(r_first, u.shape[0])
        o_ref[0, rows, :] = o_ref[0, rows, :] + gate * jnp.dot(u, wo_ref[...], preferred_element_type=F32)

    def lat_blocks(i, carry):
        blocks = []
        for j in range(AT_PROJ_BLOCKS):
            n = i * AT_PROJ_BLOCKS + j
            kpos = (n - 1) * blk + kj
            blocks.append((pl.multiple_of(tc + n * blk, blk), pl.multiple_of(tc + (n - 1) * blk, blk),
                           band & (kpos >= 0) & (kpos < t)))
        attend(blocks, pl.multiple_of(tc + i * AT_PROJ_BLOCKS * blk, AT_PROJ_BLOCKS * blk), ml_ref[0, 2:3, :])
        return carry

    lax.fori_loop(0, nb // AT_PROJ_BLOCKS, lat_blocks, 0)
    if need_ctx:
        attend([(n * blk, None, None) for n in range(tc // blk)], 0, mc_ref[0, 2:3, :])


def _attn_mixer(p, at_q_g, at_k_g, at_sink, w_out, xs, mod_l, tc, need_ctx):
    b, ta, _ = p.shape
    t = ta - tc
    d = xs.shape[2]
    assert tc >= AT_BLOCK and tc % AT_BLOCK == 0 and t % (AT_PROJ_BLOCKS * AT_BLOCK) == 0 and ta % ROW_BLOCK == 0
    assert d == AT_Q
    cos_t, sin_t = _rope_tables(tc, t)
    lane_head = (np.arange(LANES) // AT_HALF) % 2
    same_head = jnp.asarray(lane_head[:, None] == lane_head[None, :], BF16)
    tile_gain = lambda gain: jnp.tile(gain, 2)[AT_TILE_PERM].reshape(1, LANES)
    gw = AT_GROUP * AT_HEAD_DIM
    kcol = AT_Q // LANES
    vcol = (AT_Q + AT_KV) // LANES
    zcol = (AT_Q + 2 * AT_KV) // gw
    return pl.pallas_call(
        functools.partial(_attn_kernel, tc=tc, need_ctx=need_ctx),
        out_shape=jax.ShapeDtypeStruct((b, ta, d), F32),
        grid=(b, AT_KV_HEADS),
        in_specs=[pl.BlockSpec(memory_space=pltpu.SMEM),
                  pl.BlockSpec((1, ta, gw), lambda i, g: (i, 0, g)),
                  pl.BlockSpec((1, ta, LANES), lambda i, g: (i, 0, kcol + g // 2)),
                  pl.BlockSpec((1, ta, LANES), lambda i, g: (i, 0, vcol + g // 2)),
                  pl.BlockSpec((1, ta, gw), lambda i, g: (i, 0, zcol + g)),
                  pl.BlockSpec((1, LANES), lambda i, g: (0, 0)),
                  pl.BlockSpec((1, LANES), lambda i, g: (0, 0)),
                  pl.BlockSpec((ta, LANES), lambda i, g: (0, 0)),
                  pl.BlockSpec((ta, LANES), lambda i, g: (0, 0)),
                  pl.BlockSpec((LANES, LANES), lambda i, g: (0, 0)),
                  pl.BlockSpec((gw, d), lambda i, g: (g, 0)),
                  pl.BlockSpec((1, ta, gw), lambda i, g: (i, 0, g)),
                  pl.BlockSpec((1, 3, d), lambda i, g: (i, 0, 0)),
                  pl.BlockSpec((1, 3, d), lambda i, g: (0, 0, 0))],
        out_specs=pl.BlockSpec((1, ta, d), lambda i, g: (i, 0, 0)),
        scratch_shapes=[pltpu.VMEM((ta, gw), BF16)] + [pltpu.VMEM((ta + AT_BLOCK, LANES), BF16)] * 4,
        compiler_params=_cparams("parallel", "arbitrary"),
        name="attn_mixer",
    )(at_sink.reshape(AT_KV_HEADS, AT_GROUP), p, p, p, p, tile_gain(at_q_g), tile_gain(at_k_g), cos_t, sin_t,
      same_head, w_out.astype(BF16), xs, mod_l[:b], mod_l[b:b + 1])


def _attn_weight_kernel(w_ref, p_ref, o_ref, *, n_qk, n_v):
    tiles = w_ref.shape[1] // LANES
    for lt in range(tiles):
        tile = pl.program_id(0) * tiles + lt
        kind = (tile >= n_qk).astype(jnp.int32) + (tile >= n_qk + n_v).astype(jnp.int32)
        cols = slice(lt * LANES, (lt + 1) * LANES)
        o_ref[:, cols] = jnp.dot(w_ref[:, cols].astype(BF16), p_ref[kind], preferred_element_type=F32).astype(BF16)


def _attn_weight(w_in):
    d, n = w_in.shape
    n_qk, n_v = (AT_Q + AT_KV) // LANES, AT_KV // LANES
    perm = np.zeros((LANES, LANES), np.float32)
    perm[AT_TILE_PERM, np.arange(LANES)] = 1.0
    eye = np.eye(LANES, dtype=np.float32)
    mats = jnp.asarray(np.stack([perm, eye, 0.5 * eye]), BF16)
    tw = 4 * LANES
    assert n % tw == 0
    return pl.pallas_call(
        functools.partial(_attn_weight_kernel, n_qk=n_qk, n_v=n_v),
        out_shape=jax.ShapeDtypeStruct((d, n), BF16),
        grid=(n // tw,),
        in_specs=[pl.BlockSpec((d, tw), lambda j: (0, j)),
                  pl.BlockSpec((3, LANES, LANES), lambda j: (0, 0, 0))],
        out_specs=pl.BlockSpec((d, tw), lambda j: (0, j)),
        compiler_params=_cparams("parallel"),
        name="attn_weight",
    )(w_in, mats)


def _chunk_order(nc, ncc, d, s):
    bwd = jnp.where(s < ncc, ncc - 1 - s, nc - 1 - (s - ncc))
    return jnp.where(d == 0, s, bwd)


ML_GATE_PERM = np.concatenate([np.arange(0, 4), np.arange(8, 12), np.arange(4, 8), np.arange(12, 16)])
ML_NQ = 6


def _mlstm_gate_kernel(g_ref, b_ref, a_ref, c_ref, *, tc, lc):
    h = ML_HEADS
    ta = g_ref.shape[2]
    nc, ncc = ta // lc, tc // lc
    x = g_ref[0] + b_ref[...]
    li = x[0:2 * h]
    lfp = x[2 * h:4 * h]
    lf = jnp.minimum(lfp, 0.0) - jnp.log1p(jnp.exp(-jnp.abs(lfp)))
    fwd = lax.broadcasted_iota(jnp.int32, (2 * h, ta), 0) < h
    fwd1 = lax.broadcasted_iota(jnp.int32, (2 * h, 1), 0) < h
    pos = lax.broadcasted_iota(jnp.int32, (2 * h, ta), 1) % lc

    def seg_scan(y, op, fill):
        yf, yb = y, y
        s = 1
        while s < lc:
            yf = op(yf, jnp.where(pos >= s, pltpu.roll(yf, s, axis=1), fill))
            yb = op(yb, jnp.where(pos < lc - s, pltpu.roll(yb, ta - s, axis=1), fill))
            s *= 2
        return jnp.where(fwd, yf, yb)

    bsum = seg_scan(lf, jnp.add, 0.0)
    a = li - bsum
    cmax = seg_scan(a, jnp.maximum, -jnp.inf)

    def end_col(y, c):
        return jnp.where(fwd1, y[:, (c + 1) * lc - 1:(c + 1) * lc], y[:, c * lc:c * lc + 1])

    tot = [end_col(bsum, c) for c in range(nc)]
    amax = [end_col(cmax, c) for c in range(nc)]

    def chain(order):
        m = jnp.zeros((2 * h, 1), F32)
        m_in = [None] * nc
        for c in order:
            m_in[c] = m
            m = tot[c] + jnp.maximum(m, amax[c])
        return m_in

    m_f = chain(list(range(nc)))
    m_b = chain(list(range(ncc - 1, -1, -1)) + list(range(nc - 1, ncc - 1, -1)))
    for c in range(nc):
        m_in = jnp.where(fwd1, m_f[c], m_b[c])
        sl = slice(c * lc, (c + 1) * lc)
        g_run = jnp.maximum(m_in, cmax[:, sl])
        g_end = jnp.maximum(m_in, amax[c])
        nr = 2 * h
        quantities = [a[:, sl], g_run, jnp.exp(m_in - g_run), jnp.exp(-(bsum[:, sl] + g_run)),
                      jnp.exp(a[:, sl] - g_end), jnp.broadcast_to(jnp.exp(m_in - g_end), (nr, lc))]
        a_ref[0, c] = quantities[0]
        pad = jnp.zeros((LANES - ML_NQ * nr, lc), F32)
        c_ref[0, c] = jnp.concatenate(quantities + [pad], axis=0).T


def _mlstm_scan_kernel(q_ref, k_ref, v_ref, a_ref, c_ref, o_ref, cs_ref, *, rev):
    lc = q_ref.shape[1]
    nr = 2 * ML_HEADS

    @pl.when(pl.program_id(1) == 0)
    def _():
        cs_ref[...] = jnp.zeros(cs_ref.shape, F32)

    ti = lax.broadcasted_iota(jnp.int32, (lc, lc), 0)
    si = lax.broadcasted_iota(jnp.int32, (lc, lc), 1)
    mask = (si >= ti) if rev else (si <= ti)
    ones = jnp.ones((lc, LANES), BF16)
    nt = (((1,), (1,)), ((), ()))
    tn = (((0,), (0,)), ((), ()))
    heads = range(ML_HEADS)
    qb = [q_ref[0, :, h * ML_DK:(h + 1) * ML_DK] for h in heads]
    kb = [k_ref[0, :, h * ML_DK:(h + 1) * ML_DK] for h in heads]
    v_aug = [jnp.concatenate([v_ref[0, :, h * ML_DV:(h + 1) * ML_DV], ones], axis=1) for h in heads]
    cols = []
    for h in heads:
        r = (ML_HEADS if rev else 0) + h
        cols.append([a_ref[0, 0, r:r + 1, :]] + [c_ref[0, 0, :, qi * nr + r:qi * nr + r + 1] for qi in range(1, 6)])
    qk = [lax.dot_general(qb[h], kb[h], nt, preferred_element_type=F32) for h in heads]
    c_old = [cs_ref[h] for h in heads]
    q_c = [jnp.dot(qb[h], c_old[h].astype(BF16), preferred_element_type=F32) for h in heads]
    kw = [(kb[h].astype(F32) * cols[h][4]).astype(BF16) for h in heads]
    upd = [lax.dot_general(kw[h], v_aug[h], tn, preferred_element_type=F32) for h in heads]
    s = []
    for h in heads:
        a_row, g_run = cols[h][0], cols[h][1]
        s.append((qk[h] * jnp.where(mask, jnp.exp(a_row - g_run), 0.0)).astype(BF16))
    s_v = [jnp.dot(s[h], v_aug[h], preferred_element_type=F32) for h in heads]
    for h in heads:
        inter, eclamp, decay = cols[h][2], cols[h][3], cols[h][5]
        tot = inter * q_c[h] + s_v[h]
        inv = 1.0 / jnp.maximum(jnp.abs(tot[:, ML_DV:]), eclamp)
        o_ref[0, :, h * ML_DV:(h + 1) * ML_DV] = (
            tot[:, :ML_DV] * jnp.concatenate([inv] * (ML_DV // LANES), axis=1)).astype(o_ref.dtype)
        cs_ref[h] = decay[0:1, :] * c_old[h] + upd[h]


def _mlstm_scan(p, gates, gate_b, tc):
    b, ta, _ = p.shape
    lc = ML_CHUNK
    assert tc % lc == 0 and ta % lc == 0
    nc, ncc = ta // lc, tc // lc
    ng = 4 * ML_HEADS
    nr = 2 * ML_HEADS
    bias = gate_b.reshape(ng)[ML_GATE_PERM].reshape(ng, 1)
    a_rows, cols = pl.pallas_call(
        functools.partial(_mlstm_gate_kernel, tc=tc, lc=lc),
        out_shape=[jax.ShapeDtypeStruct((b, nc, nr, lc), F32), jax.ShapeDtypeStruct((b, nc, lc, LANES), F32)],
        grid=(b,),
        in_specs=[pl.BlockSpec((1, ng, ta), lambda i: (i, 0, 0)),
                  pl.BlockSpec((ng, 1), lambda i: (0, 0))],
        out_specs=[pl.BlockSpec((1, nc, nr, lc), lambda i: (i, 0, 0, 0)),
                   pl.BlockSpec((1, nc, lc, LANES), lambda i: (i, 0, 0, 0))],
        compiler_params=_cparams("parallel"),
        name="mlstm_gates",
    )(gates, bias)
    outs = []
    for d in range(2):
        chunk = functools.partial(_chunk_order, nc, ncc, d)
        outs.append(pl.pallas_call(
            functools.partial(_mlstm_scan_kernel, rev=bool(d)),
            out_shape=jax.ShapeDtypeStruct((b, ta, ML_INNER), BF16),
            grid=(b, nc),
            in_specs=[pl.BlockSpec((1, lc, ML_QK), lambda i, s, chunk=chunk: (i, chunk(s), 0)),
                      pl.BlockSpec((1, lc, ML_QK), lambda i, s, chunk=chunk: (i, chunk(s), 1)),
                      pl.BlockSpec((1, lc, ML_INNER), lambda i, s, chunk=chunk: (i, chunk(s), 1)),
                      pl.BlockSpec((1, 1, nr, lc), lambda i, s, chunk=chunk: (i, chunk(s), 0, 0)),
                      pl.BlockSpec((1, 1, lc, LANES), lambda i, s, chunk=chunk: (i, chunk(s), 0, 0))],
            out_specs=pl.BlockSpec((1, lc, ML_INNER), lambda i, s, chunk=chunk: (i, chunk(s), 0)),
            scratch_shapes=[pltpu.VMEM((ML_HEADS, ML_DK, ML_DV + LANES), F32)],
            compiler_params=_cparams("parallel", "arbitrary"),
            name="mlstm_scan_bwd" if d else "mlstm_scan_fwd",
        )(p, p, p, a_rows, cols))
    return outs


def _hgrn_lb_kernel(p_ref, o_ref, *, layer):
    for d in range(p_ref.shape[0]):
        x = p_ref[d]
        e = jnp.exp(x - jnp.max(x, axis=0, keepdims=True))
        p = e / jnp.sum(e, axis=0, keepdims=True)
        acc = jnp.zeros((1, x.shape[1]), F32)
        for j in range(1, layer + 1):
            acc = acc + p[j:j + 1, :]
        o_ref[d:d + 1, :] = acc


def _cumsum_rows(x, tri2):
    hi = x.astype(BF16)
    lo = (x - hi.astype(F32)).astype(BF16)
    return jnp.dot(tri2, jnp.concatenate([hi, lo], axis=0), preferred_element_type=F32)


def _tri2(n, rev):
    t = np.triu(np.ones((n, n), np.float32)) if rev else np.tril(np.ones((n, n), np.float32))
    return jnp.asarray(np.concatenate([t, t], axis=1), BF16)


def _anchor_rows(a, m, rev):
    n, f = a.shape
    idx = m if rev else m - 1
    if 2 * m >= SUBLANES:
        a3 = a.reshape(n // (2 * m), 2 * m, f)
        return jnp.broadcast_to(a3[:, idx:idx + 1, :], a3.shape).reshape(n, f)
    a3 = a.reshape(n // SUBLANES, SUBLANES, f)
    sub = lax.broadcasted_iota(jnp.int32, a3.shape, 1)
    out = None
    for gi in range(SUBLANES // (2 * m) - 1, -1, -1):
        cand = jnp.broadcast_to(a3[:, gi * 2 * m + idx:gi * 2 * m + idx + 1, :], a3.shape)
        out = cand if out is None else jnp.where(sub < (gi + 1) * 2 * m, cand, out)
    return out.reshape(n, f)


HG_MAX_LOG_SPAN = 64.0


def _hgrn_scan_kernel(q_ref, f_ref, i_ref, tri_ref, o_ref, s_ref, *, rev):
    lc = HG_CHUNK
    subs = range(q_ref.shape[1] // lc)
    subs = tuple(reversed(subs)) if rev else tuple(subs)

    @pl.when(pl.program_id(1) == 0)
    def _():
        s_ref[...] = jnp.zeros(s_ref.shape, F32)

    row = lax.broadcasted_iota(jnp.int32, (lc, 1), 0)
    ti = lax.broadcasted_iota(jnp.int32, (lc, lc), 0)
    si = lax.broadcasted_iota(jnp.int32, (lc, lc), 1)
    nt = (((1,), (1,)), ((), ()))
    tn = (((0,), (0,)), ((), ()))
    end = 0 if rev else lc - 1

    def all_heads_single_anchor():
        causal = (si >= ti) if rev else (si <= ti)
        cols = [slice(h * HG_DIM, (h + 1) * HG_DIM) for h in range(HG_HEADS)]
        pre = {}
        for ci in subs:
            rows = slice(ci * lc, (ci + 1) * lc)
            f = f_ref[0, rows, :]
            a = _cumsum_rows(jnp.log(f), tri_ref[...])
            q_dec = (q_ref[0, rows, :].astype(F32) * jnp.exp(a)).astype(BF16)
            k_inv = (1.0 - f) * jnp.exp(-a)
            k_inv_b = k_inv.astype(BF16)
            e_end = jnp.exp(a[end:end + 1, :])
            kd = (k_inv * e_end).astype(BF16)
            attn = [jnp.where(causal, lax.dot_general(q_dec[:, c], k_inv_b[:, c], nt, preferred_element_type=F32),
                              0.0).astype(BF16) for c in cols]
            pre[ci] = (rows, q_dec, e_end, kd, attn)
        state = [s_ref[h] for h in range(HG_HEADS)]
        for ci in subs:
            rows, q_dec, e_end, kd, attn = pre[ci]
            for h, c in enumerate(cols):
                o = jnp.dot(attn[h], i_ref[0, rows, c], preferred_element_type=F32)
                o = o + lax.dot_general(q_dec[:, c], state[h].astype(BF16), nt, preferred_element_type=F32)
                o_ref[0, rows, c] = o.astype(o_ref.dtype)
            state = [state[h] * e_end[:, c] + lax.dot_general(i_ref[0, rows, c], kd[:, c], tn,
                                                              preferred_element_type=F32)
                     for h, c in enumerate(cols)]
        for h in range(HG_HEADS):
            s_ref[h] = state[h]

    def head_per_level(r0, h, carry):
        c0 = pl.multiple_of(h * HG_DIM, HG_DIM)
        q = q_ref[0, r0:r0 + lc, pl.ds(c0, HG_DIM)].astype(F32)
        f = f_ref[0, r0:r0 + lc, pl.ds(c0, HG_DIM)]
        k = 1.0 - f
        iv = i_ref[0, r0:r0 + lc, pl.ds(c0, HG_DIM)].astype(F32)
        a = _cumsum_rows(jnp.log(f), tri_ref[...])
        attn = jnp.zeros((lc, lc), F32)
        m = 1
        while m < lc:
            e = jnp.exp(-jnp.abs(a - _anchor_rows(a, m, rev)))
            upper = (row % (2 * m)) >= m
            is_q = jnp.logical_not(upper) if rev else upper
            qt = jnp.where(is_q, q * e, 0.0).astype(BF16)
            kt = jnp.where(is_q, 0.0, k * e).astype(BF16)
            pair = lax.dot_general(qt, kt, nt, preferred_element_type=F32)
            attn = attn + jnp.where((ti // (2 * m)) == (si // (2 * m)), pair, 0.0)
            m *= 2
        a_end = a[end:end + 1, :]
        kd = (k * jnp.exp(a_end - a)).astype(BF16)
        ib = iv.astype(BF16)
        s_old = s_ref[h]
        o = jnp.dot(attn.astype(BF16), ib, preferred_element_type=F32)
        o = o + lax.dot_general((q * jnp.exp(a)).astype(BF16), s_old.astype(BF16), nt, preferred_element_type=F32)
        o_ref[0, r0:r0 + lc, pl.ds(c0, HG_DIM)] = (o + jnp.sum(q * k, axis=-1, keepdims=True) * iv).astype(o_ref.dtype)
        s_ref[h] = s_old * jnp.exp(a_end) + lax.dot_general(ib, kd, tn, preferred_element_type=F32)
        return carry

    f_min = f_ref[0, :, 0:HG_DIM]
    for h in range(1, HG_HEADS):
        f_min = jnp.minimum(f_min, f_ref[0, :, h * HG_DIM:(h + 1) * HG_DIM])
    f_min = jnp.min(f_min.reshape(f_min.shape[0] // SUBLANES, SUBLANES, HG_DIM), axis=0)
    in_range = jnp.min(f_min) >= float(np.exp(-HG_MAX_LOG_SPAN / lc))

    @pl.when(in_range)
    def _():
        all_heads_single_anchor()

    @pl.when(jnp.logical_not(in_range))
    def _():
        for ci in subs:
            lax.fori_loop(0, HG_HEADS, functools.partial(head_per_level, ci * lc), 0)


def _hgrn_scan(p, pf, tc, rev):
    b, ta, _ = p.shape
    lc = HG_CHUNK
    rows = HG_STEP_CHUNKS * lc
    hg = HG_HEADS * HG_DIM
    assert tc % rows == 0 and ta % rows == 0
    nc, ncc = ta // rows, tc // rows
    d = 1 if rev else 0
    chunk = lambda s: _chunk_order(nc, ncc, d, s)
    return pl.pallas_call(
        functools.partial(_hgrn_scan_kernel, rev=rev),
        out_shape=jax.ShapeDtypeStruct((b, ta, hg), BF16),
        grid=(b, nc),
        in_specs=[pl.BlockSpec((1, rows, hg), lambda i, s: (i, chunk(s), 0)),
                  pl.BlockSpec((1, rows, hg), lambda i, s: (i, chunk(s), d)),
                  pl.BlockSpec((1, rows, hg), lambda i, s: (i, chunk(s), 1)),
                  pl.BlockSpec((lc, 2 * lc), lambda i, s: (0, 0))],
        out_specs=pl.BlockSpec((1, rows, hg), lambda i, s: (i, chunk(s), 0)),
        scratch_shapes=[pltpu.VMEM((HG_HEADS, HG_DIM, HG_DIM), F32)],
        compiler_params=_cparams("parallel", "arbitrary"),
        name="hgrn_scan_bwd" if rev else "hgrn_scan_fwd",
    )(p, pf, p, _tri2(lc, rev))


def kernel(x, c, ctx, c_ctx, ada_w, ada_b, norm_g, ml_w_in, ml_gate_b, ml_head_g, ml_w_out, at_w_in, at_q_g, at_k_g, at_sink, at_w_out, sc_w_in, sc_conv_w, sc_conv_b, sc_w_out, hg_w_in, hg_f_b, hg_lb, hg_head_g, hg_w_out):
    tc = ctx.shape[1]
    mod = _ada_mod(c, c_ctx, ada_w, ada_b)
    xs = (ctx, x)
    for layer in range(DEPTH):
        kind, j = layer % 4, layer // 4
        need_ctx = layer < DEPTH - 1
        last = dict(lat_only=True) if layer == DEPTH - 1 else {}
        mod_l = mod[layer]
        if kind == 0:
            n_main = 2 * ML_QK + 3 * ML_INNER
            w_t = ml_w_in[j].T
            w_gate = jnp.zeros((LANES, ml_w_in.shape[1]), F32).at[:4 * ML_HEADS].set(w_t[n_main:][ML_GATE_PERM])
            col_scale = jnp.concatenate([jnp.full((ML_QK,), ML_DK ** -0.5, F32), jnp.ones((ML_QK + ML_INNER,), F32),
                                         jnp.full((2 * ML_INNER,), 0.5, F32)])
            p, gates = _inproj(xs, mod_l, norm_g[layer], w_t[:n_main] * col_scale[:, None], tc, tn=1024,
                               w_small=w_gate, out_dtype=BF16, w_rows_out=True)
            h_f, h_b = _mlstm_scan(p, gates, ml_gate_b[j], tc)
            feats = [(h_f, ML_INNER, 0), (h_b, ML_INNER, 0), (p, ML_INNER, 2), (p, ML_INNER, 3)]
            xs = _outproj("mlstm", feats, ml_w_out[j], xs, mod_l, tc, head_g=0.5 * ml_head_g[j], **last)
        elif kind == 1:
            p = _inproj(xs, mod_l, norm_g[layer], _attn_weight(at_w_in[j]), tc, out_dtype=BF16)
            if isinstance(xs, tuple):
                xs = jnp.concatenate(xs, axis=1)
            xs = _attn_mixer(p, at_q_g[j], at_k_g[j], at_sink[j], at_w_out[j], xs, mod_l, tc, need_ctx)
            if last:
                xs = xs[:, tc:, :]
        elif kind == 2:
            if isinstance(xs, tuple):
                xs = jnp.concatenate(xs, axis=1)
            u = _conv_mixer(xs, mod_l, norm_g[layer], sc_w_in[j], sc_conv_w[j], sc_conv_b[j], tc)
            xs = _outproj("plain", [(u, u.shape[2], 0)], sc_w_out[j], xs, mod_l, tc, **last)
        else:
            hg = HG_HEADS * HG_DIM
            lb = pl.pallas_call(
                functools.partial(_hgrn_lb_kernel, layer=layer),
                out_shape=jax.ShapeDtypeStruct((2, hg), F32),
                name="hgrn_lb",
            )(hg_lb[j])
            w = hg_w_in[j]
            w_qiz = jnp.concatenate([0.5 * w[:, :hg], w[:, 3 * hg:4 * hg], 0.5 * w[:, 4 * hg:]], axis=1)
            p = _inproj(xs, mod_l, norm_g[layer], w_qiz, tc, tn=1024, out_dtype=BF16,
                        epilogue=((0, hg, "silu"), (hg, 2 * hg, "id"), (2 * hg, 3 * hg, "silu")))
            lb2 = lb.reshape(2 * hg)
            pf = _inproj(xs, mod_l, norm_g[layer], 0.5 * w[:, hg:3 * hg], tc, tn=1024, epilogue=((0, 2 * hg, "fgate"),),
                         pars=(0.5 * hg_f_b[j].reshape(2 * hg), 0.5 * (1.0 + lb2), 0.5 * (1.0 - lb2)))
            o_f = _hgrn_scan(p, pf, tc, rev=False)
            o_b = _hgrn_scan(p, pf, tc, rev=True)
            feats = [(o_f, hg, 0), (o_b, hg, 0), (p, hg, 2)]
            xs = _outproj("hgrn", feats, hg_w_out[j], xs, mod_l, tc, head_g=hg_head_g[j], **last)
    return xs
```

```python
import functools

import numpy as np
import jax
import jax.numpy as jnp
from jax import lax
from jax.experimental import pallas as pl
from jax.experimental.pallas import tpu as pltpu

F32 = jnp.float32
BF16 = jnp.bfloat16
EPS = 1e-6
DEPTH = 4
GRID_W = 64
ROPE_BASE = 10000.0

ML_HEADS, ML_DK, ML_DV = 4, 256, 512
ML_QK = ML_HEADS * ML_DK
ML_INNER = ML_HEADS * ML_DV
ML_CHUNK = 256

AT_HEADS, AT_KV_HEADS, AT_HEAD_DIM = 16, 4, 64
AT_GROUP = AT_HEADS // AT_KV_HEADS
AT_BLOCK = 128
AT_PROJ_BLOCKS = 8
AT_Q = AT_HEADS * AT_HEAD_DIM
AT_KV = AT_KV_HEADS * AT_HEAD_DIM

SC_KSIZE = 3

HG_HEADS, HG_DIM = 8, 128
HG_CHUNK = 128
HG_STEP_CHUNKS = 2

LANES = 128
SUBLANES = 8
VMEM_LIMIT_BYTES = 56 * 1024 * 1024

ROW_BLOCK = 256
PROJ_ROWS = 768
OUT_ROWS = 768
OUT_ROWS_SPLIT = 256
OUT_ROWS_LAT = 1024


def _cparams(*sem):
    return pltpu.CompilerParams(dimension_semantics=sem, vmem_limit_bytes=VMEM_LIMIT_BYTES)


def _sigmoid(x):
    return 0.5 * jnp.tanh(0.5 * x) + 0.5


def _silu(x):
    return x * _sigmoid(x)


def _silu_of_half(xh):
    return xh * (1.0 + jnp.tanh(xh))


def _ada_kernel(c_ref, w_ref, b_ref, o_ref):
    s = _silu(c_ref[...])
    o_ref[0] = jnp.dot(s.astype(BF16), w_ref[0].astype(BF16), preferred_element_type=F32) + b_ref[0]


def _ada_mod(c, c_ctx, ada_w, ada_b):
    b, d = c.shape
    depth = ada_w.shape[0]
    rows = -(-(b + 1) // SUBLANES) * SUBLANES
    cc = jnp.zeros((rows, d), F32).at[:b].set(c).at[b].set(c_ctx)
    tn = 1024
    out = pl.pallas_call(
        _ada_kernel,
        out_shape=jax.ShapeDtypeStruct((depth, rows, 3 * d), F32),
        grid=(depth, 3 * d // tn),
        in_specs=[pl.BlockSpec((rows, d), lambda l, j: (0, 0)),
                  pl.BlockSpec((1, d, tn), lambda l, j: (l, 0, j)),
                  pl.BlockSpec((1, 1, tn), lambda l, j: (l, 0, j))],
        out_specs=pl.BlockSpec((1, rows, tn), lambda l, j: (l, 0, j)),
        compiler_params=_cparams("parallel", "parallel"),
        name="ada_mod",
    )(cc, ada_w, ada_b.reshape(depth, 1, 3 * d))
    return out.reshape(depth, rows, 3, d)


def _modulated(x_refs, ml_ref, mc_ref, g_ref, h_ref, tc, row0=0, nrows=None):
    nrows = h_ref.shape[0] - row0 if nrows is None else nrows
    g = g_ref[...]
    xc_ref = x_refs[0]
    xl_ref, lat0 = (x_refs[1], 0) if len(x_refs) == 2 else (x_refs[0], tc)
    for r0 in range(row0, row0 + nrows, ROW_BLOCK):
        ctx_rows = r0 < tc
        x_ref, src0, m_ref = (xc_ref, r0, mc_ref) if ctx_rows else (xl_ref, lat0 + r0 - tc, ml_ref)
        x = x_ref[0, src0:src0 + ROW_BLOCK, :]
        ms = jnp.mean(x * x, axis=-1, keepdims=True)
        xn = x * lax.rsqrt(ms + EPS) * g
        h = xn * (1.0 + m_ref[0, 1:2, :]) + m_ref[0, 0:1, :]
        h_ref[r0:r0 + ROW_BLOCK, :] = h.astype(BF16)


def _split_stream(xs):
    arrs = list(xs) if isinstance(xs, tuple) else [xs]
    specs = [pl.BlockSpec((1,) + a.shape[1:], lambda i, j: (i, 0, 0)) for a in arrs]
    ta = sum(a.shape[1] for a in arrs)
    return arrs, specs, ta


def _inproj_kernel(*refs, tc, n_x, has_small, epilogue, w_rows_out, block_min):
    x_refs, (ml_ref, mc_ref, g_ref, w_ref), rest = refs[:n_x], refs[n_x:n_x + 4], list(refs[n_x + 4:])
    ws_ref = rest.pop(0) if has_small else None
    par_refs = [rest.pop(0) for _ in range(3)] if epilogue else None
    o_ref = rest.pop(0)
    os_ref = rest.pop(0) if has_small else None
    mn_ref = rest.pop(0) if block_min else None
    h_ref = rest.pop(0)

    ta = h_ref.shape[0]
    chunk = PROJ_ROWS if ta % PROJ_ROWS == 0 else ta
    j = pl.program_id(1)
    w_dims = (((1,), (1 if w_rows_out else 0,)), ((), ()))

    def activation(acc):
        if not epilogue:
            return acc
        bias_ref, c0_ref, c1_ref = par_refs
        t = jnp.tanh(acc + bias_ref[...])
        out = acc
        for lo, hi, kind in epilogue:
            if kind == "id":
                continue
            val = acc + acc * t if kind == "silu" else c0_ref[...] + c1_ref[...] * t
            out = jnp.where((j >= lo) & (j < hi), val, out)
        return out

    def project(first):
        step = chunk if (first or epilogue) else ta
        mins = []
        for r0 in range(0, ta, step):
            if first:
                _modulated(x_refs, ml_ref, mc_ref, g_ref, h_ref, tc, r0, step)
            acc = lax.dot_general(h_ref[r0:r0 + step, :], w_ref[...], w_dims, preferred_element_type=F32)
            out = activation(acc)
            o_ref[0, r0:r0 + step, :] = out.astype(o_ref.dtype)
            if mn_ref is not None:
                for b0 in range(0, step, ROW_BLOCK):
                    m = jnp.min(out[b0:b0 + ROW_BLOCK], axis=0, keepdims=True)
                    mins.append(jnp.broadcast_to(jnp.min(m, axis=1, keepdims=True), (1, LANES)))
        if mn_ref is not None:
            pad = jnp.zeros((mn_ref.shape[2] - len(mins), LANES), F32)
            mn_ref[0, 0] = jnp.concatenate(mins + [pad], axis=0)

    @pl.when(j == 0)
    def _():
        project(True)
        if has_small:
            os_ref[0] = lax.dot_general(ws_ref[...], h_ref[...], (((1,), (1,)), ((), ())),
                                        preferred_element_type=F32)

    @pl.when(j != 0)
    def _():
        project(False)


def _inproj(xs, mod_l, norm_g, w, tc, tn=512, w_small=None, out_dtype=F32, epilogue=None, pars=None,
            w_rows_out=False, block_min=False):
    x_arrs, x_specs, ta = _split_stream(xs)
    b, d = x_arrs[0].shape[0], x_arrs[0].shape[2]
    n = w.shape[0] if w_rows_out else w.shape[1]
    assert n % tn == 0 and tc % ROW_BLOCK == 0 and (ta - tc) % ROW_BLOCK == 0
    mod_lat = mod_l[:b]
    mod_ctx = mod_l[b:b + 1]
    has_small = w_small is not None
    in_specs = x_specs + [pl.BlockSpec((1, 3, d), lambda i, j: (i, 0, 0)),
                          pl.BlockSpec((1, 3, d), lambda i, j: (0, 0, 0)),
                          pl.BlockSpec((1, d), lambda i, j: (0, 0)),
                          pl.BlockSpec((tn, d), lambda i, j: (j, 0)) if w_rows_out
                          else pl.BlockSpec((d, tn), lambda i, j: (0, j))]
    args = x_arrs + [mod_lat, mod_ctx, norm_g.reshape(1, d), w.astype(BF16)]
    out_shape = [jax.ShapeDtypeStruct((b, ta, n), out_dtype)]
    out_specs = [pl.BlockSpec((1, ta, tn), lambda i, j: (i, 0, j))]
    if has_small:
        ns = w_small.shape[0]
        in_specs.append(pl.BlockSpec((ns, d), lambda i, j: (0, 0)))
        args.append(w_small.astype(BF16))
    if epilogue:
        assert all(lo % tn == 0 and hi % tn == 0 for lo, hi, _ in epilogue)
        epilogue = tuple((lo // tn, hi // tn, kind) for lo, hi, kind in epilogue)
        zeros = jnp.zeros((n,), F32)
        for par in (pars if pars is not None else (zeros,) * 3):
            in_specs.append(pl.BlockSpec((1, tn), lambda i, j: (0, j)))
            args.append(par.reshape(1, n))
    if has_small:
        out_shape.append(jax.ShapeDtypeStruct((b, ns, ta), F32))
        out_specs.append(pl.BlockSpec((1, ns, ta), lambda i, j: (i, 0, 0)))
    if block_min:
        nblk = -(-(ta // ROW_BLOCK) // SUBLANES) * SUBLANES
        out_shape.append(jax.ShapeDtypeStruct((b, n // tn, nblk, LANES), F32))
        out_specs.append(pl.BlockSpec((1, 1, nblk, LANES), lambda i, j: (i, j, 0, 0)))
    res = pl.pallas_call(
        functools.partial(_inproj_kernel, tc=tc, n_x=len(x_arrs), has_small=has_small, epilogue=epilogue,
                          w_rows_out=w_rows_out, block_min=block_min),
        out_shape=out_shape,
        grid=(b, n // tn),
        in_specs=in_specs,
        out_specs=out_specs,
        scratch_shapes=[pltpu.VMEM((ta, d), BF16)],
        compiler_params=_cparams("parallel", "arbitrary"),
        name="inproj",
    )(*args)
    return res if (has_small or block_min) else res[0]


def _head_rms(h, gain, n_heads):
    dh = h.shape[1] // n_heads
    parts = []
    for i in range(n_heads):
        hh = h[:, i * dh:(i + 1) * dh]
        ms = jnp.mean(hh * hh, axis=-1, keepdims=True)
        parts.append(hh * lax.rsqrt(ms + EPS))
    return jnp.concatenate(parts, axis=1) * gain


def _outproj_kernel(*refs, mode, n_feat, n_x, tc, tm, row0):
    feats, rest = refs[:n_feat], list(refs[n_feat:])
    hg_ref = rest.pop(0) if mode != "plain" else None
    w_ref = rest.pop(0)
    x_refs = [rest.pop(0) for _ in range(n_x)]
    ml_ref, mc_ref, o_ref = rest
    if mode == "plain":
        u = feats[0][0]
    elif mode == "mlstm":
        h0_ref, h1_ref, og_ref, z_ref = feats
        hn = _head_rms(h0_ref[0].astype(F32) + h1_ref[0].astype(F32), hg_ref[...], ML_HEADS)
        u = (hn * (1.0 + jnp.tanh(og_ref[0].astype(F32))) * _silu_of_half(z_ref[0].astype(F32))).astype(BF16)
    else:
        h0_ref, h1_ref, z_ref = feats
        hn = _head_rms(h0_ref[0].astype(F32) + h1_ref[0].astype(F32), hg_ref[...], HG_HEADS)
        u = (hn * z_ref[0].astype(F32)).astype(BF16)
    y = jnp.dot(u, w_ref[...], preferred_element_type=F32)
    first = row0 + pl.program_id(1) * tm
    row = first + lax.broadcasted_iota(jnp.int32, (tm, 1), 0)
    gate = jnp.where(row < tc, mc_ref[0, 2:3, :], ml_ref[0, 2:3, :])
    x = x_refs[0][0] if n_x == 1 else jnp.where(first < tc, x_refs[0][0], x_refs[1][0])
    o_ref[0] = x + gate * y


def _outproj(mode, feats, w_out, xs, mod_l, tc, head_g=None, lat_only=False):
    x_arrs = list(xs) if isinstance(xs, tuple) else [xs]
    b, d = x_arrs[0].shape[0], x_arrs[0].shape[2]
    ta = sum(a.shape[1] for a in x_arrs)
    kdim = w_out.shape[0]
    if lat_only:
        tm = OUT_ROWS_LAT if (ta - tc) % OUT_ROWS_LAT == 0 else OUT_ROWS_SPLIT
    else:
        tm = OUT_ROWS if (len(x_arrs) == 1 and ta % OUT_ROWS == 0) else OUT_ROWS_SPLIT
        assert ta % tm == 0 and (tc % tm == 0 or len(x_arrs) == 1)
    row0 = tc if lat_only else 0
    nct = tc // tm
    row_spec = lambda width, col: pl.BlockSpec((pl.Element(1), pl.Element(tm), pl.Element(width)),
                                               lambda i, r, col=col: (i, pl.multiple_of(row0 + r * tm, ROW_BLOCK),
                                                                      col * width))
    in_specs, args = [], []
    for arr, width, col in feats:
        in_specs.append(row_spec(width, col))
        args.append(arr)
    if head_g is not None:
        in_specs.append(pl.BlockSpec((1, kdim), lambda i, r: (0, 0)))
        args.append(head_g.reshape(1, kdim))
    in_specs.append(pl.BlockSpec((kdim, d), lambda i, r: (0, 0)))
    args.append(w_out.astype(BF16))
    if len(x_arrs) == 1:
        in_specs.append(row_spec(d, 0))
    elif lat_only:
        x_arrs = x_arrs[1:]
        in_specs.append(pl.BlockSpec((1, tm, d), lambda i, r: (i, r, 0)))
    else:
        in_specs += [pl.BlockSpec((1, tm, d), lambda i, r: (i, jnp.minimum(r, nct - 1), 0)),
                     pl.BlockSpec((1, tm, d), lambda i, r: (i, jnp.maximum(r - nct, 0), 0))]
    args += x_arrs
    in_specs += [pl.BlockSpec((1, 3, d), lambda i, r: (i, 0, 0)),
                 pl.BlockSpec((1, 3, d), lambda i, r: (0, 0, 0))]
    args += [mod_l[:b], mod_l[b:b + 1]]
    rows_out = ta - row0
    return pl.pallas_call(
        functools.partial(_outproj_kernel, mode=mode, n_feat=len(feats), n_x=len(x_arrs), tc=tc, tm=tm,
                          row0=row0),
        out_shape=jax.ShapeDtypeStruct((b, rows_out, d), F32),
        grid=(b, rows_out // tm),
        in_specs=in_specs,
        out_specs=pl.BlockSpec((1, tm, d), lambda i, r: (i, r, 0)),
        compiler_params=_cparams("parallel", "parallel"),
        name="outproj_" + mode,
    )(*args)


def _conv_kernel(x_ref, ml_ref, mc_ref, g_ref, wx_ref, wb_ref, wc_ref, wz_ref, cw_ref, cb_ref, o_ref, h_ref, *, tc):
    ta = h_ref.shape[0]
    chunk = PROJ_ROWS if ta % PROJ_ROWS == 0 else ta

    def body(first_tile):
        wx, wc, wb, wz = (w_ref[...].astype(BF16) for w_ref in (wx_ref, wc_ref, wb_ref, wz_ref))
        step = chunk if first_tile else ta
        xin, cg = [], []
        for r0 in range(0, ta, step):
            if first_tile:
                _modulated((x_ref,), ml_ref, mc_ref, g_ref, h_ref, tc, r0, step)
            h = h_ref[r0:r0 + step, :]
            xin.append(jnp.dot(h, wx, preferred_element_type=F32))
            cg.append(jnp.dot(h, wc, preferred_element_type=F32))
        bg = jnp.dot(h_ref[...], wb, preferred_element_type=F32)
        z = jnp.dot(h_ref[...], wz, preferred_element_type=F32)
        u = jnp.concatenate(cg, axis=0) * jnp.concatenate(xin, axis=0)
        row = lax.broadcasted_iota(jnp.int32, (ta, 1), 0)
        first = (row == 0) | (row == tc)
        last = (row == tc - 1) | (row == ta - 1)
        u_prev = jnp.where(first, 0.0, pltpu.roll(u, 1, axis=0))
        u_next = jnp.where(last, 0.0, pltpu.roll(u, ta - 1, axis=0))
        cw = cw_ref[...]
        y = u_prev * cw[0:1, :] + u * cw[1:2, :] + u_next * cw[2:3, :] + cb_ref[...]
        o_ref[0] = (bg * y * _silu(z)).astype(o_ref.dtype)

    @pl.when(pl.program_id(1) == 0)
    def _():
        body(True)

    @pl.when(pl.program_id(1) != 0)
    def _():
        body(False)


def _conv_mixer(xs, mod_l, norm_g, w_in, conv_w, conv_b, tc, tw=256):
    b, ta, d = xs.shape
    e = conv_w.shape[1]
    nt = e // tw
    w_spec = lambda part: pl.BlockSpec((d, tw), lambda i, j, part=part: (0, part * nt + j))
    return pl.pallas_call(
        functools.partial(_conv_kernel, tc=tc),
        out_shape=jax.ShapeDtypeStruct((b, ta, e), BF16),
        grid=(b, nt),
        in_specs=[pl.BlockSpec((1, ta, d), lambda i, j: (i, 0, 0)),
                  pl.BlockSpec((1, 3, d), lambda i, j: (i, 0, 0)),
                  pl.BlockSpec((1, 3, d), lambda i, j: (0, 0, 0)),
                  pl.BlockSpec((1, d), lambda i, j: (0, 0)),
                  w_spec(0), w_spec(1), w_spec(2), w_spec(3),
                  pl.BlockSpec((SC_KSIZE, tw), lambda i, j: (0, j)),
                  pl.BlockSpec((1, tw), lambda i, j: (0, j))],
        out_specs=pl.BlockSpec((1, ta, tw), lambda i, j: (i, 0, j)),
        scratch_shapes=[pltpu.VMEM((ta, d), BF16)],
        compiler_params=_cparams("parallel", "arbitrary"),
        name="conv_mixer",
    )(xs, mod_l[:b], mod_l[b:b + 1], norm_g.reshape(1, d), w_in, w_in, w_in, w_in, conv_w, conv_b.reshape(1, e))


AT_HALF = AT_HEAD_DIM // 2
AT_TILE_PERM = np.concatenate([np.arange(0, AT_HALF), np.arange(2 * AT_HALF, 3 * AT_HALF),
                               np.arange(AT_HALF, 2 * AT_HALF), np.arange(3 * AT_HALF, 4 * AT_HALF)])
LOG2E = float(np.log2(np.e))


def _rope_tables(tc, t):
    rows = t // GRID_W
    row = np.repeat(np.arange(rows), GRID_W).astype(np.float64)
    col = np.tile(np.arange(GRID_W), rows).astype(np.float64)
    n_freq = AT_HEAD_DIM // 4
    freqs = np.power(ROPE_BASE, -np.arange(n_freq, dtype=np.float64) / n_freq)
    ang = np.concatenate([row[:, None] * freqs, col[:, None] * freqs], axis=-1)
    cos, sin = np.cos(ang), np.sin(ang)
    cos_t = np.concatenate([np.ones((tc, LANES)), np.tile(cos, (1, 4))], axis=0)
    sin_t = np.concatenate([np.zeros((tc, LANES)), np.concatenate([-sin, -sin, sin, sin], axis=-1)], axis=0)
    return jnp.asarray(cos_t, F32), jnp.asarray(sin_t, F32)


def _norm_rope_tiles(xs, gains, cos, sin, same_head):
    sums = []
    for x in xs:
        x2 = x * x
        hi = x2.astype(BF16)
        lo = (x2 - hi.astype(F32)).astype(BF16)
        sums.append((jnp.dot(hi, same_head, preferred_element_type=F32),
                     jnp.dot(lo, same_head, preferred_element_type=F32)))
    out = []
    for x, gain, (s_hi, s_lo) in zip(xs, gains, sums):
        xn = x * lax.rsqrt((s_hi + s_lo) * (1.0 / AT_HEAD_DIM) + EPS) * gain
        out.append(xn * cos + pltpu.roll(xn, 2 * AT_HALF, axis=1) * sin)
    return out


def _attn_kernel(sink_ref, q_ref, k_ref, v_ref, z_ref, qg_ref, kg_ref, cos_ref, sin_ref, sh_ref, wo_ref, x_ref,
                 ml_ref, mc_ref, o_ref, qs_ref, ka_ref, kb_ref, va_ref, vb_ref, *, tc, need_ctx):
    ta = q_ref.shape[1]
    t = ta - tc
    nb = t // AT_BLOCK
    blk = AT_BLOCK
    gw = x_ref.shape[2]
    g = pl.program_id(1)
    odd = (g % 2) == 1
    scale = (AT_HEAD_DIM ** -0.5) * LOG2E

    @pl.when(g == 0)
    def _():
        o_ref[...] = jnp.zeros(o_ref.shape, F32)

    def prep(i, carry):
        r0 = pl.multiple_of(i * ROW_BLOCK, ROW_BLOCK)
        res_cols = pl.ds(pl.multiple_of(g * gw, gw), gw)
        o_ref[0, pl.ds(r0, ROW_BLOCK), res_cols] = (o_ref[0, pl.ds(r0, ROW_BLOCK), res_cols]
                                                    + x_ref[0, pl.ds(r0, ROW_BLOCK), :])
        cos = cos_ref[pl.ds(r0, ROW_BLOCK), :]
        sin = sin_ref[pl.ds(r0, ROW_BLOCK), :]
        same_head = sh_ref[...]
        tiles = [q_ref[0, pl.ds(r0, ROW_BLOCK), c * LANES:(c + 1) * LANES].astype(F32) for c in range(2)]
        tiles.append(k_ref[0, pl.ds(r0, ROW_BLOCK), :].astype(F32))
        q0, q1, kn = _norm_rope_tiles(tiles, (qg_ref[...], qg_ref[...], kg_ref[...]), cos, sin, same_head)
        for c, qc in enumerate((q0, q1)):
            qs_ref[pl.ds(r0, ROW_BLOCK), c * LANES:(c + 1) * LANES] = (qc * scale).astype(BF16)
        vv = v_ref[0, pl.ds(r0, ROW_BLOCK), :].astype(F32)
        lane = lax.broadcasted_iota(jnp.int32, kn.shape, 1)
        k_own = jnp.where(((lane // AT_HALF) % 2) == (g % 2), kn, 0.0)
        k_oth = pltpu.roll(k_own, jnp.where(odd, 3 * AT_HALF, AT_HALF), axis=1)
        ka_ref[pl.ds(r0, ROW_BLOCK), :] = jnp.where(odd, k_oth, k_own).astype(BF16)
        kb_ref[pl.ds(r0, ROW_BLOCK), :] = jnp.where(odd, k_own, k_oth).astype(BF16)
        v_own = jnp.where((lane // AT_HEAD_DIM) == (g % 2), vv, 0.0)
        v_oth = pltpu.roll(v_own, AT_HEAD_DIM, axis=1)
        va = jnp.where(odd, v_oth, v_own)
        vb = jnp.where(odd, v_own, v_oth)
        va_ref[pl.ds(r0, ROW_BLOCK), :] = jnp.where(lane == AT_HEAD_DIM, 1.0, va).astype(BF16)
        vb_ref[pl.ds(r0, ROW_BLOCK), :] = jnp.where(lane == 0, 1.0, vb).astype(BF16)
        return carry

    lax.fori_loop(0, ta // ROW_BLOCK, prep, 0)
    zeros = jnp.zeros((blk, LANES), BF16)
    for ref in (ka_ref, kb_ref, va_ref, vb_ref):
        ref[ta:ta + blk, :] = zeros

    half = lax.broadcasted_iota(jnp.int32, (2 * blk, 1), 0) < blk
    sink_a = jnp.where(half, sink_ref[g, 0], sink_ref[g, 2]) * LOG2E
    sink_b = jnp.where(half, sink_ref[g, 1], sink_ref[g, 3]) * LOG2E

    qi = lax.broadcasted_iota(jnp.int32, (2 * blk, 3 * blk), 0) % blk
    kj = lax.broadcasted_iota(jnp.int32, (2 * blk, 3 * blk), 1)
    band = (kj - qi >= 0) & (kj - qi <= 2 * blk)
    out_lo = lax.broadcasted_iota(jnp.int32, (2 * blk, LANES), 1) < AT_HEAD_DIM

    nt = (((1,), (1,)), ((), ()))

    sides = ((ka_ref, va_ref, sink_a, AT_HEAD_DIM), (kb_ref, vb_ref, sink_b, 0))

    def attend(blocks, r_first, gate):
        qts = [jnp.concatenate([qs_ref[pl.ds(r0, blk), 0:LANES], qs_ref[pl.ds(r0, blk), LANES:2 * LANES]], axis=0)
               for r0, _, _ in blocks]
        scores = []
        for qt, (_, k0, mask) in zip(qts, blocks):
            for k_ref_, _, _, _ in sides:
                s_ctx = lax.dot_general(qt, k_ref_[0:tc, :], nt, preferred_element_type=F32)
                s_loc = None if mask is None else lax.dot_general(qt, k_ref_[pl.ds(k0, 3 * blk), :], nt,
                                                                  preferred_element_type=F32)
                scores.append((s_ctx, s_loc))
        probs = []
        for idx, (s_ctx, s_loc) in enumerate(scores):
            mask = blocks[idx // 2][2]
            sink = sides[idx % 2][2]
            m = jnp.maximum(sink, jnp.max(s_ctx, axis=-1, keepdims=True))
            if s_loc is not None:
                s_loc = jnp.where(mask, s_loc, -jnp.inf)
                m = jnp.maximum(m, jnp.max(s_loc, axis=-1, keepdims=True))
            probs.append((jnp.exp2(s_ctx - m).astype(BF16),
                          None if s_loc is None else jnp.exp2(s_loc - m).astype(BF16), jnp.exp2(sink - m)))
        accs = []
        for idx, (p_ctx, p_loc, _) in enumerate(probs):
            k0 = blocks[idx // 2][1]
            v_ref_ = sides[idx % 2][1]
            acc = jnp.dot(p_ctx, v_ref_[0:tc, :], preferred_element_type=F32)
            if p_loc is not None:
                acc = acc + jnp.dot(p_loc, v_ref_[pl.ds(k0, 3 * blk), :], preferred_element_type=F32)
            accs.append(acc)
        us = []
        for bi, (r0, _, _) in enumerate(blocks):
            halves = []
            for si in range(2):
                acc, ones_lane = accs[2 * bi + si], sides[si][3]
                halves.append(acc / (probs[2 * bi + si][2] + acc[:, ones_lane:ones_lane + 1]))
            o = jnp.where(out_lo, halves[0], halves[1])
            parts = []
            for c in range(2):
                zc = z_ref[0, pl.ds(r0, blk), c * LANES:(c + 1) * LANES].astype(F32)
                parts.append((o[c * blk:(c + 1) * blk] * _silu_of_half(zc)).astype(BF16))
            us.append(jnp.concatenate(parts, axis=1))
        u = jnp.concatenate(us, axis=0)
        rows = pl.ds(r_first, u.shape[0])
        o_ref[0, rows, :] = o_ref[0, rows, :] + gate * jnp.dot(u, wo_ref[...], preferred_element_type=F32)

    def lat_blocks(i, carry):
        blocks = []
        for j in range(AT_PROJ_BLOCKS):
            n = i * AT_PROJ_BLOCKS + j
            kpos = (n - 1) * blk + kj
            blocks.append((pl.multiple_of(tc + n * blk, blk), pl.multiple_of(tc + (n - 1) * blk, blk),
                           band & (kpos >= 0) & (kpos < t)))
        attend(blocks, pl.multiple_of(tc + i * AT_PROJ_BLOCKS * blk, AT_PROJ_BLOCKS * blk), ml_ref[0, 2:3, :])
        return carry

    lax.fori_loop(0, nb // AT_PROJ_BLOCKS, lat_blocks, 0)
    if need_ctx:
        attend([(n * blk, None, None) for n in range(tc // blk)], 0, mc_ref[0, 2:3, :])


def _attn_mixer(p, at_q_g, at_k_g, at_sink, w_out, xs, mod_l, tc, need_ctx):
    b, ta, _ = p.shape
    t = ta - tc
    d = xs.shape[2]
    assert tc >= AT_BLOCK and tc % AT_BLOCK == 0 and t % (AT_PROJ_BLOCKS * AT_BLOCK) == 0 and ta % ROW_BLOCK == 0
    assert d == AT_Q
    cos_t, sin_t = _rope_tables(tc, t)
    lane_head = (np.arange(LANES) // AT_HALF) % 2
    same_head = jnp.asarray(lane_head[:, None] == lane_head[None, :], BF16)
    tile_gain = lambda gain: jnp.tile(gain, 2)[AT_TILE_PERM].reshape(1, LANES)
    gw = AT_GROUP * AT_HEAD_DIM
    kcol = AT_Q // LANES
    vcol = (AT_Q + AT_KV) // LANES
    zcol = (AT_Q + 2 * AT_KV) // gw
    return pl.pallas_call(
        functools.partial(_attn_kernel, tc=tc, need_ctx=need_ctx),
        out_shape=jax.ShapeDtypeStruct((b, ta, d), F32),
        grid=(b, AT_KV_HEADS),
        in_specs=[pl.BlockSpec(memory_space=pltpu.SMEM),
                  pl.BlockSpec((1, ta, gw), lambda i, g: (i, 0, g)),
                  pl.BlockSpec((1, ta, LANES), lambda i, g: (i, 0, kcol + g // 2)),
                  pl.BlockSpec((1, ta, LANES), lambda i, g: (i, 0, vcol + g // 2)),
                  pl.BlockSpec((1, ta, gw), lambda i, g: (i, 0, zcol + g)),
                  pl.BlockSpec((1, LANES), lambda i, g: (0, 0)),
                  pl.BlockSpec((1, LANES), lambda i, g: (0, 0)),
                  pl.BlockSpec((ta, LANES), lambda i, g: (0, 0)),
                  pl.BlockSpec((ta, LANES), lambda i, g: (0, 0)),
                  pl.BlockSpec((LANES, LANES), lambda i, g: (0, 0)),
                  pl.BlockSpec((gw, d), lambda i, g: (g, 0)),
                  pl.BlockSpec((1, ta, gw), lambda i, g: (i, 0, g)),
                  pl.BlockSpec((1, 3, d), lambda i, g: (i, 0, 0)),
                  pl.BlockSpec((1, 3, d), lambda i, g: (0, 0, 0))],
        out_specs=pl.BlockSpec((1, ta, d), lambda i, g: (i, 0, 0)),
        scratch_shapes=[pltpu.VMEM((ta, gw), BF16)] + [pltpu.VMEM((ta + AT_BLOCK, LANES), BF16)] * 4,
        compiler_params=_cparams("parallel", "arbitrary"),
        name="attn_mixer",
    )(at_sink.reshape(AT_KV_HEADS, AT_GROUP), p, p, p, p, tile_gain(at_q_g), tile_gain(at_k_g), cos_t, sin_t,
      same_head, w_out.astype(BF16), xs, mod_l[:b], mod_l[b:b + 1])


def _attn_weight_kernel(w_ref, p_ref, o_ref, *, n_qk, n_v):
    tiles = w_ref.shape[1] // LANES
    for lt in range(tiles):
        tile = pl.program_id(0) * tiles + lt
        kind = (tile >= n_qk).astype(jnp.int32) + (tile >= n_qk + n_v).astype(jnp.int32)
        cols = slice(lt * LANES, (lt + 1) * LANES)
        o_ref[:, cols] = jnp.dot(w_ref[:, cols].astype(BF16), p_ref[kind], preferred_element_type=F32).astype(BF16)


def _attn_weight(w_in):
    d, n = w_in.shape
    n_qk, n_v = (AT_Q + AT_KV) // LANES, AT_KV // LANES
    perm = np.zeros((LANES, LANES), np.float32)
    perm[AT_TILE_PERM, np.arange(LANES)] = 1.0
    eye = np.eye(LANES, dtype=np.float32)
    mats = jnp.asarray(np.stack([perm, eye, 0.5 * eye]), BF16)
    tw = 4 * LANES
    assert n % tw == 0
    return pl.pallas_call(
        functools.partial(_attn_weight_kernel, n_qk=n_qk, n_v=n_v),
        out_shape=jax.ShapeDtypeStruct((d, n), BF16),
        grid=(n // tw,),
        in_specs=[pl.BlockSpec((d, tw), lambda j: (0, j)),
                  pl.BlockSpec((3, LANES, LANES), lambda j: (0, 0, 0))],
        out_specs=pl.BlockSpec((d, tw), lambda j: (0, j)),
        compiler_params=_cparams("parallel"),
        name="attn_weight",
    )(w_in, mats)


def _chunk_order(nc, ncc, d, s):
    bwd = jnp.where(s < ncc, ncc - 1 - s, nc - 1 - (s - ncc))
    return jnp.where(d == 0, s, bwd)


ML_GATE_PERM = np.concatenate([np.arange(0, 4), np.arange(8, 12), np.arange(4, 8), np.arange(12, 16)])
ML_NQ = 6


def _mlstm_gate_kernel(g_ref, b_ref, a_ref, c_ref, *, tc, lc):
    h = ML_HEADS
    ta = g_ref.shape[2]
    nc, ncc = ta // lc, tc // lc
    x = g_ref[0] + b_ref[...]
    li = x[0:2 * h]
    lfp = x[2 * h:4 * h]
    lf = jnp.minimum(lfp, 0.0) - jnp.log1p(jnp.exp(-jnp.abs(lfp)))
    fwd = lax.broadcasted_iota(jnp.int32, (2 * h, ta), 0) < h
    fwd1 = lax.broadcasted_iota(jnp.int32, (2 * h, 1), 0) < h
    pos = lax.broadcasted_iota(jnp.int32, (2 * h, ta), 1) % lc

    def seg_scan(y, op, fill):
        yf, yb = y, y
        s = 1
        while s < lc:
            yf = op(yf, jnp.where(pos >= s, pltpu.roll(yf, s, axis=1), fill))
            yb = op(yb, jnp.where(pos < lc - s, pltpu.roll(yb, ta - s, axis=1), fill))
            s *= 2
        return jnp.where(fwd, yf, yb)

    bsum = seg_scan(lf, jnp.add, 0.0)
    a = li - bsum
    cmax = seg_scan(a, jnp.maximum, -jnp.inf)

    def end_col(y, c):
        return jnp.where(fwd1, y[:, (c + 1) * lc - 1:(c + 1) * lc], y[:, c * lc:c * lc + 1])

    tot = [end_col(bsum, c) for c in range(nc)]
    amax = [end_col(cmax, c) for c in range(nc)]

    def chain(order):
        m = jnp.zeros((2 * h, 1), F32)
        m_in = [None] * nc
        for c in order:
            m_in[c] = m
            m = tot[c] + jnp.maximum(m, amax[c])
        return m_in

    m_f = chain(list(range(nc)))
    m_b = chain(list(range(ncc - 1, -1, -1)) + list(range(nc - 1, ncc - 1, -1)))
    for c in range(nc):
        m_in = jnp.where(fwd1, m_f[c], m_b[c])
        sl = slice(c * lc, (c + 1) * lc)
        g_run = jnp.maximum(m_in, cmax[:, sl])
        g_end = jnp.maximum(m_in, amax[c])
        nr = 2 * h
        quantities = [a[:, sl], g_run, jnp.exp(m_in - g_run), jnp.exp(-(bsum[:, sl] + g_run)),
                      jnp.exp(a[:, sl] - g_end), jnp.broadcast_to(jnp.exp(m_in - g_end), (nr, lc))]
        a_ref[0, c] = quantities[0]
        pad = jnp.zeros((LANES - ML_NQ * nr, lc), F32)
        c_ref[0, c] = jnp.concatenate(quantities + [pad], axis=0).T


def _mlstm_scan_kernel(q_ref, k_ref, v_ref, a_ref, c_ref, o_ref, cs_ref, *, rev):
    lc = q_ref.shape[1]
    nr = 2 * ML_HEADS

    @pl.when(pl.program_id(1) == 0)
    def _():
        cs_ref[...] = jnp.zeros(cs_ref.shape, F32)

    ti = lax.broadcasted_iota(jnp.int32, (lc, lc), 0)
    si = lax.broadcasted_iota(jnp.int32, (lc, lc), 1)
    mask = (si >= ti) if rev else (si <= ti)
    ones = jnp.ones((lc, LANES), BF16)
    nt = (((1,), (1,)), ((), ()))
    tn = (((0,), (0,)), ((), ()))
    heads = range(ML_HEADS)
    qb = [q_ref[0, :, h * ML_DK:(h + 1) * ML_DK] for h in heads]
    kb = [k_ref[0, :, h * ML_DK:(h + 1) * ML_DK] for h in heads]
    v_aug = [jnp.concatenate([v_ref[0, :, h * ML_DV:(h + 1) * ML_DV], ones], axis=1) for h in heads]
    cols = []
    for h in heads:
        r = (ML_HEADS if rev else 0) + h
        cols.append([a_ref[0, 0, r:r + 1, :]] + [c_ref[0, 0, :, qi * nr + r:qi * nr + r + 1] for qi in range(1, 6)])
    qk = [lax.dot_general(qb[h], kb[h], nt, preferred_element_type=F32) for h in heads]
    c_old = [cs_ref[h] for h in heads]
    q_c = [jnp.dot(qb[h], c_old[h].astype(BF16), preferred_element_type=F32) for h in heads]
    kw = [(kb[h].astype(F32) * cols[h][4]).astype(BF16) for h in heads]
    upd = [lax.dot_general(kw[h], v_aug[h], tn, preferred_element_type=F32) for h in heads]
    s = []
    for h in heads:
        a_row, g_run = cols[h][0], cols[h][1]
        s.append((qk[h] * jnp.where(mask, jnp.exp(a_row - g_run), 0.0)).astype(BF16))
    s_v = [jnp.dot(s[h], v_aug[h], preferred_element_type=F32) for h in heads]
    for h in heads:
        inter, eclamp, decay = cols[h][2], cols[h][3], cols[h][5]
        tot = inter * q_c[h] + s_v[h]
        inv = 1.0 / jnp.maximum(jnp.abs(tot[:, ML_DV:]), eclamp)
        o_ref[0, :, h * ML_DV:(h + 1) * ML_DV] = (
            tot[:, :ML_DV] * jnp.concatenate([inv] * (ML_DV // LANES), axis=1)).astype(o_ref.dtype)
        cs_ref[h] = decay[0:1, :] * c_old[h] + upd[h]


def _mlstm_scan(p, gates, gate_b, tc):
    b, ta, _ = p.shape
    lc = ML_CHUNK
    assert tc % lc == 0 and ta % lc == 0
    nc, ncc = ta // lc, tc // lc
    ng = 4 * ML_HEADS
    nr = 2 * ML_HEADS
    bias = gate_b.reshape(ng)[ML_GATE_PERM].reshape(ng, 1)
    a_rows, cols = pl.pallas_call(
        functools.partial(_mlstm_gate_kernel, tc=tc, lc=lc),
        out_shape=[jax.ShapeDtypeStruct((b, nc, nr, lc), F32), jax.ShapeDtypeStruct((b, nc, lc, LANES), F32)],
        grid=(b,),
        in_specs=[pl.BlockSpec((1, ng, ta), lambda i: (i, 0, 0)),
                  pl.BlockSpec((ng, 1), lambda i: (0, 0))],
        out_specs=[pl.BlockSpec((1, nc, nr, lc), lambda i: (i, 0, 0, 0)),
                   pl.BlockSpec((1, nc, lc, LANES), lambda i: (i, 0, 0, 0))],
        compiler_params=_cparams("parallel"),
        name="mlstm_gates",
    )(gates, bias)
    outs = []
    for d in range(2):
        chunk = functools.partial(_chunk_order, nc, ncc, d)
        outs.append(pl.pallas_call(
            functools.partial(_mlstm_scan_kernel, rev=bool(d)),
            out_shape=jax.ShapeDtypeStruct((b, ta, ML_INNER), BF16),
            grid=(b, nc),
            in_specs=[pl.BlockSpec((1, lc, ML_QK), lambda i, s, chunk=chunk: (i, chunk(s), 0)),
                      pl.BlockSpec((1, lc, ML_QK), lambda i, s, chunk=chunk: (i, chunk(s), 1)),
                      pl.BlockSpec((1, lc, ML_INNER), lambda i, s, chunk=chunk: (i, chunk(s), 1)),
                      pl.BlockSpec((1, 1, nr, lc), lambda i, s, chunk=chunk: (i, chunk(s), 0, 0)),
                      pl.BlockSpec((1, 1, lc, LANES), lambda i, s, chunk=chunk: (i, chunk(s), 0, 0))],
            out_specs=pl.BlockSpec((1, lc, ML_INNER), lambda i, s, chunk=chunk: (i, chunk(s), 0)),
            scratch_shapes=[pltpu.VMEM((ML_HEADS, ML_DK, ML_DV + LANES), F32)],
            compiler_params=_cparams("parallel", "arbitrary"),
            name="mlstm_scan_bwd" if d else "mlstm_scan_fwd",
        )(p, p, p, a_rows, cols))
    return outs


def _hgrn_lb_kernel(p_ref, o_ref, *, layer):
    for d in range(p_ref.shape[0]):
        x = p_ref[d]
        e = jnp.exp(x - jnp.max(x, axis=0, keepdims=True))
        p = e / jnp.sum(e, axis=0, keepdims=True)
        acc = jnp.zeros((1, x.shape[1]), F32)
        for j in range(1, layer + 1):
            acc = acc + p[j:j + 1, :]
        o_ref[d:d + 1, :] = acc


def _cumsum_rows(x, tri2):
    hi = x.astype(BF16)
    lo = (x - hi.astype(F32)).astype(BF16)
    return jnp.dot(tri2, jnp.concatenate([hi, lo], axis=0), preferred_element_type=F32)


def _tri2(n, rev):
    t = np.triu(np.ones((n, n), np.float32)) if rev else np.tril(np.ones((n, n), np.float32))
    return jnp.asarray(np.concatenate([t, t], axis=1), BF16)


def _anchor_rows(a, m, rev):
    n, f = a.shape
    idx = m if rev else m - 1
    if 2 * m >= SUBLANES:
        a3 = a.reshape(n // (2 * m), 2 * m, f)
        return jnp.broadcast_to(a3[:, idx:idx + 1, :], a3.shape).reshape(n, f)
    a3 = a.reshape(n // SUBLANES, SUBLANES, f)
    sub = lax.broadcasted_iota(jnp.int32, a3.shape, 1)
    out = None
    for gi in range(SUBLANES // (2 * m) - 1, -1, -1):
        cand = jnp.broadcast_to(a3[:, gi * 2 * m + idx:gi * 2 * m + idx + 1, :], a3.shape)
        out = cand if out is None else jnp.where(sub < (gi + 1) * 2 * m, cand, out)
    return out.reshape(n, f)


HG_MAX_LOG_SPAN = 64.0


def _hgrn_scan_kernel(fmin_ref, q_ref, f_ref, i_ref, tri_ref, o_ref, s_ref, *, rev, nc, ncc):
    lc = HG_CHUNK
    subs = range(q_ref.shape[1] // lc)
    subs = tuple(reversed(subs)) if rev else tuple(subs)
    step_block = _chunk_order(nc, ncc, 1 if rev else 0, pl.program_id(1))

    @pl.when(pl.program_id(1) == 0)
    def _():
        s_ref[...] = jnp.zeros(s_ref.shape, F32)

    row = lax.broadcasted_iota(jnp.int32, (lc, 1), 0)
    ti = lax.broadcasted_iota(jnp.int32, (lc, lc), 0)
    si = lax.broadcasted_iota(jnp.int32, (lc, lc), 1)
    nt = (((1,), (1,)), ((), ()))
    tn = (((0,), (0,)), ((), ()))
    end = 0 if rev else lc - 1

    def all_heads_single_anchor():
        causal = (si >= ti) if rev else (si <= ti)
        cols = [slice(h * HG_DIM, (h + 1) * HG_DIM) for h in range(HG_HEADS)]
        pre = {}
        for ci in subs:
            rows = slice(ci * lc, (ci + 1) * lc)
            f = f_ref[0, rows, :]
            a = _cumsum_rows(jnp.log(f), tri_ref[...])
            q_dec = (q_ref[0, rows, :].astype(F32) * jnp.exp(a)).astype(BF16)
            k_inv = (1.0 - f) * jnp.exp(-a)
            k_inv_b = k_inv.astype(BF16)
            e_end = jnp.exp(a[end:end + 1, :])
            kd = (k_inv * e_end).astype(BF16)
            attn = [jnp.where(causal, lax.dot_general(q_dec[:, c], k_inv_b[:, c], nt, preferred_element_type=F32),
                              0.0).astype(BF16) for c in cols]
            pre[ci] = (rows, q_dec, e_end, kd, attn)
        state = [s_ref[h] for h in range(HG_HEADS)]
        for ci in subs:
            rows, q_dec, e_end, kd, attn = pre[ci]
            for h, c in enumerate(cols):
                o = jnp.dot(attn[h], i_ref[0, rows, c], preferred_element_type=F32)
                o = o + lax.dot_general(q_dec[:, c], state[h].astype(BF16), nt, preferred_element_type=F32)
                o_ref[0, rows, c] = o.astype(o_ref.dtype)
            state = [state[h] * e_end[:, c] + lax.dot_general(i_ref[0, rows, c], kd[:, c], tn,
                                                              preferred_element_type=F32)
                     for h, c in enumerate(cols)]
        for h in range(HG_HEADS):
            s_ref[h] = state[h]

    def head_per_level(r0, h, carry):
        c0 = pl.multiple_of(h * HG_DIM, HG_DIM)
        q = q_ref[0, r0:r0 + lc, pl.ds(c0, HG_DIM)].astype(F32)
        f = f_ref[0, r0:r0 + lc, pl.ds(c0, HG_DIM)]
        k = 1.0 - f
        iv = i_ref[0, r0:r0 + lc, pl.ds(c0, HG_DIM)].astype(F32)
        a = _cumsum_rows(jnp.log(f), tri_ref[...])
        attn = jnp.zeros((lc, lc), F32)
        m = 1
        while m < lc:
            e = jnp.exp(-jnp.abs(a - _anchor_rows(a, m, rev)))
            upper = (row % (2 * m)) >= m
            is_q = jnp.logical_not(upper) if rev else upper
            qt = jnp.where(is_q, q * e, 0.0).astype(BF16)
            kt = jnp.where(is_q, 0.0, k * e).astype(BF16)
            pair = lax.dot_general(qt, kt, nt, preferred_element_type=F32)
            attn = attn + jnp.where((ti // (2 * m)) == (si // (2 * m)), pair, 0.0)
            m *= 2
        a_end = a[end:end + 1, :]
        kd = (k * jnp.exp(a_end - a)).astype(BF16)
        ib = iv.astype(BF16)
        s_old = s_ref[h]
        o = jnp.dot(attn.astype(BF16), ib, preferred_element_type=F32)
        o = o + lax.dot_general((q * jnp.exp(a)).astype(BF16), s_old.astype(BF16), nt, preferred_element_type=F32)
        o_ref[0, r0:r0 + lc, pl.ds(c0, HG_DIM)] = (o + jnp.sum(q * k, axis=-1, keepdims=True) * iv).astype(o_ref.dtype)
        s_ref[h] = s_old * jnp.exp(a_end) + lax.dot_general(ib, kd, tn, preferred_element_type=F32)
        return carry

    in_range = fmin_ref[pl.program_id(0), step_block] >= float(np.exp(-HG_MAX_LOG_SPAN / lc))

    @pl.when(in_range)
    def _():
        all_heads_single_anchor()

    @pl.when(jnp.logical_not(in_range))
    def _():
        for ci in subs:
            lax.fori_loop(0, HG_HEADS, functools.partial(head_per_level, ci * lc), 0)


def _hgrn_scan(p, pf, f_min, tc, rev):
    b, ta, _ = p.shape
    lc = HG_CHUNK
    rows = HG_STEP_CHUNKS * lc
    hg = HG_HEADS * HG_DIM
    assert tc % rows == 0 and ta % rows == 0 and rows == ROW_BLOCK
    nc, ncc = ta // rows, tc // rows
    d = 1 if rev else 0
    chunk = lambda s: _chunk_order(nc, ncc, d, s)
    return pl.pallas_call(
        functools.partial(_hgrn_scan_kernel, rev=rev, nc=nc, ncc=ncc),
        out_shape=jax.ShapeDtypeStruct((b, ta, hg), BF16),
        grid_spec=pltpu.PrefetchScalarGridSpec(
            num_scalar_prefetch=1,
            grid=(b, nc),
            in_specs=[pl.BlockSpec((1, rows, hg), lambda i, s, fm: (i, chunk(s), 0)),
                      pl.BlockSpec((1, rows, hg), lambda i, s, fm: (i, chunk(s), d)),
                      pl.BlockSpec((1, rows, hg), lambda i, s, fm: (i, chunk(s), 1)),
                      pl.BlockSpec((lc, 2 * lc), lambda i, s, fm: (0, 0))],
            out_specs=pl.BlockSpec((1, rows, hg), lambda i, s, fm: (i, chunk(s), 0)),
            scratch_shapes=[pltpu.VMEM((HG_HEADS, HG_DIM, HG_DIM), F32)]),
        compiler_params=_cparams("parallel", "arbitrary"),
        name="hgrn_scan_bwd" if rev else "hgrn_scan_fwd",
    )(f_min, p, pf, p, _tri2(lc, rev))


def kernel(x, c, ctx, c_ctx, ada_w, ada_b, norm_g, ml_w_in, ml_gate_b, ml_head_g, ml_w_out, at_w_in, at_q_g, at_k_g, at_sink, at_w_out, sc_w_in, sc_conv_w, sc_conv_b, sc_w_out, hg_w_in, hg_f_b, hg_lb, hg_head_g, hg_w_out):
    tc = ctx.shape[1]
    mod = _ada_mod(c, c_ctx, ada_w, ada_b)
    xs = (ctx, x)
    for layer in range(DEPTH):
        kind, j = layer % 4, layer // 4
        need_ctx = layer < DEPTH - 1
        last = dict(lat_only=True) if layer == DEPTH - 1 else {}
        mod_l = mod[layer]
        if kind == 0:
            n_main = 2 * ML_QK + 3 * ML_INNER
            w_t = ml_w_in[j].T
            w_gate = jnp.zeros((LANES, ml_w_in.shape[1]), F32).at[:4 * ML_HEADS].set(w_t[n_main:][ML_GATE_PERM])
            col_scale = jnp.concatenate([jnp.full((ML_QK,), ML_DK ** -0.5, F32), jnp.ones((ML_QK + ML_INNER,), F32),
                                         jnp.full((2 * ML_INNER,), 0.5, F32)])
            p, gates = _inproj(xs, mod_l, norm_g[layer], w_t[:n_main] * col_scale[:, None], tc, tn=1024,
                               w_small=w_gate, out_dtype=BF16, w_rows_out=True)
            h_f, h_b = _mlstm_scan(p, gates, ml_gate_b[j], tc)
            feats = [(h_f, ML_INNER, 0), (h_b, ML_INNER, 0), (p, ML_INNER, 2), (p, ML_INNER, 3)]
            xs = _outproj("mlstm", feats, ml_w_out[j], xs, mod_l, tc, head_g=0.5 * ml_head_g[j], **last)
        elif kind == 1:
            p = _inproj(xs, mod_l, norm_g[layer], _attn_weight(at_w_in[j]), tc, out_dtype=BF16)
            if isinstance(xs, tuple):
                xs = jnp.concatenate(xs, axis=1)
            xs = _attn_mixer(p, at_q_g[j], at_k_g[j], at_sink[j], at_w_out[j], xs, mod_l, tc, need_ctx)
            if last:
                xs = xs[:, tc:, :]
        elif kind == 2:
            if isinstance(xs, tuple):
                xs = jnp.concatenate(xs, axis=1)
            u = _conv_mixer(xs, mod_l, norm_g[layer], sc_w_in[j], sc_conv_w[j], sc_conv_b[j], tc)
            xs = _outproj("plain", [(u, u.shape[2], 0)], sc_w_out[j], xs, mod_l, tc, **last)
        else:
            hg = HG_HEADS * HG_DIM
            lb = pl.pallas_call(
                functools.partial(_hgrn_lb_kernel, layer=layer),
                out_shape=jax.ShapeDtypeStruct((2, hg), F32),
                name="hgrn_lb",
            )(hg_lb[j])
            w = hg_w_in[j]
            w_qiz = jnp.concatenate([0.5 * w[:, :hg], w[:, 3 * hg:4 * hg], 0.5 * w[:, 4 * hg:]], axis=1)
            p = _inproj(xs, mod_l, norm_g[layer], w_qiz, tc, tn=1024, out_dtype=BF16,
                        epilogue=((0, hg, "silu"), (hg, 2 * hg, "id"), (2 * hg, 3 * hg, "silu")))
            lb2 = lb.reshape(2 * hg)
            pf, f_min = _inproj(xs, mod_l, norm_g[layer], 0.5 * w[:, hg:3 * hg], tc, tn=hg,
                                epilogue=((0, 2 * hg, "fgate"),), block_min=True,
                                pars=(0.5 * hg_f_b[j].reshape(2 * hg), 0.5 * (1.0 + lb2), 0.5 * (1.0 - lb2)))
            n_blk = pf.shape[1] // ROW_BLOCK
            o_f = _hgrn_scan(p, pf, f_min[:, 0, :n_blk, 0], tc, rev=False)
            o_b = _hgrn_scan(p, pf, f_min[:, 1, :n_blk, 0], tc, rev=True)
            feats = [(o_f, hg, 0), (o_b, hg, 0), (p, hg, 2)]
            xs = _outproj("hgrn", feats, hg_w_out[j], xs, mod_l, tc, head_g=hg_head_g[j], **last)
    return xs
```

```python
import functools

import numpy as np
import jax
import jax.numpy as jnp
from jax import lax
from jax.experimental import pallas as pl
from jax.experimental.pallas import tpu as pltpu

F32 = jnp.float32
BF16 = jnp.bfloat16
EPS = 1e-6
DEPTH = 4
GRID_W = 64
ROPE_BASE = 10000.0

ML_HEADS, ML_DK, ML_DV = 4, 256, 512
ML_QK = ML_HEADS * ML_DK
ML_INNER = ML_HEADS * ML_DV
ML_CHUNK = 256

AT_HEADS, AT_KV_HEADS, AT_HEAD_DIM = 16, 4, 64
AT_GROUP = AT_HEADS // AT_KV_HEADS
AT_BLOCK = 128
AT_PROJ_BLOCKS = 8
AT_Q = AT_HEADS * AT_HEAD_DIM
AT_KV = AT_KV_HEADS * AT_HEAD_DIM

SC_KSIZE = 3

HG_HEADS, HG_DIM = 8, 128
HG_CHUNK = 128
HG_STEP_CHUNKS = 2

LANES = 128
SUBLANES = 8
VMEM_LIMIT_BYTES = 56 * 1024 * 1024

ROW_BLOCK = 256
PROJ_ROWS = 768
OUT_ROWS = 768
OUT_ROWS_SPLIT = 256
OUT_ROWS_LAT = 1024


def _cparams(*sem):
    return pltpu.CompilerParams(dimension_semantics=sem, vmem_limit_bytes=VMEM_LIMIT_BYTES)


def _sigmoid(x):
    return 0.5 * jnp.tanh(0.5 * x) + 0.5


def _silu(x):
    return x * _sigmoid(x)


def _silu_of_half(xh):
    return xh * (1.0 + jnp.tanh(xh))


def _ada_kernel(c_ref, w_ref, b_ref, o_ref):
    s = _silu(c_ref[...])
    o_ref[0] = jnp.dot(s.astype(BF16), w_ref[0].astype(BF16), preferred_element_type=F32) + b_ref[0]


def _ada_mod(c, c_ctx, ada_w, ada_b):
    b, d = c.shape
    depth = ada_w.shape[0]
    rows = -(-(b + 1) // SUBLANES) * SUBLANES
    cc = jnp.zeros((rows, d), F32).at[:b].set(c).at[b].set(c_ctx)
    tn = 1024
    out = pl.pallas_call(
        _ada_kernel,
        out_shape=jax.ShapeDtypeStruct((depth, rows, 3 * d), F32),
        grid=(depth, 3 * d // tn),
        in_specs=[pl.BlockSpec((rows, d), lambda l, j: (0, 0)),
                  pl.BlockSpec((1, d, tn), lambda l, j: (l, 0, j)),
                  pl.BlockSpec((1, 1, tn), lambda l, j: (l, 0, j))],
        out_specs=pl.BlockSpec((1, rows, tn), lambda l, j: (l, 0, j)),
        compiler_params=_cparams("parallel", "parallel"),
        name="ada_mod",
    )(cc, ada_w, ada_b.reshape(depth, 1, 3 * d))
    return out.reshape(depth, rows, 3, d)


def _modulated(x_refs, ml_ref, mc_ref, g_ref, h_ref, tc, row0=0, nrows=None):
    nrows = h_ref.shape[0] - row0 if nrows is None else nrows
    g = g_ref[...]
    xc_ref = x_refs[0]
    xl_ref, lat0 = (x_refs[1], 0) if len(x_refs) == 2 else (x_refs[0], tc)
    for r0 in range(row0, row0 + nrows, ROW_BLOCK):
        ctx_rows = r0 < tc
        x_ref, src0, m_ref = (xc_ref, r0, mc_ref) if ctx_rows else (xl_ref, lat0 + r0 - tc, ml_ref)
        x = x_ref[0, src0:src0 + ROW_BLOCK, :]
        ms = jnp.mean(x * x, axis=-1, keepdims=True)
        xn = x * lax.rsqrt(ms + EPS) * g
        h = xn * (1.0 + m_ref[0, 1:2, :]) + m_ref[0, 0:1, :]
        h_ref[r0:r0 + ROW_BLOCK, :] = h.astype(BF16)


def _split_stream(xs):
    arrs = list(xs) if isinstance(xs, tuple) else [xs]
    specs = [pl.BlockSpec((1,) + a.shape[1:], lambda i, j: (i, 0, 0)) for a in arrs]
    ta = sum(a.shape[1] for a in arrs)
    return arrs, specs, ta


def _inproj_kernel(*refs, tc, n_x, has_small, epilogue, w_rows_out, block_min):
    x_refs, (ml_ref, mc_ref, g_ref, w_ref), rest = refs[:n_x], refs[n_x:n_x + 4], list(refs[n_x + 4:])
    ws_ref = rest.pop(0) if has_small else None
    par_refs = [rest.pop(0) for _ in range(3)] if epilogue else None
    o_ref = rest.pop(0)
    os_ref = rest.pop(0) if has_small else None
    mn_ref = rest.pop(0) if block_min else None
    h_ref = rest.pop(0)

    ta = h_ref.shape[0]
    chunk = PROJ_ROWS if ta % PROJ_ROWS == 0 else ta
    j = pl.program_id(1)
    w_dims = (((1,), (1 if w_rows_out else 0,)), ((), ()))

    def activation(acc):
        if not epilogue:
            return acc
        bias_ref, c0_ref, c1_ref = par_refs
        t = jnp.tanh(acc + bias_ref[...])
        out = acc
        for lo, hi, kind in epilogue:
            if kind == "id":
                continue
            val = acc + acc * t if kind == "silu" else c0_ref[...] + c1_ref[...] * t
            out = jnp.where((j >= lo) & (j < hi), val, out)
        return out

    def project(first):
        step = chunk if (first or epilogue) else ta
        mins = []
        for r0 in range(0, ta, step):
            if first:
                _modulated(x_refs, ml_ref, mc_ref, g_ref, h_ref, tc, r0, step)
            acc = lax.dot_general(h_ref[r0:r0 + step, :], w_ref[...], w_dims, preferred_element_type=F32)
            out = activation(acc)
            o_ref[0, r0:r0 + step, :] = out.astype(o_ref.dtype)
            if mn_ref is not None:
                for b0 in range(0, step, ROW_BLOCK):
                    m = jnp.min(out[b0:b0 + ROW_BLOCK], axis=0, keepdims=True)
                    mins.append(jnp.broadcast_to(jnp.min(m, axis=1, keepdims=True), (1, LANES)))
        if mn_ref is not None:
            pad = jnp.zeros((mn_ref.shape[2] - len(mins), LANES), F32)
            mn_ref[0, 0] = jnp.concatenate(mins + [pad], axis=0)

    @pl.when(j == 0)
    def _():
        project(True)
        if has_small:
            os_ref[0] = lax.dot_general(ws_ref[...], h_ref[...], (((1,), (1,)), ((), ())),
                                        preferred_element_type=F32)

    @pl.when(j != 0)
    def _():
        project(False)


def _inproj(xs, mod_l, norm_g, w, tc, tn=512, w_small=None, out_dtype=F32, epilogue=None, pars=None,
            w_rows_out=False, block_min=False):
    x_arrs, x_specs, ta = _split_stream(xs)
    b, d = x_arrs[0].shape[0], x_arrs[0].shape[2]
    n = w.shape[0] if w_rows_out else w.shape[1]
    assert n % tn == 0 and tc % ROW_BLOCK == 0 and (ta - tc) % ROW_BLOCK == 0
    mod_lat = mod_l[:b]
    mod_ctx = mod_l[b:b + 1]
    has_small = w_small is not None
    in_specs = x_specs + [pl.BlockSpec((1, 3, d), lambda i, j: (i, 0, 0)),
                          pl.BlockSpec((1, 3, d), lambda i, j: (0, 0, 0)),
                          pl.BlockSpec((1, d), lambda i, j: (0, 0)),
                          pl.BlockSpec((tn, d), lambda i, j: (j, 0)) if w_rows_out
                          else pl.BlockSpec((d, tn), lambda i, j: (0, j))]
    args = x_arrs + [mod_lat, mod_ctx, norm_g.reshape(1, d), w.astype(BF16)]
    out_shape = [jax.ShapeDtypeStruct((b, ta, n), out_dtype)]
    out_specs = [pl.BlockSpec((1, ta, tn), lambda i, j: (i, 0, j))]
    if has_small:
        ns = w_small.shape[0]
        in_specs.append(pl.BlockSpec((ns, d), lambda i, j: (0, 0)))
        args.append(w_small.astype(BF16))
    if epilogue:
        assert all(lo % tn == 0 and hi % tn == 0 for lo, hi, _ in epilogue)
        epilogue = tuple((lo // tn, hi // tn, kind) for lo, hi, kind in epilogue)
        zeros = jnp.zeros((n,), F32)
        for par in (pars if pars is not None else (zeros,) * 3):
            in_specs.append(pl.BlockSpec((1, tn), lambda i, j: (0, j)))
            args.append(par.reshape(1, n))
    if has_small:
        out_shape.append(jax.ShapeDtypeStruct((b, ns, ta), F32))
        out_specs.append(pl.BlockSpec((1, ns, ta), lambda i, j: (i, 0, 0)))
    if block_min:
        nblk = -(-(ta // ROW_BLOCK) // SUBLANES) * SUBLANES
        out_shape.append(jax.ShapeDtypeStruct((b, n // tn, nblk, LANES), F32))
        out_specs.append(pl.BlockSpec((1, 1, nblk, LANES), lambda i, j: (i, j, 0, 0)))
    res = pl.pallas_call(
        functools.partial(_inproj_kernel, tc=tc, n_x=len(x_arrs), has_small=has_small, epilogue=epilogue,
                          w_rows_out=w_rows_out, block_min=block_min),
        out_shape=out_shape,
        grid=(b, n // tn),
        in_specs=in_specs,
        out_specs=out_specs,
        scratch_shapes=[pltpu.VMEM((ta, d), BF16)],
        compiler_params=_cparams("parallel", "arbitrary"),
        name="inproj",
    )(*args)
    return res if (has_small or block_min) else res[0]


def _head_rms(h, gain, n_heads):
    dh = h.shape[1] // n_heads
    parts = []
    for i in range(n_heads):
        hh = h[:, i * dh:(i + 1) * dh]
        ms = jnp.mean(hh * hh, axis=-1, keepdims=True)
        parts.append(hh * lax.rsqrt(ms + EPS))
    return jnp.concatenate(parts, axis=1) * gain


def _outproj_kernel(*refs, mode, n_feat, n_x, tc, tm, row0):
    feats, rest = refs[:n_feat], list(refs[n_feat:])
    hg_ref = rest.pop(0) if mode != "plain" else None
    w_ref = rest.pop(0)
    x_refs = [rest.pop(0) for _ in range(n_x)]
    ml_ref, mc_ref, o_ref = rest
    if mode == "plain":
        u = feats[0][0]
    elif mode == "mlstm":
        h0_ref, h1_ref, og_ref, z_ref = feats
        hn = _head_rms(h0_ref[0].astype(F32) + h1_ref[0].astype(F32), hg_ref[...], ML_HEADS)
        u = (hn * (1.0 + jnp.tanh(og_ref[0].astype(F32))) * _silu_of_half(z_ref[0].astype(F32))).astype(BF16)
    else:
        h0_ref, h1_ref, z_ref = feats
        hn = _head_rms(h0_ref[0].astype(F32) + h1_ref[0].astype(F32), hg_ref[...], HG_HEADS)
        u = (hn * z_ref[0].astype(F32)).astype(BF16)
    y = jnp.dot(u, w_ref[...], preferred_element_type=F32)
    first = row0 + pl.program_id(1) * tm
    row = first + lax.broadcasted_iota(jnp.int32, (tm, 1), 0)
    gate = jnp.where(row < tc, mc_ref[0, 2:3, :], ml_ref[0, 2:3, :])
    x = x_refs[0][0] if n_x == 1 else jnp.where(first < tc, x_refs[0][0], x_refs[1][0])
    o_ref[0] = x + gate * y


def _outproj(mode, feats, w_out, xs, mod_l, tc, head_g=None, lat_only=False):
    x_arrs = list(xs) if isinstance(xs, tuple) else [xs]
    b, d = x_arrs[0].shape[0], x_arrs[0].shape[2]
    ta = sum(a.shape[1] for a in x_arrs)
    kdim = w_out.shape[0]
    if lat_only:
        tm = OUT_ROWS_LAT if (ta - tc) % OUT_ROWS_LAT == 0 else OUT_ROWS_SPLIT
    else:
        tm = OUT_ROWS if (len(x_arrs) == 1 and ta % OUT_ROWS == 0) else OUT_ROWS_SPLIT
        assert ta % tm == 0 and (tc % tm == 0 or len(x_arrs) == 1)
    row0 = tc if lat_only else 0
    nct = tc // tm
    row_spec = lambda width, col: pl.BlockSpec((pl.Element(1), pl.Element(tm), pl.Element(width)),
                                               lambda i, r, col=col: (i, pl.multiple_of(row0 + r * tm, ROW_BLOCK),
                                                                      col * width))
    in_specs, args = [], []
    for arr, width, col in feats:
        in_specs.append(row_spec(width, col))
        args.append(arr)
    if head_g is not None:
        in_specs.append(pl.BlockSpec((1, kdim), lambda i, r: (0, 0)))
        args.append(head_g.reshape(1, kdim))
    in_specs.append(pl.BlockSpec((kdim, d), lambda i, r: (0, 0)))
    args.append(w_out.astype(BF16))
    if len(x_arrs) == 1:
        in_specs.append(row_spec(d, 0))
    elif lat_only:
        x_arrs = x_arrs[1:]
        in_specs.append(pl.BlockSpec((1, tm, d), lambda i, r: (i, r, 0)))
    else:
        in_specs += [pl.BlockSpec((1, tm, d), lambda i, r: (i, jnp.minimum(r, nct - 1), 0)),
                     pl.BlockSpec((1, tm, d), lambda i, r: (i, jnp.maximum(r - nct, 0), 0))]
    args += x_arrs
    in_specs += [pl.BlockSpec((1, 3, d), lambda i, r: (i, 0, 0)),
                 pl.BlockSpec((1, 3, d), lambda i, r: (0, 0, 0))]
    args += [mod_l[:b], mod_l[b:b + 1]]
    rows_out = ta - row0
    return pl.pallas_call(
        functools.partial(_outproj_kernel, mode=mode, n_feat=len(feats), n_x=len(x_arrs), tc=tc, tm=tm,
                          row0=row0),
        out_shape=jax.ShapeDtypeStruct((b, rows_out, d), F32),
        grid=(b, rows_out // tm),
        in_specs=in_specs,
        out_specs=pl.BlockSpec((1, tm, d), lambda i, r: (i, r, 0)),
        compiler_params=_cparams("parallel", "parallel"),
        name="outproj_" + mode,
    )(*args)


def _conv_kernel(x_ref, ml_ref, mc_ref, g_ref, wx_ref, wb_ref, wc_ref, wz_ref, cw_ref, cb_ref, o_ref, h_ref, *, tc):
    ta = h_ref.shape[0]
    chunk = PROJ_ROWS if ta % PROJ_ROWS == 0 else ta

    def body(first_tile):
        wx, wc, wb, wz = (w_ref[...].astype(BF16) for w_ref in (wx_ref, wc_ref, wb_ref, wz_ref))
        step = chunk if first_tile else ta
        xin, cg = [], []
        for r0 in range(0, ta, step):
            if first_tile:
                _modulated((x_ref,), ml_ref, mc_ref, g_ref, h_ref, tc, r0, step)
            h = h_ref[r0:r0 + step, :]
            xin.append(jnp.dot(h, wx, preferred_element_type=F32))
            cg.append(jnp.dot(h, wc, preferred_element_type=F32))
        bg = jnp.dot(h_ref[...], wb, preferred_element_type=F32)
        z = jnp.dot(h_ref[...], wz, preferred_element_type=F32)
        u = jnp.concatenate(cg, axis=0) * jnp.concatenate(xin, axis=0)
        row = lax.broadcasted_iota(jnp.int32, (ta, 1), 0)
        first = (row == 0) | (row == tc)
        last = (row == tc - 1) | (row == ta - 1)
        u_prev = jnp.where(first, 0.0, pltpu.roll(u, 1, axis=0))
        u_next = jnp.where(last, 0.0, pltpu.roll(u, ta - 1, axis=0))
        cw = cw_ref[...]
        y = u_prev * cw[0:1, :] + u * cw[1:2, :] + u_next * cw[2:3, :] + cb_ref[...]
        o_ref[0] = (bg * y * _silu(z)).astype(o_ref.dtype)

    @pl.when(pl.program_id(1) == 0)
    def _():
        body(True)

    @pl.when(pl.program_id(1) != 0)
    def _():
        body(False)


def _conv_mixer(xs, mod_l, norm_g, w_in, conv_w, conv_b, tc, tw=256):
    b, ta, d = xs.shape
    e = conv_w.shape[1]
    nt = e // tw
    w_spec = lambda part: pl.BlockSpec((d, tw), lambda i, j, part=part: (0, part * nt + j))
    return pl.pallas_call(
        functools.partial(_conv_kernel, tc=tc),
        out_shape=jax.ShapeDtypeStruct((b, ta, e), BF16),
        grid=(b, nt),
        in_specs=[pl.BlockSpec((1, ta, d), lambda i, j: (i, 0, 0)),
                  pl.BlockSpec((1, 3, d), lambda i, j: (i, 0, 0)),
                  pl.BlockSpec((1, 3, d), lambda i, j: (0, 0, 0)),
                  pl.BlockSpec((1, d), lambda i, j: (0, 0)),
                  w_spec(0), w_spec(1), w_spec(2), w_spec(3),
                  pl.BlockSpec((SC_KSIZE, tw), lambda i, j: (0, j)),
                  pl.BlockSpec((1, tw), lambda i, j: (0, j))],
        out_specs=pl.BlockSpec((1, ta, tw), lambda i, j: (i, 0, j)),
        scratch_shapes=[pltpu.VMEM((ta, d), BF16)],
        compiler_params=_cparams("parallel", "arbitrary"),
        name="conv_mixer",
    )(xs, mod_l[:b], mod_l[b:b + 1], norm_g.reshape(1, d), w_in, w_in, w_in, w_in, conv_w, conv_b.reshape(1, e))


AT_HALF = AT_HEAD_DIM // 2
AT_TILE_PERM = np.concatenate([np.arange(0, AT_HALF), np.arange(2 * AT_HALF, 3 * AT_HALF),
                               np.arange(AT_HALF, 2 * AT_HALF), np.arange(3 * AT_HALF, 4 * AT_HALF)])
LOG2E = float(np.log2(np.e))


def _rope_tables(tc, t):
    rows = t // GRID_W
    row = np.repeat(np.arange(rows), GRID_W).astype(np.float64)
    col = np.tile(np.arange(GRID_W), rows).astype(np.float64)
    n_freq = AT_HEAD_DIM // 4
    freqs = np.power(ROPE_BASE, -np.arange(n_freq, dtype=np.float64) / n_freq)
    ang = np.concatenate([row[:, None] * freqs, col[:, None] * freqs], axis=-1)
    cos, sin = np.cos(ang), np.sin(ang)
    cos_t = np.concatenate([np.ones((tc, LANES)), np.tile(cos, (1, 4))], axis=0)
    sin_t = np.concatenate([np.zeros((tc, LANES)), np.concatenate([-sin, -sin, sin, sin], axis=-1)], axis=0)
    return jnp.asarray(cos_t, F32), jnp.asarray(sin_t, F32)


def _norm_rope_tiles(xs, gains, cos, sin, same_head):
    sums = []
    for x in xs:
        x2 = x * x
        hi = x2.astype(BF16)
        lo = (x2 - hi.astype(F32)).astype(BF16)
        sums.append((jnp.dot(hi, same_head, preferred_element_type=F32),
                     jnp.dot(lo, same_head, preferred_element_type=F32)))
    out = []
    for x, gain, (s_hi, s_lo) in zip(xs, gains, sums):
        xn = x * lax.rsqrt((s_hi + s_lo) * (1.0 / AT_HEAD_DIM) + EPS) * gain
        out.append(xn * cos + pltpu.roll(xn, 2 * AT_HALF, axis=1) * sin)
    return out


def _attn_kernel(sink_ref, q_ref, k_ref, v_ref, z_ref, qg_ref, kg_ref, cos_ref, sin_ref, sh_ref, wo_ref, x_ref,
                 ml_ref, mc_ref, o_ref, qs_ref, ka_ref, kb_ref, va_ref, vb_ref, *, tc, need_ctx):
    ta = q_ref.shape[1]
    t = ta - tc
    nb = t // AT_BLOCK
    blk = AT_BLOCK
    gw = x_ref.shape[2]
    g = pl.program_id(1)
    odd = (g % 2) == 1
    scale = (AT_HEAD_DIM ** -0.5) * LOG2E

    @pl.when(g == 0)
    def _():
        o_ref[...] = jnp.zeros(o_ref.shape, F32)

    def prep(i, carry):
        r0 = pl.multiple_of(i * ROW_BLOCK, ROW_BLOCK)
        res_cols = pl.ds(pl.multiple_of(g * gw, gw), gw)
        o_ref[0, pl.ds(r0, ROW_BLOCK), res_cols] = (o_ref[0, pl.ds(r0, ROW_BLOCK), res_cols]
                                                    + x_ref[0, pl.ds(r0, ROW_BLOCK), :])
        cos = cos_ref[pl.ds(r0, ROW_BLOCK), :]
        sin = sin_ref[pl.ds(r0, ROW_BLOCK), :]
        same_head = sh_ref[...]
        tiles = [q_ref[0, pl.ds(r0, ROW_BLOCK), c * LANES:(c + 1) * LANES].astype(F32) for c in range(2)]
        tiles.append(k_ref[0, pl.ds(r0, ROW_BLOCK), :].astype(F32))
        q0, q1, kn = _norm_rope_tiles(tiles, (qg_ref[...], qg_ref[...], kg_ref[...]), cos, sin, same_head)
        for c, qc in enumerate((q0, q1)):
            qs_ref[pl.ds(r0, ROW_BLOCK), c * LANES:(c + 1) * LANES] = (qc * scale).astype(BF16)
        vv = v_ref[0, pl.ds(r0, ROW_BLOCK), :].astype(F32)
        lane = lax.broadcasted_iota(jnp.int32, kn.shape, 1)
        k_own = jnp.where(((lane // AT_HALF) % 2) == (g % 2), kn, 0.0)
        k_oth = pltpu.roll(k_own, jnp.where(odd, 3 * AT_HALF, AT_HALF), axis=1)
        ka_ref[pl.ds(r0, ROW_BLOCK), :] = jnp.where(odd, k_oth, k_own).astype(BF16)
        kb_ref[pl.ds(r0, ROW_BLOCK), :] = jnp.where(odd, k_own, k_oth).astype(BF16)
        v_own = jnp.where((lane // AT_HEAD_DIM) == (g % 2), vv, 0.0)
        v_oth = pltpu.roll(v_own, AT_HEAD_DIM, axis=1)
        va = jnp.where(odd, v_oth, v_own)
        vb = jnp.where(odd, v_own, v_oth)
        va_ref[pl.ds(r0, ROW_BLOCK), :] = jnp.where(lane == AT_HEAD_DIM, 1.0, va).astype(BF16)
        vb_ref[pl.ds(r0, ROW_BLOCK), :] = jnp.where(lane == 0, 1.0, vb).astype(BF16)
        return carry

    lax.fori_loop(0, ta // ROW_BLOCK, prep, 0)
    zeros = jnp.zeros((blk, LANES), BF16)
    for ref in (ka_ref, kb_ref, va_ref, vb_ref):
        ref[ta:ta + blk, :] = zeros

    half = lax.broadcasted_iota(jnp.int32, (2 * blk, 1), 0) < blk
    sink_a = jnp.where(half, sink_ref[g, 0], sink_ref[g, 2]) * LOG2E
    sink_b = jnp.where(half, sink_ref[g, 1], sink_ref[g, 3]) * LOG2E

    qi = lax.broadcasted_iota(jnp.int32, (2 * blk, 3 * blk), 0) % blk
    kj = lax.broadcasted_iota(jnp.int32, (2 * blk, 3 * blk), 1)
    band = (kj - qi >= 0) & (kj - qi <= 2 * blk)
    out_lo = lax.broadcasted_iota(jnp.int32, (2 * blk, LANES), 1) < AT_HEAD_DIM

    nt = (((1,), (1,)), ((), ()))

    sides = ((ka_ref, va_ref, sink_a, AT_HEAD_DIM), (kb_ref, vb_ref, sink_b, 0))

    def attend(blocks, r_first, gate):
        qts = [jnp.concatenate([qs_ref[pl.ds(r0, blk), 0:LANES], qs_ref[pl.ds(r0, blk), LANES:2 * LANES]], axis=0)
               for r0, _, _ in blocks]
        scores = []
        for qt, (_, k0, mask) in zip(qts, blocks):
            for k_ref_, _, _, _ in sides:
                s_ctx = lax.dot_general(qt, k_ref_[0:tc, :], nt, preferred_element_type=F32)
                s_loc = None if mask is None else lax.dot_general(qt, k_ref_[pl.ds(k0, 3 * blk), :], nt,
                                                                  preferred_element_type=F32)
                scores.append((s_ctx, s_loc))
        probs = []
        for idx, (s_ctx, s_loc) in enumerate(scores):
            mask = blocks[idx // 2][2]
            sink = sides[idx % 2][2]
            m = jnp.maximum(sink, jnp.max(s_ctx, axis=-1, keepdims=True))
            if s_loc is not None:
                s_loc = jnp.where(mask, s_loc, -jnp.inf)
                m = jnp.maximum(m, jnp.max(s_loc, axis=-1, keepdims=True))
            probs.append((jnp.exp2(s_ctx - m).astype(BF16),
                          None if s_loc is None else jnp.exp2(s_loc - m).astype(BF16), jnp.exp2(sink - m)))
        accs = []
        for idx, (p_ctx, p_loc, _) in enumerate(probs):
            k0 = blocks[idx // 2][1]
            v_ref_ = sides[idx % 2][1]
            acc = jnp.dot(p_ctx, v_ref_[0:tc, :], preferred_element_type=F32)
            if p_loc is not None:
                acc = acc + jnp.dot(p_loc, v_ref_[pl.ds(k0, 3 * blk), :], preferred_element_type=F32)
            accs.append(acc)
        us = []
        for bi, (r0, _, _) in enumerate(blocks):
            halves = []
            for si in range(2):
                acc, ones_lane = accs[2 * bi + si], sides[si][3]
                halves.append(acc / (probs[2 * bi + si][2] + acc[:, ones_lane:ones_lane + 1]))
            o = jnp.where(out_lo, halves[0], halves[1])
            parts = []
            for c in range(2):
                zc = z_ref[0, pl.ds(r0, blk), c * LANES:(c + 1) * LANES].astype(F32)
                parts.append((o[c * blk:(c + 1) * blk] * _silu_of_half(zc)).astype(BF16))
            us.append(jnp.concatenate(parts, axis=1))
        u = jnp.concatenate(us, axis=0)
        rows = pl.ds(r_first, u.shape[0])
        o_ref[0, rows, :] = o_ref[0, rows, :] + gate * jnp.dot(u, wo_ref[...], preferred_element_type=F32)

    def lat_blocks(i, carry):
        blocks = []
        for j in range(AT_PROJ_BLOCKS):
            n = i * AT_PROJ_BLOCKS + j
            kpos = (n - 1) * blk + kj
            blocks.append((pl.multiple_of(tc + n * blk, blk), pl.multiple_of(tc + (n - 1) * blk, blk),
                           band & (kpos >= 0) & (kpos < t)))
        attend(blocks, pl.multiple_of(tc + i * AT_PROJ_BLOCKS * blk, AT_PROJ_BLOCKS * blk), ml_ref[0, 2:3, :])
        return carry

    lax.fori_loop(0, nb // AT_PROJ_BLOCKS, lat_blocks, 0)
    if need_ctx:
        attend([(n * blk, None, None) for n in range(tc // blk)], 0, mc_ref[0, 2:3, :])


def _attn_mixer(p, at_q_g, at_k_g, at_sink, w_out, xs, mod_l, tc, need_ctx):
    b, ta, _ = p.shape
    t = ta - tc
    d = xs.shape[2]
    assert tc >= AT_BLOCK and tc % AT_BLOCK == 0 and t % (AT_PROJ_BLOCKS * AT_BLOCK) == 0 and ta % ROW_BLOCK == 0
    assert d == AT_Q
    cos_t, sin_t = _rope_tables(tc, t)
    lane_head = (np.arange(LANES) // AT_HALF) % 2
    same_head = jnp.asarray(lane_head[:, None] == lane_head[None, :], BF16)
    tile_gain = lambda gain: jnp.tile(gain, 2)[AT_TILE_PERM].reshape(1, LANES)
    gw = AT_GROUP * AT_HEAD_DIM
    kcol = AT_Q // LANES
    vcol = (AT_Q + AT_KV) // LANES
    zcol = (AT_Q + 2 * AT_KV) // gw
    return pl.pallas_call(
        functools.partial(_attn_kernel, tc=tc, need_ctx=need_ctx),
        out_shape=jax.ShapeDtypeStruct((b, ta, d), F32),
        grid=(b, AT_KV_HEADS),
        in_specs=[pl.BlockSpec(memory_space=pltpu.SMEM),
                  pl.BlockSpec((1, ta, gw), lambda i, g: (i, 0, g)),
                  pl.BlockSpec((1, ta, LANES), lambda i, g: (i, 0, kcol + g // 2)),
                  pl.BlockSpec((1, ta, LANES), lambda i, g: (i, 0, vcol + g // 2)),
                  pl.BlockSpec((1, ta, gw), lambda i, g: (i, 0, zcol + g)),
                  pl.BlockSpec((1, LANES), lambda i, g: (0, 0)),
                  pl.BlockSpec((1, LANES), lambda i, g: (0, 0)),
                  pl.BlockSpec((ta, LANES), lambda i, g: (0, 0)),
                  pl.BlockSpec((ta, LANES), lambda i, g: (0, 0)),
                  pl.BlockSpec((LANES, LANES), lambda i, g: (0, 0)),
                  pl.BlockSpec((gw, d), lambda i, g: (g, 0)),
                  pl.BlockSpec((1, ta, gw), lambda i, g: (i, 0, g)),
                  pl.BlockSpec((1, 3, d), lambda i, g: (i, 0, 0)),
                  pl.BlockSpec((1, 3, d), lambda i, g: (0, 0, 0))],
        out_specs=pl.BlockSpec((1, ta, d), lambda i, g: (i, 0, 0)),
        scratch_shapes=[pltpu.VMEM((ta, gw), BF16)] + [pltpu.VMEM((ta + AT_BLOCK, LANES), BF16)] * 4,
        compiler_params=_cparams("parallel", "arbitrary"),
        name="attn_mixer",
    )(at_sink.reshape(AT_KV_HEADS, AT_GROUP), p, p, p, p, tile_gain(at_q_g), tile_gain(at_k_g), cos_t, sin_t,
      same_head, w_out.astype(BF16), xs, mod_l[:b], mod_l[b:b + 1])


def _attn_weight_kernel(w_ref, p_ref, o_ref, *, n_qk, n_v):
    tiles = w_ref.shape[1] // LANES
    for lt in range(tiles):
        tile = pl.program_id(0) * tiles + lt
        kind = (tile >= n_qk).astype(jnp.int32) + (tile >= n_qk + n_v).astype(jnp.int32)
        cols = slice(lt * LANES, (lt + 1) * LANES)
        o_ref[:, cols] = jnp.dot(w_ref[:, cols].astype(BF16), p_ref[kind], preferred_element_type=F32).astype(BF16)


def _attn_weight(w_in):
    d, n = w_in.shape
    n_qk, n_v = (AT_Q + AT_KV) // LANES, AT_KV // LANES
    perm = np.zeros((LANES, LANES), np.float32)
    perm[AT_TILE_PERM, np.arange(LANES)] = 1.0
    eye = np.eye(LANES, dtype=np.float32)
    mats = jnp.asarray(np.stack([perm, eye, 0.5 * eye]), BF16)
    tw = 4 * LANES
    assert n % tw == 0
    return pl.pallas_call(
        functools.partial(_attn_weight_kernel, n_qk=n_qk, n_v=n_v),
        out_shape=jax.ShapeDtypeStruct((d, n), BF16),
        grid=(n // tw,),
        in_specs=[pl.BlockSpec((d, tw), lambda j: (0, j)),
                  pl.BlockSpec((3, LANES, LANES), lambda j: (0, 0, 0))],
        out_specs=pl.BlockSpec((d, tw), lambda j: (0, j)),
        compiler_params=_cparams("parallel"),
        name="attn_weight",
    )(w_in, mats)


def _chunk_order(nc, ncc, d, s):
    bwd = jnp.where(s < ncc, ncc - 1 - s, nc - 1 - (s - ncc))
    return jnp.where(d == 0, s, bwd)


ML_GATE_PERM = np.concatenate([np.arange(0, 4), np.arange(8, 12), np.arange(4, 8), np.arange(12, 16)])
ML_NQ = 6


def _mlstm_gate_kernel(g_ref, b_ref, a_ref, c_ref, *, tc, lc):
    h = ML_HEADS
    ta = g_ref.shape[2]
    nc, ncc = ta // lc, tc // lc
    x = g_ref[0] + b_ref[...]
    li = x[0:2 * h]
    lfp = x[2 * h:4 * h]
    lf = jnp.minimum(lfp, 0.0) - jnp.log1p(jnp.exp(-jnp.abs(lfp)))
    fwd = lax.broadcasted_iota(jnp.int32, (2 * h, ta), 0) < h
    fwd1 = lax.broadcasted_iota(jnp.int32, (2 * h, 1), 0) < h
    pos = lax.broadcasted_iota(jnp.int32, (2 * h, ta), 1) % lc

    def seg_scan(y, op, fill):
        yf, yb = y, y
        s = 1
        while s < lc:
            yf = op(yf, jnp.where(pos >= s, pltpu.roll(yf, s, axis=1), fill))
            yb = op(yb, jnp.where(pos < lc - s, pltpu.roll(yb, ta - s, axis=1), fill))
            s *= 2
        return jnp.where(fwd, yf, yb)

    bsum = seg_scan(lf, jnp.add, 0.0)
    a = li - bsum
    cmax = seg_scan(a, jnp.maximum, -jnp.inf)

    def end_col(y, c):
        return jnp.where(fwd1, y[:, (c + 1) * lc - 1:(c + 1) * lc], y[:, c * lc:c * lc + 1])

    tot = [end_col(bsum, c) for c in range(nc)]
    amax = [end_col(cmax, c) for c in range(nc)]

    def chain(order):
        m = jnp.zeros((2 * h, 1), F32)
        m_in = [None] * nc
        for c in order:
            m_in[c] = m
            m = tot[c] + jnp.maximum(m, amax[c])
        return m_in

    m_f = chain(list(range(nc)))
    m_b = chain(list(range(ncc - 1, -1, -1)) + list(range(nc - 1, ncc - 1, -1)))
    for c in range(nc):
        m_in = jnp.where(fwd1, m_f[c], m_b[c])
        sl = slice(c * lc, (c + 1) * lc)
        g_run = jnp.maximum(m_in, cmax[:, sl])
        g_end = jnp.maximum(m_in, amax[c])
        nr = 2 * h
        quantities = [a[:, sl], g_run, jnp.exp(m_in - g_run), jnp.exp(-(bsum[:, sl] + g_run)),
                      jnp.exp(a[:, sl] - g_end), jnp.broadcast_to(jnp.exp(m_in - g_end), (nr, lc))]
        a_ref[0, c] = quantities[0]
        pad = jnp.zeros((LANES - ML_NQ * nr, lc), F32)
        c_ref[0, c] = jnp.concatenate(quantities + [pad], axis=0).T


def _mlstm_scan_kernel(q_ref, k_ref, v_ref, a_ref, c_ref, *rest, rev):
    lc = q_ref.shape[1]
    nr = 2 * ML_HEADS
    finish, (o_ref, cs_ref) = (rest[:4] if len(rest) == 6 else None), rest[-2:]

    @pl.when(pl.program_id(1) == 0)
    def _():
        cs_ref[...] = jnp.zeros(cs_ref.shape, F32)

    ti = lax.broadcasted_iota(jnp.int32, (lc, lc), 0)
    si = lax.broadcasted_iota(jnp.int32, (lc, lc), 1)
    mask = (si >= ti) if rev else (si <= ti)
    ones = jnp.ones((lc, LANES), BF16)
    nt = (((1,), (1,)), ((), ()))
    tn = (((0,), (0,)), ((), ()))
    heads = range(ML_HEADS)
    qb = [q_ref[0, :, h * ML_DK:(h + 1) * ML_DK] for h in heads]
    kb = [k_ref[0, :, h * ML_DK:(h + 1) * ML_DK] for h in heads]
    v_aug = [jnp.concatenate([v_ref[0, :, h * ML_DV:(h + 1) * ML_DV], ones], axis=1) for h in heads]
    cols = []
    for h in heads:
        r = (ML_HEADS if rev else 0) + h
        cols.append([a_ref[0, 0, r:r + 1, :]] + [c_ref[0, 0, :, qi * nr + r:qi * nr + r + 1] for qi in range(1, 6)])
    qk = [lax.dot_general(qb[h], kb[h], nt, preferred_element_type=F32) for h in heads]
    c_old = [cs_ref[h] for h in heads]
    q_c = [jnp.dot(qb[h], c_old[h].astype(BF16), preferred_element_type=F32) for h in heads]
    kw = [(kb[h].astype(F32) * cols[h][4]).astype(BF16) for h in heads]
    upd = [lax.dot_general(kw[h], v_aug[h], tn, preferred_element_type=F32) for h in heads]
    s = []
    for h in heads:
        a_row, g_run = cols[h][0], cols[h][1]
        s.append((qk[h] * jnp.where(mask, jnp.exp(a_row - g_run), 0.0)).astype(BF16))
    s_v = [jnp.dot(s[h], v_aug[h], preferred_element_type=F32) for h in heads]
    for h in heads:
        inter, eclamp, decay = cols[h][2], cols[h][3], cols[h][5]
        tot = inter * q_c[h] + s_v[h]
        inv = 1.0 / jnp.maximum(jnp.abs(tot[:, ML_DV:]), eclamp)
        hd = tot[:, :ML_DV] * jnp.concatenate([inv] * (ML_DV // LANES), axis=1)
        hc = slice(h * ML_DV, (h + 1) * ML_DV)
        if finish is not None:
            hf_ref, og_ref, z_ref, hg_ref = finish
            hs = hd + hf_ref[0, :, hc].astype(F32)
            hn = hs * lax.rsqrt(jnp.mean(hs * hs, axis=-1, keepdims=True) + EPS) * hg_ref[:, hc]
            hd = hn * (1.0 + jnp.tanh(og_ref[0, :, hc].astype(F32))) * _silu_of_half(z_ref[0, :, hc].astype(F32))
        o_ref[0, :, hc] = hd.astype(o_ref.dtype)
        cs_ref[h] = decay[0:1, :] * c_old[h] + upd[h]


def _mlstm_scan(p, gates, gate_b, head_g, tc):
    b, ta, _ = p.shape
    lc = ML_CHUNK
    assert tc % lc == 0 and ta % lc == 0
    nc, ncc = ta // lc, tc // lc
    ng = 4 * ML_HEADS
    nr = 2 * ML_HEADS
    bias = gate_b.reshape(ng)[ML_GATE_PERM].reshape(ng, 1)
    a_rows, cols = pl.pallas_call(
        functools.partial(_mlstm_gate_kernel, tc=tc, lc=lc),
        out_shape=[jax.ShapeDtypeStruct((b, nc, nr, lc), F32), jax.ShapeDtypeStruct((b, nc, lc, LANES), F32)],
        grid=(b,),
        in_specs=[pl.BlockSpec((1, ng, ta), lambda i: (i, 0, 0)),
                  pl.BlockSpec((ng, 1), lambda i: (0, 0))],
        out_specs=[pl.BlockSpec((1, nc, nr, lc), lambda i: (i, 0, 0, 0)),
                   pl.BlockSpec((1, nc, lc, LANES), lambda i: (i, 0, 0, 0))],
        compiler_params=_cparams("parallel"),
        name="mlstm_gates",
    )(gates, bias)
    out = None
    for d in range(2):
        chunk = functools.partial(_chunk_order, nc, ncc, d)
        wide = lambda col, chunk=chunk: pl.BlockSpec((1, lc, ML_INNER), lambda i, s: (i, chunk(s), col))
        in_specs = [pl.BlockSpec((1, lc, ML_QK), lambda i, s, chunk=chunk: (i, chunk(s), 0)),
                    pl.BlockSpec((1, lc, ML_QK), lambda i, s, chunk=chunk: (i, chunk(s), 1)),
                    wide(1),
                    pl.BlockSpec((1, 1, nr, lc), lambda i, s, chunk=chunk: (i, chunk(s), 0, 0)),
                    pl.BlockSpec((1, 1, lc, LANES), lambda i, s, chunk=chunk: (i, chunk(s), 0, 0))]
        args = [p, p, p, a_rows, cols]
        if d == 1:
            in_specs += [wide(0), wide(2), wide(3), pl.BlockSpec((1, ML_INNER), lambda i, s: (0, 0))]
            args += [out, p, p, 0.5 * head_g.reshape(1, ML_INNER)]
        out = pl.pallas_call(
            functools.partial(_mlstm_scan_kernel, rev=bool(d)),
            out_shape=jax.ShapeDtypeStruct((b, ta, ML_INNER), BF16),
            grid=(b, nc),
            in_specs=in_specs,
            out_specs=wide(0),
            scratch_shapes=[pltpu.VMEM((ML_HEADS, ML_DK, ML_DV + LANES), F32)],
            compiler_params=_cparams("parallel", "arbitrary"),
            name="mlstm_scan_bwd" if d else "mlstm_scan_fwd",
        )(*args)
    return out


def _hgrn_lb_kernel(p_ref, o_ref, *, layer):
    for d in range(p_ref.shape[0]):
        x = p_ref[d]
        e = jnp.exp(x - jnp.max(x, axis=0, keepdims=True))
        p = e / jnp.sum(e, axis=0, keepdims=True)
        acc = jnp.zeros((1, x.shape[1]), F32)
        for j in range(1, layer + 1):
            acc = acc + p[j:j + 1, :]
        o_ref[d:d + 1, :] = acc


def _cumsum_rows(x, tri2):
    hi = x.astype(BF16)
    lo = (x - hi.astype(F32)).astype(BF16)
    return jnp.dot(tri2, jnp.concatenate([hi, lo], axis=0), preferred_element_type=F32)


def _tri2(n, rev):
    t = np.triu(np.ones((n, n), np.float32)) if rev else np.tril(np.ones((n, n), np.float32))
    return jnp.asarray(np.concatenate([t, t], axis=1), BF16)


def _anchor_rows(a, m, rev):
    n, f = a.shape
    idx = m if rev else m - 1
    if 2 * m >= SUBLANES:
        a3 = a.reshape(n // (2 * m), 2 * m, f)
        return jnp.broadcast_to(a3[:, idx:idx + 1, :], a3.shape).reshape(n, f)
    a3 = a.reshape(n // SUBLANES, SUBLANES, f)
    sub = lax.broadcasted_iota(jnp.int32, a3.shape, 1)
    out = None
    for gi in range(SUBLANES // (2 * m) - 1, -1, -1):
        cand = jnp.broadcast_to(a3[:, gi * 2 * m + idx:gi * 2 * m + idx + 1, :], a3.shape)
        out = cand if out is None else jnp.where(sub < (gi + 1) * 2 * m, cand, out)
    return out.reshape(n, f)


HG_MAX_LOG_SPAN = 64.0


def _hgrn_scan_kernel(fmin_ref, q_ref, f_ref, i_ref, tri_ref, o_ref, s_ref, *, rev, nc, ncc):
    lc = HG_CHUNK
    subs = range(q_ref.shape[1] // lc)
    subs = tuple(reversed(subs)) if rev else tuple(subs)
    step_block = _chunk_order(nc, ncc, 1 if rev else 0, pl.program_id(1))

    @pl.when(pl.program_id(1) == 0)
    def _():
        s_ref[...] = jnp.zeros(s_ref.shape, F32)

    row = lax.broadcasted_iota(jnp.int32, (lc, 1), 0)
    ti = lax.broadcasted_iota(jnp.int32, (lc, lc), 0)
    si = lax.broadcasted_iota(jnp.int32, (lc, lc), 1)
    nt = (((1,), (1,)), ((), ()))
    tn = (((0,), (0,)), ((), ()))
    end = 0 if rev else lc - 1

    def all_heads_single_anchor():
        causal = (si >= ti) if rev else (si <= ti)
        cols = [slice(h * HG_DIM, (h + 1) * HG_DIM) for h in range(HG_HEADS)]
        pre = {}
        for ci in subs:
            rows = slice(ci * lc, (ci + 1) * lc)
            f = f_ref[0, rows, :]
            a = _cumsum_rows(jnp.log(f), tri_ref[...])
            q_dec = (q_ref[0, rows, :].astype(F32) * jnp.exp(a)).astype(BF16)
            k_inv = (1.0 - f) * jnp.exp(-a)
            k_inv_b = k_inv.astype(BF16)
            e_end = jnp.exp(a[end:end + 1, :])
            kd = (k_inv * e_end).astype(BF16)
            attn = [jnp.where(causal, lax.dot_general(q_dec[:, c], k_inv_b[:, c], nt, preferred_element_type=F32),
                              0.0).astype(BF16) for c in cols]
            pre[ci] = (rows, q_dec, e_end, kd, attn)
        state = [s_ref[h] for h in range(HG_HEADS)]
        for ci in subs:
            rows, q_dec, e_end, kd, attn = pre[ci]
            for h, c in enumerate(cols):
                o = jnp.dot(attn[h], i_ref[0, rows, c], preferred_element_type=F32)
                o = o + lax.dot_general(q_dec[:, c], state[h].astype(BF16), nt, preferred_element_type=F32)
                o_ref[0, rows, c] = o.astype(o_ref.dtype)
            state = [state[h] * e_end[:, c] + lax.dot_general(i_ref[0, rows, c], kd[:, c], tn,
                                                              preferred_element_type=F32)
                     for h, c in enumerate(cols)]
        for h in range(HG_HEADS):
            s_ref[h] = state[h]

    def head_per_level(r0, h, carry):
        c0 = pl.multiple_of(h * HG_DIM, HG_DIM)
        q = q_ref[0, r0:r0 + lc, pl.ds(c0, HG_DIM)].astype(F32)
        f = f_ref[0, r0:r0 + lc, pl.ds(c0, HG_DIM)]
        k = 1.0 - f
        iv = i_ref[0, r0:r0 + lc, pl.ds(c0, HG_DIM)].astype(F32)
        a = _cumsum_rows(jnp.log(f), tri_ref[...])
        attn = jnp.zeros((lc, lc), F32)
        m = 1
        while m < lc:
            e = jnp.exp(-jnp.abs(a - _anchor_rows(a, m, rev)))
            upper = (row % (2 * m)) >= m
            is_q = jnp.logical_not(upper) if rev else upper
            qt = jnp.where(is_q, q * e, 0.0).astype(BF16)
            kt = jnp.where(is_q, 0.0, k * e).astype(BF16)
            pair = lax.dot_general(qt, kt, nt, preferred_element_type=F32)
            attn = attn + jnp.where((ti // (2 * m)) == (si // (2 * m)), pair, 0.0)
            m *= 2
        a_end = a[end:end + 1, :]
        kd = (k * jnp.exp(a_end - a)).astype(BF16)
        ib = iv.astype(BF16)
        s_old = s_ref[h]
        o = jnp.dot(attn.astype(BF16), ib, preferred_element_type=F32)
        o = o + lax.dot_general((q * jnp.exp(a)).astype(BF16), s_old.astype(BF16), nt, preferred_element_type=F32)
        o_ref[0, r0:r0 + lc, pl.ds(c0, HG_DIM)] = (o + jnp.sum(q * k, axis=-1, keepdims=True) * iv).astype(o_ref.dtype)
        s_ref[h] = s_old * jnp.exp(a_end) + lax.dot_general(ib, kd, tn, preferred_element_type=F32)
        return carry

    in_range = fmin_ref[pl.program_id(0), step_block] >= float(np.exp(-HG_MAX_LOG_SPAN / lc))

    @pl.when(in_range)
    def _():
        all_heads_single_anchor()

    @pl.when(jnp.logical_not(in_range))
    def _():
        for ci in subs:
            lax.fori_loop(0, HG_HEADS, functools.partial(head_per_level, ci * lc), 0)


def _hgrn_scan(p, pf, f_min, tc, rev):
    b, ta, _ = p.shape
    lc = HG_CHUNK
    rows = HG_STEP_CHUNKS * lc
    hg = HG_HEADS * HG_DIM
    assert tc % rows == 0 and ta % rows == 0 and rows == ROW_BLOCK
    nc, ncc = ta // rows, tc // rows
    d = 1 if rev else 0
    chunk = lambda s: _chunk_order(nc, ncc, d, s)
    return pl.pallas_call(
        functools.partial(_hgrn_scan_kernel, rev=rev, nc=nc, ncc=ncc),
        out_shape=jax.ShapeDtypeStruct((b, ta, hg), BF16),
        grid_spec=pltpu.PrefetchScalarGridSpec(
            num_scalar_prefetch=1,
            grid=(b, nc),
            in_specs=[pl.BlockSpec((1, rows, hg), lambda i, s, fm: (i, chunk(s), 0)),
                      pl.BlockSpec((1, rows, hg), lambda i, s, fm: (i, chunk(s), d)),
                      pl.BlockSpec((1, rows, hg), lambda i, s, fm: (i, chunk(s), 1)),
                      pl.BlockSpec((lc, 2 * lc), lambda i, s, fm: (0, 0))],
            out_specs=pl.BlockSpec((1, rows, hg), lambda i, s, fm: (i, chunk(s), 0)),
            scratch_shapes=[pltpu.VMEM((HG_HEADS, HG_DIM, HG_DIM), F32)]),
        compiler_params=_cparams("parallel", "arbitrary"),
        name="hgrn_scan_bwd" if rev else "hgrn_scan_fwd",
    )(f_min, p, pf, p, _tri2(lc, rev))


def kernel(x, c, ctx, c_ctx, ada_w, ada_b, norm_g, ml_w_in, ml_gate_b, ml_head_g, ml_w_out, at_w_in, at_q_g, at_k_g, at_sink, at_w_out, sc_w_in, sc_conv_w, sc_conv_b, sc_w_out, hg_w_in, hg_f_b, hg_lb, hg_head_g, hg_w_out):
    tc = ctx.shape[1]
    mod = _ada_mod(c, c_ctx, ada_w, ada_b)
    xs = (ctx, x)
    for layer in range(DEPTH):
        kind, j = layer % 4, layer // 4
        need_ctx = layer < DEPTH - 1
        last = dict(lat_only=True) if layer == DEPTH - 1 else {}
        mod_l = mod[layer]
        if kind == 0:
            n_main = 2 * ML_QK + 3 * ML_INNER
            w_t = ml_w_in[j].T
            w_gate = jnp.zeros((LANES, ml_w_in.shape[1]), F32).at[:4 * ML_HEADS].set(w_t[n_main:][ML_GATE_PERM])
            col_scale = jnp.concatenate([jnp.full((ML_QK,), ML_DK ** -0.5, F32), jnp.ones((ML_QK + ML_INNER,), F32),
                                         jnp.full((2 * ML_INNER,), 0.5, F32)])
            p, gates = _inproj(xs, mod_l, norm_g[layer], w_t[:n_main] * col_scale[:, None], tc, tn=1024,
                               w_small=w_gate, out_dtype=BF16, w_rows_out=True)
            u = _mlstm_scan(p, gates, ml_gate_b[j], ml_head_g[j], tc)
            xs = _outproj("plain", [(u, ML_INNER, 0)], ml_w_out[j], xs, mod_l, tc, **last)
        elif kind == 1:
            p = _inproj(xs, mod_l, norm_g[layer], _attn_weight(at_w_in[j]), tc, out_dtype=BF16)
            if isinstance(xs, tuple):
                xs = jnp.concatenate(xs, axis=1)
            xs = _attn_mixer(p, at_q_g[j], at_k_g[j], at_sink[j], at_w_out[j], xs, mod_l, tc, need_ctx)
            if last:
                xs = xs[:, tc:, :]
        elif kind == 2:
            if isinstance(xs, tuple):
                xs = jnp.concatenate(xs, axis=1)
            u = _conv_mixer(xs, mod_l, norm_g[layer], sc_w_in[j], sc_conv_w[j], sc_conv_b[j], tc)
            xs = _outproj("plain", [(u, u.shape[2], 0)], sc_w_out[j], xs, mod_l, tc, **last)
        else:
            hg = HG_HEADS * HG_DIM
            lb = pl.pallas_call(
                functools.partial(_hgrn_lb_kernel, layer=layer),
                out_shape=jax.ShapeDtypeStruct((2, hg), F32),
                name="hgrn_lb",
            )(hg_lb[j])
            w = hg_w_in[j]
            w_qiz = jnp.concatenate([0.5 * w[:, :hg], w[:, 3 * hg:4 * hg], 0.5 * w[:, 4 * hg:]], axis=1)
            p = _inproj(xs, mod_l, norm_g[layer], w_qiz, tc, tn=1024, out_dtype=BF16,
                        epilogue=((0, hg, "silu"), (hg, 2 * hg, "id"), (2 * hg, 3 * hg, "silu")))
            lb2 = lb.reshape(2 * hg)
            pf, f_min = _inproj(xs, mod_l, norm_g[layer], 0.5 * w[:, hg:3 * hg], tc, tn=hg,
                                epilogue=((0, 2 * hg, "fgate"),), block_min=True,
                                pars=(0.5 * hg_f_b[j].reshape(2 * hg), 0.5 * (1.0 + lb2), 0.5 * (1.0 - lb2)))
            n_blk = pf.shape[1] // ROW_BLOCK
            o_f = _hgrn_scan(p, pf, f_min[:, 0, :n_blk, 0], tc, rev=False)
            o_b = _hgrn_scan(p, pf, f_min[:, 1, :n_blk, 0], tc, rev=True)
            feats = [(o_f, hg, 0), (o_b, hg, 0), (p, hg, 2)]
            xs = _outproj("hgrn", feats, hg_w_out[j], xs, mod_l, tc, head_g=hg_head_g[j], **last)
    return xs
```

```python
import functools

import numpy as np
import jax
import jax.numpy as jnp
from jax import lax
from jax.experimental import pallas as pl
from jax.experimental.pallas import tpu as pltpu

F32 = jnp.float32
BF16 = jnp.bfloat16
EPS = 1e-6
DEPTH = 4
GRID_W = 64
ROPE_BASE = 10000.0

ML_HEADS, ML_DK, ML_DV = 4, 256, 512
ML_QK = ML_HEADS * ML_DK
ML_INNER = ML_HEADS * ML_DV
ML_CHUNK = 256

AT_HEADS, AT_KV_HEADS, AT_HEAD_DIM = 16, 4, 64
AT_GROUP = AT_HEADS // AT_KV_HEADS
AT_BLOCK = 128
AT_PROJ_BLOCKS = 8
AT_Q = AT_HEADS * AT_HEAD_DIM
AT_KV = AT_KV_HEADS * AT_HEAD_DIM

SC_KSIZE = 3

HG_HEADS, HG_DIM = 8, 128
HG_CHUNK = 128
HG_STEP_CHUNKS = 2

LANES = 128
SUBLANES = 8
VMEM_LIMIT_BYTES = 56 * 1024 * 1024

ROW_BLOCK = 256
PROJ_ROWS = 768
OUT_ROWS = 768
OUT_ROWS_SPLIT = 256
OUT_ROWS_LAT = 1024


def _cparams(*sem):
    return pltpu.CompilerParams(dimension_semantics=sem, vmem_limit_bytes=VMEM_LIMIT_BYTES)


def _sigmoid(x):
    return 0.5 * jnp.tanh(0.5 * x) + 0.5


def _silu(x):
    return x * _sigmoid(x)


def _silu_of_half(xh):
    return xh * (1.0 + jnp.tanh(xh))


def _ada_kernel(c_ref, w_ref, b_ref, o_ref):
    s = _silu(c_ref[...])
    o_ref[0] = jnp.dot(s.astype(BF16), w_ref[0].astype(BF16), preferred_element_type=F32) + b_ref[0]


def _ada_mod(c, c_ctx, ada_w, ada_b):
    b, d = c.shape
    depth = ada_w.shape[0]
    rows = -(-(b + 1) // SUBLANES) * SUBLANES
    cc = jnp.zeros((rows, d), F32).at[:b].set(c).at[b].set(c_ctx)
    tn = 1024
    out = pl.pallas_call(
        _ada_kernel,
        out_shape=jax.ShapeDtypeStruct((depth, rows, 3 * d), F32),
        grid=(depth, 3 * d // tn),
        in_specs=[pl.BlockSpec((rows, d), lambda l, j: (0, 0)),
                  pl.BlockSpec((1, d, tn), lambda l, j: (l, 0, j)),
                  pl.BlockSpec((1, 1, tn), lambda l, j: (l, 0, j))],
        out_specs=pl.BlockSpec((1, rows, tn), lambda l, j: (l, 0, j)),
        compiler_params=_cparams("parallel", "parallel"),
        name="ada_mod",
    )(cc, ada_w, ada_b.reshape(depth, 1, 3 * d))
    return out.reshape(depth, rows, 3, d)


def _modulated(x_refs, ml_ref, mc_ref, g_ref, h_ref, tc, row0=0, nrows=None):
    nrows = h_ref.shape[0] - row0 if nrows is None else nrows
    g = g_ref[...]
    xc_ref = x_refs[0]
    xl_ref, lat0 = (x_refs[1], 0) if len(x_refs) == 2 else (x_refs[0], tc)
    for r0 in range(row0, row0 + nrows, ROW_BLOCK):
        ctx_rows = r0 < tc
        x_ref, src0, m_ref = (xc_ref, r0, mc_ref) if ctx_rows else (xl_ref, lat0 + r0 - tc, ml_ref)
        x = x_ref[0, src0:src0 + ROW_BLOCK, :]
        ms = jnp.mean(x * x, axis=-1, keepdims=True)
        xn = x * lax.rsqrt(ms + EPS) * g
        h = xn * (1.0 + m_ref[0, 1:2, :]) + m_ref[0, 0:1, :]
        h_ref[r0:r0 + ROW_BLOCK, :] = h.astype(BF16)


def _split_stream(xs):
    arrs = list(xs) if isinstance(xs, tuple) else [xs]
    specs = [pl.BlockSpec((1,) + a.shape[1:], lambda i, j: (i, 0, 0)) for a in arrs]
    ta = sum(a.shape[1] for a in arrs)
    return arrs, specs, ta


def _inproj_kernel(*refs, tc, n_x, has_small, epilogue, w_rows_out, block_min):
    x_refs, (ml_ref, mc_ref, g_ref, w_ref), rest = refs[:n_x], refs[n_x:n_x + 4], list(refs[n_x + 4:])
    ws_ref = rest.pop(0) if has_small else None
    par_refs = [rest.pop(0) for _ in range(3)] if epilogue else None
    o_ref = rest.pop(0)
    os_ref = rest.pop(0) if has_small else None
    mn_ref = rest.pop(0) if block_min else None
    h_ref = rest.pop(0)

    ta = h_ref.shape[0]
    chunk = PROJ_ROWS if ta % PROJ_ROWS == 0 else ta
    j = pl.program_id(1)
    w_dims = (((1,), (1 if w_rows_out else 0,)), ((), ()))

    def activation(acc):
        if not epilogue:
            return acc
        bias_ref, c0_ref, c1_ref = par_refs
        t = jnp.tanh(acc + bias_ref[...])
        out = acc
        for lo, hi, kind in epilogue:
            if kind == "id":
                continue
            val = acc + acc * t if kind == "silu" else c0_ref[...] + c1_ref[...] * t
            out = jnp.where((j >= lo) & (j < hi), val, out)
        return out

    def project(first):
        step = chunk if (first or epilogue) else ta
        mins = []
        for r0 in range(0, ta, step):
            if first:
                _modulated(x_refs, ml_ref, mc_ref, g_ref, h_ref, tc, r0, step)
            acc = lax.dot_general(h_ref[r0:r0 + step, :], w_ref[...], w_dims, preferred_element_type=F32)
            out = activation(acc)
            o_ref[0, r0:r0 + step, :] = out.astype(o_ref.dtype)
            if mn_ref is not None:
                for b0 in range(0, step, ROW_BLOCK):
                    m = jnp.min(out[b0:b0 + ROW_BLOCK], axis=0, keepdims=True)
                    mins.append(jnp.broadcast_to(jnp.min(m, axis=1, keepdims=True), (1, LANES)))
        if mn_ref is not None:
            pad = jnp.zeros((mn_ref.shape[2] - len(mins), LANES), F32)
            mn_ref[0, 0] = jnp.concatenate(mins + [pad], axis=0)

    @pl.when(j == 0)
    def _():
        project(True)
        if has_small:
            os_ref[0] = lax.dot_general(ws_ref[...], h_ref[...], (((1,), (1,)), ((), ())),
                                        preferred_element_type=F32)

    @pl.when(j != 0)
    def _():
        project(False)


def _inproj(xs, mod_l, norm_g, w, tc, tn=512, w_small=None, out_dtype=F32, epilogue=None, pars=None,
            w_rows_out=False, block_min=False):
    x_arrs, x_specs, ta = _split_stream(xs)
    b, d = x_arrs[0].shape[0], x_arrs[0].shape[2]
    n = w.shape[0] if w_rows_out else w.shape[1]
    assert n % tn == 0 and tc % ROW_BLOCK == 0 and (ta - tc) % ROW_BLOCK == 0
    mod_lat = mod_l[:b]
    mod_ctx = mod_l[b:b + 1]
    has_small = w_small is not None
    in_specs = x_specs + [pl.BlockSpec((1, 3, d), lambda i, j: (i, 0, 0)),
                          pl.BlockSpec((1, 3, d), lambda i, j: (0, 0, 0)),
                          pl.BlockSpec((1, d), lambda i, j: (0, 0)),
                          pl.BlockSpec((tn, d), lambda i, j: (j, 0)) if w_rows_out
                          else pl.BlockSpec((d, tn), lambda i, j: (0, j))]
    args = x_arrs + [mod_lat, mod_ctx, norm_g.reshape(1, d), w.astype(BF16)]
    out_shape = [jax.ShapeDtypeStruct((b, ta, n), out_dtype)]
    out_specs = [pl.BlockSpec((1, ta, tn), lambda i, j: (i, 0, j))]
    if has_small:
        ns = w_small.shape[0]
        in_specs.append(pl.BlockSpec((ns, d), lambda i, j: (0, 0)))
        args.append(w_small.astype(BF16))
    if epilogue:
        assert all(lo % tn == 0 and hi % tn == 0 for lo, hi, _ in epilogue)
        epilogue = tuple((lo // tn, hi // tn, kind) for lo, hi, kind in epilogue)
        zeros = jnp.zeros((n,), F32)
        for par in (pars if pars is not None else (zeros,) * 3):
            in_specs.append(pl.BlockSpec((1, tn), lambda i, j: (0, j)))
            args.append(par.reshape(1, n))
    if has_small:
        out_shape.append(jax.ShapeDtypeStruct((b, ns, ta), F32))
        out_specs.append(pl.BlockSpec((1, ns, ta), lambda i, j: (i, 0, 0)))
    if block_min:
        nblk = -(-(ta // ROW_BLOCK) // SUBLANES) * SUBLANES
        out_shape.append(jax.ShapeDtypeStruct((b, n // tn, nblk, LANES), F32))
        out_specs.append(pl.BlockSpec((1, 1, nblk, LANES), lambda i, j: (i, j, 0, 0)))
    res = pl.pallas_call(
        functools.partial(_inproj_kernel, tc=tc, n_x=len(x_arrs), has_small=has_small, epilogue=epilogue,
                          w_rows_out=w_rows_out, block_min=block_min),
        out_shape=out_shape,
        grid=(b, n // tn),
        in_specs=in_specs,
        out_specs=out_specs,
        scratch_shapes=[pltpu.VMEM((ta, d), BF16)],
        compiler_params=_cparams("parallel", "arbitrary"),
        name="inproj",
    )(*args)
    return res if (has_small or block_min) else res[0]


def _head_rms(h, gain, n_heads):
    dh = h.shape[1] // n_heads
    parts = []
    for i in range(n_heads):
        hh = h[:, i * dh:(i + 1) * dh]
        ms = jnp.mean(hh * hh, axis=-1, keepdims=True)
        parts.append(hh * lax.rsqrt(ms + EPS))
    return jnp.concatenate(parts, axis=1) * gain


def _outproj_kernel(*refs, mode, n_feat, n_x, tc, tm, row0):
    feats, rest = refs[:n_feat], list(refs[n_feat:])
    hg_ref = rest.pop(0) if mode != "plain" else None
    w_ref = rest.pop(0)
    x_refs = [rest.pop(0) for _ in range(n_x)]
    ml_ref, mc_ref, o_ref = rest
    if mode == "plain":
        u = feats[0][0]
    elif mode == "mlstm":
        h0_ref, h1_ref, og_ref, z_ref = feats
        hn = _head_rms(h0_ref[0].astype(F32) + h1_ref[0].astype(F32), hg_ref[...], ML_HEADS)
        u = (hn * (1.0 + jnp.tanh(og_ref[0].astype(F32))) * _silu_of_half(z_ref[0].astype(F32))).astype(BF16)
    else:
        h0_ref, h1_ref, z_ref = feats
        hn = _head_rms(h0_ref[0].astype(F32) + h1_ref[0].astype(F32), hg_ref[...], HG_HEADS)
        u = (hn * z_ref[0].astype(F32)).astype(BF16)
    y = jnp.dot(u, w_ref[...], preferred_element_type=F32)
    first = row0 + pl.program_id(1) * tm
    row = first + lax.broadcasted_iota(jnp.int32, (tm, 1), 0)
    gate = jnp.where(row < tc, mc_ref[0, 2:3, :], ml_ref[0, 2:3, :])
    x = x_refs[0][0] if n_x == 1 else jnp.where(first < tc, x_refs[0][0], x_refs[1][0])
    o_ref[0] = x + gate * y


def _outproj(mode, feats, w_out, xs, mod_l, tc, head_g=None, lat_only=False):
    x_arrs = list(xs) if isinstance(xs, tuple) else [xs]
    b, d = x_arrs[0].shape[0], x_arrs[0].shape[2]
    ta = sum(a.shape[1] for a in x_arrs)
    kdim = w_out.shape[0]
    if lat_only:
        tm = OUT_ROWS_LAT if (ta - tc) % OUT_ROWS_LAT == 0 else OUT_ROWS_SPLIT
    else:
        tm = OUT_ROWS if (len(x_arrs) == 1 and ta % OUT_ROWS == 0) else OUT_ROWS_SPLIT
        assert ta % tm == 0 and (tc % tm == 0 or len(x_arrs) == 1)
    row0 = tc if lat_only else 0
    nct = tc // tm
    row_spec = lambda width, col: pl.BlockSpec((pl.Element(1), pl.Element(tm), pl.Element(width)),
                                               lambda i, r, col=col: (i, pl.multiple_of(row0 + r * tm, ROW_BLOCK),
                                                                      col * width))
    in_specs, args = [], []
    for arr, width, col in feats:
        in_specs.append(row_spec(width, col))
        args.append(arr)
    if head_g is not None:
        in_specs.append(pl.BlockSpec((1, kdim), lambda i, r: (0, 0)))
        args.append(head_g.reshape(1, kdim))
    in_specs.append(pl.BlockSpec((kdim, d), lambda i, r: (0, 0)))
    args.append(w_out.astype(BF16))
    if len(x_arrs) == 1:
        in_specs.append(row_spec(d, 0))
    elif lat_only:
        x_arrs = x_arrs[1:]
        in_specs.append(pl.BlockSpec((1, tm, d), lambda i, r: (i, r, 0)))
    else:
        in_specs += [pl.BlockSpec((1, tm, d), lambda i, r: (i, jnp.minimum(r, nct - 1), 0)),
                     pl.BlockSpec((1, tm, d), lambda i, r: (i, jnp.maximum(r - nct, 0), 0))]
    args += x_arrs
    in_specs += [pl.BlockSpec((1, 3, d), lambda i, r: (i, 0, 0)),
                 pl.BlockSpec((1, 3, d), lambda i, r: (0, 0, 0))]
    args += [mod_l[:b], mod_l[b:b + 1]]
    rows_out = ta - row0
    return pl.pallas_call(
        functools.partial(_outproj_kernel, mode=mode, n_feat=len(feats), n_x=len(x_arrs), tc=tc, tm=tm,
                          row0=row0),
        out_shape=jax.ShapeDtypeStruct((b, rows_out, d), F32),
        grid=(b, rows_out // tm),
        in_specs=in_specs,
        out_specs=pl.BlockSpec((1, tm, d), lambda i, r: (i, r, 0)),
        compiler_params=_cparams("parallel", "parallel"),
        name="outproj_" + mode,
    )(*args)


def _conv_kernel(x_ref, ml_ref, mc_ref, g_ref, wx_ref, wb_ref, wc_ref, wz_ref, cw_ref, cb_ref, o_ref, h_ref, *, tc):
    ta = h_ref.shape[0]
    chunk = PROJ_ROWS if ta % PROJ_ROWS == 0 else ta

    def body(first_tile):
        wx, wc, wb, wz = (w_ref[...].astype(BF16) for w_ref in (wx_ref, wc_ref, wb_ref, wz_ref))
        step = chunk if first_tile else ta
        xin, cg = [], []
        for r0 in range(0, ta, step):
            if first_tile:
                _modulated((x_ref,), ml_ref, mc_ref, g_ref, h_ref, tc, r0, step)
            h = h_ref[r0:r0 + step, :]
            xin.append(jnp.dot(h, wx, preferred_element_type=F32))
            cg.append(jnp.dot(h, wc, preferred_element_type=F32))
        bg = jnp.dot(h_ref[...], wb, preferred_element_type=F32)
        z = jnp.dot(h_ref[...], wz, preferred_element_type=F32)
        u = jnp.concatenate(cg, axis=0) * jnp.concatenate(xin, axis=0)
        row = lax.broadcasted_iota(jnp.int32, (ta, 1), 0)
        first = (row == 0) | (row == tc)
        last = (row == tc - 1) | (row == ta - 1)
        u_prev = jnp.where(first, 0.0, pltpu.roll(u, 1, axis=0))
        u_next = jnp.where(last, 0.0, pltpu.roll(u, ta - 1, axis=0))
        cw = cw_ref[...]
        y = u_prev * cw[0:1, :] + u * cw[1:2, :] + u_next * cw[2:3, :] + cb_ref[...]
        o_ref[0] = (bg * y * _silu(z)).astype(o_ref.dtype)

    @pl.when(pl.program_id(1) == 0)
    def _():
        body(True)

    @pl.when(pl.program_id(1) != 0)
    def _():
        body(False)


def _conv_mixer(xs, mod_l, norm_g, w_in, conv_w, conv_b, tc, tw=256):
    b, ta, d = xs.shape
    e = conv_w.shape[1]
    nt = e // tw
    w_spec = lambda part: pl.BlockSpec((d, tw), lambda i, j, part=part: (0, part * nt + j))
    return pl.pallas_call(
        functools.partial(_conv_kernel, tc=tc),
        out_shape=jax.ShapeDtypeStruct((b, ta, e), BF16),
        grid=(b, nt),
        in_specs=[pl.BlockSpec((1, ta, d), lambda i, j: (i, 0, 0)),
                  pl.BlockSpec((1, 3, d), lambda i, j: (i, 0, 0)),
                  pl.BlockSpec((1, 3, d), lambda i, j: (0, 0, 0)),
                  pl.BlockSpec((1, d), lambda i, j: (0, 0)),
                  w_spec(0), w_spec(1), w_spec(2), w_spec(3),
                  pl.BlockSpec((SC_KSIZE, tw), lambda i, j: (0, j)),
                  pl.BlockSpec((1, tw), lambda i, j: (0, j))],
        out_specs=pl.BlockSpec((1, ta, tw), lambda i, j: (i, 0, j)),
        scratch_shapes=[pltpu.VMEM((ta, d), BF16)],
        compiler_params=_cparams("parallel", "arbitrary"),
        name="conv_mixer",
    )(xs, mod_l[:b], mod_l[b:b + 1], norm_g.reshape(1, d), w_in, w_in, w_in, w_in, conv_w, conv_b.reshape(1, e))


AT_HALF = AT_HEAD_DIM // 2
AT_TILE_PERM = np.concatenate([np.arange(0, AT_HALF), np.arange(2 * AT_HALF, 3 * AT_HALF),
                               np.arange(AT_HALF, 2 * AT_HALF), np.arange(3 * AT_HALF, 4 * AT_HALF)])
LOG2E = float(np.log2(np.e))


def _rope_tables(tc, t):
    rows = t // GRID_W
    row = np.repeat(np.arange(rows), GRID_W).astype(np.float64)
    col = np.tile(np.arange(GRID_W), rows).astype(np.float64)
    n_freq = AT_HEAD_DIM // 4
    freqs = np.power(ROPE_BASE, -np.arange(n_freq, dtype=np.float64) / n_freq)
    ang = np.concatenate([row[:, None] * freqs, col[:, None] * freqs], axis=-1)
    cos, sin = np.cos(ang), np.sin(ang)
    cos_t = np.concatenate([np.ones((tc, LANES)), np.tile(cos, (1, 4))], axis=0)
    sin_t = np.concatenate([np.zeros((tc, LANES)), np.concatenate([-sin, -sin, sin, sin], axis=-1)], axis=0)
    return jnp.asarray(cos_t, F32), jnp.asarray(sin_t, F32)


def _norm_rope_tiles(xs, gains, cos, sin, same_head):
    sums = []
    for x in xs:
        x2 = x * x
        hi = x2.astype(BF16)
        lo = (x2 - hi.astype(F32)).astype(BF16)
        sums.append((jnp.dot(hi, same_head, preferred_element_type=F32),
                     jnp.dot(lo, same_head, preferred_element_type=F32)))
    out = []
    for x, gain, (s_hi, s_lo) in zip(xs, gains, sums):
        xn = x * lax.rsqrt(s_hi + s_lo + EPS) * gain
        out.append(xn * cos + pltpu.roll(xn, 2 * AT_HALF, axis=1) * sin)
    return out


def _attn_kernel(sink_ref, q_ref, k_ref, v_ref, z_ref, qg_ref, kg_ref, cos_ref, sin_ref, sh_ref, wo_ref, x_ref,
                 ml_ref, mc_ref, o_ref, qs_ref, ka_ref, kb_ref, va_ref, vb_ref, *, tc, need_ctx):
    ta = q_ref.shape[1]
    t = ta - tc
    nb = t // AT_BLOCK
    blk = AT_BLOCK
    gw = x_ref.shape[2]
    g = pl.program_id(1)
    odd = (g % 2) == 1

    @pl.when(g == 0)
    def _():
        o_ref[...] = jnp.zeros(o_ref.shape, F32)

    def prep(i, carry):
        r0 = pl.multiple_of(i * ROW_BLOCK, ROW_BLOCK)
        res_cols = pl.ds(pl.multiple_of(g * gw, gw), gw)
        o_ref[0, pl.ds(r0, ROW_BLOCK), res_cols] = (o_ref[0, pl.ds(r0, ROW_BLOCK), res_cols]
                                                    + x_ref[0, pl.ds(r0, ROW_BLOCK), :])
        cos = cos_ref[pl.ds(r0, ROW_BLOCK), :]
        sin = sin_ref[pl.ds(r0, ROW_BLOCK), :]
        same_head = sh_ref[...]
        tiles = [q_ref[0, pl.ds(r0, ROW_BLOCK), c * LANES:(c + 1) * LANES].astype(F32) for c in range(2)]
        tiles.append(k_ref[0, pl.ds(r0, ROW_BLOCK), :].astype(F32))
        q0, q1, kn = _norm_rope_tiles(tiles, (qg_ref[...], qg_ref[...], kg_ref[...]), cos, sin, same_head)
        for c, qc in enumerate((q0, q1)):
            qs_ref[pl.ds(r0, ROW_BLOCK), c * LANES:(c + 1) * LANES] = qc.astype(BF16)
        vv = v_ref[0, pl.ds(r0, ROW_BLOCK), :].astype(F32)
        lane = lax.broadcasted_iota(jnp.int32, kn.shape, 1)
        k_own = jnp.where(((lane // AT_HALF) % 2) == (g % 2), kn, 0.0)
        k_oth = pltpu.roll(k_own, jnp.where(odd, 3 * AT_HALF, AT_HALF), axis=1)
        ka_ref[pl.ds(r0, ROW_BLOCK), :] = jnp.where(odd, k_oth, k_own).astype(BF16)
        kb_ref[pl.ds(r0, ROW_BLOCK), :] = jnp.where(odd, k_own, k_oth).astype(BF16)
        v_own = jnp.where((lane // AT_HEAD_DIM) == (g % 2), vv, 0.0)
        v_oth = pltpu.roll(v_own, AT_HEAD_DIM, axis=1)
        va = jnp.where(odd, v_oth, v_own)
        vb = jnp.where(odd, v_own, v_oth)
        va_ref[pl.ds(r0, ROW_BLOCK), :] = jnp.where(lane == AT_HEAD_DIM, 1.0, va).astype(BF16)
        vb_ref[pl.ds(r0, ROW_BLOCK), :] = jnp.where(lane == 0, 1.0, vb).astype(BF16)
        return carry

    lax.fori_loop(0, ta // ROW_BLOCK, prep, 0, unroll=3)
    zeros = jnp.zeros((blk, LANES), BF16)
    for ref in (ka_ref, kb_ref, va_ref, vb_ref):
        ref[ta:ta + blk, :] = zeros

    half = lax.broadcasted_iota(jnp.int32, (2 * blk, 1), 0) < blk
    sink_a = jnp.where(half, sink_ref[g, 0], sink_ref[g, 2]) * LOG2E
    sink_b = jnp.where(half, sink_ref[g, 1], sink_ref[g, 3]) * LOG2E

    qi = lax.broadcasted_iota(jnp.int32, (2 * blk, 3 * blk), 0) % blk
    kj = lax.broadcasted_iota(jnp.int32, (2 * blk, 3 * blk), 1)
    band = (kj - qi >= 0) & (kj - qi <= 2 * blk)
    out_lo = lax.broadcasted_iota(jnp.int32, (2 * blk, LANES), 1) < AT_HEAD_DIM

    nt = (((1,), (1,)), ((), ()))

    sides = ((ka_ref, va_ref, sink_a, AT_HEAD_DIM), (kb_ref, vb_ref, sink_b, 0))

    def attend(blocks, r_first, gate):
        qts = [jnp.concatenate([qs_ref[pl.ds(r0, blk), 0:LANES], qs_ref[pl.ds(r0, blk), LANES:2 * LANES]], axis=0)
               for r0, _, _ in blocks]
        scores = []
        for qt, (_, k0, mask) in zip(qts, blocks):
            for k_ref_, _, _, _ in sides:
                s_ctx = lax.dot_general(qt, k_ref_[0:tc, :], nt, preferred_element_type=F32)
                s_loc = None if mask is None else lax.dot_general(qt, k_ref_[pl.ds(k0, 3 * blk), :], nt,
                                                                  preferred_element_type=F32)
                scores.append((s_ctx, s_loc))
        probs = []
        for idx, (s_ctx, s_loc) in enumerate(scores):
            mask = blocks[idx // 2][2]
            sink = sides[idx % 2][2]
            m = jnp.maximum(sink, jnp.max(s_ctx, axis=-1, keepdims=True))
            if s_loc is not None:
                s_loc = jnp.where(mask, s_loc, -jnp.inf)
                m = jnp.maximum(m, jnp.max(s_loc, axis=-1, keepdims=True))
            probs.append((jnp.exp2(s_ctx - m).astype(BF16),
                          None if s_loc is None else jnp.exp2(s_loc - m).astype(BF16), jnp.exp2(sink - m)))
        accs = []
        for idx, (p_ctx, p_loc, _) in enumerate(probs):
            k0 = blocks[idx // 2][1]
            v_ref_ = sides[idx % 2][1]
            acc = jnp.dot(p_ctx, v_ref_[0:tc, :], preferred_element_type=F32)
            if p_loc is not None:
                acc = acc + jnp.dot(p_loc, v_ref_[pl.ds(k0, 3 * blk), :], preferred_element_type=F32)
            accs.append(acc)
        us = []
        for bi, (r0, _, _) in enumerate(blocks):
            halves = []
            for si in range(2):
                acc, ones_lane = accs[2 * bi + si], sides[si][3]
                halves.append(acc / (probs[2 * bi + si][2] + acc[:, ones_lane:ones_lane + 1]))
            o = jnp.where(out_lo, halves[0], halves[1])
            parts = []
            for c in range(2):
                zc = z_ref[0, pl.ds(r0, blk), c * LANES:(c + 1) * LANES].astype(F32)
                parts.append((o[c * blk:(c + 1) * blk] * _silu_of_half(zc)).astype(BF16))
            us.append(jnp.concatenate(parts, axis=1))
        u = jnp.concatenate(us, axis=0)
        rows = pl.ds(r_first, u.shape[0])
        o_ref[0, rows, :] = o_ref[0, rows, :] + gate * jnp.dot(u, wo_ref[...], preferred_element_type=F32)

    def lat_blocks(i, carry):
        blocks = []
        for j in range(AT_PROJ_BLOCKS):
            n = i * AT_PROJ_BLOCKS + j
            kpos = (n - 1) * blk + kj
            blocks.append((pl.multiple_of(tc + n * blk, blk), pl.multiple_of(tc + (n - 1) * blk, blk),
                           band & (kpos >= 0) & (kpos < t)))
        attend(blocks, pl.multiple_of(tc + i * AT_PROJ_BLOCKS * blk, AT_PROJ_BLOCKS * blk), ml_ref[0, 2:3, :])
        return carry

    lax.fori_loop(0, nb // AT_PROJ_BLOCKS, lat_blocks, 0)
    if need_ctx:
        attend([(n * blk, None, None) for n in range(tc // blk)], 0, mc_ref[0, 2:3, :])


def _attn_mixer(p, at_q_g, at_k_g, at_sink, w_out, xs, mod_l, tc, need_ctx):
    b, ta, _ = p.shape
    t = ta - tc
    d = xs.shape[2]
    assert tc >= AT_BLOCK and tc % AT_BLOCK == 0 and t % (AT_PROJ_BLOCKS * AT_BLOCK) == 0 and ta % ROW_BLOCK == 0
    assert d == AT_Q
    cos_t, sin_t = _rope_tables(tc, t)
    lane_head = (np.arange(LANES) // AT_HALF) % 2
    same_head = jnp.asarray((lane_head[:, None] == lane_head[None, :]) / AT_HEAD_DIM, BF16)
    q_scale = (AT_HEAD_DIM ** -0.5) * LOG2E
    tile_gain = lambda gain: jnp.tile(gain, 2)[AT_TILE_PERM].reshape(1, LANES)
    gw = AT_GROUP * AT_HEAD_DIM
    kcol = AT_Q // LANES
    vcol = (AT_Q + AT_KV) // LANES
    zcol = (AT_Q + 2 * AT_KV) // gw
    return pl.pallas_call(
        functools.partial(_attn_kernel, tc=tc, need_ctx=need_ctx),
        out_shape=jax.ShapeDtypeStruct((b, ta, d), F32),
        grid=(b, AT_KV_HEADS),
        in_specs=[pl.BlockSpec(memory_space=pltpu.SMEM),
                  pl.BlockSpec((1, ta, gw), lambda i, g: (i, 0, g)),
                  pl.BlockSpec((1, ta, LANES), lambda i, g: (i, 0, kcol + g // 2)),
                  pl.BlockSpec((1, ta, LANES), lambda i, g: (i, 0, vcol + g // 2)),
                  pl.BlockSpec((1, ta, gw), lambda i, g: (i, 0, zcol + g)),
                  pl.BlockSpec((1, LANES), lambda i, g: (0, 0)),
                  pl.BlockSpec((1, LANES), lambda i, g: (0, 0)),
                  pl.BlockSpec((ta, LANES), lambda i, g: (0, 0)),
                  pl.BlockSpec((ta, LANES), lambda i, g: (0, 0)),
                  pl.BlockSpec((LANES, LANES), lambda i, g: (0, 0)),
                  pl.BlockSpec((gw, d), lambda i, g: (g, 0)),
                  pl.BlockSpec((1, ta, gw), lambda i, g: (i, 0, g)),
                  pl.BlockSpec((1, 3, d), lambda i, g: (i, 0, 0)),
                  pl.BlockSpec((1, 3, d), lambda i, g: (0, 0, 0))],
        out_specs=pl.BlockSpec((1, ta, d), lambda i, g: (i, 0, 0)),
        scratch_shapes=[pltpu.VMEM((ta, gw), BF16)] + [pltpu.VMEM((ta + AT_BLOCK, LANES), BF16)] * 4,
        compiler_params=_cparams("parallel", "arbitrary"),
        name="attn_mixer",
    )(at_sink.reshape(AT_KV_HEADS, AT_GROUP), p, p, p, p, q_scale * tile_gain(at_q_g), tile_gain(at_k_g), cos_t, sin_t,
      same_head, w_out.astype(BF16), xs, mod_l[:b], mod_l[b:b + 1])


def _attn_weight_kernel(w_ref, p_ref, o_ref, *, n_qk, n_v):
    tiles = w_ref.shape[1] // LANES
    for lt in range(tiles):
        tile = pl.program_id(0) * tiles + lt
        kind = (tile >= n_qk).astype(jnp.int32) + (tile >= n_qk + n_v).astype(jnp.int32)
        cols = slice(lt * LANES, (lt + 1) * LANES)
        o_ref[:, cols] = jnp.dot(w_ref[:, cols].astype(BF16), p_ref[kind], preferred_element_type=F32).astype(BF16)


def _attn_weight(w_in):
    d, n = w_in.shape
    n_qk, n_v = (AT_Q + AT_KV) // LANES, AT_KV // LANES
    perm = np.zeros((LANES, LANES), np.float32)
    perm[AT_TILE_PERM, np.arange(LANES)] = 1.0
    eye = np.eye(LANES, dtype=np.float32)
    mats = jnp.asarray(np.stack([perm, eye, 0.5 * eye]), BF16)
    tw = 4 * LANES
    assert n % tw == 0
    return pl.pallas_call(
        functools.partial(_attn_weight_kernel, n_qk=n_qk, n_v=n_v),
        out_shape=jax.ShapeDtypeStruct((d, n), BF16),
        grid=(n // tw,),
        in_specs=[pl.BlockSpec((d, tw), lambda j: (0, j)),
                  pl.BlockSpec((3, LANES, LANES), lambda j: (0, 0, 0))],
        out_specs=pl.BlockSpec((d, tw), lambda j: (0, j)),
        compiler_params=_cparams("parallel"),
        name="attn_weight",
    )(w_in, mats)


def _chunk_order(nc, ncc, d, s):
    bwd = jnp.where(s < ncc, ncc - 1 - s, nc - 1 - (s - ncc))
    return jnp.where(d == 0, s, bwd)


ML_GATE_PERM = np.concatenate([np.arange(0, 4), np.arange(8, 12), np.arange(4, 8), np.arange(12, 16)])
ML_NQ = 6


def _mlstm_gate_kernel(g_ref, b_ref, a_ref, c_ref, *, tc, lc):
    h = ML_HEADS
    ta = g_ref.shape[2]
    nc, ncc = ta // lc, tc // lc
    x = g_ref[0] + b_ref[...]
    li = x[0:2 * h]
    lfp = x[2 * h:4 * h]
    lf = jnp.minimum(lfp, 0.0) - jnp.log1p(jnp.exp(-jnp.abs(lfp)))
    fwd = lax.broadcasted_iota(jnp.int32, (2 * h, ta), 0) < h
    fwd1 = lax.broadcasted_iota(jnp.int32, (2 * h, 1), 0) < h
    pos = lax.broadcasted_iota(jnp.int32, (2 * h, ta), 1) % lc

    def seg_scan(y, op, fill):
        yf, yb = y, y
        s = 1
        while s < lc:
            yf = op(yf, jnp.where(pos >= s, pltpu.roll(yf, s, axis=1), fill))
            yb = op(yb, jnp.where(pos < lc - s, pltpu.roll(yb, ta - s, axis=1), fill))
            s *= 2
        return jnp.where(fwd, yf, yb)

    bsum = seg_scan(lf, jnp.add, 0.0)
    a = li - bsum
    cmax = seg_scan(a, jnp.maximum, -jnp.inf)

    def end_col(y, c):
        return jnp.where(fwd1, y[:, (c + 1) * lc - 1:(c + 1) * lc], y[:, c * lc:c * lc + 1])

    tot = [end_col(bsum, c) for c in range(nc)]
    amax = [end_col(cmax, c) for c in range(nc)]

    def chain(order):
        m = jnp.zeros((2 * h, 1), F32)
        m_in = [None] * nc
        for c in order:
            m_in[c] = m
            m = tot[c] + jnp.maximum(m, amax[c])
        return m_in

    m_f = chain(list(range(nc)))
    m_b = chain(list(range(ncc - 1, -1, -1)) + list(range(nc - 1, ncc - 1, -1)))
    for c in range(nc):
        m_in = jnp.where(fwd1, m_f[c], m_b[c])
        sl = slice(c * lc, (c + 1) * lc)
        g_run = jnp.maximum(m_in, cmax[:, sl])
        g_end = jnp.maximum(m_in, amax[c])
        nr = 2 * h
        quantities = [a[:, sl], g_run, jnp.exp(m_in - g_run), jnp.exp(-(bsum[:, sl] + g_run)),
                      jnp.exp(a[:, sl] - g_end), jnp.broadcast_to(jnp.exp(m_in - g_end), (nr, lc))]
        a_ref[0, c] = quantities[0]
        pad = jnp.zeros((LANES - ML_NQ * nr, lc), F32)
        c_ref[0, c] = jnp.concatenate(quantities + [pad], axis=0).T


def _mlstm_scan_kernel(q_ref, k_ref, v_ref, a_ref, c_ref, o_ref, cs_ref, *, rev):
    lc = q_ref.shape[1]
    nr = 2 * ML_HEADS

    @pl.when(pl.program_id(1) == 0)
    def _():
        cs_ref[...] = jnp.zeros(cs_ref.shape, F32)

    ti = lax.broadcasted_iota(jnp.int32, (lc, lc), 0)
    si = lax.broadcasted_iota(jnp.int32, (lc, lc), 1)
    mask = (si >= ti) if rev else (si <= ti)
    ones = jnp.ones((lc, LANES), BF16)
    nt = (((1,), (1,)), ((), ()))
    tn = (((0,), (0,)), ((), ()))
    heads = range(ML_HEADS)
    qb = [q_ref[0, :, h * ML_DK:(h + 1) * ML_DK] for h in heads]
    kb = [k_ref[0, :, h * ML_DK:(h + 1) * ML_DK] for h in heads]
    v_aug = [jnp.concatenate([v_ref[0, :, h * ML_DV:(h + 1) * ML_DV], ones], axis=1) for h in heads]
    cols = []
    for h in heads:
        r = (ML_HEADS if rev else 0) + h
        cols.append([a_ref[0, 0, r:r + 1, :]] + [c_ref[0, 0, :, qi * nr + r:qi * nr + r + 1] for qi in range(1, 6)])
    qk = [lax.dot_general(qb[h], kb[h], nt, preferred_element_type=F32) for h in heads]
    c_old = [cs_ref[h] for h in heads]
    q_c = [jnp.dot(qb[h], c_old[h].astype(BF16), preferred_element_type=F32) for h in heads]
    kw = [(kb[h].astype(F32) * cols[h][4]).astype(BF16) for h in heads]
    upd = [lax.dot_general(kw[h], v_aug[h], tn, preferred_element_type=F32) for h in heads]
    s = []
    for h in heads:
        a_row, g_run = cols[h][0], cols[h][1]
        s.append((qk[h] * jnp.where(mask, jnp.exp(a_row - g_run), 0.0)).astype(BF16))
    s_v = [jnp.dot(s[h], v_aug[h], preferred_element_type=F32) for h in heads]
    for h in heads:
        inter, eclamp, decay = cols[h][2], cols[h][3], cols[h][5]
        tot = inter * q_c[h] + s_v[h]
        inv = 1.0 / jnp.maximum(jnp.abs(tot[:, ML_DV:]), eclamp)
        o_ref[0, :, h * ML_DV:(h + 1) * ML_DV] = (
            tot[:, :ML_DV] * jnp.concatenate([inv] * (ML_DV // LANES), axis=1)).astype(o_ref.dtype)
        cs_ref[h] = decay[0:1, :] * c_old[h] + upd[h]


def _mlstm_scan(p, gates, gate_b, tc):
    b, ta, _ = p.shape
    lc = ML_CHUNK
    assert tc % lc == 0 and ta % lc == 0
    nc, ncc = ta // lc, tc // lc
    ng = 4 * ML_HEADS
    nr = 2 * ML_HEADS
    bias = gate_b.reshape(ng)[ML_GATE_PERM].reshape(ng, 1)
    a_rows, cols = pl.pallas_call(
        functools.partial(_mlstm_gate_kernel, tc=tc, lc=lc),
        out_shape=[jax.ShapeDtypeStruct((b, nc, nr, lc), F32), jax.ShapeDtypeStruct((b, nc, lc, LANES), F32)],
        grid=(b,),
        in_specs=[pl.BlockSpec((1, ng, ta), lambda i: (i, 0, 0)),
                  pl.BlockSpec((ng, 1), lambda i: (0, 0))],
        out_specs=[pl.BlockSpec((1, nc, nr, lc), lambda i: (i, 0, 0, 0)),
                   pl.BlockSpec((1, nc, lc, LANES), lambda i: (i, 0, 0, 0))],
        compiler_params=_cparams("parallel"),
        name="mlstm_gates",
    )(gates, bias)
    outs = []
    for d in range(2):
        chunk = functools.partial(_chunk_order, nc, ncc, d)
        outs.append(pl.pallas_call(
            functools.partial(_mlstm_scan_kernel, rev=bool(d)),
            out_shape=jax.ShapeDtypeStruct((b, ta, ML_INNER), BF16),
            grid=(b, nc),
            in_specs=[pl.BlockSpec((1, lc, ML_QK), lambda i, s, chunk=chunk: (i, chunk(s), 0)),
                      pl.BlockSpec((1, lc, ML_QK), lambda i, s, chunk=chunk: (i, chunk(s), 1)),
                      pl.BlockSpec((1, lc, ML_INNER), lambda i, s, chunk=chunk: (i, chunk(s), 1)),
                      pl.BlockSpec((1, 1, nr, lc), lambda i, s, chunk=chunk: (i, chunk(s), 0, 0)),
                      pl.BlockSpec((1, 1, lc, LANES), lambda i, s, chunk=chunk: (i, chunk(s), 0, 0))],
            out_specs=pl.BlockSpec((1, lc, ML_INNER), lambda i, s, chunk=chunk: (i, chunk(s), 0)),
            scratch_shapes=[pltpu.VMEM((ML_HEADS, ML_DK, ML_DV + LANES), F32)],
            compiler_params=_cparams("parallel", "arbitrary"),
            name="mlstm_scan_bwd" if d else "mlstm_scan_fwd",
        )(p, p, p, a_rows, cols))
    return outs


def _hgrn_lb_kernel(p_ref, o_ref, *, layer):
    for d in range(p_ref.shape[0]):
        x = p_ref[d]
        e = jnp.exp(x - jnp.max(x, axis=0, keepdims=True))
        p = e / jnp.sum(e, axis=0, keepdims=True)
        acc = jnp.zeros((1, x.shape[1]), F32)
        for j in range(1, layer + 1):
            acc = acc + p[j:j + 1, :]
        o_ref[d:d + 1, :] = acc


def _cumsum_rows(x, tri2):
    hi = x.astype(BF16)
    lo = (x - hi.astype(F32)).astype(BF16)
    return jnp.dot(tri2, jnp.concatenate([hi, lo], axis=0), preferred_element_type=F32)


def _tri2(n, rev):
    t = np.triu(np.ones((n, n), np.float32)) if rev else np.tril(np.ones((n, n), np.float32))
    return jnp.asarray(np.concatenate([t, t], axis=1), BF16)


def _anchor_rows(a, m, rev):
    n, f = a.shape
    idx = m if rev else m - 1
    if 2 * m >= SUBLANES:
        a3 = a.reshape(n // (2 * m), 2 * m, f)
        return jnp.broadcast_to(a3[:, idx:idx + 1, :], a3.shape).reshape(n, f)
    a3 = a.reshape(n // SUBLANES, SUBLANES, f)
    sub = lax.broadcasted_iota(jnp.int32, a3.shape, 1)
    out = None
    for gi in range(SUBLANES // (2 * m) - 1, -1, -1):
        cand = jnp.broadcast_to(a3[:, gi * 2 * m + idx:gi * 2 * m + idx + 1, :], a3.shape)
        out = cand if out is None else jnp.where(sub < (gi + 1) * 2 * m, cand, out)
    return out.reshape(n, f)


HG_MAX_LOG_SPAN = 64.0


def _hgrn_scan_kernel(fmin_ref, q_ref, f_ref, i_ref, tri_ref, o_ref, s_ref, *, rev, nc, ncc):
    lc = HG_CHUNK
    subs = range(q_ref.shape[1] // lc)
    subs = tuple(reversed(subs)) if rev else tuple(subs)
    step_block = _chunk_order(nc, ncc, 1 if rev else 0, pl.program_id(1))

    @pl.when(pl.program_id(1) == 0)
    def _():
        s_ref[...] = jnp.zeros(s_ref.shape, F32)

    row = lax.broadcasted_iota(jnp.int32, (lc, 1), 0)
    ti = lax.broadcasted_iota(jnp.int32, (lc, lc), 0)
    si = lax.broadcasted_iota(jnp.int32, (lc, lc), 1)
    nt = (((1,), (1,)), ((), ()))
    tn = (((0,), (0,)), ((), ()))
    end = 0 if rev else lc - 1

    def all_heads_single_anchor():
        causal = (si >= ti) if rev else (si <= ti)
        cols = [slice(h * HG_DIM, (h + 1) * HG_DIM) for h in range(HG_HEADS)]
        pre = {}
        for ci in subs:
            rows = slice(ci * lc, (ci + 1) * lc)
            f = f_ref[0, rows, :]
            a = _cumsum_rows(jnp.log(f), tri_ref[...])
            q_dec = (q_ref[0, rows, :].astype(F32) * jnp.exp(a)).astype(BF16)
            k_inv = (1.0 - f) * jnp.exp(-a)
            k_inv_b = k_inv.astype(BF16)
            e_end = jnp.exp(a[end:end + 1, :])
            kd = (k_inv * e_end).astype(BF16)
            attn = [jnp.where(causal, lax.dot_general(q_dec[:, c], k_inv_b[:, c], nt, preferred_element_type=F32),
                              0.0).astype(BF16) for c in cols]
            pre[ci] = (rows, q_dec, e_end, kd, attn)
        state = [s_ref[h] for h in range(HG_HEADS)]
        for ci in subs:
            rows, q_dec, e_end, kd, attn = pre[ci]
            for h, c in enumerate(cols):
                o = jnp.dot(attn[h], i_ref[0, rows, c], preferred_element_type=F32)
                o = o + lax.dot_general(q_dec[:, c], state[h].astype(BF16), nt, preferred_element_type=F32)
                o_ref[0, rows, c] = o.astype(o_ref.dtype)
            state = [state[h] * e_end[:, c] + lax.dot_general(i_ref[0, rows, c], kd[:, c], tn,
                                                              preferred_element_type=F32)
                     for h, c in enumerate(cols)]
        for h in range(HG_HEADS):
            s_ref[h] = state[h]

    def head_per_level(r0, h, carry):
        c0 = pl.multiple_of(h * HG_DIM, HG_DIM)
        q = q_ref[0, r0:r0 + lc, pl.ds(c0, HG_DIM)].astype(F32)
        f = f_ref[0, r0:r0 + lc, pl.ds(c0, HG_DIM)]
        k = 1.0 - f
        iv = i_ref[0, r0:r0 + lc, pl.ds(c0, HG_DIM)].astype(F32)
        a = _cumsum_rows(jnp.log(f), tri_ref[...])
        attn = jnp.zeros((lc, lc), F32)
        m = 1
        while m < lc:
            e = jnp.exp(-jnp.abs(a - _anchor_rows(a, m, rev)))
            upper = (row % (2 * m)) >= m
            is_q = jnp.logical_not(upper) if rev else upper
            qt = jnp.where(is_q, q * e, 0.0).astype(BF16)
            kt = jnp.where(is_q, 0.0, k * e).astype(BF16)
            pair = lax.dot_general(qt, kt, nt, preferred_element_type=F32)
            attn = attn + jnp.where((ti // (2 * m)) == (si // (2 * m)), pair, 0.0)
            m *= 2
        a_end = a[end:end + 1, :]
        kd = (k * jnp.exp(a_end - a)).astype(BF16)
        ib = iv.astype(BF16)
        s_old = s_ref[h]
        o = jnp.dot(attn.astype(BF16), ib, preferred_element_type=F32)
        o = o + lax.dot_general((q * jnp.exp(a)).astype(BF16), s_old.astype(BF16), nt, preferred_element_type=F32)
        o_ref[0, r0:r0 + lc, pl.ds(c0, HG_DIM)] = (o + jnp.sum(q * k, axis=-1, keepdims=True) * iv).astype(o_ref.dtype)
        s_ref[h] = s_old * jnp.exp(a_end) + lax.dot_general(ib, kd, tn, preferred_element_type=F32)
        return carry

    in_range = fmin_ref[pl.program_id(0), step_block] >= float(np.exp(-HG_MAX_LOG_SPAN / lc))

    @pl.when(in_range)
    def _():
        all_heads_single_anchor()

    @pl.when(jnp.logical_not(in_range))
    def _():
        for ci in subs:
            lax.fori_loop(0, HG_HEADS, functools.partial(head_per_level, ci * lc), 0)


def _hgrn_scan(p, pf, f_min, tc, rev):
    b, ta, _ = p.shape
    lc = HG_CHUNK
    rows = HG_STEP_CHUNKS * lc
    hg = HG_HEADS * HG_DIM
    assert tc % rows == 0 and ta % rows == 0 and rows == ROW_BLOCK
    nc, ncc = ta // rows, tc // rows
    d = 1 if rev else 0
    chunk = lambda s: _chunk_order(nc, ncc, d, s)
    return pl.pallas_call(
        functools.partial(_hgrn_scan_kernel, rev=rev, nc=nc, ncc=ncc),
        out_shape=jax.ShapeDtypeStruct((b, ta, hg), BF16),
        grid_spec=pltpu.PrefetchScalarGridSpec(
            num_scalar_prefetch=1,
            grid=(b, nc),
            in_specs=[pl.BlockSpec((1, rows, hg), lambda i, s, fm: (i, chunk(s), 0)),
                      pl.BlockSpec((1, rows, hg), lambda i, s, fm: (i, chunk(s), d)),
                      pl.BlockSpec((1, rows, hg), lambda i, s, fm: (i, chunk(s), 1)),
                      pl.BlockSpec((lc, 2 * lc), lambda i, s, fm: (0, 0))],
            out_specs=pl.BlockSpec((1, rows, hg), lambda i, s, fm: (i, chunk(s), 0)),
            scratch_shapes=[pltpu.VMEM((HG_HEADS, HG_DIM, HG_DIM), F32)]),
        compiler_params=_cparams("parallel", "arbitrary"),
        name="hgrn_scan_bwd" if rev else "hgrn_scan_fwd",
    )(f_min, p, pf, p, _tri2(lc, rev))


def kernel(x, c, ctx, c_ctx, ada_w, ada_b, norm_g, ml_w_in, ml_gate_b, ml_head_g, ml_w_out, at_w_in, at_q_g, at_k_g, at_sink, at_w_out, sc_w_in, sc_conv_w, sc_conv_b, sc_w_out, hg_w_in, hg_f_b, hg_lb, hg_head_g, hg_w_out):
    tc = ctx.shape[1]
    mod = _ada_mod(c, c_ctx, ada_w, ada_b)
    xs = (ctx, x)
    for layer in range(DEPTH):
        kind, j = layer % 4, layer // 4
        need_ctx = layer < DEPTH - 1
        last = dict(lat_only=True) if layer == DEPTH - 1 else {}
        mod_l = mod[layer]
        if kind == 0:
            n_main = 2 * ML_QK + 3 * ML_INNER
            w_t = ml_w_in[j].T
            w_gate = jnp.zeros((LANES, ml_w_in.shape[1]), F32).at[:4 * ML_HEADS].set(w_t[n_main:][ML_GATE_PERM])
            col_scale = jnp.concatenate([jnp.full((ML_QK,), ML_DK ** -0.5, F32), jnp.ones((ML_QK + ML_INNER,), F32),
                                         jnp.full((2 * ML_INNER,), 0.5, F32)])
            p, gates = _inproj(xs, mod_l, norm_g[layer], w_t[:n_main] * col_scale[:, None], tc, tn=1024,
                               w_small=w_gate, out_dtype=BF16, w_rows_out=True)
            h_f, h_b = _mlstm_scan(p, gates, ml_gate_b[j], tc)
            feats = [(h_f, ML_INNER, 0), (h_b, ML_INNER, 0), (p, ML_INNER, 2), (p, ML_INNER, 3)]
            xs = _outproj("mlstm", feats, ml_w_out[j], xs, mod_l, tc, head_g=0.5 * ml_head_g[j], **last)
        elif kind == 1:
            p = _inproj(xs, mod_l, norm_g[layer], _attn_weight(at_w_in[j]), tc, out_dtype=BF16)
            if isinstance(xs, tuple):
                xs = jnp.concatenate(xs, axis=1)
            xs = _attn_mixer(p, at_q_g[j], at_k_g[j], at_sink[j], at_w_out[j], xs, mod_l, tc, need_ctx)
            if last:
                xs = xs[:, tc:, :]
        elif kind == 2:
            if isinstance(xs, tuple):
                xs = jnp.concatenate(xs, axis=1)
            u = _conv_mixer(xs, mod_l, norm_g[layer], sc_w_in[j], sc_conv_w[j], sc_conv_b[j], tc)
            xs = _outproj("plain", [(u, u.shape[2], 0)], sc_w_out[j], xs, mod_l, tc, **last)
        else:
            hg = HG_HEADS * HG_DIM
            lb = pl.pallas_call(
                functools.partial(_hgrn_lb_kernel, layer=layer),
                out_shape=jax.ShapeDtypeStruct((2, hg), F32),
                name="hgrn_lb",
            )(hg_lb[j])
            w = hg_w_in[j]
            w_qiz = jnp.concatenate([0.5 * w[:, :hg], w[:, 3 * hg:4 * hg], 0.5 * w[:, 4 * hg:]], axis=1)
            p = _inproj(xs, mod_l, norm_g[layer], w_qiz, tc, tn=1024, out_dtype=BF16,
                        epilogue=((0, hg, "silu"), (hg, 2 * hg, "id"), (2 * hg, 3 * hg, "silu")))
            lb2 = lb.reshape(2 * hg)
            pf, f_min = _inproj(xs, mod_l, norm_g[layer], 0.5 * w[:, hg:3 * hg], tc, tn=hg,
                                epilogue=((0, 2 * hg, "fgate"),), block_min=True,
                                pars=(0.5 * hg_f_b[j].reshape(2 * hg), 0.5 * (1.0 + lb2), 0.5 * (1.0 - lb2)))
            n_blk = pf.shape[1] // ROW_BLOCK
            o_f = _hgrn_scan(p, pf, f_min[:, 0, :n_blk, 0], tc, rev=False)
            o_b = _hgrn_scan(p, pf, f_min[:, 1, :n_blk, 0], tc, rev=True)
            feats = [(o_f, hg, 0), (o_b, hg, 0), (p, hg, 2)]
            xs = _outproj("hgrn", feats, hg_w_out[j], xs, mod_l, tc, head_g=hg_head_g[j], **last)
    return xs
```

```python
import functools

import numpy as np
import jax
import jax.numpy as jnp
from jax import lax
from jax.experimental import pallas as pl
from jax.experimental.pallas import tpu as pltpu

F32 = jnp.float32
BF16 = jnp.bfloat16
EPS = 1e-6
DEPTH = 4
GRID_W = 64
ROPE_BASE = 10000.0

ML_HEADS, ML_DK, ML_DV = 4, 256, 512
ML_QK = ML_HEADS * ML_DK
ML_INNER = ML_HEADS * ML_DV
ML_CHUNK = 256

AT_HEADS, AT_KV_HEADS, AT_HEAD_DIM = 16, 4, 64
AT_GROUP = AT_HEADS // AT_KV_HEADS
AT_BLOCK = 128
AT_PROJ_BLOCKS = 8
AT_Q = AT_HEADS * AT_HEAD_DIM
AT_KV = AT_KV_HEADS * AT_HEAD_DIM

SC_KSIZE = 3

HG_HEADS, HG_DIM = 8, 128
HG_CHUNK = 128
HG_STEP_CHUNKS = 2

LANES = 128
SUBLANES = 8
VMEM_LIMIT_BYTES = 56 * 1024 * 1024

ROW_BLOCK = 256
PROJ_ROWS = 768
OUT_ROWS = 768
OUT_ROWS_SPLIT = 256
OUT_ROWS_LAT = 1024


def _cparams(*sem):
    return pltpu.CompilerParams(dimension_semantics=sem, vmem_limit_bytes=VMEM_LIMIT_BYTES)


def _sigmoid(x):
    return 0.5 * jnp.tanh(0.5 * x) + 0.5


def _silu(x):
    return x * _sigmoid(x)


def _silu_of_half(xh):
    return xh * (1.0 + jnp.tanh(xh))


def _ada_kernel(c_ref, w_ref, b_ref, o_ref):
    s = _silu(c_ref[...])
    o_ref[0] = jnp.dot(s.astype(BF16), w_ref[0].astype(BF16), preferred_element_type=F32) + b_ref[0]


def _ada_mod(c, c_ctx, ada_w, ada_b):
    b, d = c.shape
    depth = ada_w.shape[0]
    rows = -(-(b + 1) // SUBLANES) * SUBLANES
    cc = jnp.zeros((rows, d), F32).at[:b].set(c).at[b].set(c_ctx)
    tn = 1024
    out = pl.pallas_call(
        _ada_kernel,
        out_shape=jax.ShapeDtypeStruct((depth, rows, 3 * d), F32),
        grid=(depth, 3 * d // tn),
        in_specs=[pl.BlockSpec((rows, d), lambda l, j: (0, 0)),
                  pl.BlockSpec((1, d, tn), lambda l, j: (l, 0, j)),
                  pl.BlockSpec((1, 1, tn), lambda l, j: (l, 0, j))],
        out_specs=pl.BlockSpec((1, rows, tn), lambda l, j: (l, 0, j)),
        compiler_params=_cparams("parallel", "parallel"),
        name="ada_mod",
    )(cc, ada_w, ada_b.reshape(depth, 1, 3 * d))
    return out.reshape(depth, rows, 3, d)


def _modulated(x_refs, ml_ref, mc_ref, g_ref, h_ref, tc, row0=0, nrows=None):
    nrows = h_ref.shape[0] - row0 if nrows is None else nrows
    g = g_ref[...]
    xc_ref = x_refs[0]
    xl_ref, lat0 = (x_refs[1], 0) if len(x_refs) == 2 else (x_refs[0], tc)
    for r0 in range(row0, row0 + nrows, ROW_BLOCK):
        ctx_rows = r0 < tc
        x_ref, src0, m_ref = (xc_ref, r0, mc_ref) if ctx_rows else (xl_ref, lat0 + r0 - tc, ml_ref)
        x = x_ref[0, src0:src0 + ROW_BLOCK, :]
        ms = jnp.mean(x * x, axis=-1, keepdims=True)
        xn = x * lax.rsqrt(ms + EPS) * g
        h = xn * (1.0 + m_ref[0, 1:2, :]) + m_ref[0, 0:1, :]
        h_ref[r0:r0 + ROW_BLOCK, :] = h.astype(BF16)


def _split_stream(xs):
    arrs = list(xs) if isinstance(xs, tuple) else [xs]
    specs = [pl.BlockSpec((1,) + a.shape[1:], lambda i, j: (i, 0, 0)) for a in arrs]
    ta = sum(a.shape[1] for a in arrs)
    return arrs, specs, ta


def _inproj_kernel(*refs, tc, n_x, has_small, epilogue, w_rows_out, block_min):
    x_refs, (ml_ref, mc_ref, g_ref, w_ref), rest = refs[:n_x], refs[n_x:n_x + 4], list(refs[n_x + 4:])
    ws_ref = rest.pop(0) if has_small else None
    par_refs = [rest.pop(0) for _ in range(3)] if epilogue else None
    o_ref = rest.pop(0)
    os_ref = rest.pop(0) if has_small else None
    mn_ref = rest.pop(0) if block_min else None
    h_ref = rest.pop(0)

    ta = h_ref.shape[0]
    chunk = PROJ_ROWS if ta % PROJ_ROWS == 0 else ta
    j = pl.program_id(1)
    w_dims = (((1,), (1 if w_rows_out else 0,)), ((), ()))

    def activation(acc):
        if not epilogue:
            return acc
        bias_ref, c0_ref, c1_ref = par_refs
        t = jnp.tanh(acc + bias_ref[...])
        out = acc
        for lo, hi, kind in epilogue:
            if kind == "id":
                continue
            val = acc + acc * t if kind == "silu" else c0_ref[...] + c1_ref[...] * t
            out = jnp.where((j >= lo) & (j < hi), val, out)
        return out

    def project(first):
        step = chunk if (first or epilogue) else ta
        mins = []
        for r0 in range(0, ta, step):
            if first:
                _modulated(x_refs, ml_ref, mc_ref, g_ref, h_ref, tc, r0, step)
            acc = lax.dot_general(h_ref[r0:r0 + step, :], w_ref[...], w_dims, preferred_element_type=F32)
            out = activation(acc)
            o_ref[0, r0:r0 + step, :] = out.astype(o_ref.dtype)
            if mn_ref is not None:
                for b0 in range(0, step, ROW_BLOCK):
                    m = jnp.min(out[b0:b0 + ROW_BLOCK], axis=0, keepdims=True)
                    mins.append(jnp.broadcast_to(jnp.min(m, axis=1, keepdims=True), (1, LANES)))
        if mn_ref is not None:
            pad = jnp.zeros((mn_ref.shape[2] - len(mins), LANES), F32)
            mn_ref[0, 0] = jnp.concatenate(mins + [pad], axis=0)

    @pl.when(j == 0)
    def _():
        project(True)
        if has_small:
            os_ref[0] = lax.dot_general(ws_ref[...], h_ref[...], (((1,), (1,)), ((), ())),
                                        preferred_element_type=F32)

    @pl.when(j != 0)
    def _():
        project(False)


def _inproj(xs, mod_l, norm_g, w, tc, tn=512, w_small=None, out_dtype=F32, epilogue=None, pars=None,
            w_rows_out=False, block_min=False):
    x_arrs, x_specs, ta = _split_stream(xs)
    b, d = x_arrs[0].shape[0], x_arrs[0].shape[2]
    n = w.shape[0] if w_rows_out else w.shape[1]
    assert n % tn == 0 and tc % ROW_BLOCK == 0 and (ta - tc) % ROW_BLOCK == 0
    mod_lat = mod_l[:b]
    mod_ctx = mod_l[b:b + 1]
    has_small = w_small is not None
    in_specs = x_specs + [pl.BlockSpec((1, 3, d), lambda i, j: (i, 0, 0)),
                          pl.BlockSpec((1, 3, d), lambda i, j: (0, 0, 0)),
                          pl.BlockSpec((1, d), lambda i, j: (0, 0)),
                          pl.BlockSpec((tn, d), lambda i, j: (j, 0)) if w_rows_out
                          else pl.BlockSpec((d, tn), lambda i, j: (0, j))]
    args = x_arrs + [mod_lat, mod_ctx, norm_g.reshape(1, d), w.astype(BF16)]
    out_shape = [jax.ShapeDtypeStruct((b, ta, n), out_dtype)]
    out_specs = [pl.BlockSpec((1, ta, tn), lambda i, j: (i, 0, j))]
    if has_small:
        ns = w_small.shape[0]
        in_specs.append(pl.BlockSpec((ns, d), lambda i, j: (0, 0)))
        args.append(w_small.astype(BF16))
    if epilogue:
        assert all(lo % tn == 0 and hi % tn == 0 for lo, hi, _ in epilogue)
        epilogue = tuple((lo // tn, hi // tn, kind) for lo, hi, kind in epilogue)
        zeros = jnp.zeros((n,), F32)
        for par in (pars if pars is not None else (zeros,) * 3):
            in_specs.append(pl.BlockSpec((1, tn), lambda i, j: (0, j)))
            args.append(par.reshape(1, n))
    if has_small:
        out_shape.append(jax.ShapeDtypeStruct((b, ns, ta), F32))
        out_specs.append(pl.BlockSpec((1, ns, ta), lambda i, j: (i, 0, 0)))
    if block_min:
        nblk = -(-(ta // ROW_BLOCK) // SUBLANES) * SUBLANES
        out_shape.append(jax.ShapeDtypeStruct((b, n // tn, nblk, LANES), F32))
        out_specs.append(pl.BlockSpec((1, 1, nblk, LANES), lambda i, j: (i, j, 0, 0)))
    res = pl.pallas_call(
        functools.partial(_inproj_kernel, tc=tc, n_x=len(x_arrs), has_small=has_small, epilogue=epilogue,
                          w_rows_out=w_rows_out, block_min=block_min),
        out_shape=out_shape,
        grid=(b, n // tn),
        in_specs=in_specs,
        out_specs=out_specs,
        scratch_shapes=[pltpu.VMEM((ta, d), BF16)],
        compiler_params=_cparams("parallel", "arbitrary"),
        name="inproj",
    )(*args)
    return res if (has_small or block_min) else res[0]


def _head_rms(h, gain, n_heads):
    dh = h.shape[1] // n_heads
    parts = []
    for i in range(n_heads):
        hh = h[:, i * dh:(i + 1) * dh]
        ms = jnp.mean(hh * hh, axis=-1, keepdims=True)
        parts.append(hh * lax.rsqrt(ms + EPS))
    return jnp.concatenate(parts, axis=1) * gain


def _outproj_kernel(*refs, mode, n_feat, n_x, tc, tm, row0):
    feats, rest = refs[:n_feat], list(refs[n_feat:])
    hg_ref = rest.pop(0) if mode != "plain" else None
    w_ref = rest.pop(0)
    x_refs = [rest.pop(0) for _ in range(n_x)]
    ml_ref, mc_ref, o_ref = rest
    if mode == "plain":
        u = feats[0][0]
    elif mode == "mlstm":
        h0_ref, h1_ref, og_ref, z_ref = feats
        hn = _head_rms(h0_ref[0].astype(F32) + h1_ref[0].astype(F32), hg_ref[...], ML_HEADS)
        u = (hn * (1.0 + jnp.tanh(og_ref[0].astype(F32))) * _silu_of_half(z_ref[0].astype(F32))).astype(BF16)
    else:
        h0_ref, h1_ref, z_ref = feats
        hn = _head_rms(h0_ref[0].astype(F32) + h1_ref[0].astype(F32), hg_ref[...], HG_HEADS)
        u = (hn * z_ref[0].astype(F32)).astype(BF16)
    y = jnp.dot(u, w_ref[...], preferred_element_type=F32)
    first = row0 + pl.program_id(1) * tm
    row = first + lax.broadcasted_iota(jnp.int32, (tm, 1), 0)
    gate = jnp.where(row < tc, mc_ref[0, 2:3, :], ml_ref[0, 2:3, :])
    x = x_refs[0][0] if n_x == 1 else jnp.where(first < tc, x_refs[0][0], x_refs[1][0])
    o_ref[0] = x + gate * y


def _outproj(mode, feats, w_out, xs, mod_l, tc, head_g=None, lat_only=False):
    x_arrs = list(xs) if isinstance(xs, tuple) else [xs]
    b, d = x_arrs[0].shape[0], x_arrs[0].shape[2]
    ta = sum(a.shape[1] for a in x_arrs)
    kdim = w_out.shape[0]
    if lat_only:
        tm = OUT_ROWS_LAT if (ta - tc) % OUT_ROWS_LAT == 0 else OUT_ROWS_SPLIT
    else:
        tm = OUT_ROWS if (len(x_arrs) == 1 and ta % OUT_ROWS == 0) else OUT_ROWS_SPLIT
        assert ta % tm == 0 and (tc % tm == 0 or len(x_arrs) == 1)
    row0 = tc if lat_only else 0
    nct = tc // tm
    row_spec = lambda width, col: pl.BlockSpec((pl.Element(1), pl.Element(tm), pl.Element(width)),
                                               lambda i, r, col=col: (i, pl.multiple_of(row0 + r * tm, ROW_BLOCK),
                                                                      col * width))
    in_specs, args = [], []
    for arr, width, col in feats:
        in_specs.append(row_spec(width, col))
        args.append(arr)
    if head_g is not None:
        in_specs.append(pl.BlockSpec((1, kdim), lambda i, r: (0, 0)))
        args.append(head_g.reshape(1, kdim))
    in_specs.append(pl.BlockSpec((kdim, d), lambda i, r: (0, 0)))
    args.append(w_out.astype(BF16))
    if len(x_arrs) == 1:
        in_specs.append(row_spec(d, 0))
    elif lat_only:
        x_arrs = x_arrs[1:]
        in_specs.append(pl.BlockSpec((1, tm, d), lambda i, r: (i, r, 0)))
    else:
        in_specs += [pl.BlockSpec((1, tm, d), lambda i, r: (i, jnp.minimum(r, nct - 1), 0)),
                     pl.BlockSpec((1, tm, d), lambda i, r: (i, jnp.maximum(r - nct, 0), 0))]
    args += x_arrs
    in_specs += [pl.BlockSpec((1, 3, d), lambda i, r: (i, 0, 0)),
                 pl.BlockSpec((1, 3, d), lambda i, r: (0, 0, 0))]
    args += [mod_l[:b], mod_l[b:b + 1]]
    rows_out = ta - row0
    return pl.pallas_call(
        functools.partial(_outproj_kernel, mode=mode, n_feat=len(feats), n_x=len(x_arrs), tc=tc, tm=tm,
                          row0=row0),
        out_shape=jax.ShapeDtypeStruct((b, rows_out, d), F32),
        grid=(b, rows_out // tm),
        in_specs=in_specs,
        out_specs=pl.BlockSpec((1, tm, d), lambda i, r: (i, r, 0)),
        compiler_params=_cparams("parallel", "parallel"),
        name="outproj_" + mode,
    )(*args)


def _conv_kernel(x_ref, ml_ref, mc_ref, g_ref, wx_ref, wb_ref, wc_ref, wz_ref, cw_ref, cb_ref, o_ref, h_ref, *, tc):
    ta = h_ref.shape[0]
    chunk = PROJ_ROWS if ta % PROJ_ROWS == 0 else ta

    def body(first_tile):
        wx, wc, wb, wz = (w_ref[...].astype(BF16) for w_ref in (wx_ref, wc_ref, wb_ref, wz_ref))
        step = chunk if first_tile else ta
        xin, cg = [], []
        for r0 in range(0, ta, step):
            if first_tile:
                _modulated((x_ref,), ml_ref, mc_ref, g_ref, h_ref, tc, r0, step)
            h = h_ref[r0:r0 + step, :]
            xin.append(jnp.dot(h, wx, preferred_element_type=F32))
            cg.append(jnp.dot(h, wc, preferred_element_type=F32))
        bg = jnp.dot(h_ref[...], wb, preferred_element_type=F32)
        z = jnp.dot(h_ref[...], wz, preferred_element_type=F32)
        u = jnp.concatenate(cg, axis=0) * jnp.concatenate(xin, axis=0)
        row = lax.broadcasted_iota(jnp.int32, (ta, 1), 0)
        first = (row == 0) | (row == tc)
        last = (row == tc - 1) | (row == ta - 1)
        u_prev = jnp.where(first, 0.0, pltpu.roll(u, 1, axis=0))
        u_next = jnp.where(last, 0.0, pltpu.roll(u, ta - 1, axis=0))
        cw = cw_ref[...]
        y = u_prev * cw[0:1, :] + u * cw[1:2, :] + u_next * cw[2:3, :] + cb_ref[...]
        o_ref[0] = (bg * y * _silu(z)).astype(o_ref.dtype)

    @pl.when(pl.program_id(1) == 0)
    def _():
        body(True)

    @pl.when(pl.program_id(1) != 0)
    def _():
        body(False)


def _conv_mixer(xs, mod_l, norm_g, w_in, conv_w, conv_b, tc, tw=256):
    b, ta, d = xs.shape
    e = conv_w.shape[1]
    nt = e // tw
    w_spec = lambda part: pl.BlockSpec((d, tw), lambda i, j, part=part: (0, part * nt + j))
    return pl.pallas_call(
        functools.partial(_conv_kernel, tc=tc),
        out_shape=jax.ShapeDtypeStruct((b, ta, e), BF16),
        grid=(b, nt),
        in_specs=[pl.BlockSpec((1, ta, d), lambda i, j: (i, 0, 0)),
                  pl.BlockSpec((1, 3, d), lambda i, j: (i, 0, 0)),
                  pl.BlockSpec((1, 3, d), lambda i, j: (0, 0, 0)),
                  pl.BlockSpec((1, d), lambda i, j: (0, 0)),
                  w_spec(0), w_spec(1), w_spec(2), w_spec(3),
                  pl.BlockSpec((SC_KSIZE, tw), lambda i, j: (0, j)),
                  pl.BlockSpec((1, tw), lambda i, j: (0, j))],
        out_specs=pl.BlockSpec((1, ta, tw), lambda i, j: (i, 0, j)),
        scratch_shapes=[pltpu.VMEM((ta, d), BF16)],
        compiler_params=_cparams("parallel", "arbitrary"),
        name="conv_mixer",
    )(xs, mod_l[:b], mod_l[b:b + 1], norm_g.reshape(1, d), w_in, w_in, w_in, w_in, conv_w, conv_b.reshape(1, e))


AT_HALF = AT_HEAD_DIM // 2
AT_TILE_PERM = np.concatenate([np.arange(0, AT_HALF), np.arange(2 * AT_HALF, 3 * AT_HALF),
                               np.arange(AT_HALF, 2 * AT_HALF), np.arange(3 * AT_HALF, 4 * AT_HALF)])
LOG2E = float(np.log2(np.e))


def _rope_tables(tc, t):
    rows = t // GRID_W
    row = np.repeat(np.arange(rows), GRID_W).astype(np.float64)
    col = np.tile(np.arange(GRID_W), rows).astype(np.float64)
    n_freq = AT_HEAD_DIM // 4
    freqs = np.power(ROPE_BASE, -np.arange(n_freq, dtype=np.float64) / n_freq)
    ang = np.concatenate([row[:, None] * freqs, col[:, None] * freqs], axis=-1)
    cos, sin = np.cos(ang), np.sin(ang)
    cos_t = np.concatenate([np.ones((tc, LANES)), np.tile(cos, (1, 4))], axis=0)
    sin_t = np.concatenate([np.zeros((tc, LANES)), np.concatenate([-sin, -sin, sin, sin], axis=-1)], axis=0)
    return jnp.asarray(cos_t, F32), jnp.asarray(sin_t, F32)


def _norm_rope_tiles(xs, gains, cos, sin, same_head):
    sums = []
    for x in xs:
        x2 = x * x
        hi = x2.astype(BF16)
        lo = (x2 - hi.astype(F32)).astype(BF16)
        sums.append((jnp.dot(hi, same_head, preferred_element_type=F32),
                     jnp.dot(lo, same_head, preferred_element_type=F32)))
    out = []
    for x, gain, (s_hi, s_lo) in zip(xs, gains, sums):
        xn = x * lax.rsqrt(s_hi + s_lo + EPS) * gain
        out.append(xn * cos + pltpu.roll(xn, 2 * AT_HALF, axis=1) * sin)
    return out


def _attn_kernel(sink_ref, q_ref, k_ref, v_ref, z_ref, qg_ref, kg_ref, cos_ref, sin_ref, sh_ref, wo_ref, x_ref,
                 ml_ref, mc_ref, o_ref, qs_ref, ka_ref, kb_ref, va_ref, vb_ref, *, tc, need_ctx):
    ta = q_ref.shape[1]
    t = ta - tc
    nb = t // AT_BLOCK
    blk = AT_BLOCK
    gw = x_ref.shape[2]
    g = pl.program_id(1)
    odd = (g % 2) == 1

    @pl.when(g == 0)
    def _():
        o_ref[...] = jnp.zeros(o_ref.shape, F32)

    def prep(i, carry):
        r0 = pl.multiple_of(i * ROW_BLOCK, ROW_BLOCK)
        res_cols = pl.ds(pl.multiple_of(g * gw, gw), gw)
        o_ref[0, pl.ds(r0, ROW_BLOCK), res_cols] = (o_ref[0, pl.ds(r0, ROW_BLOCK), res_cols]
                                                    + x_ref[0, pl.ds(r0, ROW_BLOCK), :])
        cos = cos_ref[pl.ds(r0, ROW_BLOCK), :]
        sin = sin_ref[pl.ds(r0, ROW_BLOCK), :]
        same_head = sh_ref[...]
        tiles = [q_ref[0, pl.ds(r0, ROW_BLOCK), c * LANES:(c + 1) * LANES].astype(F32) for c in range(2)]
        tiles.append(k_ref[0, pl.ds(r0, ROW_BLOCK), :].astype(F32))
        q0, q1, kn = _norm_rope_tiles(tiles, (qg_ref[...], qg_ref[...], kg_ref[...]), cos, sin, same_head)
        for c, qc in enumerate((q0, q1)):
            qs_ref[pl.ds(r0, ROW_BLOCK), c * LANES:(c + 1) * LANES] = qc.astype(BF16)
        vv = v_ref[0, pl.ds(r0, ROW_BLOCK), :].astype(F32)
        lane = lax.broadcasted_iota(jnp.int32, kn.shape, 1)
        k_own = jnp.where(((lane // AT_HALF) % 2) == (g % 2), kn, 0.0)
        k_oth = pltpu.roll(k_own, jnp.where(odd, 3 * AT_HALF, AT_HALF), axis=1)
        ka_ref[pl.ds(r0, ROW_BLOCK), :] = jnp.where(odd, k_oth, k_own).astype(BF16)
        kb_ref[pl.ds(r0, ROW_BLOCK), :] = jnp.where(odd, k_own, k_oth).astype(BF16)
        v_own = jnp.where((lane // AT_HEAD_DIM) == (g % 2), vv, 0.0)
        v_oth = pltpu.roll(v_own, AT_HEAD_DIM, axis=1)
        va = jnp.where(odd, v_oth, v_own)
        vb = jnp.where(odd, v_own, v_oth)
        va_ref[pl.ds(r0, ROW_BLOCK), :] = jnp.where(lane == AT_HEAD_DIM, 1.0, va).astype(BF16)
        vb_ref[pl.ds(r0, ROW_BLOCK), :] = jnp.where(lane == 0, 1.0, vb).astype(BF16)
        return carry

    lax.fori_loop(0, ta // ROW_BLOCK, prep, 0, unroll=3)
    zeros = jnp.zeros((blk, LANES), BF16)
    for ref in (ka_ref, kb_ref, va_ref, vb_ref):
        ref[ta:ta + blk, :] = zeros

    half = lax.broadcasted_iota(jnp.int32, (2 * blk, 1), 0) < blk
    sink_a = jnp.where(half, sink_ref[g, 0], sink_ref[g, 2]) * LOG2E
    sink_b = jnp.where(half, sink_ref[g, 1], sink_ref[g, 3]) * LOG2E

    qi = lax.broadcasted_iota(jnp.int32, (2 * blk, 3 * blk), 0) % blk
    kj = lax.broadcasted_iota(jnp.int32, (2 * blk, 3 * blk), 1)
    band = (kj - qi >= 0) & (kj - qi <= 2 * blk)
    out_lo = lax.broadcasted_iota(jnp.int32, (2 * blk, LANES), 1) < AT_HEAD_DIM

    nt = (((1,), (1,)), ((), ()))

    sides = ((ka_ref, va_ref, sink_a, AT_HEAD_DIM), (kb_ref, vb_ref, sink_b, 0))

    def attend(blocks, r_first, gate):
        qts = [jnp.concatenate([qs_ref[pl.ds(r0, blk), 0:LANES], qs_ref[pl.ds(r0, blk), LANES:2 * LANES]], axis=0)
               for r0, _, _ in blocks]
        scores = []
        for qt, (_, k0, mask) in zip(qts, blocks):
            for k_ref_, _, _, _ in sides:
                s_ctx = lax.dot_general(qt, k_ref_[0:tc, :], nt, preferred_element_type=F32)
                s_loc = None if mask is None else lax.dot_general(qt, k_ref_[pl.ds(k0, 3 * blk), :], nt,
                                                                  preferred_element_type=F32)
                scores.append((s_ctx, s_loc))
        probs = []
        for idx, (s_ctx, s_loc) in enumerate(scores):
            mask = blocks[idx // 2][2]
            sink = sides[idx % 2][2]
            m = jnp.maximum(sink, jnp.max(s_ctx, axis=-1, keepdims=True))
            if s_loc is not None:
                s_loc = jnp.where(mask, s_loc, -jnp.inf)
                m = jnp.maximum(m, jnp.max(s_loc, axis=-1, keepdims=True))
            probs.append((jnp.exp2(s_ctx - m).astype(BF16),
                          None if s_loc is None else jnp.exp2(s_loc - m).astype(BF16), jnp.exp2(sink - m)))
        accs = []
        for idx, (p_ctx, p_loc, _) in enumerate(probs):
            k0 = blocks[idx // 2][1]
            v_ref_ = sides[idx % 2][1]
            acc = jnp.dot(p_ctx, v_ref_[0:tc, :], preferred_element_type=F32)
            if p_loc is not None:
                acc = acc + jnp.dot(p_loc, v_ref_[pl.ds(k0, 3 * blk), :], preferred_element_type=F32)
            accs.append(acc)
        us = []
        for bi, (r0, _, _) in enumerate(blocks):
            halves = []
            for si in range(2):
                acc, ones_lane = accs[2 * bi + si], sides[si][3]
                halves.append(acc / (probs[2 * bi + si][2] + acc[:, ones_lane:ones_lane + 1]))
            o = jnp.where(out_lo, halves[0], halves[1])
            parts = []
            for c in range(2):
                zc = z_ref[0, pl.ds(r0, blk), c * LANES:(c + 1) * LANES].astype(F32)
                parts.append((o[c * blk:(c + 1) * blk] * _silu_of_half(zc)).astype(BF16))
            us.append(jnp.concatenate(parts, axis=1))
        u = jnp.concatenate(us, axis=0)
        rows = pl.ds(r_first, u.shape[0])
        o_ref[0, rows, :] = o_ref[0, rows, :] + gate * jnp.dot(u, wo_ref[...], preferred_element_type=F32)

    def lat_blocks(i, carry):
        blocks = []
        for j in range(AT_PROJ_BLOCKS):
            n = i * AT_PROJ_BLOCKS + j
            kpos = (n - 1) * blk + kj
            blocks.append((pl.multiple_of(tc + n * blk, blk), pl.multiple_of(tc + (n - 1) * blk, blk),
                           band & (kpos >= 0) & (kpos < t)))
        attend(blocks, pl.multiple_of(tc + i * AT_PROJ_BLOCKS * blk, AT_PROJ_BLOCKS * blk), ml_ref[0, 2:3, :])
        return carry

    lax.fori_loop(0, nb // AT_PROJ_BLOCKS, lat_blocks, 0)
    if need_ctx:
        attend([(n * blk, None, None) for n in range(tc // blk)], 0, mc_ref[0, 2:3, :])


def _attn_mixer(p, at_q_g, at_k_g, at_sink, w_out, xs, mod_l, tc, need_ctx):
    b, ta, _ = p.shape
    t = ta - tc
    d = xs.shape[2]
    assert tc >= AT_BLOCK and tc % AT_BLOCK == 0 and t % (AT_PROJ_BLOCKS * AT_BLOCK) == 0 and ta % ROW_BLOCK == 0
    assert d == AT_Q
    cos_t, sin_t = _rope_tables(tc, t)
    lane_head = (np.arange(LANES) // AT_HALF) % 2
    same_head = jnp.asarray((lane_head[:, None] == lane_head[None, :]) / AT_HEAD_DIM, BF16)
    q_scale = (AT_HEAD_DIM ** -0.5) * LOG2E
    tile_gain = lambda gain: jnp.tile(gain, 2)[AT_TILE_PERM].reshape(1, LANES)
    gw = AT_GROUP * AT_HEAD_DIM
    kcol = AT_Q // LANES
    vcol = (AT_Q + AT_KV) // LANES
    zcol = (AT_Q + 2 * AT_KV) // gw
    return pl.pallas_call(
        functools.partial(_attn_kernel, tc=tc, need_ctx=need_ctx),
        out_shape=jax.ShapeDtypeStruct((b, ta, d), F32),
        grid=(b, AT_KV_HEADS),
        in_specs=[pl.BlockSpec(memory_space=pltpu.SMEM),
                  pl.BlockSpec((1, ta, gw), lambda i, g: (i, 0, g)),
                  pl.BlockSpec((1, ta, LANES), lambda i, g: (i, 0, kcol + g // 2)),
                  pl.BlockSpec((1, ta, LANES), lambda i, g: (i, 0, vcol + g // 2)),
                  pl.BlockSpec((1, ta, gw), lambda i, g: (i, 0, zcol + g)),
                  pl.BlockSpec((1, LANES), lambda i, g: (0, 0)),
                  pl.BlockSpec((1, LANES), lambda i, g: (0, 0)),
                  pl.BlockSpec((ta, LANES), lambda i, g: (0, 0)),
                  pl.BlockSpec((ta, LANES), lambda i, g: (0, 0)),
                  pl.BlockSpec((LANES, LANES), lambda i, g: (0, 0)),
                  pl.BlockSpec((gw, d), lambda i, g: (g, 0)),
                  pl.BlockSpec((1, ta, gw), lambda i, g: (i, 0, g)),
                  pl.BlockSpec((1, 3, d), lambda i, g: (i, 0, 0)),
                  pl.BlockSpec((1, 3, d), lambda i, g: (0, 0, 0))],
        out_specs=pl.BlockSpec((1, ta, d), lambda i, g: (i, 0, 0)),
        scratch_shapes=[pltpu.VMEM((ta, gw), BF16)] + [pltpu.VMEM((ta + AT_BLOCK, LANES), BF16)] * 4,
        compiler_params=_cparams("parallel", "arbitrary"),
        name="attn_mixer",
    )(at_sink.reshape(AT_KV_HEADS, AT_GROUP), p, p, p, p, q_scale * tile_gain(at_q_g), tile_gain(at_k_g), cos_t, sin_t,
      same_head, w_out.astype(BF16), xs, mod_l[:b], mod_l[b:b + 1])


def _attn_weight_kernel(w_ref, p_ref, o_ref, *, n_qk, n_v):
    tiles = w_ref.shape[1] // LANES
    for lt in range(tiles):
        tile = pl.program_id(0) * tiles + lt
        kind = (tile >= n_qk).astype(jnp.int32) + (tile >= n_qk + n_v).astype(jnp.int32)
        cols = slice(lt * LANES, (lt + 1) * LANES)
        o_ref[:, cols] = jnp.dot(w_ref[:, cols].astype(BF16), p_ref[kind], preferred_element_type=F32).astype(BF16)


def _attn_weight(w_in):
    d, n = w_in.shape
    n_qk, n_v = (AT_Q + AT_KV) // LANES, AT_KV // LANES
    perm = np.zeros((LANES, LANES), np.float32)
    perm[AT_TILE_PERM, np.arange(LANES)] = 1.0
    eye = np.eye(LANES, dtype=np.float32)
    mats = jnp.asarray(np.stack([perm, eye, 0.5 * eye]), BF16)
    tw = 4 * LANES
    assert n % tw == 0
    return pl.pallas_call(
        functools.partial(_attn_weight_kernel, n_qk=n_qk, n_v=n_v),
        out_shape=jax.ShapeDtypeStruct((d, n), BF16),
        grid=(n // tw,),
        in_specs=[pl.BlockSpec((d, tw), lambda j: (0, j)),
                  pl.BlockSpec((3, LANES, LANES), lambda j: (0, 0, 0))],
        out_specs=pl.BlockSpec((d, tw), lambda j: (0, j)),
        compiler_params=_cparams("parallel"),
        name="attn_weight",
    )(w_in, mats)


def _chunk_order(nc, ncc, d, s):
    bwd = jnp.where(s < ncc, ncc - 1 - s, nc - 1 - (s - ncc))
    return jnp.where(d == 0, s, bwd)


ML_GATE_PERM = np.concatenate([np.arange(0, 4), np.arange(8, 12), np.arange(4, 8), np.arange(12, 16)])
ML_NQ = 6


def _mlstm_gate_kernel(g_ref, b_ref, a_ref, c_ref, *, tc, lc):
    h = ML_HEADS
    ta = g_ref.shape[2]
    nc, ncc = ta // lc, tc // lc
    x = g_ref[0] + b_ref[...]
    li = x[0:2 * h]
    lfp = x[2 * h:4 * h]
    lf = jnp.minimum(lfp, 0.0) - jnp.log1p(jnp.exp(-jnp.abs(lfp)))
    fwd = lax.broadcasted_iota(jnp.int32, (2 * h, ta), 0) < h
    fwd1 = lax.broadcasted_iota(jnp.int32, (2 * h, 1), 0) < h
    pos = lax.broadcasted_iota(jnp.int32, (2 * h, ta), 1) % lc

    def seg_scan(y, op, fill):
        yf, yb = y, y
        s = 1
        while s < lc:
            yf = op(yf, jnp.where(pos >= s, pltpu.roll(yf, s, axis=1), fill))
            yb = op(yb, jnp.where(pos < lc - s, pltpu.roll(yb, ta - s, axis=1), fill))
            s *= 2
        return jnp.where(fwd, yf, yb)

    bsum = seg_scan(lf, jnp.add, 0.0)
    a = li - bsum
    cmax = seg_scan(a, jnp.maximum, -jnp.inf)

    def end_col(y, c):
        return jnp.where(fwd1, y[:, (c + 1) * lc - 1:(c + 1) * lc], y[:, c * lc:c * lc + 1])

    tot = [end_col(bsum, c) for c in range(nc)]
    amax = [end_col(cmax, c) for c in range(nc)]

    def chain(order):
        m = jnp.zeros((2 * h, 1), F32)
        m_in = [None] * nc
        for c in order:
            m_in[c] = m
            m = tot[c] + jnp.maximum(m, amax[c])
        return m_in

    m_f = chain(list(range(nc)))
    m_b = chain(list(range(ncc - 1, -1, -1)) + list(range(nc - 1, ncc - 1, -1)))
    for c in range(nc):
        m_in = jnp.where(fwd1, m_f[c], m_b[c])
        sl = slice(c * lc, (c + 1) * lc)
        g_run = jnp.maximum(m_in, cmax[:, sl])
        g_end = jnp.maximum(m_in, amax[c])
        nr = 2 * h
        quantities = [a[:, sl], g_run, jnp.exp(m_in - g_run), jnp.exp(-(bsum[:, sl] + g_run)),
                      jnp.exp(a[:, sl] - g_end), jnp.broadcast_to(jnp.exp(m_in - g_end), (nr, lc))]
        a_ref[0, c] = quantities[0]
        pad = jnp.zeros((LANES - ML_NQ * nr, lc), F32)
        c_ref[0, c] = jnp.concatenate(quantities + [pad], axis=0).T


def _mlstm_scan_kernel(q_ref, k_ref, v_ref, a_ref, c_ref, o_ref, cs_ref, *, rev):
    lc = q_ref.shape[1]
    nr = 2 * ML_HEADS

    @pl.when(pl.program_id(1) == 0)
    def _():
        cs_ref[...] = jnp.zeros(cs_ref.shape, F32)

    ti = lax.broadcasted_iota(jnp.int32, (lc, lc), 0)
    si = lax.broadcasted_iota(jnp.int32, (lc, lc), 1)
    mask = (si >= ti) if rev else (si <= ti)
    ones = jnp.ones((lc, LANES), BF16)
    nt = (((1,), (1,)), ((), ()))
    tn = (((0,), (0,)), ((), ()))
    heads = range(ML_HEADS)
    qb = [q_ref[0, :, h * ML_DK:(h + 1) * ML_DK] for h in heads]
    kb = [k_ref[0, :, h * ML_DK:(h + 1) * ML_DK] for h in heads]
    v_aug = [jnp.concatenate([v_ref[0, :, h * ML_DV:(h + 1) * ML_DV], ones], axis=1) for h in heads]
    cols = []
    for h in heads:
        r = (ML_HEADS if rev else 0) + h
        cols.append([a_ref[0, 0, r:r + 1, :]] + [c_ref[0, 0, :, qi * nr + r:qi * nr + r + 1] for qi in range(1, 6)])
    qk = [lax.dot_general(qb[h], kb[h], nt, preferred_element_type=F32) for h in heads]
    c_old = [cs_ref[h] for h in heads]
    q_c = [jnp.dot(qb[h], c_old[h].astype(BF16), preferred_element_type=F32) for h in heads]
    kw = [(kb[h].astype(F32) * cols[h][4]).astype(BF16) for h in heads]
    upd = [lax.dot_general(kw[h], v_aug[h], tn, preferred_element_type=F32) for h in heads]
    s = []
    for h in heads:
        a_row, g_run = cols[h][0], cols[h][1]
        s.append((qk[h] * jnp.where(mask, jnp.exp(a_row - g_run), 0.0)).astype(BF16))
    s_v = [jnp.dot(s[h], v_aug[h], preferred_element_type=F32) for h in heads]
    for h in heads:
        inter, eclamp, decay = cols[h][2], cols[h][3], cols[h][5]
        tot = inter * q_c[h] + s_v[h]
        inv = 1.0 / jnp.maximum(jnp.abs(tot[:, ML_DV:]), eclamp)
        o_ref[0, :, h * ML_DV:(h + 1) * ML_DV] = (
            tot[:, :ML_DV] * jnp.concatenate([inv] * (ML_DV // LANES), axis=1)).astype(o_ref.dtype)
        cs_ref[h] = decay[0:1, :] * c_old[h] + upd[h]


def _mlstm_scan(p, gates, gate_b, tc):
    b, ta, _ = p.shape
    lc = ML_CHUNK
    assert tc % lc == 0 and ta % lc == 0
    nc, ncc = ta // lc, tc // lc
    ng = 4 * ML_HEADS
    nr = 2 * ML_HEADS
    bias = gate_b.reshape(ng)[ML_GATE_PERM].reshape(ng, 1)
    a_rows, cols = pl.pallas_call(
        functools.partial(_mlstm_gate_kernel, tc=tc, lc=lc),
        out_shape=[jax.ShapeDtypeStruct((b, nc, nr, lc), F32), jax.ShapeDtypeStruct((b, nc, lc, LANES), F32)],
        grid=(b,),
        in_specs=[pl.BlockSpec((1, ng, ta), lambda i: (i, 0, 0)),
                  pl.BlockSpec((ng, 1), lambda i: (0, 0))],
        out_specs=[pl.BlockSpec((1, nc, nr, lc), lambda i: (i, 0, 0, 0)),
                   pl.BlockSpec((1, nc, lc, LANES), lambda i: (i, 0, 0, 0))],
        compiler_params=_cparams("parallel"),
        name="mlstm_gates",
    )(gates, bias)
    outs = []
    for d in range(2):
        chunk = functools.partial(_chunk_order, nc, ncc, d)
        outs.append(pl.pallas_call(
            functools.partial(_mlstm_scan_kernel, rev=bool(d)),
            out_shape=jax.ShapeDtypeStruct((b, ta, ML_INNER), BF16),
            grid=(b, nc),
            in_specs=[pl.BlockSpec((1, lc, ML_QK), lambda i, s, chunk=chunk: (i, chunk(s), 0)),
                      pl.BlockSpec((1, lc, ML_QK), lambda i, s, chunk=chunk: (i, chunk(s), 1)),
                      pl.BlockSpec((1, lc, ML_INNER), lambda i, s, chunk=chunk: (i, chunk(s), 1)),
                      pl.BlockSpec((1, 1, nr, lc), lambda i, s, chunk=chunk: (i, chunk(s), 0, 0)),
                      pl.BlockSpec((1, 1, lc, LANES), lambda i, s, chunk=chunk: (i, chunk(s), 0, 0))],
            out_specs=pl.BlockSpec((1, lc, ML_INNER), lambda i, s, chunk=chunk: (i, chunk(s), 0)),
            scratch_shapes=[pltpu.VMEM((ML_HEADS, ML_DK, ML_DV + LANES), F32)],
            compiler_params=_cparams("parallel", "arbitrary"),
            name="mlstm_scan_bwd" if d else "mlstm_scan_fwd",
        )(p, p, p, a_rows, cols))
    return outs


def _hgrn_lb_kernel(p_ref, o_ref, *, layer):
    for d in range(p_ref.shape[0]):
        x = p_ref[d]
        e = jnp.exp(x - jnp.max(x, axis=0, keepdims=True))
        p = e / jnp.sum(e, axis=0, keepdims=True)
        acc = jnp.zeros((1, x.shape[1]), F32)
        for j in range(1, layer + 1):
            acc = acc + p[j:j + 1, :]
        o_ref[d:d + 1, :] = acc


def _cumsum_rows(x, tri2):
    hi = x.astype(BF16)
    lo = (x - hi.astype(F32)).astype(BF16)
    return jnp.dot(tri2, jnp.concatenate([hi, lo], axis=0), preferred_element_type=F32)


def _tri2(n, rev):
    t = np.triu(np.ones((n, n), np.float32)) if rev else np.tril(np.ones((n, n), np.float32))
    return jnp.asarray(np.concatenate([t, t], axis=1), BF16)


def _anchor_rows(a, m, rev):
    n, f = a.shape
    idx = m if rev else m - 1
    if 2 * m >= SUBLANES:
        a3 = a.reshape(n // (2 * m), 2 * m, f)
        return jnp.broadcast_to(a3[:, idx:idx + 1, :], a3.shape).reshape(n, f)
    a3 = a.reshape(n // SUBLANES, SUBLANES, f)
    sub = lax.broadcasted_iota(jnp.int32, a3.shape, 1)
    out = None
    for gi in range(SUBLANES // (2 * m) - 1, -1, -1):
        cand = jnp.broadcast_to(a3[:, gi * 2 * m + idx:gi * 2 * m + idx + 1, :], a3.shape)
        out = cand if out is None else jnp.where(sub < (gi + 1) * 2 * m, cand, out)
    return out.reshape(n, f)


HG_MAX_LOG_SPAN = 64.0


def _hgrn_scan_kernel(fmin_ref, q_ref, f_ref, i_ref, tri_ref, o_ref, s_ref, *, rev, nc, ncc):
    lc = HG_CHUNK
    subs = range(q_ref.shape[1] // lc)
    subs = tuple(reversed(subs)) if rev else tuple(subs)
    step_block = _chunk_order(nc, ncc, 1 if rev else 0, pl.program_id(1))

    @pl.when(pl.program_id(1) == 0)
    def _():
        s_ref[...] = jnp.zeros(s_ref.shape, F32)

    row = lax.broadcasted_iota(jnp.int32, (lc, 1), 0)
    ti = lax.broadcasted_iota(jnp.int32, (lc, lc), 0)
    si = lax.broadcasted_iota(jnp.int32, (lc, lc), 1)
    nt = (((1,), (1,)), ((), ()))
    tn = (((0,), (0,)), ((), ()))
    end = 0 if rev else lc - 1

    def all_heads_single_anchor():
        causal = (si >= ti) if rev else (si <= ti)
        cols = [slice(h * HG_DIM, (h + 1) * HG_DIM) for h in range(HG_HEADS)]
        pre = {}
        for ci in subs:
            rows = slice(ci * lc, (ci + 1) * lc)
            f = f_ref[0, rows, :]
            a = _cumsum_rows(jnp.log(f), tri_ref[...])
            q_dec = (q_ref[0, rows, :].astype(F32) * jnp.exp(a)).astype(BF16)
            k_inv = (1.0 - f) * jnp.exp(-a)
            k_inv_b = k_inv.astype(BF16)
            e_end = jnp.exp(a[end:end + 1, :])
            kd = (k_inv * e_end).astype(BF16)
            attn = [jnp.where(causal, lax.dot_general(q_dec[:, c], k_inv_b[:, c], nt, preferred_element_type=F32),
                              0.0).astype(BF16) for c in cols]
            pre[ci] = (rows, q_dec, e_end, kd, attn)
        state = [s_ref[h] for h in range(HG_HEADS)]
        for ci in subs:
            rows, q_dec, e_end, kd, attn = pre[ci]
            for h, c in enumerate(cols):
                o = jnp.dot(attn[h], i_ref[0, rows, c], preferred_element_type=F32)
                o = o + lax.dot_general(q_dec[:, c], state[h].astype(BF16), nt, preferred_element_type=F32)
                o_ref[0, rows, c] = o.astype(o_ref.dtype)
            state = [state[h] * e_end[:, c] + lax.dot_general(i_ref[0, rows, c], kd[:, c], tn,
                                                              preferred_element_type=F32)
                     for h, c in enumerate(cols)]
        for h in range(HG_HEADS):
            s_ref[h] = state[h]

    def head_per_level(r0, h, carry):
        c0 = pl.multiple_of(h * HG_DIM, HG_DIM)
        q = q_ref[0, r0:r0 + lc, pl.ds(c0, HG_DIM)].astype(F32)
        f = f_ref[0, r0:r0 + lc, pl.ds(c0, HG_DIM)]
        k = 1.0 - f
        iv = i_ref[0, r0:r0 + lc, pl.ds(c0, HG_DIM)].astype(F32)
        a = _cumsum_rows(jnp.log(f), tri_ref[...])
        attn = jnp.zeros((lc, lc), F32)
        m = 1
        while m < lc:
            e = jnp.exp(-jnp.abs(a - _anchor_rows(a, m, rev)))
            upper = (row % (2 * m)) >= m
            is_q = jnp.logical_not(upper) if rev else upper
            qt = jnp.where(is_q, q * e, 0.0).astype(BF16)
            kt = jnp.where(is_q, 0.0, k * e).astype(BF16)
            pair = lax.dot_general(qt, kt, nt, preferred_element_type=F32)
            attn = attn + jnp.where((ti // (2 * m)) == (si // (2 * m)), pair, 0.0)
            m *= 2
        a_end = a[end:end + 1, :]
        kd = (k * jnp.exp(a_end - a)).astype(BF16)
        ib = iv.astype(BF16)
        s_old = s_ref[h]
        o = jnp.dot(attn.astype(BF16), ib, preferred_element_type=F32)
        o = o + lax.dot_general((q * jnp.exp(a)).astype(BF16), s_old.astype(BF16), nt, preferred_element_type=F32)
        o_ref[0, r0:r0 + lc, pl.ds(c0, HG_DIM)] = (o + jnp.sum(q * k, axis=-1, keepdims=True) * iv).astype(o_ref.dtype)
        s_ref[h] = s_old * jnp.exp(a_end) + lax.dot_general(ib, kd, tn, preferred_element_type=F32)
        return carry

    in_range = fmin_ref[pl.program_id(0), step_block] >= float(np.exp(-HG_MAX_LOG_SPAN / lc))

    @pl.when(in_range)
    def _():
        all_heads_single_anchor()

    @pl.when(jnp.logical_not(in_range))
    def _():
        for ci in subs:
            lax.fori_loop(0, HG_HEADS, functools.partial(head_per_level, ci * lc), 0)


def _hgrn_scan(p, pf, f_min, tc, rev):
    b, ta, _ = p.shape
    lc = HG_CHUNK
    rows = HG_STEP_CHUNKS * lc
    hg = HG_HEADS * HG_DIM
    assert tc % rows == 0 and ta % rows == 0 and rows == ROW_BLOCK
    nc, ncc = ta // rows, tc // rows
    d = 1 if rev else 0
    chunk = lambda s: _chunk_order(nc, ncc, d, s)
    return pl.pallas_call(
        functools.partial(_hgrn_scan_kernel, rev=rev, nc=nc, ncc=ncc),
        out_shape=jax.ShapeDtypeStruct((b, ta, hg), BF16),
        grid_spec=pltpu.PrefetchScalarGridSpec(
            num_scalar_prefetch=1,
            grid=(b, nc),
            in_specs=[pl.BlockSpec((1, rows, hg), lambda i, s, fm: (i, chunk(s), 0)),
                      pl.BlockSpec((1, rows, hg), lambda i, s, fm: (i, chunk(s), d)),
                      pl.BlockSpec((1, rows, hg), lambda i, s, fm: (i, chunk(s), 1)),
                      pl.BlockSpec((lc, 2 * lc), lambda i, s, fm: (0, 0))],
            out_specs=pl.BlockSpec((1, rows, hg), lambda i, s, fm: (i, chunk(s), 0)),
            scratch_shapes=[pltpu.VMEM((HG_HEADS, HG_DIM, HG_DIM), F32)]),
        compiler_params=_cparams("parallel", "arbitrary"),
        name="hgrn_scan_bwd" if rev else "hgrn_scan_fwd",
    )(f_min, p, pf, p, _tri2(lc, rev))


def kernel(x, c, ctx, c_ctx, ada_w, ada_b, norm_g, ml_w_in, ml_gate_b, ml_head_g, ml_w_out, at_w_in, at_q_g, at_k_g, at_sink, at_w_out, sc_w_in, sc_conv_w, sc_conv_b, sc_w_out, hg_w_in, hg_f_b, hg_lb, hg_head_g, hg_w_out):
    tc = ctx.shape[1]
    mod = _ada_mod(c, c_ctx, ada_w, ada_b)
    xs = (ctx, x)
    for layer in range(DEPTH):
        kind, j = layer % 4, layer // 4
        need_ctx = layer < DEPTH - 1
        last = dict(lat_only=True) if layer == DEPTH - 1 else {}
        mod_l = mod[layer]
        if kind == 0:
            n_main = 2 * ML_QK + 3 * ML_INNER
            w_t = ml_w_in[j].T
            w_gate = jnp.zeros((LANES, ml_w_in.shape[1]), F32).at[:4 * ML_HEADS].set(w_t[n_main:][ML_GATE_PERM])
            col_scale = jnp.concatenate([jnp.full((ML_QK,), ML_DK ** -0.5, F32), jnp.ones((ML_QK + ML_INNER,), F32),
                                         jnp.full((2 * ML_INNER,), 0.5, F32)])
            p, gates = _inproj(xs, mod_l, norm_g[layer], w_t[:n_main] * col_scale[:, None], tc, tn=1024,
                               w_small=w_gate, out_dtype=BF16, w_rows_out=True)
            h_f, h_b = _mlstm_scan(p, gates, ml_gate_b[j], tc)
            feats = [(h_f, ML_INNER, 0), (h_b, ML_INNER, 0), (p, ML_INNER, 2), (p, ML_INNER, 3)]
            xs = _outproj("mlstm", feats, ml_w_out[j], xs, mod_l, tc, head_g=0.5 * ml_head_g[j], **last)
        elif kind == 1:
            p = _inproj(xs, mod_l, norm_g[layer], _attn_weight(at_w_in[j]), tc, tn=AT_Q + AT_KV, out_dtype=BF16)
            if isinstance(xs, tuple):
                xs = jnp.concatenate(xs, axis=1)
            xs = _attn_mixer(p, at_q_g[j], at_k_g[j], at_sink[j], at_w_out[j], xs, mod_l, tc, need_ctx)
            if last:
                xs = xs[:, tc:, :]
        elif kind == 2:
            if isinstance(xs, tuple):
                xs = jnp.concatenate(xs, axis=1)
            u = _conv_mixer(xs, mod_l, norm_g[layer], sc_w_in[j], sc_conv_w[j], sc_conv_b[j], tc)
            xs = _outproj("plain", [(u, u.shape[2], 0)], sc_w_out[j], xs, mod_l, tc, **last)
        else:
            hg = HG_HEADS * HG_DIM
            lb = pl.pallas_call(
                functools.partial(_hgrn_lb_kernel, layer=layer),
                out_shape=jax.ShapeDtypeStruct((2, hg), F32),
                name="hgrn_lb",
            )(hg_lb[j])
            w = hg_w_in[j]
            w_qiz = jnp.concatenate([0.5 * w[:, :hg], w[:, 3 * hg:4 * hg], 0.5 * w[:, 4 * hg:]], axis=1)
            p = _inproj(xs, mod_l, norm_g[layer], w_qiz, tc, tn=1024, out_dtype=BF16,
                        epilogue=((0, hg, "silu"), (hg, 2 * hg, "id"), (2 * hg, 3 * hg, "silu")))
            lb2 = lb.reshape(2 * hg)
            pf, f_min = _inproj(xs, mod_l, norm_g[layer], 0.5 * w[:, hg:3 * hg], tc, tn=hg,
                                epilogue=((0, 2 * hg, "fgate"),), block_min=True,
                                pars=(0.5 * hg_f_b[j].reshape(2 * hg), 0.5 * (1.0 + lb2), 0.5 * (1.0 - lb2)))
            n_blk = pf.shape[1] // ROW_BLOCK
            o_f = _hgrn_scan(p, pf, f_min[:, 0, :n_blk, 0], tc, rev=False)
            o_b = _hgrn_scan(p, pf, f_min[:, 1, :n_blk, 0], tc, rev=True)
            feats = [(o_f, hg, 0), (o_b, hg, 0), (p, hg, 2)]
            xs = _outproj("hgrn", feats, hg_w_out[j], xs, mod_l, tc, head_g=hg_head_g[j], **last)
    return xs
```

```python
import functools

import numpy as np
import jax
import jax.numpy as jnp
from jax import lax
from jax.experimental import pallas as pl
from jax.experimental.pallas import tpu as pltpu

F32 = jnp.float32
BF16 = jnp.bfloat16
EPS = 1e-6
DEPTH = 4
GRID_W = 64
ROPE_BASE = 10000.0

ML_HEADS, ML_DK, ML_DV = 4, 256, 512
ML_QK = ML_HEADS * ML_DK
ML_INNER = ML_HEADS * ML_DV
ML_CHUNK = 256

AT_HEADS, AT_KV_HEADS, AT_HEAD_DIM = 16, 4, 64
AT_GROUP = AT_HEADS // AT_KV_HEADS
AT_BLOCK = 128
AT_PROJ_BLOCKS = 8
AT_Q = AT_HEADS * AT_HEAD_DIM
AT_KV = AT_KV_HEADS * AT_HEAD_DIM

SC_KSIZE = 3

HG_HEADS, HG_DIM = 8, 128
HG_CHUNK = 128
HG_STEP_CHUNKS = 2

LANES = 128
SUBLANES = 8
VMEM_LIMIT_BYTES = 56 * 1024 * 1024

ROW_BLOCK = 256
PROJ_ROWS = 768
OUT_ROWS = 768
OUT_ROWS_SPLIT = 256
OUT_ROWS_LAT = 1024


def _cparams(*sem):
    return pltpu.CompilerParams(dimension_semantics=sem, vmem_limit_bytes=VMEM_LIMIT_BYTES)


def _sigmoid(x):
    return 0.5 * jnp.tanh(0.5 * x) + 0.5


def _silu(x):
    return x * _sigmoid(x)


def _silu_of_half(xh):
    return xh * (1.0 + jnp.tanh(xh))


def _ada_kernel(c_ref, w_ref, b_ref, o_ref):
    s = _silu(c_ref[...])
    o_ref[0] = jnp.dot(s.astype(BF16), w_ref[0].astype(BF16), preferred_element_type=F32) + b_ref[0]


def _ada_mod(c, c_ctx, ada_w, ada_b):
    b, d = c.shape
    depth = ada_w.shape[0]
    rows = -(-(b + 1) // SUBLANES) * SUBLANES
    cc = jnp.zeros((rows, d), F32).at[:b].set(c).at[b].set(c_ctx)
    tn = 1024
    out = pl.pallas_call(
        _ada_kernel,
        out_shape=jax.ShapeDtypeStruct((depth, rows, 3 * d), F32),
        grid=(depth, 3 * d // tn),
        in_specs=[pl.BlockSpec((rows, d), lambda l, j: (0, 0)),
                  pl.BlockSpec((1, d, tn), lambda l, j: (l, 0, j)),
                  pl.BlockSpec((1, 1, tn), lambda l, j: (l, 0, j))],
        out_specs=pl.BlockSpec((1, rows, tn), lambda l, j: (l, 0, j)),
        compiler_params=_cparams("parallel", "parallel"),
        name="ada_mod",
    )(cc, ada_w, ada_b.reshape(depth, 1, 3 * d))
    return out.reshape(depth, rows, 3, d)


def _modulated(x_refs, ml_ref, mc_ref, g_ref, h_ref, tc, row0=0, nrows=None):
    nrows = h_ref.shape[0] - row0 if nrows is None else nrows
    g = g_ref[...]
    xc_ref = x_refs[0]
    xl_ref, lat0 = (x_refs[1], 0) if len(x_refs) == 2 else (x_refs[0], tc)
    for r0 in range(row0, row0 + nrows, ROW_BLOCK):
        ctx_rows = r0 < tc
        x_ref, src0, m_ref = (xc_ref, r0, mc_ref) if ctx_rows else (xl_ref, lat0 + r0 - tc, ml_ref)
        x = x_ref[0, src0:src0 + ROW_BLOCK, :]
        ms = jnp.mean(x * x, axis=-1, keepdims=True)
        xn = x * lax.rsqrt(ms + EPS) * g
        h = xn * (1.0 + m_ref[0, 1:2, :]) + m_ref[0, 0:1, :]
        h_ref[r0:r0 + ROW_BLOCK, :] = h.astype(BF16)


def _split_stream(xs):
    arrs = list(xs) if isinstance(xs, tuple) else [xs]
    specs = [pl.BlockSpec((1,) + a.shape[1:], lambda i, j: (i, 0, 0)) for a in arrs]
    ta = sum(a.shape[1] for a in arrs)
    return arrs, specs, ta


def _inproj_kernel(*refs, tc, n_x, has_small, epilogue, w_rows_out, block_min):
    x_refs, (ml_ref, mc_ref, g_ref, w_ref), rest = refs[:n_x], refs[n_x:n_x + 4], list(refs[n_x + 4:])
    ws_ref = rest.pop(0) if has_small else None
    par_refs = [rest.pop(0) for _ in range(3)] if epilogue else None
    o_ref = rest.pop(0)
    os_ref = rest.pop(0) if has_small else None
    mn_ref = rest.pop(0) if block_min else None
    h_ref = rest.pop(0)

    ta = h_ref.shape[0]
    chunk = PROJ_ROWS if ta % PROJ_ROWS == 0 else ta
    j = pl.program_id(1)
    w_dims = (((1,), (1 if w_rows_out else 0,)), ((), ()))

    def activation(acc):
        if not epilogue:
            return acc
        bias_ref, c0_ref, c1_ref = par_refs
        t = jnp.tanh(acc + bias_ref[...])
        out = acc
        for lo, hi, kind in epilogue:
            if kind == "id":
                continue
            val = acc + acc * t if kind == "silu" else c0_ref[...] + c1_ref[...] * t
            out = jnp.where((j >= lo) & (j < hi), val, out)
        return out

    def project(first):
        step = chunk if (first or epilogue) else ta
        mins = []
        for r0 in range(0, ta, step):
            if first:
                _modulated(x_refs, ml_ref, mc_ref, g_ref, h_ref, tc, r0, step)
            acc = lax.dot_general(h_ref[r0:r0 + step, :], w_ref[...], w_dims, preferred_element_type=F32)
            out = activation(acc)
            o_ref[0, r0:r0 + step, :] = out.astype(o_ref.dtype)
            if mn_ref is not None:
                for b0 in range(0, step, ROW_BLOCK):
                    m = jnp.min(out[b0:b0 + ROW_BLOCK], axis=0, keepdims=True)
                    mins.append(jnp.broadcast_to(jnp.min(m, axis=1, keepdims=True), (1, LANES)))
        if mn_ref is not None:
            pad = jnp.zeros((mn_ref.shape[2] - len(mins), LANES), F32)
            mn_ref[0, 0] = jnp.concatenate(mins + [pad], axis=0)

    @pl.when(j == 0)
    def _():
        project(True)
        if has_small:
            os_ref[0] = lax.dot_general(ws_ref[...], h_ref[...], (((1,), (1,)), ((), ())),
                                        preferred_element_type=F32)

    @pl.when(j != 0)
    def _():
        project(False)


def _inproj(xs, mod_l, norm_g, w, tc, tn=512, w_small=None, out_dtype=F32, epilogue=None, pars=None,
            w_rows_out=False, block_min=False):
    x_arrs, x_specs, ta = _split_stream(xs)
    b, d = x_arrs[0].shape[0], x_arrs[0].shape[2]
    n = w.shape[0] if w_rows_out else w.shape[1]
    assert n % tn == 0 and tc % ROW_BLOCK == 0 and (ta - tc) % ROW_BLOCK == 0
    mod_lat = mod_l[:b]
    mod_ctx = mod_l[b:b + 1]
    has_small = w_small is not None
    in_specs = x_specs + [pl.BlockSpec((1, 3, d), lambda i, j: (i, 0, 0)),
                          pl.BlockSpec((1, 3, d), lambda i, j: (0, 0, 0)),
                          pl.BlockSpec((1, d), lambda i, j: (0, 0)),
                          pl.BlockSpec((tn, d), lambda i, j: (j, 0)) if w_rows_out
                          else pl.BlockSpec((d, tn), lambda i, j: (0, j))]
    args = x_arrs + [mod_lat, mod_ctx, norm_g.reshape(1, d), w.astype(BF16)]
    out_shape = [jax.ShapeDtypeStruct((b, ta, n), out_dtype)]
    out_specs = [pl.BlockSpec((1, ta, tn), lambda i, j: (i, 0, j))]
    if has_small:
        ns = w_small.shape[0]
        in_specs.append(pl.BlockSpec((ns, d), lambda i, j: (0, 0)))
        args.append(w_small.astype(BF16))
    if epilogue:
        assert all(lo % tn == 0 and hi % tn == 0 for lo, hi, _ in epilogue)
        epilogue = tuple((lo // tn, hi // tn, kind) for lo, hi, kind in epilogue)
        zeros = jnp.zeros((n,), F32)
        for par in (pars if pars is not None else (zeros,) * 3):
            in_specs.append(pl.BlockSpec((1, tn), lambda i, j: (0, j)))
            args.append(par.reshape(1, n))
    if has_small:
        out_shape.append(jax.ShapeDtypeStruct((b, ns, ta), F32))
        out_specs.append(pl.BlockSpec((1, ns, ta), lambda i, j: (i, 0, 0)))
    if block_min:
        nblk = -(-(ta // ROW_BLOCK) // SUBLANES) * SUBLANES
        out_shape.append(jax.ShapeDtypeStruct((b, n // tn, nblk, LANES), F32))
        out_specs.append(pl.BlockSpec((1, 1, nblk, LANES), lambda i, j: (i, j, 0, 0)))
    res = pl.pallas_call(
        functools.partial(_inproj_kernel, tc=tc, n_x=len(x_arrs), has_small=has_small, epilogue=epilogue,
                          w_rows_out=w_rows_out, block_min=block_min),
        out_shape=out_shape,
        grid=(b, n // tn),
        in_specs=in_specs,
        out_specs=out_specs,
        scratch_shapes=[pltpu.VMEM((ta, d), BF16)],
        compiler_params=_cparams("parallel", "arbitrary"),
        name="inproj",
    )(*args)
    return res if (has_small or block_min) else res[0]


def _head_rms(h, gain, n_heads):
    dh = h.shape[1] // n_heads
    parts = []
    for i in range(n_heads):
        hh = h[:, i * dh:(i + 1) * dh]
        ms = jnp.mean(hh * hh, axis=-1, keepdims=True)
        parts.append(hh * lax.rsqrt(ms + EPS))
    return jnp.concatenate(parts, axis=1) * gain


def _outproj_kernel(*refs, mode, n_feat, n_x, tc, tm, row0):
    feats, rest = refs[:n_feat], list(refs[n_feat:])
    hg_ref = rest.pop(0) if mode != "plain" else None
    w_ref = rest.pop(0)
    x_refs = [rest.pop(0) for _ in range(n_x)]
    ml_ref, mc_ref, o_ref = rest
    if mode == "plain":
        u = feats[0][0]
    elif mode == "mlstm":
        h0_ref, h1_ref, og_ref, z_ref = feats
        hn = _head_rms(h0_ref[0].astype(F32) + h1_ref[0].astype(F32), hg_ref[...], ML_HEADS)
        u = (hn * (1.0 + jnp.tanh(og_ref[0].astype(F32))) * _silu_of_half(z_ref[0].astype(F32))).astype(BF16)
    else:
        h0_ref, h1_ref, z_ref = feats
        hn = _head_rms(h0_ref[0].astype(F32) + h1_ref[0].astype(F32), hg_ref[...], HG_HEADS)
        u = (hn * z_ref[0].astype(F32)).astype(BF16)
    y = jnp.dot(u, w_ref[...], preferred_element_type=F32)
    first = row0 + pl.program_id(1) * tm
    row = first + lax.broadcasted_iota(jnp.int32, (tm, 1), 0)
    gate = jnp.where(row < tc, mc_ref[0, 2:3, :], ml_ref[0, 2:3, :])
    if n_x == 1:
        x = x_refs[0][0]
    elif tm > tc:
        lat = x_refs[1][0]
        x = jnp.where(first < tc, jnp.concatenate([x_refs[0][0], lat[:tm - tc]], axis=0), lat)
    else:
        x = jnp.where(first < tc, x_refs[0][0], x_refs[1][0])
    o_ref[0] = x + gate * y


def _outproj(mode, feats, w_out, xs, mod_l, tc, head_g=None, lat_only=False):
    x_arrs = list(xs) if isinstance(xs, tuple) else [xs]
    b, d = x_arrs[0].shape[0], x_arrs[0].shape[2]
    ta = sum(a.shape[1] for a in x_arrs)
    kdim = w_out.shape[0]
    if lat_only:
        tm = OUT_ROWS_LAT if (ta - tc) % OUT_ROWS_LAT == 0 else OUT_ROWS_SPLIT
    else:
        wide = ta % OUT_ROWS == 0 and (len(x_arrs) == 1 or (tc < OUT_ROWS and tc % ROW_BLOCK == 0))
        tm = OUT_ROWS if wide else OUT_ROWS_SPLIT
        assert ta % tm == 0 and (tc % tm == 0 or len(x_arrs) == 1 or wide)
    row0 = tc if lat_only else 0
    nct = tc // tm
    row_spec = lambda width, col: pl.BlockSpec((pl.Element(1), pl.Element(tm), pl.Element(width)),
                                               lambda i, r, col=col: (i, pl.multiple_of(row0 + r * tm, ROW_BLOCK),
                                                                      col * width))
    in_specs, args = [], []
    for arr, width, col in feats:
        in_specs.append(row_spec(width, col))
        args.append(arr)
    if head_g is not None:
        in_specs.append(pl.BlockSpec((1, kdim), lambda i, r: (0, 0)))
        args.append(head_g.reshape(1, kdim))
    in_specs.append(pl.BlockSpec((kdim, d), lambda i, r: (0, 0)))
    args.append(w_out.astype(BF16))
    if len(x_arrs) == 1:
        in_specs.append(row_spec(d, 0))
    elif lat_only:
        x_arrs = x_arrs[1:]
        in_specs.append(pl.BlockSpec((1, tm, d), lambda i, r: (i, r, 0)))
    elif tm > tc:
        in_specs += [pl.BlockSpec((1, tc, d), lambda i, r: (i, 0, 0)),
                     pl.BlockSpec((pl.Element(1), pl.Element(tm), pl.Element(d)),
                                  lambda i, r: (i, pl.multiple_of(jnp.maximum(r * tm - tc, 0), ROW_BLOCK), 0))]
    else:
        in_specs += [pl.BlockSpec((1, tm, d), lambda i, r: (i, jnp.minimum(r, nct - 1), 0)),
                     pl.BlockSpec((1, tm, d), lambda i, r: (i, jnp.maximum(r - nct, 0), 0))]
    args += x_arrs
    in_specs += [pl.BlockSpec((1, 3, d), lambda i, r: (i, 0, 0)),
                 pl.BlockSpec((1, 3, d), lambda i, r: (0, 0, 0))]
    args += [mod_l[:b], mod_l[b:b + 1]]
    rows_out = ta - row0
    return pl.pallas_call(
        functools.partial(_outproj_kernel, mode=mode, n_feat=len(feats), n_x=len(x_arrs), tc=tc, tm=tm,
                          row0=row0),
        out_shape=jax.ShapeDtypeStruct((b, rows_out, d), F32),
        grid=(b, rows_out // tm),
        in_specs=in_specs,
        out_specs=pl.BlockSpec((1, tm, d), lambda i, r: (i, r, 0)),
        compiler_params=_cparams("parallel", "parallel"),
        name="outproj_" + mode,
    )(*args)


def _conv_kernel(x_ref, ml_ref, mc_ref, g_ref, wx_ref, wb_ref, wc_ref, wz_ref, cw_ref, cb_ref, o_ref, h_ref, *, tc):
    ta = h_ref.shape[0]
    chunk = PROJ_ROWS if ta % PROJ_ROWS == 0 else ta

    def body(first_tile):
        wx, wc, wb, wz = (w_ref[...].astype(BF16) for w_ref in (wx_ref, wc_ref, wb_ref, wz_ref))
        step = chunk if first_tile else ta
        xin, cg = [], []
        for r0 in range(0, ta, step):
            if first_tile:
                _modulated((x_ref,), ml_ref, mc_ref, g_ref, h_ref, tc, r0, step)
            h = h_ref[r0:r0 + step, :]
            xin.append(jnp.dot(h, wx, preferred_element_type=F32))
            cg.append(jnp.dot(h, wc, preferred_element_type=F32))
        bg = jnp.dot(h_ref[...], wb, preferred_element_type=F32)
        z = jnp.dot(h_ref[...], wz, preferred_element_type=F32)
        u = jnp.concatenate(cg, axis=0) * jnp.concatenate(xin, axis=0)
        row = lax.broadcasted_iota(jnp.int32, (ta, 1), 0)
        first = (row == 0) | (row == tc)
        last = (row == tc - 1) | (row == ta - 1)
        u_prev = jnp.where(first, 0.0, pltpu.roll(u, 1, axis=0))
        u_next = jnp.where(last, 0.0, pltpu.roll(u, ta - 1, axis=0))
        cw = cw_ref[...]
        y = u_prev * cw[0:1, :] + u * cw[1:2, :] + u_next * cw[2:3, :] + cb_ref[...]
        o_ref[0] = (bg * y * _silu(z)).astype(o_ref.dtype)

    @pl.when(pl.program_id(1) == 0)
    def _():
        body(True)

    @pl.when(pl.program_id(1) != 0)
    def _():
        body(False)


def _conv_mixer(xs, mod_l, norm_g, w_in, conv_w, conv_b, tc, tw=256):
    b, ta, d = xs.shape
    e = conv_w.shape[1]
    nt = e // tw
    w_spec = lambda part: pl.BlockSpec((d, tw), lambda i, j, part=part: (0, part * nt + j))
    return pl.pallas_call(
        functools.partial(_conv_kernel, tc=tc),
        out_shape=jax.ShapeDtypeStruct((b, ta, e), BF16),
        grid=(b, nt),
        in_specs=[pl.BlockSpec((1, ta, d), lambda i, j: (i, 0, 0)),
                  pl.BlockSpec((1, 3, d), lambda i, j: (i, 0, 0)),
                  pl.BlockSpec((1, 3, d), lambda i, j: (0, 0, 0)),
                  pl.BlockSpec((1, d), lambda i, j: (0, 0)),
                  w_spec(0), w_spec(1), w_spec(2), w_spec(3),
                  pl.BlockSpec((SC_KSIZE, tw), lambda i, j: (0, j)),
                  pl.BlockSpec((1, tw), lambda i, j: (0, j))],
        out_specs=pl.BlockSpec((1, ta, tw), lambda i, j: (i, 0, j)),
        scratch_shapes=[pltpu.VMEM((ta, d), BF16)],
        compiler_params=_cparams("parallel", "arbitrary"),
        name="conv_mixer",
    )(xs, mod_l[:b], mod_l[b:b + 1], norm_g.reshape(1, d), w_in, w_in, w_in, w_in, conv_w, conv_b.reshape(1, e))


AT_HALF = AT_HEAD_DIM // 2
AT_TILE_PERM = np.concatenate([np.arange(0, AT_HALF), np.arange(2 * AT_HALF, 3 * AT_HALF),
                               np.arange(AT_HALF, 2 * AT_HALF), np.arange(3 * AT_HALF, 4 * AT_HALF)])
LOG2E = float(np.log2(np.e))


def _rope_tables(tc, t):
    rows = t // GRID_W
    row = np.repeat(np.arange(rows), GRID_W).astype(np.float64)
    col = np.tile(np.arange(GRID_W), rows).astype(np.float64)
    n_freq = AT_HEAD_DIM // 4
    freqs = np.power(ROPE_BASE, -np.arange(n_freq, dtype=np.float64) / n_freq)
    ang = np.concatenate([row[:, None] * freqs, col[:, None] * freqs], axis=-1)
    cos, sin = np.cos(ang), np.sin(ang)
    cos_t = np.concatenate([np.ones((tc, LANES)), np.tile(cos, (1, 4))], axis=0)
    sin_t = np.concatenate([np.zeros((tc, LANES)), np.concatenate([-sin, -sin, sin, sin], axis=-1)], axis=0)
    return jnp.asarray(cos_t, F32), jnp.asarray(sin_t, F32)


def _norm_rope_tiles(xs, gains, cos, sin, same_head):
    sums = []
    for x in xs:
        x2 = x * x
        hi = x2.astype(BF16)
        lo = (x2 - hi.astype(F32)).astype(BF16)
        sums.append((jnp.dot(hi, same_head, preferred_element_type=F32),
                     jnp.dot(lo, same_head, preferred_element_type=F32)))
    out = []
    for x, gain, (s_hi, s_lo) in zip(xs, gains, sums):
        xn = x * lax.rsqrt(s_hi + s_lo + EPS) * gain
        out.append(xn * cos + pltpu.roll(xn, 2 * AT_HALF, axis=1) * sin)
    return out


def _attn_kernel(sink_ref, q_ref, k_ref, v_ref, z_ref, qg_ref, kg_ref, cos_ref, sin_ref, sh_ref, wo_ref, x_ref,
                 ml_ref, mc_ref, o_ref, qs_ref, ka_ref, kb_ref, va_ref, vb_ref, *, tc, need_ctx):
    ta = q_ref.shape[1]
    t = ta - tc
    nb = t // AT_BLOCK
    blk = AT_BLOCK
    gw = x_ref.shape[2]
    g = pl.program_id(1)
    odd = (g % 2) == 1

    @pl.when(g == 0)
    def _():
        o_ref[...] = jnp.zeros(o_ref.shape, F32)

    def prep(i, carry):
        r0 = pl.multiple_of(i * ROW_BLOCK, ROW_BLOCK)
        res_cols = pl.ds(pl.multiple_of(g * gw, gw), gw)
        o_ref[0, pl.ds(r0, ROW_BLOCK), res_cols] = (o_ref[0, pl.ds(r0, ROW_BLOCK), res_cols]
                                                    + x_ref[0, pl.ds(r0, ROW_BLOCK), :])
        cos = cos_ref[pl.ds(r0, ROW_BLOCK), :]
        sin = sin_ref[pl.ds(r0, ROW_BLOCK), :]
        same_head = sh_ref[...]
        tiles = [q_ref[0, pl.ds(r0, ROW_BLOCK), c * LANES:(c + 1) * LANES].astype(F32) for c in range(2)]
        tiles.append(k_ref[0, pl.ds(r0, ROW_BLOCK), :].astype(F32))
        q0, q1, kn = _norm_rope_tiles(tiles, (qg_ref[...], qg_ref[...], kg_ref[...]), cos, sin, same_head)
        for c, qc in enumerate((q0, q1)):
            qs_ref[pl.ds(r0, ROW_BLOCK), c * LANES:(c + 1) * LANES] = qc.astype(BF16)
        vv = v_ref[0, pl.ds(r0, ROW_BLOCK), :].astype(F32)
        lane = lax.broadcasted_iota(jnp.int32, kn.shape, 1)
        k_own = jnp.where(((lane // AT_HALF) % 2) == (g % 2), kn, 0.0)
        k_oth = pltpu.roll(k_own, jnp.where(odd, 3 * AT_HALF, AT_HALF), axis=1)
        ka_ref[pl.ds(r0, ROW_BLOCK), :] = jnp.where(odd, k_oth, k_own).astype(BF16)
        kb_ref[pl.ds(r0, ROW_BLOCK), :] = jnp.where(odd, k_own, k_oth).astype(BF16)
        v_own = jnp.where((lane // AT_HEAD_DIM) == (g % 2), vv, 0.0)
        v_oth = pltpu.roll(v_own, AT_HEAD_DIM, axis=1)
        va = jnp.where(odd, v_oth, v_own)
        vb = jnp.where(odd, v_own, v_oth)
        va_ref[pl.ds(r0, ROW_BLOCK), :] = jnp.where(lane == AT_HEAD_DIM, 1.0, va).astype(BF16)
        vb_ref[pl.ds(r0, ROW_BLOCK), :] = jnp.where(lane == 0, 1.0, vb).astype(BF16)
        return carry

    lax.fori_loop(0, ta // ROW_BLOCK, prep, 0, unroll=3)
    zeros = jnp.zeros((blk, LANES), BF16)
    for ref in (ka_ref, kb_ref, va_ref, vb_ref):
        ref[ta:ta + blk, :] = zeros

    half = lax.broadcasted_iota(jnp.int32, (2 * blk, 1), 0) < blk
    sink_a = jnp.where(half, sink_ref[g, 0], sink_ref[g, 2]) * LOG2E
    sink_b = jnp.where(half, sink_ref[g, 1], sink_ref[g, 3]) * LOG2E

    qi = lax.broadcasted_iota(jnp.int32, (2 * blk, 3 * blk), 0) % blk
    kj = lax.broadcasted_iota(jnp.int32, (2 * blk, 3 * blk), 1)
    band = (kj - qi >= 0) & (kj - qi <= 2 * blk)
    out_lo = lax.broadcasted_iota(jnp.int32, (2 * blk, LANES), 1) < AT_HEAD_DIM

    nt = (((1,), (1,)), ((), ()))

    sides = ((ka_ref, va_ref, sink_a, AT_HEAD_DIM), (kb_ref, vb_ref, sink_b, 0))

    def attend(blocks, r_first, gate):
        qts = [jnp.concatenate([qs_ref[pl.ds(r0, blk), 0:LANES], qs_ref[pl.ds(r0, blk), LANES:2 * LANES]], axis=0)
               for r0, _, _ in blocks]
        scores = []
        for qt, (_, k0, mask) in zip(qts, blocks):
            for k_ref_, _, _, _ in sides:
                s_ctx = lax.dot_general(qt, k_ref_[0:tc, :], nt, preferred_element_type=F32)
                s_loc = None if mask is None else lax.dot_general(qt, k_ref_[pl.ds(k0, 3 * blk), :], nt,
                                                                  preferred_element_type=F32)
                scores.append((s_ctx, s_loc))
        probs = []
        for idx, (s_ctx, s_loc) in enumerate(scores):
            mask = blocks[idx // 2][2]
            sink = sides[idx % 2][2]
            m = jnp.maximum(sink, jnp.max(s_ctx, axis=-1, keepdims=True))
            if s_loc is not None:
                s_loc = jnp.where(mask, s_loc, -jnp.inf)
                m = jnp.maximum(m, jnp.max(s_loc, axis=-1, keepdims=True))
            probs.append((jnp.exp2(s_ctx - m).astype(BF16),
                          None if s_loc is None else jnp.exp2(s_loc - m).astype(BF16), jnp.exp2(sink - m)))
        accs = []
        for idx, (p_ctx, p_loc, _) in enumerate(probs):
            k0 = blocks[idx // 2][1]
            v_ref_ = sides[idx % 2][1]
            acc = jnp.dot(p_ctx, v_ref_[0:tc, :], preferred_element_type=F32)
            if p_loc is not None:
                acc = acc + jnp.dot(p_loc, v_ref_[pl.ds(k0, 3 * blk), :], preferred_element_type=F32)
            accs.append(acc)
        us = []
        for bi, (r0, _, _) in enumerate(blocks):
            halves = []
            for si in range(2):
                acc, ones_lane = accs[2 * bi + si], sides[si][3]
                halves.append(acc / (probs[2 * bi + si][2] + acc[:, ones_lane:ones_lane + 1]))
            o = jnp.where(out_lo, halves[0], halves[1])
            parts = []
            for c in range(2):
                zc = z_ref[0, pl.ds(r0, blk), c * LANES:(c + 1) * LANES].astype(F32)
                parts.append((o[c * blk:(c + 1) * blk] * _silu_of_half(zc)).astype(BF16))
            us.append(jnp.concatenate(parts, axis=1))
        u = jnp.concatenate(us, axis=0)
        rows = pl.ds(r_first, u.shape[0])
        o_ref[0, rows, :] = o_ref[0, rows, :] + gate * jnp.dot(u, wo_ref[...], preferred_element_type=F32)

    def lat_blocks(i, carry):
        blocks = []
        for j in range(AT_PROJ_BLOCKS):
            n = i * AT_PROJ_BLOCKS + j
            kpos = (n - 1) * blk + kj
            blocks.append((pl.multiple_of(tc + n * blk, blk), pl.multiple_of(tc + (n - 1) * blk, blk),
                           band & (kpos >= 0) & (kpos < t)))
        attend(blocks, pl.multiple_of(tc + i * AT_PROJ_BLOCKS * blk, AT_PROJ_BLOCKS * blk), ml_ref[0, 2:3, :])
        return carry

    lax.fori_loop(0, nb // AT_PROJ_BLOCKS, lat_blocks, 0)
    if need_ctx:
        attend([(n * blk, None, None) for n in range(tc // blk)], 0, mc_ref[0, 2:3, :])


def _attn_mixer(p, at_q_g, at_k_g, at_sink, w_out, xs, mod_l, tc, need_ctx):
    b, ta, _ = p.shape
    t = ta - tc
    d = xs.shape[2]
    assert tc >= AT_BLOCK and tc % AT_BLOCK == 0 and t % (AT_PROJ_BLOCKS * AT_BLOCK) == 0 and ta % ROW_BLOCK == 0
    assert d == AT_Q
    cos_t, sin_t = _rope_tables(tc, t)
    lane_head = (np.arange(LANES) // AT_HALF) % 2
    same_head = jnp.asarray((lane_head[:, None] == lane_head[None, :]) / AT_HEAD_DIM, BF16)
    q_scale = (AT_HEAD_DIM ** -0.5) * LOG2E
    tile_gain = lambda gain: jnp.tile(gain, 2)[AT_TILE_PERM].reshape(1, LANES)
    gw = AT_GROUP * AT_HEAD_DIM
    kcol = AT_Q // LANES
    vcol = (AT_Q + AT_KV) // LANES
    zcol = (AT_Q + 2 * AT_KV) // gw
    return pl.pallas_call(
        functools.partial(_attn_kernel, tc=tc, need_ctx=need_ctx),
        out_shape=jax.ShapeDtypeStruct((b, ta, d), F32),
        grid=(b, AT_KV_HEADS),
        in_specs=[pl.BlockSpec(memory_space=pltpu.SMEM),
                  pl.BlockSpec((1, ta, gw), lambda i, g: (i, 0, g)),
                  pl.BlockSpec((1, ta, LANES), lambda i, g: (i, 0, kcol + g // 2)),
                  pl.BlockSpec((1, ta, LANES), lambda i, g: (i, 0, vcol + g // 2)),
                  pl.BlockSpec((1, ta, gw), lambda i, g: (i, 0, zcol + g)),
                  pl.BlockSpec((1, LANES), lambda i, g: (0, 0)),
                  pl.BlockSpec((1, LANES), lambda i, g: (0, 0)),
                  pl.BlockSpec((ta, LANES), lambda i, g: (0, 0)),
                  pl.BlockSpec((ta, LANES), lambda i, g: (0, 0)),
                  pl.BlockSpec((LANES, LANES), lambda i, g: (0, 0)),
                  pl.BlockSpec((gw, d), lambda i, g: (g, 0)),
                  pl.BlockSpec((1, ta, gw), lambda i, g: (i, 0, g)),
                  pl.BlockSpec((1, 3, d), lambda i, g: (i, 0, 0)),
                  pl.BlockSpec((1, 3, d), lambda i, g: (0, 0, 0))],
        out_specs=pl.BlockSpec((1, ta, d), lambda i, g: (i, 0, 0)),
        scratch_shapes=[pltpu.VMEM((ta, gw), BF16)] + [pltpu.VMEM((ta + AT_BLOCK, LANES), BF16)] * 4,
        compiler_params=_cparams("parallel", "arbitrary"),
        name="attn_mixer",
    )(at_sink.reshape(AT_KV_HEADS, AT_GROUP), p, p, p, p, q_scale * tile_gain(at_q_g), tile_gain(at_k_g), cos_t, sin_t,
      same_head, w_out.astype(BF16), xs, mod_l[:b], mod_l[b:b + 1])


def _attn_weight_kernel(w_ref, p_ref, o_ref, *, n_qk, n_v):
    tiles = w_ref.shape[1] // LANES
    for lt in range(tiles):
        tile = pl.program_id(0) * tiles + lt
        kind = (tile >= n_qk).astype(jnp.int32) + (tile >= n_qk + n_v).astype(jnp.int32)
        cols = slice(lt * LANES, (lt + 1) * LANES)
        o_ref[:, cols] = jnp.dot(w_ref[:, cols].astype(BF16), p_ref[kind], preferred_element_type=F32).astype(BF16)


def _attn_weight(w_in):
    d, n = w_in.shape
    n_qk, n_v = (AT_Q + AT_KV) // LANES, AT_KV // LANES
    perm = np.zeros((LANES, LANES), np.float32)
    perm[AT_TILE_PERM, np.arange(LANES)] = 1.0
    eye = np.eye(LANES, dtype=np.float32)
    mats = jnp.asarray(np.stack([perm, eye, 0.5 * eye]), BF16)
    tw = 4 * LANES
    assert n % tw == 0
    return pl.pallas_call(
        functools.partial(_attn_weight_kernel, n_qk=n_qk, n_v=n_v),
        out_shape=jax.ShapeDtypeStruct((d, n), BF16),
        grid=(n // tw,),
        in_specs=[pl.BlockSpec((d, tw), lambda j: (0, j)),
                  pl.BlockSpec((3, LANES, LANES), lambda j: (0, 0, 0))],
        out_specs=pl.BlockSpec((d, tw), lambda j: (0, j)),
        compiler_params=_cparams("parallel"),
        name="attn_weight",
    )(w_in, mats)


def _chunk_order(nc, ncc, d, s):
    bwd = jnp.where(s < ncc, ncc - 1 - s, nc - 1 - (s - ncc))
    return jnp.where(d == 0, s, bwd)


ML_GATE_PERM = np.concatenate([np.arange(0, 4), np.arange(8, 12), np.arange(4, 8), np.arange(12, 16)])
ML_NQ = 6


def _mlstm_gate_kernel(g_ref, b_ref, a_ref, c_ref, *, tc, lc):
    h = ML_HEADS
    ta = g_ref.shape[2]
    nc, ncc = ta // lc, tc // lc
    x = g_ref[0] + b_ref[...]
    li = x[0:2 * h]
    lfp = x[2 * h:4 * h]
    lf = jnp.minimum(lfp, 0.0) - jnp.log1p(jnp.exp(-jnp.abs(lfp)))
    fwd = lax.broadcasted_iota(jnp.int32, (2 * h, ta), 0) < h
    fwd1 = lax.broadcasted_iota(jnp.int32, (2 * h, 1), 0) < h
    pos = lax.broadcasted_iota(jnp.int32, (2 * h, ta), 1) % lc

    def seg_scan(y, op, fill):
        yf, yb = y, y
        s = 1
        while s < lc:
            yf = op(yf, jnp.where(pos >= s, pltpu.roll(yf, s, axis=1), fill))
            yb = op(yb, jnp.where(pos < lc - s, pltpu.roll(yb, ta - s, axis=1), fill))
            s *= 2
        return jnp.where(fwd, yf, yb)

    bsum = seg_scan(lf, jnp.add, 0.0)
    a = li - bsum
    cmax = seg_scan(a, jnp.maximum, -jnp.inf)

    def end_col(y, c):
        return jnp.where(fwd1, y[:, (c + 1) * lc - 1:(c + 1) * lc], y[:, c * lc:c * lc + 1])

    tot = [end_col(bsum, c) for c in range(nc)]
    amax = [end_col(cmax, c) for c in range(nc)]

    def chain(order):
        m = jnp.zeros((2 * h, 1), F32)
        m_in = [None] * nc
        for c in order:
            m_in[c] = m
            m = tot[c] + jnp.maximum(m, amax[c])
        return m_in

    m_f = chain(list(range(nc)))
    m_b = chain(list(range(ncc - 1, -1, -1)) + list(range(nc - 1, ncc - 1, -1)))
    for c in range(nc):
        m_in = jnp.where(fwd1, m_f[c], m_b[c])
        sl = slice(c * lc, (c + 1) * lc)
        g_run = jnp.maximum(m_in, cmax[:, sl])
        g_end = jnp.maximum(m_in, amax[c])
        nr = 2 * h
        quantities = [a[:, sl], g_run, jnp.exp(m_in - g_run), jnp.exp(-(bsum[:, sl] + g_run)),
                      jnp.exp(a[:, sl] - g_end), jnp.broadcast_to(jnp.exp(m_in - g_end), (nr, lc))]
        a_ref[0, c] = quantities[0]
        pad = jnp.zeros((LANES - ML_NQ * nr, lc), F32)
        c_ref[0, c] = jnp.concatenate(quantities + [pad], axis=0).T


def _mlstm_scan_kernel(q_ref, k_ref, v_ref, a_ref, c_ref, o_ref, cs_ref, *, rev):
    lc = q_ref.shape[1]
    nr = 2 * ML_HEADS

    @pl.when(pl.program_id(1) == 0)
    def _():
        cs_ref[...] = jnp.zeros(cs_ref.shape, F32)

    ti = lax.broadcasted_iota(jnp.int32, (lc, lc), 0)
    si = lax.broadcasted_iota(jnp.int32, (lc, lc), 1)
    mask = (si >= ti) if rev else (si <= ti)
    ones = jnp.ones((lc, LANES), BF16)
    nt = (((1,), (1,)), ((), ()))
    tn = (((0,), (0,)), ((), ()))
    heads = range(ML_HEADS)
    qb = [q_ref[0, :, h * ML_DK:(h + 1) * ML_DK] for h in heads]
    kb = [k_ref[0, :, h * ML_DK:(h + 1) * ML_DK] for h in heads]
    v_aug = [jnp.concatenate([v_ref[0, :, h * ML_DV:(h + 1) * ML_DV], ones], axis=1) for h in heads]
    cols = []
    for h in heads:
        r = (ML_HEADS if rev else 0) + h
        cols.append([a_ref[0, 0, r:r + 1, :]] + [c_ref[0, 0, :, qi * nr + r:qi * nr + r + 1] for qi in range(1, 6)])
    qk = [lax.dot_general(qb[h], kb[h], nt, preferred_element_type=F32) for h in heads]
    c_old = [cs_ref[h] for h in heads]
    q_c = [jnp.dot(qb[h], c_old[h].astype(BF16), preferred_element_type=F32) for h in heads]
    kw = [(kb[h].astype(F32) * cols[h][4]).astype(BF16) for h in heads]
    upd = [lax.dot_general(kw[h], v_aug[h], tn, preferred_element_type=F32) for h in heads]
    s = []
    for h in heads:
        a_row, g_run = cols[h][0], cols[h][1]
        s.append((qk[h] * jnp.where(mask, jnp.exp(a_row - g_run), 0.0)).astype(BF16))
    s_v = [jnp.dot(s[h], v_aug[h], preferred_element_type=F32) for h in heads]
    for h in heads:
        inter, eclamp, decay = cols[h][2], cols[h][3], cols[h][5]
        tot = inter * q_c[h] + s_v[h]
        inv = 1.0 / jnp.maximum(jnp.abs(tot[:, ML_DV:]), eclamp)
        o_ref[0, :, h * ML_DV:(h + 1) * ML_DV] = (
            tot[:, :ML_DV] * jnp.concatenate([inv] * (ML_DV // LANES), axis=1)).astype(o_ref.dtype)
        cs_ref[h] = decay[0:1, :] * c_old[h] + upd[h]


def _mlstm_scan(p, gates, gate_b, tc):
    b, ta, _ = p.shape
    lc = ML_CHUNK
    assert tc % lc == 0 and ta % lc == 0
    nc, ncc = ta // lc, tc // lc
    ng = 4 * ML_HEADS
    nr = 2 * ML_HEADS
    bias = gate_b.reshape(ng)[ML_GATE_PERM].reshape(ng, 1)
    a_rows, cols = pl.pallas_call(
        functools.partial(_mlstm_gate_kernel, tc=tc, lc=lc),
        out_shape=[jax.ShapeDtypeStruct((b, nc, nr, lc), F32), jax.ShapeDtypeStruct((b, nc, lc, LANES), F32)],
        grid=(b,),
        in_specs=[pl.BlockSpec((1, ng, ta), lambda i: (i, 0, 0)),
                  pl.BlockSpec((ng, 1), lambda i: (0, 0))],
        out_specs=[pl.BlockSpec((1, nc, nr, lc), lambda i: (i, 0, 0, 0)),
                   pl.BlockSpec((1, nc, lc, LANES), lambda i: (i, 0, 0, 0))],
        compiler_params=_cparams("parallel"),
        name="mlstm_gates",
    )(gates, bias)
    outs = []
    for d in range(2):
        chunk = functools.partial(_chunk_order, nc, ncc, d)
        outs.append(pl.pallas_call(
            functools.partial(_mlstm_scan_kernel, rev=bool(d)),
            out_shape=jax.ShapeDtypeStruct((b, ta, ML_INNER), BF16),
            grid=(b, nc),
            in_specs=[pl.BlockSpec((1, lc, ML_QK), lambda i, s, chunk=chunk: (i, chunk(s), 0)),
                      pl.BlockSpec((1, lc, ML_QK), lambda i, s, chunk=chunk: (i, chunk(s), 1)),
                      pl.BlockSpec((1, lc, ML_INNER), lambda i, s, chunk=chunk: (i, chunk(s), 1)),
                      pl.BlockSpec((1, 1, nr, lc), lambda i, s, chunk=chunk: (i, chunk(s), 0, 0)),
                      pl.BlockSpec((1, 1, lc, LANES), lambda i, s, chunk=chunk: (i, chunk(s), 0, 0))],
            out_specs=pl.BlockSpec((1, lc, ML_INNER), lambda i, s, chunk=chunk: (i, chunk(s), 0)),
            scratch_shapes=[pltpu.VMEM((ML_HEADS, ML_DK, ML_DV + LANES), F32)],
            compiler_params=_cparams("parallel", "arbitrary"),
            name="mlstm_scan_bwd" if d else "mlstm_scan_fwd",
        )(p, p, p, a_rows, cols))
    return outs


def _hgrn_lb_kernel(p_ref, o_ref, *, layer):
    for d in range(p_ref.shape[0]):
        x = p_ref[d]
        e = jnp.exp(x - jnp.max(x, axis=0, keepdims=True))
        p = e / jnp.sum(e, axis=0, keepdims=True)
        acc = jnp.zeros((1, x.shape[1]), F32)
        for j in range(1, layer + 1):
            acc = acc + p[j:j + 1, :]
        o_ref[d:d + 1, :] = acc


def _cumsum_rows(x, tri2):
    hi = x.astype(BF16)
    lo = (x - hi.astype(F32)).astype(BF16)
    return jnp.dot(tri2, jnp.concatenate([hi, lo], axis=0), preferred_element_type=F32)


def _tri2(n, rev):
    t = np.triu(np.ones((n, n), np.float32)) if rev else np.tril(np.ones((n, n), np.float32))
    return jnp.asarray(np.concatenate([t, t], axis=1), BF16)


def _anchor_rows(a, m, rev):
    n, f = a.shape
    idx = m if rev else m - 1
    if 2 * m >= SUBLANES:
        a3 = a.reshape(n // (2 * m), 2 * m, f)
        return jnp.broadcast_to(a3[:, idx:idx + 1, :], a3.shape).reshape(n, f)
    a3 = a.reshape(n // SUBLANES, SUBLANES, f)
    sub = lax.broadcasted_iota(jnp.int32, a3.shape, 1)
    out = None
    for gi in range(SUBLANES // (2 * m) - 1, -1, -1):
        cand = jnp.broadcast_to(a3[:, gi * 2 * m + idx:gi * 2 * m + idx + 1, :], a3.shape)
        out = cand if out is None else jnp.where(sub < (gi + 1) * 2 * m, cand, out)
    return out.reshape(n, f)


HG_MAX_LOG_SPAN = 64.0


def _hgrn_scan_kernel(fmin_ref, q_ref, f_ref, i_ref, tri_ref, o_ref, s_ref, *, rev, nc, ncc):
    lc = HG_CHUNK
    subs = range(q_ref.shape[1] // lc)
    subs = tuple(reversed(subs)) if rev else tuple(subs)
    step_block = _chunk_order(nc, ncc, 1 if rev else 0, pl.program_id(1))

    @pl.when(pl.program_id(1) == 0)
    def _():
        s_ref[...] = jnp.zeros(s_ref.shape, F32)

    row = lax.broadcasted_iota(jnp.int32, (lc, 1), 0)
    ti = lax.broadcasted_iota(jnp.int32, (lc, lc), 0)
    si = lax.broadcasted_iota(jnp.int32, (lc, lc), 1)
    nt = (((1,), (1,)), ((), ()))
    tn = (((0,), (0,)), ((), ()))
    end = 0 if rev else lc - 1

    def all_heads_single_anchor():
        causal = (si >= ti) if rev else (si <= ti)
        cols = [slice(h * HG_DIM, (h + 1) * HG_DIM) for h in range(HG_HEADS)]
        pre = {}
        for ci in subs:
            rows = slice(ci * lc, (ci + 1) * lc)
            f = f_ref[0, rows, :]
            a = _cumsum_rows(jnp.log(f), tri_ref[...])
            q_dec = (q_ref[0, rows, :].astype(F32) * jnp.exp(a)).astype(BF16)
            k_inv = (1.0 - f) * jnp.exp(-a)
            k_inv_b = k_inv.astype(BF16)
            e_end = jnp.exp(a[end:end + 1, :])
            kd = (k_inv * e_end).astype(BF16)
            attn = [jnp.where(causal, lax.dot_general(q_dec[:, c], k_inv_b[:, c], nt, preferred_element_type=F32),
                              0.0).astype(BF16) for c in cols]
            pre[ci] = (rows, q_dec, e_end, kd, attn)
        state = [s_ref[h] for h in range(HG_HEADS)]
        for ci in subs:
            rows, q_dec, e_end, kd, attn = pre[ci]
            for h, c in enumerate(cols):
                o = jnp.dot(attn[h], i_ref[0, rows, c], preferred_element_type=F32)
                o = o + lax.dot_general(q_dec[:, c], state[h].astype(BF16), nt, preferred_element_type=F32)
                o_ref[0, rows, c] = o.astype(o_ref.dtype)
            state = [state[h] * e_end[:, c] + lax.dot_general(i_ref[0, rows, c], kd[:, c], tn,
                                                              preferred_element_type=F32)
                     for h, c in enumerate(cols)]
        for h in range(HG_HEADS):
            s_ref[h] = state[h]

    def head_per_level(r0, h, carry):
        c0 = pl.multiple_of(h * HG_DIM, HG_DIM)
        q = q_ref[0, r0:r0 + lc, pl.ds(c0, HG_DIM)].astype(F32)
        f = f_ref[0, r0:r0 + lc, pl.ds(c0, HG_DIM)]
        k = 1.0 - f
        iv = i_ref[0, r0:r0 + lc, pl.ds(c0, HG_DIM)].astype(F32)
        a = _cumsum_rows(jnp.log(f), tri_ref[...])
        attn = jnp.zeros((lc, lc), F32)
        m = 1
        while m < lc:
            e = jnp.exp(-jnp.abs(a - _anchor_rows(a, m, rev)))
            upper = (row % (2 * m)) >= m
            is_q = jnp.logical_not(upper) if rev else upper
            qt = jnp.where(is_q, q * e, 0.0).astype(BF16)
            kt = jnp.where(is_q, 0.0, k * e).astype(BF16)
            pair = lax.dot_general(qt, kt, nt, preferred_element_type=F32)
            attn = attn + jnp.where((ti // (2 * m)) == (si // (2 * m)), pair, 0.0)
            m *= 2
        a_end = a[end:end + 1, :]
        kd = (k * jnp.exp(a_end - a)).astype(BF16)
        ib = iv.astype(BF16)
        s_old = s_ref[h]
        o = jnp.dot(attn.astype(BF16), ib, preferred_element_type=F32)
        o = o + lax.dot_general((q * jnp.exp(a)).astype(BF16), s_old.astype(BF16), nt, preferred_element_type=F32)
        o_ref[0, r0:r0 + lc, pl.ds(c0, HG_DIM)] = (o + jnp.sum(q * k, axis=-1, keepdims=True) * iv).astype(o_ref.dtype)
        s_ref[h] = s_old * jnp.exp(a_end) + lax.dot_general(ib, kd, tn, preferred_element_type=F32)
        return carry

    in_range = fmin_ref[pl.program_id(0), step_block] >= float(np.exp(-HG_MAX_LOG_SPAN / lc))

    @pl.when(in_range)
    def _():
        all_heads_single_anchor()

    @pl.when(jnp.logical_not(in_range))
    def _():
        for ci in subs:
            lax.fori_loop(0, HG_HEADS, functools.partial(head_per_level, ci * lc), 0)


def _hgrn_scan(p, pf, f_min, tc, rev):
    b, ta, _ = p.shape
    lc = HG_CHUNK
    rows = HG_STEP_CHUNKS * lc
    hg = HG_HEADS * HG_DIM
    assert tc % rows == 0 and ta % rows == 0 and rows == ROW_BLOCK
    nc, ncc = ta // rows, tc // rows
    d = 1 if rev else 0
    chunk = lambda s: _chunk_order(nc, ncc, d, s)
    return pl.pallas_call(
        functools.partial(_hgrn_scan_kernel, rev=rev, nc=nc, ncc=ncc),
        out_shape=jax.ShapeDtypeStruct((b, ta, hg), BF16),
        grid_spec=pltpu.PrefetchScalarGridSpec(
            num_scalar_prefetch=1,
            grid=(b, nc),
            in_specs=[pl.BlockSpec((1, rows, hg), lambda i, s, fm: (i, chunk(s), 0)),
                      pl.BlockSpec((1, rows, hg), lambda i, s, fm: (i, chunk(s), d)),
                      pl.BlockSpec((1, rows, hg), lambda i, s, fm: (i, chunk(s), 1)),
                      pl.BlockSpec((lc, 2 * lc), lambda i, s, fm: (0, 0))],
            out_specs=pl.BlockSpec((1, rows, hg), lambda i, s, fm: (i, chunk(s), 0)),
            scratch_shapes=[pltpu.VMEM((HG_HEADS, HG_DIM, HG_DIM), F32)]),
        compiler_params=_cparams("parallel", "arbitrary"),
        name="hgrn_scan_bwd" if rev else "hgrn_scan_fwd",
    )(f_min, p, pf, p, _tri2(lc, rev))


def kernel(x, c, ctx, c_ctx, ada_w, ada_b, norm_g, ml_w_in, ml_gate_b, ml_head_g, ml_w_out, at_w_in, at_q_g, at_k_g, at_sink, at_w_out, sc_w_in, sc_conv_w, sc_conv_b, sc_w_out, hg_w_in, hg_f_b, hg_lb, hg_head_g, hg_w_out):
    tc = ctx.shape[1]
    mod = _ada_mod(c, c_ctx, ada_w, ada_b)
    xs = (ctx, x)
    for layer in range(DEPTH):
        kind, j = layer % 4, layer // 4
        need_ctx = layer < DEPTH - 1
        last = dict(lat_only=True) if layer == DEPTH - 1 else {}
        mod_l = mod[layer]
        if kind == 0:
            n_main = 2 * ML_QK + 3 * ML_INNER
            w_t = ml_w_in[j].T
            w_gate = jnp.zeros((LANES, ml_w_in.shape[1]), F32).at[:4 * ML_HEADS].set(w_t[n_main:][ML_GATE_PERM])
            col_scale = jnp.concatenate([jnp.full((ML_QK,), ML_DK ** -0.5, F32), jnp.ones((ML_QK + ML_INNER,), F32),
                                         jnp.full((2 * ML_INNER,), 0.5, F32)])
            p, gates = _inproj(xs, mod_l, norm_g[layer], w_t[:n_main] * col_scale[:, None], tc, tn=1024,
                               w_small=w_gate, out_dtype=BF16, w_rows_out=True)
            h_f, h_b = _mlstm_scan(p, gates, ml_gate_b[j], tc)
            feats = [(h_f, ML_INNER, 0), (h_b, ML_INNER, 0), (p, ML_INNER, 2), (p, ML_INNER, 3)]
            xs = _outproj("mlstm", feats, ml_w_out[j], xs, mod_l, tc, head_g=0.5 * ml_head_g[j], **last)
        elif kind == 1:
            p = _inproj(xs, mod_l, norm_g[layer], _attn_weight(at_w_in[j]), tc, tn=AT_Q + AT_KV, out_dtype=BF16)
            if isinstance(xs, tuple):
                xs = jnp.concatenate(xs, axis=1)
            xs = _attn_mixer(p, at_q_g[j], at_k_g[j], at_sink[j], at_w_out[j], xs, mod_l, tc, need_ctx)
            if last:
                xs = xs[:, tc:, :]
        elif kind == 2:
            if isinstance(xs, tuple):
                xs = jnp.concatenate(xs, axis=1)
            u = _conv_mixer(xs, mod_l, norm_g[layer], sc_w_in[j], sc_conv_w[j], sc_conv_b[j], tc)
            xs = _outproj("plain", [(u, u.shape[2], 0)], sc_w_out[j], xs, mod_l, tc, **last)
        else:
            hg = HG_HEADS * HG_DIM
            lb = pl.pallas_call(
                functools.partial(_hgrn_lb_kernel, layer=layer),
                out_shape=jax.ShapeDtypeStruct((2, hg), F32),
                name="hgrn_lb",
            )(hg_lb[j])
            w = hg_w_in[j]
            w_qiz = jnp.concatenate([0.5 * w[:, :hg], w[:, 3 * hg:4 * hg], 0.5 * w[:, 4 * hg:]], axis=1)
            p = _inproj(xs, mod_l, norm_g[layer], w_qiz, tc, tn=1024, out_dtype=BF16,
                        epilogue=((0, hg, "silu"), (hg, 2 * hg, "id"), (2 * hg, 3 * hg, "silu")))
            lb2 = lb.reshape(2 * hg)
            pf, f_min = _inproj(xs, mod_l, norm_g[layer], 0.5 * w[:, hg:3 * hg], tc, tn=hg,
                                epilogue=((0, 2 * hg, "fgate"),), block_min=True,
                                pars=(0.5 * hg_f_b[j].reshape(2 * hg), 0.5 * (1.0 + lb2), 0.5 * (1.0 - lb2)))
            n_blk = pf.shape[1] // ROW_BLOCK
            o_f = _hgrn_scan(p, pf, f_min[:, 0, :n_blk, 0], tc, rev=False)
            o_b = _hgrn_scan(p, pf, f_min[:, 1, :n_blk, 0], tc, rev=True)
            feats = [(o_f, hg, 0), (o_b, hg, 0), (p, hg, 2)]
            xs = _outproj("hgrn", feats, hg_w_out[j], xs, mod_l, tc, head_g=hg_head_g[j], **last)
    return xs
```

```python
import functools

import numpy as np
import jax
import jax.numpy as jnp
from jax import lax
from jax.experimental import pallas as pl
from jax.experimental.pallas import tpu as pltpu

F32 = jnp.float32
BF16 = jnp.bfloat16
EPS = 1e-6
DEPTH = 4
GRID_W = 64
ROPE_BASE = 10000.0

ML_HEADS, ML_DK, ML_DV = 4, 256, 512
ML_QK = ML_HEADS * ML_DK
ML_INNER = ML_HEADS * ML_DV
ML_CHUNK = 256

AT_HEADS, AT_KV_HEADS, AT_HEAD_DIM = 16, 4, 64
AT_GROUP = AT_HEADS // AT_KV_HEADS
AT_BLOCK = 128
AT_PROJ_BLOCKS = 8
AT_Q = AT_HEADS * AT_HEAD_DIM
AT_KV = AT_KV_HEADS * AT_HEAD_DIM

SC_KSIZE = 3

HG_HEADS, HG_DIM = 8, 128
HG_CHUNK = 128
HG_STEP_CHUNKS = 2

LANES = 128
SUBLANES = 8
VMEM_LIMIT_BYTES = 56 * 1024 * 1024

ROW_BLOCK = 256
PROJ_ROWS = 768
OUT_ROWS = 768
OUT_ROWS_SPLIT = 256
OUT_ROWS_LAT = 1024


def _cparams(*sem):
    return pltpu.CompilerParams(dimension_semantics=sem, vmem_limit_bytes=VMEM_LIMIT_BYTES)


def _sigmoid(x):
    return 0.5 * jnp.tanh(0.5 * x) + 0.5


def _silu(x):
    return x * _sigmoid(x)


def _silu_of_half(xh):
    return xh * (1.0 + jnp.tanh(xh))


def _ada_kernel(c_ref, w_ref, b_ref, o_ref):
    s = _silu(c_ref[...])
    o_ref[0] = jnp.dot(s.astype(BF16), w_ref[0].astype(BF16), preferred_element_type=F32) + b_ref[0]


def _ada_mod(c, c_ctx, ada_w, ada_b):
    b, d = c.shape
    depth = ada_w.shape[0]
    rows = -(-(b + 1) // SUBLANES) * SUBLANES
    cc = jnp.zeros((rows, d), F32).at[:b].set(c).at[b].set(c_ctx)
    tn = 1024
    out = pl.pallas_call(
        _ada_kernel,
        out_shape=jax.ShapeDtypeStruct((depth, rows, 3 * d), F32),
        grid=(depth, 3 * d // tn),
        in_specs=[pl.BlockSpec((rows, d), lambda l, j: (0, 0)),
                  pl.BlockSpec((1, d, tn), lambda l, j: (l, 0, j)),
                  pl.BlockSpec((1, 1, tn), lambda l, j: (l, 0, j))],
        out_specs=pl.BlockSpec((1, rows, tn), lambda l, j: (l, 0, j)),
        compiler_params=_cparams("parallel", "parallel"),
        name="ada_mod",
    )(cc, ada_w, ada_b.reshape(depth, 1, 3 * d))
    return out.reshape(depth, rows, 3, d)


def _modulated(x_refs, ml_ref, mc_ref, g_ref, h_ref, tc, row0=0, nrows=None):
    nrows = h_ref.shape[0] - row0 if nrows is None else nrows
    g = g_ref[...]
    xc_ref = x_refs[0]
    xl_ref, lat0 = (x_refs[1], 0) if len(x_refs) == 2 else (x_refs[0], tc)
    for r0 in range(row0, row0 + nrows, ROW_BLOCK):
        ctx_rows = r0 < tc
        x_ref, src0, m_ref = (xc_ref, r0, mc_ref) if ctx_rows else (xl_ref, lat0 + r0 - tc, ml_ref)
        x = x_ref[0, src0:src0 + ROW_BLOCK, :]
        ms = jnp.mean(x * x, axis=-1, keepdims=True)
        xn = x * lax.rsqrt(ms + EPS) * g
        h = xn * (1.0 + m_ref[0, 1:2, :]) + m_ref[0, 0:1, :]
        h_ref[r0:r0 + ROW_BLOCK, :] = h.astype(BF16)


def _split_stream(xs, single_buffer=False):
    arrs = list(xs) if isinstance(xs, tuple) else [xs]
    mode = dict(pipeline_mode=pl.Buffered(1)) if single_buffer else {}
    specs = [pl.BlockSpec((1,) + a.shape[1:], lambda i, j: (i, 0, 0), **mode) for a in arrs]
    ta = sum(a.shape[1] for a in arrs)
    return arrs, specs, ta


def _inproj_kernel(*refs, tc, n_x, has_small, epilogue, w_rows_out, block_min, chunk_all):
    x_refs, (ml_ref, mc_ref, g_ref, w_ref), rest = refs[:n_x], refs[n_x:n_x + 4], list(refs[n_x + 4:])
    ws_ref = rest.pop(0) if has_small else None
    par_refs = [rest.pop(0) for _ in range(3)] if epilogue else None
    o_ref = rest.pop(0)
    os_ref = rest.pop(0) if has_small else None
    mn_ref = rest.pop(0) if block_min else None
    h_ref = rest.pop(0)

    ta = h_ref.shape[0]
    chunk = PROJ_ROWS if ta % PROJ_ROWS == 0 else ta
    j = pl.program_id(1)
    w_dims = (((1,), (1 if w_rows_out else 0,)), ((), ()))

    def activation(acc):
        if not epilogue:
            return acc
        bias_ref, c0_ref, c1_ref = par_refs
        t = jnp.tanh(acc + bias_ref[...])
        out = acc
        for lo, hi, kind in epilogue:
            if kind == "id":
                continue
            val = acc + acc * t if kind == "silu" else c0_ref[...] + c1_ref[...] * t
            out = jnp.where((j >= lo) & (j < hi), val, out)
        return out

    def project(first):
        step = chunk if (first or epilogue or chunk_all) else ta
        mins = []
        for r0 in range(0, ta, step):
            if first:
                _modulated(x_refs, ml_ref, mc_ref, g_ref, h_ref, tc, r0, step)
            acc = lax.dot_general(h_ref[r0:r0 + step, :], w_ref[...], w_dims, preferred_element_type=F32)
            out = activation(acc)
            o_ref[0, r0:r0 + step, :] = out.astype(o_ref.dtype)
            if mn_ref is not None:
                for b0 in range(0, step, ROW_BLOCK):
                    m = jnp.min(out[b0:b0 + ROW_BLOCK], axis=0, keepdims=True)
                    mins.append(jnp.broadcast_to(jnp.min(m, axis=1, keepdims=True), (1, LANES)))
        if mn_ref is not None:
            pad = jnp.zeros((mn_ref.shape[2] - len(mins), LANES), F32)
            mn_ref[0, 0] = jnp.concatenate(mins + [pad], axis=0)

    @pl.when(j == 0)
    def _():
        project(True)
        if has_small:
            os_ref[0] = lax.dot_general(ws_ref[...], h_ref[...], (((1,), (1,)), ((), ())),
                                        preferred_element_type=F32)

    @pl.when(j != 0)
    def _():
        project(False)


def _inproj(xs, mod_l, norm_g, w, tc, tn=512, w_small=None, out_dtype=F32, epilogue=None, pars=None,
            w_rows_out=False, block_min=False, wide=False):
    x_arrs, x_specs, ta = _split_stream(xs, single_buffer=wide)
    b, d = x_arrs[0].shape[0], x_arrs[0].shape[2]
    n = w.shape[0] if w_rows_out else w.shape[1]
    assert n % tn == 0 and tc % ROW_BLOCK == 0 and (ta - tc) % ROW_BLOCK == 0
    mod_lat = mod_l[:b]
    mod_ctx = mod_l[b:b + 1]
    has_small = w_small is not None
    in_specs = x_specs + [pl.BlockSpec((1, 3, d), lambda i, j: (i, 0, 0)),
                          pl.BlockSpec((1, 3, d), lambda i, j: (0, 0, 0)),
                          pl.BlockSpec((1, d), lambda i, j: (0, 0)),
                          pl.BlockSpec((tn, d), lambda i, j: (j, 0)) if w_rows_out
                          else pl.BlockSpec((d, tn), lambda i, j: (0, j))]
    args = x_arrs + [mod_lat, mod_ctx, norm_g.reshape(1, d), w.astype(BF16)]
    out_shape = [jax.ShapeDtypeStruct((b, ta, n), out_dtype)]
    out_specs = [pl.BlockSpec((1, ta, tn), lambda i, j: (i, 0, j))]
    if has_small:
        ns = w_small.shape[0]
        in_specs.append(pl.BlockSpec((ns, d), lambda i, j: (0, 0)))
        args.append(w_small.astype(BF16))
    if epilogue:
        assert all(lo % tn == 0 and hi % tn == 0 for lo, hi, _ in epilogue)
        epilogue = tuple((lo // tn, hi // tn, kind) for lo, hi, kind in epilogue)
        zeros = jnp.zeros((n,), F32)
        for par in (pars if pars is not None else (zeros,) * 3):
            in_specs.append(pl.BlockSpec((1, tn), lambda i, j: (0, j)))
            args.append(par.reshape(1, n))
    if has_small:
        out_shape.append(jax.ShapeDtypeStruct((b, ns, ta), F32))
        out_specs.append(pl.BlockSpec((1, ns, ta), lambda i, j: (i, 0, 0)))
    if block_min:
        nblk = -(-(ta // ROW_BLOCK) // SUBLANES) * SUBLANES
        out_shape.append(jax.ShapeDtypeStruct((b, n // tn, nblk, LANES), F32))
        out_specs.append(pl.BlockSpec((1, 1, nblk, LANES), lambda i, j: (i, j, 0, 0)))
    res = pl.pallas_call(
        functools.partial(_inproj_kernel, tc=tc, n_x=len(x_arrs), has_small=has_small, epilogue=epilogue,
                          w_rows_out=w_rows_out, block_min=block_min, chunk_all=wide),
        out_shape=out_shape,
        grid=(b, n // tn),
        in_specs=in_specs,
        out_specs=out_specs,
        scratch_shapes=[pltpu.VMEM((ta, d), BF16)],
        compiler_params=_cparams("parallel", "arbitrary"),
        name="inproj",
    )(*args)
    return res if (has_small or block_min) else res[0]


def _head_rms(h, gain, n_heads):
    dh = h.shape[1] // n_heads
    parts = []
    for i in range(n_heads):
        hh = h[:, i * dh:(i + 1) * dh]
        ms = jnp.mean(hh * hh, axis=-1, keepdims=True)
        parts.append(hh * lax.rsqrt(ms + EPS))
    return jnp.concatenate(parts, axis=1) * gain


def _outproj_kernel(*refs, mode, n_feat, n_x, tc, tm, row0):
    feats, rest = refs[:n_feat], list(refs[n_feat:])
    hg_ref = rest.pop(0) if mode != "plain" else None
    w_ref = rest.pop(0)
    x_refs = [rest.pop(0) for _ in range(n_x)]
    ml_ref, mc_ref, o_ref = rest
    if mode == "plain":
        u = feats[0][0]
    elif mode == "mlstm":
        h0_ref, h1_ref, og_ref, z_ref = feats
        hn = _head_rms(h0_ref[0].astype(F32) + h1_ref[0].astype(F32), hg_ref[...], ML_HEADS)
        u = (hn * (1.0 + jnp.tanh(og_ref[0].astype(F32))) * _silu_of_half(z_ref[0].astype(F32))).astype(BF16)
    else:
        h0_ref, h1_ref, z_ref = feats
        hn = _head_rms(h0_ref[0].astype(F32) + h1_ref[0].astype(F32), hg_ref[...], HG_HEADS)
        u = (hn * z_ref[0].astype(F32)).astype(BF16)
    y = jnp.dot(u, w_ref[...], preferred_element_type=F32)
    first = row0 + pl.program_id(1) * tm
    row = first + lax.broadcasted_iota(jnp.int32, (tm, 1), 0)
    gate = jnp.where(row < tc, mc_ref[0, 2:3, :], ml_ref[0, 2:3, :])
    if n_x == 1:
        x = x_refs[0][0]
    elif tm > tc:
        lat = x_refs[1][0]
        x = jnp.where(first < tc, jnp.concatenate([x_refs[0][0], lat[:tm - tc]], axis=0), lat)
    else:
        x = jnp.where(first < tc, x_refs[0][0], x_refs[1][0])
    o_ref[0] = x + gate * y


def _outproj(mode, feats, w_out, xs, mod_l, tc, head_g=None, lat_only=False):
    x_arrs = list(xs) if isinstance(xs, tuple) else [xs]
    b, d = x_arrs[0].shape[0], x_arrs[0].shape[2]
    ta = sum(a.shape[1] for a in x_arrs)
    kdim = w_out.shape[0]
    if lat_only:
        tm = OUT_ROWS_LAT if (ta - tc) % OUT_ROWS_LAT == 0 else OUT_ROWS_SPLIT
    else:
        wide = ta % OUT_ROWS == 0 and (len(x_arrs) == 1 or (tc < OUT_ROWS and tc % ROW_BLOCK == 0))
        tm = OUT_ROWS if wide else OUT_ROWS_SPLIT
        assert ta % tm == 0 and (tc % tm == 0 or len(x_arrs) == 1 or wide)
    row0 = tc if lat_only else 0
    nct = tc // tm
    row_spec = lambda width, col: pl.BlockSpec((pl.Element(1), pl.Element(tm), pl.Element(width)),
                                               lambda i, r, col=col: (i, pl.multiple_of(row0 + r * tm, ROW_BLOCK),
                                                                      col * width))
    in_specs, args = [], []
    for arr, width, col in feats:
        in_specs.append(row_spec(width, col))
        args.append(arr)
    if head_g is not None:
        in_specs.append(pl.BlockSpec((1, kdim), lambda i, r: (0, 0)))
        args.append(head_g.reshape(1, kdim))
    in_specs.append(pl.BlockSpec((kdim, d), lambda i, r: (0, 0)))
    args.append(w_out.astype(BF16))
    if len(x_arrs) == 1:
        in_specs.append(row_spec(d, 0))
    elif lat_only:
        x_arrs = x_arrs[1:]
        in_specs.append(pl.BlockSpec((1, tm, d), lambda i, r: (i, r, 0)))
    elif tm > tc:
        in_specs += [pl.BlockSpec((1, tc, d), lambda i, r: (i, 0, 0)),
                     pl.BlockSpec((pl.Element(1), pl.Element(tm), pl.Element(d)),
                                  lambda i, r: (i, pl.multiple_of(jnp.maximum(r * tm - tc, 0), ROW_BLOCK), 0))]
    else:
        in_specs += [pl.BlockSpec((1, tm, d), lambda i, r: (i, jnp.minimum(r, nct - 1), 0)),
                     pl.BlockSpec((1, tm, d), lambda i, r: (i, jnp.maximum(r - nct, 0), 0))]
    args += x_arrs
    in_specs += [pl.BlockSpec((1, 3, d), lambda i, r: (i, 0, 0)),
                 pl.BlockSpec((1, 3, d), lambda i, r: (0, 0, 0))]
    args += [mod_l[:b], mod_l[b:b + 1]]
    rows_out = ta - row0
    return pl.pallas_call(
        functools.partial(_outproj_kernel, mode=mode, n_feat=len(feats), n_x=len(x_arrs), tc=tc, tm=tm,
                          row0=row0),
        out_shape=jax.ShapeDtypeStruct((b, rows_out, d), F32),
        grid=(b, rows_out // tm),
        in_specs=in_specs,
        out_specs=pl.BlockSpec((1, tm, d), lambda i, r: (i, r, 0)),
        compiler_params=_cparams("parallel", "parallel"),
        name="outproj_" + mode,
    )(*args)


def _conv_kernel(x_ref, ml_ref, mc_ref, g_ref, wx_ref, wb_ref, wc_ref, wz_ref, cw_ref, cb_ref, o_ref, h_ref, *, tc):
    ta = h_ref.shape[0]
    chunk = PROJ_ROWS if ta % PROJ_ROWS == 0 else ta

    def body(first_tile):
        wx, wc, wb, wz = (w_ref[...].astype(BF16) for w_ref in (wx_ref, wc_ref, wb_ref, wz_ref))
        step = chunk if first_tile else ta
        xin, cg = [], []
        for r0 in range(0, ta, step):
            if first_tile:
                _modulated((x_ref,), ml_ref, mc_ref, g_ref, h_ref, tc, r0, step)
            h = h_ref[r0:r0 + step, :]
            xin.append(jnp.dot(h, wx, preferred_element_type=F32))
            cg.append(jnp.dot(h, wc, preferred_element_type=F32))
        bg = jnp.dot(h_ref[...], wb, preferred_element_type=F32)
        z = jnp.dot(h_ref[...], wz, preferred_element_type=F32)
        u = jnp.concatenate(cg, axis=0) * jnp.concatenate(xin, axis=0)
        row = lax.broadcasted_iota(jnp.int32, (ta, 1), 0)
        first = (row == 0) | (row == tc)
        last = (row == tc - 1) | (row == ta - 1)
        u_prev = jnp.where(first, 0.0, pltpu.roll(u, 1, axis=0))
        u_next = jnp.where(last, 0.0, pltpu.roll(u, ta - 1, axis=0))
        cw = cw_ref[...]
        y = u_prev * cw[0:1, :] + u * cw[1:2, :] + u_next * cw[2:3, :] + cb_ref[...]
        o_ref[0] = (bg * y * _silu(z)).astype(o_ref.dtype)

    @pl.when(pl.program_id(1) == 0)
    def _():
        body(True)

    @pl.when(pl.program_id(1) != 0)
    def _():
        body(False)


def _conv_mixer(xs, mod_l, norm_g, w_in, conv_w, conv_b, tc, tw=256):
    b, ta, d = xs.shape
    e = conv_w.shape[1]
    nt = e // tw
    w_spec = lambda part: pl.BlockSpec((d, tw), lambda i, j, part=part: (0, part * nt + j))
    return pl.pallas_call(
        functools.partial(_conv_kernel, tc=tc),
        out_shape=jax.ShapeDtypeStruct((b, ta, e), BF16),
        grid=(b, nt),
        in_specs=[pl.BlockSpec((1, ta, d), lambda i, j: (i, 0, 0)),
                  pl.BlockSpec((1, 3, d), lambda i, j: (i, 0, 0)),
                  pl.BlockSpec((1, 3, d), lambda i, j: (0, 0, 0)),
                  pl.BlockSpec((1, d), lambda i, j: (0, 0)),
                  w_spec(0), w_spec(1), w_spec(2), w_spec(3),
                  pl.BlockSpec((SC_KSIZE, tw), lambda i, j: (0, j)),
                  pl.BlockSpec((1, tw), lambda i, j: (0, j))],
        out_specs=pl.BlockSpec((1, ta, tw), lambda i, j: (i, 0, j)),
        scratch_shapes=[pltpu.VMEM((ta, d), BF16)],
        compiler_params=_cparams("parallel", "arbitrary"),
        name="conv_mixer",
    )(xs, mod_l[:b], mod_l[b:b + 1], norm_g.reshape(1, d), w_in, w_in, w_in, w_in, conv_w, conv_b.reshape(1, e))


AT_HALF = AT_HEAD_DIM // 2
AT_TILE_PERM = np.concatenate([np.arange(0, AT_HALF), np.arange(2 * AT_HALF, 3 * AT_HALF),
                               np.arange(AT_HALF, 2 * AT_HALF), np.arange(3 * AT_HALF, 4 * AT_HALF)])
LOG2E = float(np.log2(np.e))


def _rope_tables(tc, t):
    rows = t // GRID_W
    row = np.repeat(np.arange(rows), GRID_W).astype(np.float64)
    col = np.tile(np.arange(GRID_W), rows).astype(np.float64)
    n_freq = AT_HEAD_DIM // 4
    freqs = np.power(ROPE_BASE, -np.arange(n_freq, dtype=np.float64) / n_freq)
    ang = np.concatenate([row[:, None] * freqs, col[:, None] * freqs], axis=-1)
    cos, sin = np.cos(ang), np.sin(ang)
    cos_t = np.concatenate([np.ones((tc, LANES)), np.tile(cos, (1, 4))], axis=0)
    sin_t = np.concatenate([np.zeros((tc, LANES)), np.concatenate([-sin, -sin, sin, sin], axis=-1)], axis=0)
    return jnp.asarray(cos_t, F32), jnp.asarray(sin_t, F32)


def _norm_rope_tiles(xs, gains, cos, sin, same_head):
    sums = []
    for x in xs:
        x2 = x * x
        hi = x2.astype(BF16)
        lo = (x2 - hi.astype(F32)).astype(BF16)
        sums.append((jnp.dot(hi, same_head, preferred_element_type=F32),
                     jnp.dot(lo, same_head, preferred_element_type=F32)))
    out = []
    for x, gain, (s_hi, s_lo) in zip(xs, gains, sums):
        xn = x * lax.rsqrt(s_hi + s_lo + EPS) * gain
        out.append(xn * cos + pltpu.roll(xn, 2 * AT_HALF, axis=1) * sin)
    return out


def _attn_kernel(sink_ref, q_ref, k_ref, v_ref, z_ref, qg_ref, kg_ref, cos_ref, sin_ref, sh_ref, wo_ref, x_ref,
                 ml_ref, mc_ref, o_ref, qs_ref, ka_ref, kb_ref, va_ref, vb_ref, *, tc, need_ctx):
    ta = q_ref.shape[1]
    t = ta - tc
    nb = t // AT_BLOCK
    blk = AT_BLOCK
    gw = x_ref.shape[2]
    g = pl.program_id(1)
    odd = (g % 2) == 1

    @pl.when(g == 0)
    def _():
        o_ref[...] = jnp.zeros(o_ref.shape, F32)

    def prep(i, carry):
        r0 = pl.multiple_of(i * ROW_BLOCK, ROW_BLOCK)
        res_cols = pl.ds(pl.multiple_of(g * gw, gw), gw)
        o_ref[0, pl.ds(r0, ROW_BLOCK), res_cols] = (o_ref[0, pl.ds(r0, ROW_BLOCK), res_cols]
                                                    + x_ref[0, pl.ds(r0, ROW_BLOCK), :])
        cos = cos_ref[pl.ds(r0, ROW_BLOCK), :]
        sin = sin_ref[pl.ds(r0, ROW_BLOCK), :]
        same_head = sh_ref[...]
        tiles = [q_ref[0, pl.ds(r0, ROW_BLOCK), c * LANES:(c + 1) * LANES].astype(F32) for c in range(2)]
        tiles.append(k_ref[0, pl.ds(r0, ROW_BLOCK), :].astype(F32))
        q0, q1, kn = _norm_rope_tiles(tiles, (qg_ref[...], qg_ref[...], kg_ref[...]), cos, sin, same_head)
        for c, qc in enumerate((q0, q1)):
            qs_ref[pl.ds(r0, ROW_BLOCK), c * LANES:(c + 1) * LANES] = qc.astype(BF16)
        vv = v_ref[0, pl.ds(r0, ROW_BLOCK), :].astype(F32)
        lane = lax.broadcasted_iota(jnp.int32, kn.shape, 1)
        k_own = jnp.where(((lane // AT_HALF) % 2) == (g % 2), kn, 0.0)
        k_oth = pltpu.roll(k_own, jnp.where(odd, 3 * AT_HALF, AT_HALF), axis=1)
        ka_ref[pl.ds(r0, ROW_BLOCK), :] = jnp.where(odd, k_oth, k_own).astype(BF16)
        kb_ref[pl.ds(r0, ROW_BLOCK), :] = jnp.where(odd, k_own, k_oth).astype(BF16)
        v_own = jnp.where((lane // AT_HEAD_DIM) == (g % 2), vv, 0.0)
        v_oth = pltpu.roll(v_own, AT_HEAD_DIM, axis=1)
        va = jnp.where(odd, v_oth, v_own)
        vb = jnp.where(odd, v_own, v_oth)
        va_ref[pl.ds(r0, ROW_BLOCK), :] = jnp.where(lane == AT_HEAD_DIM, 1.0, va).astype(BF16)
        vb_ref[pl.ds(r0, ROW_BLOCK), :] = jnp.where(lane == 0, 1.0, vb).astype(BF16)
        return carry

    lax.fori_loop(0, ta // ROW_BLOCK, prep, 0, unroll=3)
    zeros = jnp.zeros((blk, LANES), BF16)
    for ref in (ka_ref, kb_ref, va_ref, vb_ref):
        ref[ta:ta + blk, :] = zeros

    half = lax.broadcasted_iota(jnp.int32, (2 * blk, 1), 0) < blk
    sink_a = jnp.where(half, sink_ref[g, 0], sink_ref[g, 2]) * LOG2E
    sink_b = jnp.where(half, sink_ref[g, 1], sink_ref[g, 3]) * LOG2E

    qi = lax.broadcasted_iota(jnp.int32, (2 * blk, 3 * blk), 0) % blk
    kj = lax.broadcasted_iota(jnp.int32, (2 * blk, 3 * blk), 1)
    band = (kj - qi >= 0) & (kj - qi <= 2 * blk)
    out_lo = lax.broadcasted_iota(jnp.int32, (2 * blk, LANES), 1) < AT_HEAD_DIM

    nt = (((1,), (1,)), ((), ()))

    sides = ((ka_ref, va_ref, sink_a, AT_HEAD_DIM), (kb_ref, vb_ref, sink_b, 0))

    def attend(blocks, r_first, gate):
        qts = [jnp.concatenate([qs_ref[pl.ds(r0, blk), 0:LANES], qs_ref[pl.ds(r0, blk), LANES:2 * LANES]], axis=0)
               for r0, _, _ in blocks]
        scores = []
        for qt, (_, k0, mask) in zip(qts, blocks):
            for k_ref_, _, _, _ in sides:
                s_ctx = lax.dot_general(qt, k_ref_[0:tc, :], nt, preferred_element_type=F32)
                s_loc = None if mask is None else lax.dot_general(qt, k_ref_[pl.ds(k0, 3 * blk), :], nt,
                                                                  preferred_element_type=F32)
                scores.append((s_ctx, s_loc))
        probs = []
        for idx, (s_ctx, s_loc) in enumerate(scores):
            mask = blocks[idx // 2][2]
            sink = sides[idx % 2][2]
            m = jnp.maximum(sink, jnp.max(s_ctx, axis=-1, keepdims=True))
            if s_loc is not None:
                s_loc = jnp.where(mask, s_loc, -jnp.inf)
                m = jnp.maximum(m, jnp.max(s_loc, axis=-1, keepdims=True))
            probs.append((jnp.exp2(s_ctx - m).astype(BF16),
                          None if s_loc is None else jnp.exp2(s_loc - m).astype(BF16), jnp.exp2(sink - m)))
        accs = []
        for idx, (p_ctx, p_loc, _) in enumerate(probs):
            k0 = blocks[idx // 2][1]
            v_ref_ = sides[idx % 2][1]
            acc = jnp.dot(p_ctx, v_ref_[0:tc, :], preferred_element_type=F32)
            if p_loc is not None:
                acc = acc + jnp.dot(p_loc, v_ref_[pl.ds(k0, 3 * blk), :], preferred_element_type=F32)
            accs.append(acc)
        us = []
        for bi, (r0, _, _) in enumerate(blocks):
            halves = []
            for si in range(2):
                acc, ones_lane = accs[2 * bi + si], sides[si][3]
                halves.append(acc / (probs[2 * bi + si][2] + acc[:, ones_lane:ones_lane + 1]))
            o = jnp.where(out_lo, halves[0], halves[1])
            parts = []
            for c in range(2):
                zc = z_ref[0, pl.ds(r0, blk), c * LANES:(c + 1) * LANES].astype(F32)
                parts.append((o[c * blk:(c + 1) * blk] * _silu_of_half(zc)).astype(BF16))
            us.append(jnp.concatenate(parts, axis=1))
        u = jnp.concatenate(us, axis=0)
        rows = pl.ds(r_first, u.shape[0])
        o_ref[0, rows, :] = o_ref[0, rows, :] + gate * jnp.dot(u, wo_ref[...], preferred_element_type=F32)

    def lat_blocks(i, carry):
        blocks = []
        for j in range(AT_PROJ_BLOCKS):
            n = i * AT_PROJ_BLOCKS + j
            kpos = (n - 1) * blk + kj
            blocks.append((pl.multiple_of(tc + n * blk, blk), pl.multiple_of(tc + (n - 1) * blk, blk),
                           band & (kpos >= 0) & (kpos < t)))
        attend(blocks, pl.multiple_of(tc + i * AT_PROJ_BLOCKS * blk, AT_PROJ_BLOCKS * blk), ml_ref[0, 2:3, :])
        return carry

    lax.fori_loop(0, nb // AT_PROJ_BLOCKS, lat_blocks, 0)
    if need_ctx:
        attend([(n * blk, None, None) for n in range(tc // blk)], 0, mc_ref[0, 2:3, :])


def _attn_mixer(p, at_q_g, at_k_g, at_sink, w_out, xs, mod_l, tc, need_ctx):
    b, ta, _ = p.shape
    t = ta - tc
    d = xs.shape[2]
    assert tc >= AT_BLOCK and tc % AT_BLOCK == 0 and t % (AT_PROJ_BLOCKS * AT_BLOCK) == 0 and ta % ROW_BLOCK == 0
    assert d == AT_Q
    cos_t, sin_t = _rope_tables(tc, t)
    lane_head = (np.arange(LANES) // AT_HALF) % 2
    same_head = jnp.asarray((lane_head[:, None] == lane_head[None, :]) / AT_HEAD_DIM, BF16)
    q_scale = (AT_HEAD_DIM ** -0.5) * LOG2E
    tile_gain = lambda gain: jnp.tile(gain, 2)[AT_TILE_PERM].reshape(1, LANES)
    gw = AT_GROUP * AT_HEAD_DIM
    kcol = AT_Q // LANES
    vcol = (AT_Q + AT_KV) // LANES
    zcol = (AT_Q + 2 * AT_KV) // gw
    return pl.pallas_call(
        functools.partial(_attn_kernel, tc=tc, need_ctx=need_ctx),
        out_shape=jax.ShapeDtypeStruct((b, ta, d), F32),
        grid=(b, AT_KV_HEADS),
        in_specs=[pl.BlockSpec(memory_space=pltpu.SMEM),
                  pl.BlockSpec((1, ta, gw), lambda i, g: (i, 0, g)),
                  pl.BlockSpec((1, ta, LANES), lambda i, g: (i, 0, kcol + g // 2)),
                  pl.BlockSpec((1, ta, LANES), lambda i, g: (i, 0, vcol + g // 2)),
                  pl.BlockSpec((1, ta, gw), lambda i, g: (i, 0, zcol + g)),
                  pl.BlockSpec((1, LANES), lambda i, g: (0, 0)),
                  pl.BlockSpec((1, LANES), lambda i, g: (0, 0)),
                  pl.BlockSpec((ta, LANES), lambda i, g: (0, 0)),
                  pl.BlockSpec((ta, LANES), lambda i, g: (0, 0)),
                  pl.BlockSpec((LANES, LANES), lambda i, g: (0, 0)),
                  pl.BlockSpec((gw, d), lambda i, g: (g, 0)),
                  pl.BlockSpec((1, ta, gw), lambda i, g: (i, 0, g)),
                  pl.BlockSpec((1, 3, d), lambda i, g: (i, 0, 0)),
                  pl.BlockSpec((1, 3, d), lambda i, g: (0, 0, 0))],
        out_specs=pl.BlockSpec((1, ta, d), lambda i, g: (i, 0, 0)),
        scratch_shapes=[pltpu.VMEM((ta, gw), BF16)] + [pltpu.VMEM((ta + AT_BLOCK, LANES), BF16)] * 4,
        compiler_params=_cparams("parallel", "arbitrary"),
        name="attn_mixer",
    )(at_sink.reshape(AT_KV_HEADS, AT_GROUP), p, p, p, p, q_scale * tile_gain(at_q_g), tile_gain(at_k_g), cos_t, sin_t,
      same_head, w_out.astype(BF16), xs, mod_l[:b], mod_l[b:b + 1])


def _attn_weight_kernel(w_ref, p_ref, o_ref, *, n_qk, n_v):
    tiles = w_ref.shape[1] // LANES
    for lt in range(tiles):
        tile = pl.program_id(0) * tiles + lt
        kind = (tile >= n_qk).astype(jnp.int32) + (tile >= n_qk + n_v).astype(jnp.int32)
        cols = slice(lt * LANES, (lt + 1) * LANES)
        o_ref[:, cols] = jnp.dot(w_ref[:, cols].astype(BF16), p_ref[kind], preferred_element_type=F32).astype(BF16)


def _attn_weight(w_in):
    d, n = w_in.shape
    n_qk, n_v = (AT_Q + AT_KV) // LANES, AT_KV // LANES
    perm = np.zeros((LANES, LANES), np.float32)
    perm[AT_TILE_PERM, np.arange(LANES)] = 1.0
    eye = np.eye(LANES, dtype=np.float32)
    mats = jnp.asarray(np.stack([perm, eye, 0.5 * eye]), BF16)
    tw = 4 * LANES
    assert n % tw == 0
    return pl.pallas_call(
        functools.partial(_attn_weight_kernel, n_qk=n_qk, n_v=n_v),
        out_shape=jax.ShapeDtypeStruct((d, n), BF16),
        grid=(n // tw,),
        in_specs=[pl.BlockSpec((d, tw), lambda j: (0, j)),
                  pl.BlockSpec((3, LANES, LANES), lambda j: (0, 0, 0))],
        out_specs=pl.BlockSpec((d, tw), lambda j: (0, j)),
        compiler_params=_cparams("parallel"),
        name="attn_weight",
    )(w_in, mats)


def _chunk_order(nc, ncc, d, s):
    bwd = jnp.where(s < ncc, ncc - 1 - s, nc - 1 - (s - ncc))
    return jnp.where(d == 0, s, bwd)


ML_GATE_PERM = np.concatenate([np.arange(0, 4), np.arange(8, 12), np.arange(4, 8), np.arange(12, 16)])
ML_NQ = 6


def _mlstm_gate_kernel(g_ref, b_ref, a_ref, c_ref, *, tc, lc):
    h = ML_HEADS
    ta = g_ref.shape[2]
    nc, ncc = ta // lc, tc // lc
    x = g_ref[0] + b_ref[...]
    li = x[0:2 * h]
    lfp = x[2 * h:4 * h]
    lf = jnp.minimum(lfp, 0.0) - jnp.log1p(jnp.exp(-jnp.abs(lfp)))
    fwd = lax.broadcasted_iota(jnp.int32, (2 * h, ta), 0) < h
    fwd1 = lax.broadcasted_iota(jnp.int32, (2 * h, 1), 0) < h
    pos = lax.broadcasted_iota(jnp.int32, (2 * h, ta), 1) % lc

    def seg_scan(y, op, fill):
        yf, yb = y, y
        s = 1
        while s < lc:
            yf = op(yf, jnp.where(pos >= s, pltpu.roll(yf, s, axis=1), fill))
            yb = op(yb, jnp.where(pos < lc - s, pltpu.roll(yb, ta - s, axis=1), fill))
            s *= 2
        return jnp.where(fwd, yf, yb)

    bsum = seg_scan(lf, jnp.add, 0.0)
    a = li - bsum
    cmax = seg_scan(a, jnp.maximum, -jnp.inf)

    def end_col(y, c):
        return jnp.where(fwd1, y[:, (c + 1) * lc - 1:(c + 1) * lc], y[:, c * lc:c * lc + 1])

    tot = [end_col(bsum, c) for c in range(nc)]
    amax = [end_col(cmax, c) for c in range(nc)]

    def chain(order):
        m = jnp.zeros((2 * h, 1), F32)
        m_in = [None] * nc
        for c in order:
            m_in[c] = m
            m = tot[c] + jnp.maximum(m, amax[c])
        return m_in

    m_f = chain(list(range(nc)))
    m_b = chain(list(range(ncc - 1, -1, -1)) + list(range(nc - 1, ncc - 1, -1)))
    for c in range(nc):
        m_in = jnp.where(fwd1, m_f[c], m_b[c])
        sl = slice(c * lc, (c + 1) * lc)
        g_run = jnp.maximum(m_in, cmax[:, sl])
        g_end = jnp.maximum(m_in, amax[c])
        nr = 2 * h
        quantities = [a[:, sl], g_run, jnp.exp(m_in - g_run), jnp.exp(-(bsum[:, sl] + g_run)),
                      jnp.exp(a[:, sl] - g_end), jnp.broadcast_to(jnp.exp(m_in - g_end), (nr, lc))]
        a_ref[0, c] = quantities[0]
        pad = jnp.zeros((LANES - ML_NQ * nr, lc), F32)
        c_ref[0, c] = jnp.concatenate(quantities + [pad], axis=0).T


def _mlstm_scan_kernel(q_ref, k_ref, v_ref, a_ref, c_ref, o_ref, cs_ref, *, rev):
    lc = q_ref.shape[1]
    nr = 2 * ML_HEADS

    @pl.when(pl.program_id(1) == 0)
    def _():
        cs_ref[...] = jnp.zeros(cs_ref.shape, F32)

    ti = lax.broadcasted_iota(jnp.int32, (lc, lc), 0)
    si = lax.broadcasted_iota(jnp.int32, (lc, lc), 1)
    mask = (si >= ti) if rev else (si <= ti)
    ones = jnp.ones((lc, LANES), BF16)
    nt = (((1,), (1,)), ((), ()))
    tn = (((0,), (0,)), ((), ()))
    heads = range(ML_HEADS)
    qb = [q_ref[0, :, h * ML_DK:(h + 1) * ML_DK] for h in heads]
    kb = [k_ref[0, :, h * ML_DK:(h + 1) * ML_DK] for h in heads]
    v_aug = [jnp.concatenate([v_ref[0, :, h * ML_DV:(h + 1) * ML_DV], ones], axis=1) for h in heads]
    cols = []
    for h in heads:
        r = (ML_HEADS if rev else 0) + h
        cols.append([a_ref[0, 0, r:r + 1, :]] + [c_ref[0, 0, :, qi * nr + r:qi * nr + r + 1] for qi in range(1, 6)])
    qk = [lax.dot_general(qb[h], kb[h], nt, preferred_element_type=F32) for h in heads]
    c_old = [cs_ref[h] for h in heads]
    q_c = [jnp.dot(qb[h], c_old[h].astype(BF16), preferred_element_type=F32) for h in heads]
    kw = [(kb[h].astype(F32) * cols[h][4]).astype(BF16) for h in heads]
    upd = [lax.dot_general(kw[h], v_aug[h], tn, preferred_element_type=F32) for h in heads]
    s = []
    for h in heads:
        a_row, g_run = cols[h][0], cols[h][1]
        s.append((qk[h] * jnp.where(mask, jnp.exp(a_row - g_run), 0.0)).astype(BF16))
    s_v = [jnp.dot(s[h], v_aug[h], preferred_element_type=F32) for h in heads]
    for h in heads:
        inter, eclamp, decay = cols[h][2], cols[h][3], cols[h][5]
        tot = inter * q_c[h] + s_v[h]
        inv = 1.0 / jnp.maximum(jnp.abs(tot[:, ML_DV:]), eclamp)
        o_ref[0, :, h * ML_DV:(h + 1) * ML_DV] = (
            tot[:, :ML_DV] * jnp.concatenate([inv] * (ML_DV // LANES), axis=1)).astype(o_ref.dtype)
        cs_ref[h] = decay[0:1, :] * c_old[h] + upd[h]


def _mlstm_scan(p, gates, gate_b, tc):
    b, ta, _ = p.shape
    lc = ML_CHUNK
    assert tc % lc == 0 and ta % lc == 0
    nc, ncc = ta // lc, tc // lc
    ng = 4 * ML_HEADS
    nr = 2 * ML_HEADS
    bias = gate_b.reshape(ng)[ML_GATE_PERM].reshape(ng, 1)
    a_rows, cols = pl.pallas_call(
        functools.partial(_mlstm_gate_kernel, tc=tc, lc=lc),
        out_shape=[jax.ShapeDtypeStruct((b, nc, nr, lc), F32), jax.ShapeDtypeStruct((b, nc, lc, LANES), F32)],
        grid=(b,),
        in_specs=[pl.BlockSpec((1, ng, ta), lambda i: (i, 0, 0)),
                  pl.BlockSpec((ng, 1), lambda i: (0, 0))],
        out_specs=[pl.BlockSpec((1, nc, nr, lc), lambda i: (i, 0, 0, 0)),
                   pl.BlockSpec((1, nc, lc, LANES), lambda i: (i, 0, 0, 0))],
        compiler_params=_cparams("parallel"),
        name="mlstm_gates",
    )(gates, bias)
    outs = []
    for d in range(2):
        chunk = functools.partial(_chunk_order, nc, ncc, d)
        outs.append(pl.pallas_call(
            functools.partial(_mlstm_scan_kernel, rev=bool(d)),
            out_shape=jax.ShapeDtypeStruct((b, ta, ML_INNER), BF16),
            grid=(b, nc),
            in_specs=[pl.BlockSpec((1, lc, ML_QK), lambda i, s, chunk=chunk: (i, chunk(s), 0)),
                      pl.BlockSpec((1, lc, ML_QK), lambda i, s, chunk=chunk: (i, chunk(s), 1)),
                      pl.BlockSpec((1, lc, ML_INNER), lambda i, s, chunk=chunk: (i, chunk(s), 1)),
                      pl.BlockSpec((1, 1, nr, lc), lambda i, s, chunk=chunk: (i, chunk(s), 0, 0)),
                      pl.BlockSpec((1, 1, lc, LANES), lambda i, s, chunk=chunk: (i, chunk(s), 0, 0))],
            out_specs=pl.BlockSpec((1, lc, ML_INNER), lambda i, s, chunk=chunk: (i, chunk(s), 0)),
            scratch_shapes=[pltpu.VMEM((ML_HEADS, ML_DK, ML_DV + LANES), F32)],
            compiler_params=_cparams("parallel", "arbitrary"),
            name="mlstm_scan_bwd" if d else "mlstm_scan_fwd",
        )(p, p, p, a_rows, cols))
    return outs


def _hgrn_lb_kernel(p_ref, o_ref, *, layer):
    for d in range(p_ref.shape[0]):
        x = p_ref[d]
        e = jnp.exp(x - jnp.max(x, axis=0, keepdims=True))
        p = e / jnp.sum(e, axis=0, keepdims=True)
        acc = jnp.zeros((1, x.shape[1]), F32)
        for j in range(1, layer + 1):
            acc = acc + p[j:j + 1, :]
        o_ref[d:d + 1, :] = acc


def _cumsum_rows(x, tri2):
    hi = x.astype(BF16)
    lo = (x - hi.astype(F32)).astype(BF16)
    return jnp.dot(tri2, jnp.concatenate([hi, lo], axis=0), preferred_element_type=F32)


def _tri2(n, rev):
    t = np.triu(np.ones((n, n), np.float32)) if rev else np.tril(np.ones((n, n), np.float32))
    return jnp.asarray(np.concatenate([t, t], axis=1), BF16)


def _anchor_rows(a, m, rev):
    n, f = a.shape
    idx = m if rev else m - 1
    if 2 * m >= SUBLANES:
        a3 = a.reshape(n // (2 * m), 2 * m, f)
        return jnp.broadcast_to(a3[:, idx:idx + 1, :], a3.shape).reshape(n, f)
    a3 = a.reshape(n // SUBLANES, SUBLANES, f)
    sub = lax.broadcasted_iota(jnp.int32, a3.shape, 1)
    out = None
    for gi in range(SUBLANES // (2 * m) - 1, -1, -1):
        cand = jnp.broadcast_to(a3[:, gi * 2 * m + idx:gi * 2 * m + idx + 1, :], a3.shape)
        out = cand if out is None else jnp.where(sub < (gi + 1) * 2 * m, cand, out)
    return out.reshape(n, f)


HG_MAX_LOG_SPAN = 64.0


def _hgrn_scan_kernel(fmin_ref, q_ref, f_ref, i_ref, tri_ref, o_ref, s_ref, *, rev, nc, ncc):
    lc = HG_CHUNK
    subs = range(q_ref.shape[1] // lc)
    subs = tuple(reversed(subs)) if rev else tuple(subs)
    step_block = _chunk_order(nc, ncc, 1 if rev else 0, pl.program_id(1))

    @pl.when(pl.program_id(1) == 0)
    def _():
        s_ref[...] = jnp.zeros(s_ref.shape, F32)

    row = lax.broadcasted_iota(jnp.int32, (lc, 1), 0)
    ti = lax.broadcasted_iota(jnp.int32, (lc, lc), 0)
    si = lax.broadcasted_iota(jnp.int32, (lc, lc), 1)
    nt = (((1,), (1,)), ((), ()))
    tn = (((0,), (0,)), ((), ()))
    end = 0 if rev else lc - 1

    def all_heads_single_anchor():
        causal = (si >= ti) if rev else (si <= ti)
        cols = [slice(h * HG_DIM, (h + 1) * HG_DIM) for h in range(HG_HEADS)]
        pre = {}
        for ci in subs:
            rows = slice(ci * lc, (ci + 1) * lc)
            f = f_ref[0, rows, :]
            a = _cumsum_rows(jnp.log(f), tri_ref[...])
            q_dec = (q_ref[0, rows, :].astype(F32) * jnp.exp(a)).astype(BF16)
            k_inv = (1.0 - f) * jnp.exp(-a)
            k_inv_b = k_inv.astype(BF16)
            e_end = jnp.exp(a[end:end + 1, :])
            kd = (k_inv * e_end).astype(BF16)
            attn = [jnp.where(causal, lax.dot_general(q_dec[:, c], k_inv_b[:, c], nt, preferred_element_type=F32),
                              0.0).astype(BF16) for c in cols]
            pre[ci] = (rows, q_dec, e_end, kd, attn)
        state = [s_ref[h] for h in range(HG_HEADS)]
        for ci in subs:
            rows, q_dec, e_end, kd, attn = pre[ci]
            for h, c in enumerate(cols):
                o = jnp.dot(attn[h], i_ref[0, rows, c], preferred_element_type=F32)
                o = o + lax.dot_general(q_dec[:, c], state[h].astype(BF16), nt, preferred_element_type=F32)
                o_ref[0, rows, c] = o.astype(o_ref.dtype)
            state = [state[h] * e_end[:, c] + lax.dot_general(i_ref[0, rows, c], kd[:, c], tn,
                                                              preferred_element_type=F32)
                     for h, c in enumerate(cols)]
        for h in range(HG_HEADS):
            s_ref[h] = state[h]

    def head_per_level(r0, h, carry):
        c0 = pl.multiple_of(h * HG_DIM, HG_DIM)
        q = q_ref[0, r0:r0 + lc, pl.ds(c0, HG_DIM)].astype(F32)
        f = f_ref[0, r0:r0 + lc, pl.ds(c0, HG_DIM)]
        k = 1.0 - f
        iv = i_ref[0, r0:r0 + lc, pl.ds(c0, HG_DIM)].astype(F32)
        a = _cumsum_rows(jnp.log(f), tri_ref[...])
        attn = jnp.zeros((lc, lc), F32)
        m = 1
        while m < lc:
            e = jnp.exp(-jnp.abs(a - _anchor_rows(a, m, rev)))
            upper = (row % (2 * m)) >= m
            is_q = jnp.logical_not(upper) if rev else upper
            qt = jnp.where(is_q, q * e, 0.0).astype(BF16)
            kt = jnp.where(is_q, 0.0, k * e).astype(BF16)
            pair = lax.dot_general(qt, kt, nt, preferred_element_type=F32)
            attn = attn + jnp.where((ti // (2 * m)) == (si // (2 * m)), pair, 0.0)
            m *= 2
        a_end = a[end:end + 1, :]
        kd = (k * jnp.exp(a_end - a)).astype(BF16)
        ib = iv.astype(BF16)
        s_old = s_ref[h]
        o = jnp.dot(attn.astype(BF16), ib, preferred_element_type=F32)
        o = o + lax.dot_general((q * jnp.exp(a)).astype(BF16), s_old.astype(BF16), nt, preferred_element_type=F32)
        o_ref[0, r0:r0 + lc, pl.ds(c0, HG_DIM)] = (o + jnp.sum(q * k, axis=-1, keepdims=True) * iv).astype(o_ref.dtype)
        s_ref[h] = s_old * jnp.exp(a_end) + lax.dot_general(ib, kd, tn, preferred_element_type=F32)
        return carry

    in_range = fmin_ref[pl.program_id(0), step_block] >= float(np.exp(-HG_MAX_LOG_SPAN / lc))

    @pl.when(in_range)
    def _():
        all_heads_single_anchor()

    @pl.when(jnp.logical_not(in_range))
    def _():
        for ci in subs:
            lax.fori_loop(0, HG_HEADS, functools.partial(head_per_level, ci * lc), 0)


def _hgrn_scan(p, pf, f_min, tc, rev):
    b, ta, _ = p.shape
    lc = HG_CHUNK
    rows = HG_STEP_CHUNKS * lc
    hg = HG_HEADS * HG_DIM
    assert tc % rows == 0 and ta % rows == 0 and rows == ROW_BLOCK
    nc, ncc = ta // rows, tc // rows
    d = 1 if rev else 0
    chunk = lambda s: _chunk_order(nc, ncc, d, s)
    return pl.pallas_call(
        functools.partial(_hgrn_scan_kernel, rev=rev, nc=nc, ncc=ncc),
        out_shape=jax.ShapeDtypeStruct((b, ta, hg), BF16),
        grid_spec=pltpu.PrefetchScalarGridSpec(
            num_scalar_prefetch=1,
            grid=(b, nc),
            in_specs=[pl.BlockSpec((1, rows, hg), lambda i, s, fm: (i, chunk(s), 0)),
                      pl.BlockSpec((1, rows, hg), lambda i, s, fm: (i, chunk(s), d)),
                      pl.BlockSpec((1, rows, hg), lambda i, s, fm: (i, chunk(s), 1)),
                      pl.BlockSpec((lc, 2 * lc), lambda i, s, fm: (0, 0))],
            out_specs=pl.BlockSpec((1, rows, hg), lambda i, s, fm: (i, chunk(s), 0)),
            scratch_shapes=[pltpu.VMEM((HG_HEADS, HG_DIM, HG_DIM), F32)]),
        compiler_params=_cparams("parallel", "arbitrary"),
        name="hgrn_scan_bwd" if rev else "hgrn_scan_fwd",
    )(f_min, p, pf, p, _tri2(lc, rev))


def kernel(x, c, ctx, c_ctx, ada_w, ada_b, norm_g, ml_w_in, ml_gate_b, ml_head_g, ml_w_out, at_w_in, at_q_g, at_k_g, at_sink, at_w_out, sc_w_in, sc_conv_w, sc_conv_b, sc_w_out, hg_w_in, hg_f_b, hg_lb, hg_head_g, hg_w_out):
    tc = ctx.shape[1]
    mod = _ada_mod(c, c_ctx, ada_w, ada_b)
    xs = (ctx, x)
    for layer in range(DEPTH):
        kind, j = layer % 4, layer // 4
        need_ctx = layer < DEPTH - 1
        last = dict(lat_only=True) if layer == DEPTH - 1 else {}
        mod_l = mod[layer]
        if kind == 0:
            n_main = 2 * ML_QK + 3 * ML_INNER
            w_t = ml_w_in[j].T
            w_gate = jnp.zeros((LANES, ml_w_in.shape[1]), F32).at[:4 * ML_HEADS].set(w_t[n_main:][ML_GATE_PERM])
            col_scale = jnp.concatenate([jnp.full((ML_QK,), ML_DK ** -0.5, F32), jnp.ones((ML_QK + ML_INNER,), F32),
                                         jnp.full((2 * ML_INNER,), 0.5, F32)])
            p, gates = _inproj(xs, mod_l, norm_g[layer], w_t[:n_main] * col_scale[:, None], tc, tn=2048,
                               w_small=w_gate, out_dtype=BF16, w_rows_out=True, wide=True)
            h_f, h_b = _mlstm_scan(p, gates, ml_gate_b[j], tc)
            feats = [(h_f, ML_INNER, 0), (h_b, ML_INNER, 0), (p, ML_INNER, 2), (p, ML_INNER, 3)]
            xs = _outproj("mlstm", feats, ml_w_out[j], xs, mod_l, tc, head_g=0.5 * ml_head_g[j], **last)
        elif kind == 1:
            p = _inproj(xs, mod_l, norm_g[layer], _attn_weight(at_w_in[j]), tc, tn=AT_Q + AT_KV, out_dtype=BF16)
            if isinstance(xs, tuple):
                xs = jnp.concatenate(xs, axis=1)
            xs = _attn_mixer(p, at_q_g[j], at_k_g[j], at_sink[j], at_w_out[j], xs, mod_l, tc, need_ctx)
            if last:
                xs = xs[:, tc:, :]
        elif kind == 2:
            if isinstance(xs, tuple):
                xs = jnp.concatenate(xs, axis=1)
            u = _conv_mixer(xs, mod_l, norm_g[layer], sc_w_in[j], sc_conv_w[j], sc_conv_b[j], tc)
            xs = _outproj("plain", [(u, u.shape[2], 0)], sc_w_out[j], xs, mod_l, tc, **last)
        else:
            hg = HG_HEADS * HG_DIM
            lb = pl.pallas_call(
                functools.partial(_hgrn_lb_kernel, layer=layer),
                out_shape=jax.ShapeDtypeStruct((2, hg), F32),
                name="hgrn_lb",
            )(hg_lb[j])
            w = hg_w_in[j]
            w_qiz = jnp.concatenate([0.5 * w[:, :hg], w[:, 3 * hg:4 * hg], 0.5 * w[:, 4 * hg:]], axis=1)
            p = _inproj(xs, mod_l, norm_g[layer], w_qiz, tc, tn=1024, out_dtype=BF16,
                        epilogue=((0, hg, "silu"), (hg, 2 * hg, "id"), (2 * hg, 3 * hg, "silu")))
            lb2 = lb.reshape(2 * hg)
            pf, f_min = _inproj(xs, mod_l, norm_g[layer], 0.5 * w[:, hg:3 * hg], tc, tn=hg,
                                epilogue=((0, 2 * hg, "fgate"),), block_min=True,
                                pars=(0.5 * hg_f_b[j].reshape(2 * hg), 0.5 * (1.0 + lb2), 0.5 * (1.0 - lb2)))
            n_blk = pf.shape[1] // ROW_BLOCK
            o_f = _hgrn_scan(p, pf, f_min[:, 0, :n_blk, 0], tc, rev=False)
            o_b = _hgrn_scan(p, pf, f_min[:, 1, :n_blk, 0], tc, rev=True)
            feats = [(o_f, hg, 0), (o_b, hg, 0), (p, hg, 2)]
            xs = _outproj("hgrn", feats, hg_w_out[j], xs, mod_l, tc, head_g=hg_head_g[j], **last)
    return xs
```
